```python
import functools
import jax
import jax.numpy as jnp
from jax import lax
import numpy as np

D_MODEL = 1024
BATCH = 16
SEQ = 256
DEPTH = 2
DEC_BATCH = 2
DEC_SEQ = 2048
PAST_LEN = 512

GRID_W = 64
ROPE_BASE = 10000.0
NORM_EPS = 1e-6
Q_BLOCK = 128
N_MOD = 9
D_FF = 2816

MLA_HEADS = 8
MLA_Q_LORA = 384
MLA_KV_LORA = 256
MLA_NOPE = 64
MLA_ROPE = 32
MLA_V = 64

SWA_HEADS = 8
SWA_KV_HEADS = 2
SWA_GROUP = SWA_HEADS // SWA_KV_HEADS
SWA_HD = 64
SWA_WINDOW = 128

GDN_HEADS = 4
GDN_DK = 128
GDN_DV = 128
GDN_CONV = 5
GDN_CHUNK = 64
GDN_QK_DIM = GDN_HEADS * GDN_DK
GDN_V_DIM = GDN_HEADS * GDN_DV
GDN_CONV_CH = 2 * GDN_QK_DIM + GDN_V_DIM

IN_SPLIT_SIZES = (MLA_Q_LORA, MLA_KV_LORA, MLA_ROPE,
                  SWA_HEADS * SWA_HD, SWA_KV_HEADS * SWA_HD, SWA_KV_HEADS * SWA_HD,
                  GDN_QK_DIM, GDN_QK_DIM, GDN_V_DIM, GDN_V_DIM, 4 * GDN_HEADS)
IN_DIM = (MLA_Q_LORA + MLA_KV_LORA + MLA_ROPE + SWA_HEADS * SWA_HD + 2 * SWA_KV_HEADS * SWA_HD
          + 2 * GDN_QK_DIM + 2 * GDN_V_DIM + 4 * GDN_HEADS)
D_MIX = MLA_HEADS * MLA_V + SWA_HEADS * SWA_HD + GDN_HEADS * GDN_DV

kernel_name = 'hybrid_mla_swa_gdn_macaron_diffusion_step'


def rms_norm(x, gain):
    xf = x.astype(jnp.float32)
    y = xf * lax.rsqrt(jnp.mean(xf * xf, axis=-1, keepdims=True) + NORM_EPS)
    return (y * gain.astype(jnp.float32)).astype(x.dtype)


def l2_normalize(x):
    xf = x.astype(jnp.float32)
    return (xf * lax.rsqrt(jnp.sum(xf * xf, axis=-1, keepdims=True) + 1e-6)).astype(x.dtype)


def modulate(x, shift, scale):
    return x * (1 + scale) + shift


def adaln_params(cond, w_ada, b_ada):
    m = jax.nn.silu(cond) @ w_ada + b_ada
    return jnp.split(m[:, None, :], N_MOD, axis=-1)


def swiglu(h, w1, w2):
    gate, up = jnp.split(h @ w1, 2, axis=-1)
    return (jax.nn.silu(gate) * up) @ w2


def axial_rope_tables(n_tokens, dim):
    rows = n_tokens // GRID_W
    row = jnp.repeat(jnp.arange(rows, dtype=jnp.float32), GRID_W)
    col = jnp.tile(jnp.arange(GRID_W, dtype=jnp.float32), rows)
    axis_dim = dim // 2
    inv_freq = 1.0 / (ROPE_BASE ** (jnp.arange(0, axis_dim, 2, dtype=jnp.float32) / axis_dim))
    ang_r = row[:, None] * inv_freq[None, :]
    ang_c = col[:, None] * inv_freq[None, :]
    ang = jnp.concatenate([ang_r, ang_r, ang_c, ang_c], axis=-1)
    return jnp.cos(ang), jnp.sin(ang)


def rotate_half(z):
    z1, z2 = jnp.split(z, 2, axis=-1)
    return jnp.concatenate([-z2, z1], axis=-1)


def apply_rope(x, cos, sin):
    shape = (cos.shape[0],) + (1,) * (x.ndim - 3) + (cos.shape[1],)
    xf = x.astype(jnp.float32)
    half = x.shape[-1] // 2
    rot = jnp.concatenate([rotate_half(xf[..., :half]), rotate_half(xf[..., half:])], axis=-1)
    return (xf * cos.reshape(shape) + rot * sin.reshape(shape)).astype(x.dtype)


def centred_depthwise_conv(x, w):
    k = w.shape[0]
    return lax.conv_general_dilated(x, w[:, None, :], window_strides=(1,), padding=[(k // 2, k // 2)],
                                    dimension_numbers=('NWC', 'WIO', 'NWC'),
                                    feature_group_count=x.shape[-1])


def project_mixers(h, lp):
    b, t, _ = h.shape
    offsets = np.cumsum(IN_SPLIT_SIZES)[:-1].tolist()
    u = h @ lp['w_in']
    cq, ckv, krope, sq, sk, sv, gq, gk, gv, gz, gates = jnp.split(u, offsets, axis=-1)
    q = (rms_norm(cq, lp['mla_q_norm']) @ lp['mla_w_qb']).reshape(b, t, MLA_HEADS, MLA_NOPE + MLA_ROPE)
    qkv = jax.nn.silu(centred_depthwise_conv(jnp.concatenate([gq, gk, gv], axis=-1), lp['gdn_conv_w']))
    gq, gk, gv = jnp.split(qkv, [GDN_QK_DIM, 2 * GDN_QK_DIM], axis=-1)
    gates = gates.reshape(b, t, 2, 2, GDN_HEADS)
    return {
        'q_nope': q[..., :MLA_NOPE], 'q_rope': q[..., MLA_NOPE:],
        'ckv': rms_norm(ckv, lp['mla_kv_norm']), 'krope': krope,
        'sq': sq.reshape(b, t, SWA_KV_HEADS, SWA_GROUP, SWA_HD),
        'sk': sk.reshape(b, t, SWA_KV_HEADS, SWA_HD),
        'sv': sv.reshape(b, t, SWA_KV_HEADS, SWA_HD),
        'gq': l2_normalize(gq.reshape(b, t, GDN_HEADS, GDN_DK)),
        'gk': l2_normalize(gk.reshape(b, t, GDN_HEADS, GDN_DK)),
        'gv': gv.reshape(b, t, GDN_HEADS, GDN_DV),
        'gz': gz.reshape(b, t, GDN_HEADS, GDN_DV),
        'ga': gates[:, :, 0], 'gb': gates[:, :, 1],
    }


def mla_expand(ckv, w_kvb):
    b, s, _ = ckv.shape
    kv = (ckv @ w_kvb).reshape(b, s, MLA_HEADS, MLA_NOPE + MLA_V)
    return kv[..., :MLA_NOPE], kv[..., MLA_NOPE:]


def mla_attend(q_nope, q_rope, k_nope, k_rope, v):
    b, t, h, _ = q_nope.shape
    nb = t // Q_BLOCK
    scale = (MLA_NOPE + MLA_ROPE) ** -0.5

    def blocks(z):
        return jnp.swapaxes(z.reshape((b, nb, Q_BLOCK) + z.shape[2:]), 0, 1)

    def one(qs):
        qn, qr = qs
        s = (jnp.einsum('bqhd,bshd->bhqs', qn, k_nope)
             + jnp.einsum('bqhr,bsr->bhqs', qr, k_rope)).astype(jnp.float32) * scale
        p = jax.nn.softmax(s, axis=-1).astype(v.dtype)
        return jnp.einsum('bhqs,bshd->bqhd', p, v)

    o = lax.map(one, (blocks(q_nope), blocks(q_rope)))
    return jnp.swapaxes(o, 0, 1).reshape(b, t, h * MLA_V)


def gqa_sink_attend(q, k, v, sink):
    b, t, kvh, g, d = q.shape
    nb = t // Q_BLOCK
    scale = d ** -0.5
    snk = sink.reshape(kvh, g)[None, :, :, None, None].astype(jnp.float32)
    qb = jnp.swapaxes(q.reshape(b, nb, Q_BLOCK, kvh, g, d), 0, 1)

    def one(q_blk):
        s = jnp.einsum('bqkgd,bskd->bkgqs', q_blk, k).astype(jnp.float32) * scale
        s = jnp.concatenate([s, jnp.broadcast_to(snk, s.shape[:-1] + (1,))], axis=-1)
        p = jax.nn.softmax(s, axis=-1)[..., :-1].astype(v.dtype)
        return jnp.einsum('bkgqs,bskd->bqkgd', p, v)

    o = lax.map(one, qb)
    return jnp.swapaxes(o, 0, 1).reshape(b, t, kvh * g * d)


def swa_band_attend(q, k, v, k_ctx, v_ctx, sink):
    b, t, kvh, g, d = q.shape
    w = SWA_WINDOW
    nb = t // w
    scale = d ** -0.5
    qb = q.reshape(b, nb, w, kvh, g, d)

    def band(z):
        zp = jnp.pad(z, ((0, 0), (w, w), (0, 0), (0, 0))).reshape(b, nb + 2, w, kvh, d)
        return jnp.concatenate([zp[:, :-2], zp[:, 1:-1], zp[:, 2:]], axis=2)

    kb, vb = band(k), band(v)
    qi = jnp.arange(nb)[:, None] * w + jnp.arange(w)[None, :]
    kj = (jnp.arange(nb)[:, None] - 1) * w + jnp.arange(3 * w)[None, :]
    valid = ((jnp.abs(qi[:, :, None] - kj[:, None, :]) <= w)
             & (kj[:, None, :] >= 0) & (kj[:, None, :] < t))
    s_loc = jnp.einsum('bnqkgd,bnskd->bnkgqs', qb, kb).astype(jnp.float32) * scale
    s_loc = jnp.where(valid[None, :, None, None], s_loc, -jnp.inf)
    s_ctx = jnp.einsum('bnqkgd,blkd->bnkgql', qb, k_ctx).astype(jnp.float32) * scale
    snk = jnp.broadcast_to(sink.reshape(kvh, g)[None, None, :, :, None, None].astype(jnp.float32),
                           s_loc.shape[:-1] + (1,))
    p = jax.nn.softmax(jnp.concatenate([s_loc, s_ctx, snk], axis=-1), axis=-1)
    p_loc = p[..., :3 * w].astype(v.dtype)
    p_ctx = p[..., 3 * w:-1].astype(v.dtype)
    o = (jnp.einsum('bnkgqs,bnskd->bnqkgd', p_loc, vb)
         + jnp.einsum('bnkgql,blkd->bnqkgd', p_ctx, v_ctx))
    return o.reshape(b, t, kvh * g * d)


def gated_delta_chunked(q, k, v, g, beta, s0):
    b, t, h, dk = q.shape
    dv = v.shape[-1]
    c = GDN_CHUNK
    n = t // c

    def chunks(z):
        return z.astype(jnp.float32).reshape(b, n, c, h, -1).transpose(0, 3, 1, 2, 4)

    qc = chunks(q) * dk ** -0.5
    kc = chunks(k)
    vc = chunks(v)
    gc = jnp.cumsum(g.astype(jnp.float32).reshape(b, n, c, h).transpose(0, 3, 1, 2), axis=-1)
    bc = beta.astype(jnp.float32).reshape(b, n, c, h).transpose(0, 3, 1, 2)
    causal = jnp.tril(jnp.ones((c, c), dtype=bool))
    strict = jnp.tril(jnp.ones((c, c), dtype=bool), -1)
    decay = jnp.exp(jnp.where(causal, gc[..., :, None] - gc[..., None, :], -jnp.inf))
    kb = kc * bc[..., None]
    lower = jnp.where(strict, jnp.einsum('bhnid,bhnjd->bhnij', kb, kc) * decay, 0.0)
    eye = jnp.eye(c, dtype=jnp.float32)
    tinv = lax.linalg.triangular_solve(eye + lower, jnp.broadcast_to(eye, lower.shape),
                                       left_side=True, lower=True)
    u = tinv @ (vc * bc[..., None])
    wk = tinv @ (kb * jnp.exp(gc)[..., None])
    a_intra = jnp.where(causal, jnp.einsum('bhnid,bhnjd->bhnij', qc, kc) * decay, 0.0)
    q_dec = qc * jnp.exp(gc)[..., None]
    k_dec = kc * jnp.exp(gc[..., -1:] - gc)[..., None]
    g_tot = jnp.exp(gc[..., -1])

    def step(s, xs):
        w_i, u_i, qd_i, kd_i, a_i, gt_i = xs
        v_new = u_i - w_i @ s
        o_i = qd_i @ s + a_i @ v_new
        s = s * gt_i[..., None, None] + jnp.swapaxes(kd_i, -1, -2) @ v_new
        return s, o_i

    xs = tuple(jnp.moveaxis(z, 2, 0) for z in (wk, u, q_dec, k_dec, a_intra, g_tot))
    s_fin, o = lax.scan(step, s0.astype(jnp.float32), xs)
    o = o.transpose(1, 0, 3, 2, 4).reshape(b, t, h, dv)
    return o.astype(v.dtype), s_fin


def gdn_bidirectional(pr, lp, s_fwd0, s_bwd0):
    b, t = pr['gq'].shape[:2]
    g = -jnp.exp(lp['gdn_a_log'].astype(jnp.float32)) * jax.nn.softplus(
        pr['ga'].astype(jnp.float32) + lp['gdn_dt_bias'].astype(jnp.float32))
    beta = jax.nn.sigmoid(pr['gb'].astype(jnp.float32))
    q, k, v = pr['gq'], pr['gk'], pr['gv']
    o_f, s_f = gated_delta_chunked(q, k, v, g[:, :, 0], beta[:, :, 0], s_fwd0)

    def rev(z):
        return jnp.flip(z, axis=1)

    o_b, s_b = gated_delta_chunked(rev(q), rev(k), rev(v), rev(g[:, :, 1]), rev(beta[:, :, 1]), s_bwd0)
    o = rms_norm(o_f + rev(o_b), lp['gdn_norm']) * jax.nn.silu(pr['gz'])
    return o.reshape(b, t, GDN_HEADS * GDN_DV), jnp.stack([s_f, s_b], axis=1)


def mixer_context(h, lp):
    b = h.shape[0]
    pr = project_mixers(h, lp)
    k_nope, v = mla_expand(pr['ckv'], lp['mla_w_kvb'])
    o_mla = mla_attend(pr['q_nope'], pr['q_rope'], k_nope, pr['krope'], v)
    o_swa = gqa_sink_attend(pr['sq'], pr['sk'], pr['sv'], lp['swa_sink'])
    zero = jnp.zeros((b, GDN_HEADS, GDN_DK, GDN_DV), jnp.float32)
    o_gdn, st = gdn_bidirectional(pr, lp, zero, zero)
    out = jnp.concatenate([o_mla, o_swa, o_gdn], axis=-1) @ lp['w_out']
    return out, (pr['ckv'], pr['krope'], pr['sk'], pr['sv'], st)


def mixer_latent(h, lp, ctx):
    ckv_c, krope_c, sk_c, sv_c, st_c = ctx
    t = h.shape[1]
    pr = project_mixers(h, lp)
    cos_m, sin_m = axial_rope_tables(t, MLA_ROPE)
    cos_s, sin_s = axial_rope_tables(t, SWA_HD)
    kn_l, v_l = mla_expand(pr['ckv'], lp['mla_w_kvb'])
    kn_c, v_c = mla_expand(ckv_c, lp['mla_w_kvb'])
    o_mla = mla_attend(pr['q_nope'], apply_rope(pr['q_rope'], cos_m, sin_m),
                       jnp.concatenate([kn_l, kn_c], axis=1),
                       jnp.concatenate([apply_rope(pr['krope'], cos_m, sin_m), krope_c], axis=1),
                       jnp.concatenate([v_l, v_c], axis=1))
    o_swa = swa_band_attend(apply_rope(pr['sq'], cos_s, sin_s), apply_rope(pr['sk'], cos_s, sin_s),
                            pr['sv'], sk_c, sv_c, lp['swa_sink'])
    o_gdn, _ = gdn_bidirectional(pr, lp, st_c[:, 0], st_c[:, 1])
    out = jnp.concatenate([o_mla, o_swa, o_gdn], axis=-1) @ lp['w_out']
    return out, None


def macaron_layer(x, mod, lp, mix_fn):
    sh1, sc1, g1, sh2, sc2, g2, sh3, sc3, g3 = mod
    x = x + g1 * (0.5 * swiglu(modulate(rms_norm(x, lp['norm_ffn1']), sh1, sc1), lp['ffn1_w1'], lp['ffn1_w2']))
    m, aux = mix_fn(modulate(rms_norm(x, lp['norm_mix']), sh2, sc2))
    x = x + g2 * m
    x = x + g3 * (0.5 * swiglu(modulate(rms_norm(x, lp['norm_ffn2']), sh3, sc3), lp['ffn2_w1'], lp['ffn2_w2']))
    return x, aux


def setup_inputs(seed: int = 0) -> dict:
    key = jax.random.key(seed)
    ks = list(jax.random.split(key, 32))

    def nrm(i, shape, scale):
        return jax.random.normal(ks[i], shape, jnp.float32) * scale

    L = DEPTH
    dt = jax.random.uniform(ks[20], (L, 2, GDN_HEADS), jnp.float32, minval=1e-3, maxval=1e-1)
    return {
        'x_prompt': nrm(0, (BATCH, SEQ, D_MODEL), 1.0),
        'x_sample': nrm(1, (DEC_BATCH, DEC_SEQ, D_MODEL), 1.0),
        'cache_mla_ckv': nrm(2, (DEC_BATCH, DEPTH, PAST_LEN, MLA_KV_LORA), 1.0),
        'cache_mla_krope': nrm(3, (DEC_BATCH, DEPTH, PAST_LEN, MLA_ROPE), 1.0),
        'cache_swa_k': nrm(4, (DEC_BATCH, DEPTH, PAST_LEN, SWA_KV_HEADS, SWA_HD), 1.0),
        'cache_swa_v': nrm(5, (DEC_BATCH, DEPTH, PAST_LEN, SWA_KV_HEADS, SWA_HD), 1.0),
        'state_gdn': nrm(6, (DEC_BATCH, DEPTH, 2, GDN_HEADS, GDN_DK, GDN_DV), 0.1),
        'c': nrm(7, (DEC_BATCH, D_MODEL), 1.0),
        'c_ctx': nrm(8, (D_MODEL,), 1.0),
        'w_ada': nrm(9, (L, D_MODEL, N_MOD * D_MODEL), 0.5 * D_MODEL ** -0.5),
        'b_ada': nrm(10, (L, N_MOD * D_MODEL), 0.02),
        'norm_ffn1': 1.0 + nrm(11, (L, D_MODEL), 0.05),
        'ffn1_w1': nrm(12, (L, D_MODEL, 2 * D_FF), D_MODEL ** -0.5),
        'ffn1_w2': nrm(13, (L, D_FF, D_MODEL), D_FF ** -0.5),
        'norm_mix': 1.0 + nrm(14, (L, D_MODEL), 0.05),
        'w_in': nrm(15, (L, D_MODEL, IN_DIM), D_MODEL ** -0.5),
        'mla_q_norm': 1.0 + nrm(16, (L, MLA_Q_LORA), 0.05),
        'mla_w_qb': nrm(17, (L, MLA_Q_LORA, MLA_HEADS * (MLA_NOPE + MLA_ROPE)), MLA_Q_LORA ** -0.5),
        'mla_kv_norm': 1.0 + nrm(18, (L, MLA_KV_LORA), 0.05),
        'mla_w_kvb': nrm(19, (L, MLA_KV_LORA, MLA_HEADS * (MLA_NOPE + MLA_V)), MLA_KV_LORA ** -0.5),
        'swa_sink': nrm(21, (L, SWA_HEADS), 0.5),
        'gdn_conv_w': nrm(22, (L, GDN_CONV, GDN_CONV_CH), GDN_CONV ** -0.5),
        'gdn_a_log': jnp.log(jax.random.uniform(ks[23], (L, 2, GDN_HEADS), jnp.float32, minval=1.0, maxval=8.0)),
        'gdn_dt_bias': dt + jnp.log(-jnp.expm1(-dt)),
        'gdn_norm': 1.0 + nrm(24, (L, GDN_DV), 0.05),
        'w_out': nrm(25, (L, D_MIX, D_MODEL), D_MIX ** -0.5),
        'norm_ffn2': 1.0 + nrm(26, (L, D_MODEL), 0.05),
        'ffn2_w1': nrm(27, (L, D_MODEL, 2 * D_FF), D_MODEL ** -0.5),
        'ffn2_w2': nrm(28, (L, D_FF, D_MODEL), D_FF ** -0.5),
        'final_norm': 1.0 + nrm(29, (D_MODEL,), 0.05),
    }


def reference(x_prompt, x_sample, cache_mla_ckv, cache_mla_krope, cache_swa_k, cache_swa_v, state_gdn, c,
              c_ctx, w_ada, b_ada, norm_ffn1, ffn1_w1, ffn1_w2, norm_mix, w_in, mla_q_norm, mla_w_qb,
              mla_kv_norm, mla_w_kvb, swa_sink, gdn_conv_w, gdn_a_log, gdn_dt_bias, gdn_norm, w_out,
              norm_ffn2, ffn2_w1, ffn2_w2, final_norm):
    xp, xs = x_prompt, x_sample
    ctx_ckv, ctx_krope, ctx_sk, ctx_sv, ctx_st = [], [], [], [], []
    for l in range(DEPTH):
        lp = {
            'norm_ffn1': norm_ffn1[l], 'ffn1_w1': ffn1_w1[l], 'ffn1_w2': ffn1_w2[l],
            'norm_mix': norm_mix[l], 'w_in': w_in[l],
            'mla_q_norm': mla_q_norm[l], 'mla_w_qb': mla_w_qb[l],
            'mla_kv_norm': mla_kv_norm[l], 'mla_w_kvb': mla_w_kvb[l],
            'swa_sink': swa_sink[l], 'gdn_conv_w': gdn_conv_w[l],
            'gdn_a_log': gdn_a_log[l], 'gdn_dt_bias': gdn_dt_bias[l], 'gdn_norm': gdn_norm[l],
            'w_out': w_out[l], 'norm_ffn2': norm_ffn2[l], 'ffn2_w1': ffn2_w1[l], 'ffn2_w2': ffn2_w2[l],
        }
        mod_ctx = adaln_params(c_ctx[None, :], w_ada[l], b_ada[l])
        mod_lat = adaln_params(c, w_ada[l], b_ada[l])
        xp, (ckv_l, krope_l, sk_l, sv_l, st_l) = macaron_layer(
            xp, mod_ctx, lp, functools.partial(mixer_context, lp=lp))
        ctx_ckv.append(ckv_l)
        ctx_krope.append(krope_l)
        ctx_sk.append(sk_l)
        ctx_sv.append(sv_l)
        ctx_st.append(st_l)
        cache_l = (cache_mla_ckv[:, l], cache_mla_krope[:, l], cache_swa_k[:, l], cache_swa_v[:, l],
                   state_gdn[:, l])
        xs, _ = macaron_layer(xs, mod_lat, lp, functools.partial(mixer_latent, lp=lp, ctx=cache_l))
    y_prompt = rms_norm(xp, final_norm)
    y_sample = rms_norm(xs, final_norm)
    new_mla_ckv = jnp.stack(ctx_ckv, axis=1)
    new_mla_krope = jnp.stack(ctx_krope, axis=1)
    new_swa_k = jnp.stack(ctx_sk, axis=1)
    new_swa_v = jnp.stack(ctx_sv, axis=1)
    new_state_gdn = jnp.stack(ctx_st, axis=1).astype(x_prompt.dtype)
    return (y_prompt, y_sample, new_mla_ckv, new_mla_krope, new_swa_k, new_swa_v, new_state_gdn)
```

```python
import functools

import numpy as np
import jax
import jax.numpy as jnp
from jax import lax
from jax.experimental import pallas as pl
from jax.experimental.pallas import tpu as pltpu

D_MODEL = 1024
BATCH = 16
SEQ = 256
DEPTH = 2
DEC_BATCH = 2
DEC_SEQ = 2048
PAST_LEN = 512
GRID_W = 64
ROPE_BASE = 10000.0
NORM_EPS = 1e-6
N_MOD = 9
D_FF = 2816
MLA_HEADS = 8
MLA_Q_LORA = 384
MLA_KV_LORA = 256
MLA_NOPE = 64
MLA_ROPE = 32
MLA_V = 64
SWA_HEADS = 8
SWA_KV_HEADS = 2
SWA_GROUP = SWA_HEADS // SWA_KV_HEADS
SWA_HD = 64
SWA_WINDOW = 128
GDN_HEADS = 4
GDN_DK = 128
GDN_DV = 128
GDN_CONV = 5
GDN_CHUNK = 64
GDN_QK_DIM = GDN_HEADS * GDN_DK
GDN_V_DIM = GDN_HEADS * GDN_DV
GDN_CONV_CH = 2 * GDN_QK_DIM + GDN_V_DIM

N_CTX = BATCH * SEQ
N_LAT = DEC_BATCH * DEC_SEQ
N_TOK = N_CTX + N_LAT
N_GROUPS = 1 + DEC_BATCH
COND_ROWS = 8

LANES = 128
VMEM_LIMIT_BYTES = 56 * 1024 * 1024

BF = jnp.bfloat16
F32 = jnp.float32

_C_CQ = 0
_C_CKV = _C_CQ + MLA_Q_LORA
_C_SQ = _C_CKV + MLA_KV_LORA
_C_SQR = _C_SQ + SWA_HEADS * SWA_HD
_C_SK = _C_SQR + SWA_HEADS * SWA_HD
_C_SKR = _C_SK + SWA_KV_HEADS * SWA_HD
_C_SV = _C_SKR + SWA_KV_HEADS * SWA_HD
_C_G3 = _C_SV + SWA_KV_HEADS * SWA_HD
_C_GZ = _C_G3 + GDN_CONV_CH
_C_MA = _C_GZ + GDN_V_DIM
_C_MB = _C_MA + LANES
_C_END = _C_MB + LANES
_M_G = MLA_ROPE
_M_B = MLA_ROPE + 2 * GDN_HEADS
_M_END = MLA_ROPE + 4 * GDN_HEADS
_T_CS, _T_SS, _T_CA, _T_SA, _T_CQ, _T_SQ = (i * LANES for i in range(6))
_T_END = 6 * LANES


def _params(*sem):
    return pltpu.CompilerParams(dimension_semantics=sem, vmem_limit_bytes=VMEM_LIMIT_BYTES)


def _bdot(a, b):
    return jnp.dot(a.astype(BF), b.astype(BF), preferred_element_type=F32)


def _bdot_nt(a, b):
    return lax.dot_general(a.astype(BF), b.astype(BF), (((1,), (1,)), ((), ())),
                           preferred_element_type=F32)


def _bdot_tn(a, b):
    return lax.dot_general(a.astype(BF), b.astype(BF), (((0,), (0,)), ((), ())),
                           preferred_element_type=F32)


def _split2(a):
    hi = a.astype(BF)
    lo = (a - hi.astype(F32)).astype(BF)
    return hi, lo


def _dot_3pass(a, b):
    ah, al = _split2(a)
    bh, bl = _split2(b)
    return (jnp.dot(ah, bh, preferred_element_type=F32)
            + (jnp.dot(ah, bl, preferred_element_type=F32)
               + jnp.dot(al, bh, preferred_element_type=F32)))


def _silu(x):
    return x / (1.0 + jnp.exp(-x))


def _rms(x, gain, eps=NORM_EPS):
    return x * lax.rsqrt(jnp.mean(x * x, axis=-1, keepdims=True) + eps) * gain


def _group_of_row(r):
    return jnp.where(r < N_CTX, 0, 1 + (r - N_CTX) // DEC_SEQ)


def _adaln_kernel(c_ref, w_ref, b_ref, o_ref):
    o_ref[0] = _bdot(_silu(c_ref[...]), w_ref[0]) + b_ref[0]


def _adaln(cond, w_ada, b_ada, tn=1536):
    n = N_MOD * D_MODEL
    return pl.pallas_call(
        _adaln_kernel,
        grid=(DEPTH, n // tn),
        in_specs=[pl.BlockSpec((COND_ROWS, D_MODEL), lambda l, j: (0, 0)),
                  pl.BlockSpec((1, D_MODEL, tn), lambda l, j: (l, 0, j)),
                  pl.BlockSpec((1, 1, tn), lambda l, j: (l, 0, j))],
        out_specs=pl.BlockSpec((1, COND_ROWS, tn), lambda l, j: (l, 0, j)),
        out_shape=jax.ShapeDtypeStruct((DEPTH, COND_ROWS, n), F32),
        compiler_params=_params("parallel", "parallel"),
        name="adaln",
    )(cond, w_ada, b_ada.reshape(DEPTH, 1, n))


def _ffn_kernel(x_ref, mod_ref, gain_ref, w1_ref, w2_ref, o_ref):
    x = x_ref[...]
    mod = mod_ref[0]
    shift, scale, gate = mod[:, :D_MODEL], mod[:, D_MODEL:2 * D_MODEL], mod[:, 2 * D_MODEL:]
    h = _rms(x, gain_ref[...]) * (1.0 + scale) + shift
    gu = jnp.dot(h.astype(BF), w1_ref[...], preferred_element_type=F32)
    a = _silu(gu[:, :D_FF]) * gu[:, D_FF:]
    y = jnp.dot(a.astype(BF), w2_ref[...], preferred_element_type=F32)
    o_ref[...] = x + gate * (0.5 * y)


def _ffn(x, mods, which, gain, w1, w2, tm=256):
    n = x.shape[0]
    return pl.pallas_call(
        _ffn_kernel,
        grid=(n // tm,),
        in_specs=[pl.BlockSpec((tm, D_MODEL), lambda i: (i, 0)),
                  pl.BlockSpec((1, 1, 3 * D_MODEL), lambda i: (_group_of_row(i * tm), 0, which)),
                  pl.BlockSpec((1, D_MODEL), lambda i: (0, 0)),
                  pl.BlockSpec((D_MODEL, 2 * D_FF), lambda i: (0, 0)),
                  pl.BlockSpec((D_FF, D_MODEL), lambda i: (0, 0))],
        out_specs=pl.BlockSpec((tm, D_MODEL), lambda i: (i, 0)),
        out_shape=jax.ShapeDtypeStruct((n, D_MODEL), F32),
        compiler_params=_params("parallel"),
        name="ffn",
    )(x, mods, gain.reshape(1, D_MODEL), w1, w2)


def _kv_expand(ckv_n, misc, wk_ref, wv_ref):
    kin = jnp.concatenate([ckv_n, misc], axis=1).astype(BF)
    kk = jnp.dot(kin, wk_ref[...], preferred_element_type=F32)
    vv = jnp.dot(ckv_n.astype(BF), wv_ref[...], preferred_element_type=F32)
    return kk.astype(BF), vv.astype(BF)


def _proj_kernel(x_ref, mod_ref, gain_ref, win_ref, qg_ref, wqb_ref, kvg_ref, wk_ref, wv_ref, tab_ref,
                 alog_ref, dtb_ref,
                 q_ref, ckv_ref, misc_ref, kmla_ref, vmla_ref, sq_ref, sk_ref, sv_ref, g3_ref, gz_ref):
    x = x_ref[...]
    mod = mod_ref[0]
    shift, scale = mod[:, :D_MODEL], mod[:, D_MODEL:2 * D_MODEL]
    h = _rms(x, gain_ref[...]) * (1.0 + scale) + shift
    u = jnp.dot(h.astype(BF), win_ref[...], preferred_element_type=F32)
    tab = tab_ref[...]

    qn = _rms(u[:, _C_CQ:_C_CKV], qg_ref[...])
    q2 = jnp.dot(qn.astype(BF), wqb_ref[...], preferred_element_type=F32)
    nq = MLA_HEADS * LANES
    cosq = jnp.concatenate([tab[:, _T_CQ:_T_CQ + LANES]] * MLA_HEADS, axis=1)
    sinq = jnp.concatenate([tab[:, _T_SQ:_T_SQ + LANES]] * MLA_HEADS, axis=1)
    q_ref[...] = (q2[:, :nq] * cosq + q2[:, nq:] * sinq).astype(BF)

    ckv_n = _rms(u[:, _C_CKV:_C_SQ], kvg_ref[...])
    ckv_ref[...] = ckv_n

    m = (u[:, _C_MA:_C_MB] * tab[:, _T_CA:_T_CA + LANES] + u[:, _C_MB:_C_END] * tab[:, _T_SA:_T_SA + LANES])
    lane = lax.broadcasted_iota(jnp.int32, m.shape, 1)
    z = m + dtb_ref[...]
    softplus = jnp.maximum(z, 0.0) + jnp.log(1.0 + jnp.exp(-jnp.abs(z)))
    decay = -jnp.exp(alog_ref[...]) * softplus
    strength = 1.0 / (1.0 + jnp.exp(-m))
    misc = jnp.where((lane >= _M_G) & (lane < _M_B), decay,
                     jnp.where((lane >= _M_B) & (lane < _M_END), strength, m))
    misc_ref[...] = misc

    kk, vv = _kv_expand(ckv_n, misc, wk_ref, wv_ref)
    kmla_ref[...] = kk
    vmla_ref[...] = vv

    n_sq = SWA_HEADS * SWA_HD
    cos_s = tab[:, _T_CS:_T_CS + LANES]
    sin_s = tab[:, _T_SS:_T_SS + LANES]
    cos4 = jnp.concatenate([cos_s] * (n_sq // LANES), axis=1)
    sin4 = jnp.concatenate([sin_s] * (n_sq // LANES), axis=1)
    sq = u[:, _C_SQ:_C_SQR] * cos4 + u[:, _C_SQR:_C_SK] * sin4
    sq_ref[...] = (sq * (SWA_HD ** -0.5)).astype(BF)
    sk_ref[...] = u[:, _C_SK:_C_SKR] * cos_s + u[:, _C_SKR:_C_SV] * sin_s
    sv_ref[...] = u[:, _C_SV:_C_G3]
    g3_ref[...] = u[:, _C_G3:_C_GZ]
    gz_ref[...] = u[:, _C_GZ:_C_MA]


def _proj(x, mods, gain, win, qg, wqb, kvg, wk, wv, tab, alog, dtb, tm=256):
    n = x.shape[0]
    const = lambda i: (0, 0)
    n_ctx_tiles = N_CTX // tm
    lat_tiles = DEC_SEQ // tm

    def tab_map(i):
        return (jnp.where(i < n_ctx_tiles, 0, 1 + (i - n_ctx_tiles) % lat_tiles), 0)

    widths = [(MLA_HEADS * LANES, BF), (MLA_KV_LORA, F32), (LANES, F32), (MLA_HEADS * LANES, BF),
              (MLA_HEADS * MLA_V, BF), (SWA_HEADS * SWA_HD, BF), (SWA_KV_HEADS * SWA_HD, F32),
              (SWA_KV_HEADS * SWA_HD, F32), (GDN_CONV_CH, F32), (GDN_V_DIM, F32)]
    return pl.pallas_call(
        _proj_kernel,
        grid=(n // tm,),
        in_specs=[pl.BlockSpec((tm, D_MODEL), lambda i: (i, 0)),
                  pl.BlockSpec((1, 1, 3 * D_MODEL), lambda i: (_group_of_row(i * tm), 0, 1)),
                  pl.BlockSpec((1, D_MODEL), const),
                  pl.BlockSpec(win.shape, const),
                  pl.BlockSpec((1, MLA_Q_LORA), const),
                  pl.BlockSpec(wqb.shape, const),
                  pl.BlockSpec((1, MLA_KV_LORA), const),
                  pl.BlockSpec(wk.shape, const),
                  pl.BlockSpec(wv.shape, const),
                  pl.BlockSpec((tm, _T_END), tab_map),
                  pl.BlockSpec((1, LANES), const),
                  pl.BlockSpec((1, LANES), const)],
        out_specs=[pl.BlockSpec((tm, w), lambda i: (i, 0)) for w, _ in widths],
        out_shape=[jax.ShapeDtypeStruct((n, w), dt) for w, dt in widths],
        compiler_params=_params("parallel"),
        name="proj",
    )(x, mods, gain.reshape(1, D_MODEL), win, qg.reshape(1, -1), wqb, kvg.reshape(1, -1), wk, wv, tab,
      alog, dtb)


def _kv_cache_kernel(ckv_ref, misc_ref, wk_ref, wv_ref, k_ref, v_ref):
    kk, vv = _kv_expand(ckv_ref[...], misc_ref[...], wk_ref, wv_ref)
    k_ref[...] = kk
    v_ref[...] = vv


def _kv_cache(ckv, misc, wk, wv, tm=PAST_LEN):
    n = ckv.shape[0]
    const = lambda i: (0, 0)
    return pl.pallas_call(
        _kv_cache_kernel,
        grid=(n // tm,),
        in_specs=[pl.BlockSpec((tm, MLA_KV_LORA), lambda i: (i, 0)),
                  pl.BlockSpec((tm, LANES), lambda i: (i, 0)),
                  pl.BlockSpec(wk.shape, const),
                  pl.BlockSpec(wv.shape, const)],
        out_specs=[pl.BlockSpec((tm, MLA_HEADS * LANES), lambda i: (i, 0)),
                   pl.BlockSpec((tm, MLA_HEADS * MLA_V), lambda i: (i, 0))],
        out_shape=[jax.ShapeDtypeStruct((n, MLA_HEADS * LANES), BF),
                   jax.ShapeDtypeStruct((n, MLA_HEADS * MLA_V), BF)],
        compiler_params=_params("parallel"),
        name="kv_cache",
    )(ckv, misc, wk, wv)


def _mla_kernel(*refs, n_seg):
    q_ref = refs[0]
    k_refs = refs[1:1 + n_seg]
    v_refs = refs[1 + n_seg:1 + 2 * n_seg]
    o_ref = refs[1 + 2 * n_seg]
    scale = (MLA_NOPE + MLA_ROPE) ** -0.5
    tq = q_ref.shape[0]
    lane = lax.broadcasted_iota(jnp.int32, (tq, LANES), 1)
    for pair in range(MLA_HEADS // 2):
        outs = []
        for h in (2 * pair, 2 * pair + 1):
            qh = q_ref[:, h * LANES:(h + 1) * LANES]
            ss = [_bdot_nt(qh, k[:, h * LANES:(h + 1) * LANES]) * scale for k in k_refs]
            mx = functools.reduce(jnp.maximum, [jnp.max(s, axis=-1, keepdims=True) for s in ss])
            ps = [jnp.exp(s - mx) for s in ss]
            den = functools.reduce(jnp.add, [jnp.sum(p, axis=-1, keepdims=True) for p in ps])
            acc = functools.reduce(jnp.add, [_bdot(p, v[:, pair * LANES:(pair + 1) * LANES])
                                             for p, v in zip(ps, v_refs)])
            outs.append(acc / den)
        o_ref[:, pair * LANES:(pair + 1) * LANES] = jnp.where(lane < MLA_V, outs[0], outs[1]).astype(BF)


def _mla_attend(q, ks, vs, n_batch, t, row0, k_blocks, tq=256):
    n_seg = len(ks)
    qb0 = row0 // tq
    tiles = t // tq
    in_specs = [pl.BlockSpec((tq, MLA_HEADS * LANES), lambda b, i: (qb0 + b * tiles + i, 0))]
    for (b0, s), width in [(kb, MLA_HEADS * LANES) for kb in k_blocks] + [(kb, MLA_HEADS * MLA_V) for kb in k_blocks]:
        in_specs.append(pl.BlockSpec((s, width), functools.partial(lambda b, i, b0: (b0 + b, 0), b0=b0)))
    return pl.pallas_call(
        functools.partial(_mla_kernel, n_seg=n_seg),
        grid=(n_batch, tiles),
        in_specs=in_specs,
        out_specs=pl.BlockSpec((tq, MLA_HEADS * MLA_V), lambda b, i: (b * tiles + i, 0)),
        out_shape=jax.ShapeDtypeStruct((n_batch * t, MLA_HEADS * MLA_V), BF),
        compiler_params=_params("parallel", "parallel"),
        name="mla_attend",
    )(q, *ks, *vs)


def _gqa_heads(q_ref, k_segs, v_segs, masks, sink_ref, o_ref):
    tq = q_ref.shape[0]
    lane_q = lax.broadcasted_iota(jnp.int32, (tq, LANES), 1)
    lo_q = lane_q < SWA_HD
    k_roll = [pltpu.roll(k, SWA_HD, 1) for k in k_segs]
    v_roll = [pltpu.roll(v, SWA_HD, 1) for v in v_segs]
    for kvh in range(SWA_KV_HEADS):
        tiles = [q_ref[:, (kvh * 2 + j) * LANES:(kvh * 2 + j + 1) * LANES] for j in range(2)]
        zero = jnp.zeros_like(tiles[0])
        qs = jnp.concatenate([jnp.where(lo_q, tiles[0], zero), jnp.where(lo_q, zero, tiles[0]),
                              jnp.where(lo_q, tiles[1], zero), jnp.where(lo_q, zero, tiles[1])], axis=0)
        sink = jnp.concatenate(
            [jnp.broadcast_to(sink_ref[kvh * SWA_GROUP + g:kvh * SWA_GROUP + g + 1, 0:1], (tq, 1))
             for g in range(SWA_GROUP)], axis=0)
        mx = sink
        ss = []
        for k, kr, msk in zip(k_segs, k_roll, masks):
            lane_k = lax.broadcasted_iota(jnp.int32, k.shape, 1)
            first = (lane_k < SWA_HD) == (kvh == 0)
            kd = jnp.where(first, k, kr)
            s = _bdot_nt(qs, kd)
            if msk is not None:
                s = jnp.where(msk, s, -1e30)
            ss.append(s)
            mx = jnp.maximum(mx, jnp.max(s, axis=-1, keepdims=True))
        den = jnp.exp(sink - mx)
        acc = None
        for s, v, vr in zip(ss, v_segs, v_roll):
            lane_v = lax.broadcasted_iota(jnp.int32, v.shape, 1)
            first = (lane_v < SWA_HD) == (kvh == 0)
            vd = jnp.where(first, v, vr)
            p = jnp.exp(s - mx)
            den = den + jnp.sum(p, axis=-1, keepdims=True)
            pv = _bdot(p, vd)
            acc = pv if acc is None else acc + pv
        o = acc / den
        for j in range(2):
            o_ref[:, (kvh * 2 + j) * LANES:(kvh * 2 + j + 1) * LANES] = jnp.where(
                lo_q, o[(2 * j) * tq:(2 * j + 1) * tq], o[(2 * j + 1) * tq:(2 * j + 2) * tq]).astype(BF)


def _swa_ctx_kernel(q_ref, k_ref, v_ref, sink_ref, o_ref):
    _gqa_heads(q_ref, [k_ref[...]], [v_ref[...]], [None], sink_ref, o_ref)


def _swa_ctx(sq, sk, sv, sink_rows):
    return pl.pallas_call(
        _swa_ctx_kernel,
        grid=(BATCH,),
        in_specs=[pl.BlockSpec((SEQ, SWA_HEADS * SWA_HD), lambda b: (b, 0)),
                  pl.BlockSpec((SEQ, LANES), lambda b: (b, 0)),
                  pl.BlockSpec((SEQ, LANES), lambda b: (b, 0)),
                  pl.BlockSpec((SWA_HEADS, LANES), lambda b: (0, 0))],
        out_specs=pl.BlockSpec((SEQ, SWA_HEADS * SWA_HD), lambda b: (b, 0)),
        out_shape=jax.ShapeDtypeStruct((N_CTX, SWA_HEADS * SWA_HD), BF),
        compiler_params=_params("parallel"),
        name="swa_ctx",
    )(sq, sk, sv, sink_rows)


def _swa_lat_kernel(q_ref, kp_ref, kc_ref, kn_ref, vp_ref, vc_ref, vn_ref, kx_ref, vx_ref, sink_ref, o_ref):
    w = SWA_WINDOW
    n = pl.program_id(1)
    nb = pl.num_programs(1)
    k_band = jnp.concatenate([kp_ref[...], kc_ref[...], kn_ref[...]], axis=0)
    v_band = jnp.concatenate([vp_ref[...], vc_ref[...], vn_ref[...]], axis=0)
    rows = SWA_GROUP * w
    r = lax.broadcasted_iota(jnp.int32, (rows, 3 * w), 0) & (w - 1)
    c = lax.broadcasted_iota(jnp.int32, (rows, 3 * w), 1)
    valid = (c >= r) & (c <= r + 2 * w) & ((c >= w) | (n > 0)) & ((c < 2 * w) | (n < nb - 1))
    _gqa_heads(q_ref, [k_band, kx_ref[...]], [v_band, vx_ref[...]], [valid, None], sink_ref, o_ref)


def _swa_lat(sq, sk, sv, k_ctx, v_ctx, sink_rows):
    w = SWA_WINDOW
    nb = DEC_SEQ // w
    q0 = N_CTX // w

    def blk(d):
        return lambda b, n: (q0 + b * nb + jnp.clip(n + d, 0, nb - 1), 0)

    kv_specs = [pl.BlockSpec((w, LANES), blk(d)) for d in (-1, 0, 1)]
    return pl.pallas_call(
        _swa_lat_kernel,
        grid=(DEC_BATCH, nb),
        in_specs=[pl.BlockSpec((w, SWA_HEADS * SWA_HD), blk(0))] + kv_specs + kv_specs
        + [pl.BlockSpec((PAST_LEN, LANES), lambda b, n: (b, 0)),
           pl.BlockSpec((PAST_LEN, LANES), lambda b, n: (b, 0)),
           pl.BlockSpec((SWA_HEADS, LANES), lambda b, n: (0, 0))],
        out_specs=pl.BlockSpec((w, SWA_HEADS * SWA_HD), lambda b, n: (b * nb + n, 0)),
        out_shape=jax.ShapeDtypeStruct((N_LAT, SWA_HEADS * SWA_HD), BF),
        compiler_params=_params("parallel", "parallel"),
        name="swa_lat",
    )(sq, sk, sk, sk, sv, sv, sv, k_ctx, v_ctx, sink_rows)


def _gdn_conv_kernel(x_ref, w_ref, o_ref):
    x = x_ref[...]
    t = x.shape[0]
    w = w_ref[...]
    row = lax.broadcasted_iota(jnp.int32, x.shape, 0)
    half = GDN_CONV // 2
    acc = x * w[half:half + 1, :]
    for k in range(GDN_CONV):
        d = k - half
        if d == 0:
            continue
        xs = pltpu.roll(x, (-d) % t, 0)
        ok = (row + d >= 0) & (row + d < t)
        acc = acc + jnp.where(ok, xs, 0.0) * w[k:k + 1, :]
    y = _silu(acc)
    is_qk = pl.program_id(1) < 2
    cols = []
    for h in range(x.shape[1] // LANES):
        yh = y[:, h * LANES:(h + 1) * LANES]
        nrm = lax.rsqrt(jnp.sum(yh * yh, axis=-1, keepdims=True) + 1e-6)
        cols.append(yh * jnp.where(is_qk, nrm, 1.0))
    o_ref[...] = jnp.concatenate(cols, axis=1)


def _gdn_conv(g3, conv_w, row0, n_seq, t):
    blk0 = row0 // t
    return pl.pallas_call(
        _gdn_conv_kernel,
        grid=(n_seq, 3),
        in_specs=[pl.BlockSpec((t, GDN_QK_DIM), lambda s, j: (blk0 + s, j)),
                  pl.BlockSpec((GDN_CONV, GDN_QK_DIM), lambda s, j: (0, j))],
        out_specs=pl.BlockSpec((t, GDN_QK_DIM), lambda s, j: (s, j)),
        out_shape=jax.ShapeDtypeStruct((n_seq * t, GDN_CONV_CH), F32),
        compiler_params=_params("parallel", "parallel"),
        name="gdn_conv",
    )(g3, conv_w)


def _unit_triangular_inverse(m):
    c = m.shape[0]
    eye = (lax.broadcasted_iota(jnp.int32, (c, c), 0) == lax.broadcasted_iota(jnp.int32, (c, c), 1)).astype(F32)
    x = eye + m
    p = m
    steps = int(np.log2(c)) - 1
    for _ in range(steps):
        p = _dot_3pass(p, p)
        x = x + _dot_3pass(x, p)
    return x


def _gdn_chunk(qkv, g_col, g_row, beta_col, state, head, backward):
    c = GDN_CHUNK
    q = qkv[:, head * GDN_DK:(head + 1) * GDN_DK] * (GDN_DK ** -0.5)
    k = qkv[:, GDN_QK_DIM + head * GDN_DK:GDN_QK_DIM + (head + 1) * GDN_DK]
    v = qkv[:, 2 * GDN_QK_DIM + head * GDN_DV:2 * GDN_QK_DIM + (head + 1) * GDN_DV]
    ri = lax.broadcasted_iota(jnp.int32, (c, c), 0)
    ci = lax.broadcasted_iota(jnp.int32, (c, c), 1)
    incl = (ri <= ci) if backward else (ri >= ci)
    strict = (ri < ci) if backward else (ri > ci)
    incl_t = (ri >= ci) if backward else (ri <= ci)
    gc_col = jnp.sum(jnp.where(incl, g_row, 0.0), axis=1, keepdims=True)
    gc_row = jnp.sum(jnp.where(incl_t, g_col, 0.0), axis=0, keepdims=True)
    decay = jnp.where(incl, jnp.exp(gc_col - gc_row), 0.0)
    kb = k * beta_col
    lower = jnp.where(strict, _bdot_nt(kb, k) * decay, 0.0)
    tinv = _unit_triangular_inverse(-lower)
    e_col = jnp.exp(gc_col)
    uw = _bdot(tinv, jnp.concatenate([v * beta_col, kb * e_col], axis=1))
    u, wk = uw[:, :GDN_DV], uw[:, GDN_DV:]
    a = jnp.where(incl, _bdot_nt(q, k) * decay, 0.0)
    g_last = gc_col[0:1, :] if backward else gc_col[c - 1:c, :]
    wq = _bdot(jnp.concatenate([wk, q * e_col], axis=0), state)
    v_new = u - wq[:c]
    o = wq[c:] + _bdot(a, v_new)
    new_state = state * jnp.exp(g_last) + _bdot_tn(k * jnp.exp(g_last - gc_col), v_new)
    return o, new_state


def _gdn_kernel(*refs, zero_init):
    if zero_init:
        (qf_ref, qb_ref, gcf_ref, gcb_ref, grf_ref, grb_ref, bf_ref, bb_ref,
         of_ref, ob_ref, sfin_ref, st_ref) = refs
        s0_ref = None
    else:
        (qf_ref, qb_ref, gcf_ref, gcb_ref, grf_ref, grb_ref, bf_ref, bb_ref, s0_ref,
         of_ref, ob_ref, sfin_ref, st_ref) = refs
    n = pl.program_id(1)

    @pl.when(n == 0)
    def _():
        if zero_init:
            st_ref[...] = jnp.zeros_like(st_ref)
        else:
            st_ref[...] = s0_ref[0]

    for d, (q_ref, gc_ref, gr_ref, b_ref, o_ref) in enumerate(
            [(qf_ref, gcf_ref, grf_ref, bf_ref, of_ref), (qb_ref, gcb_ref, grb_ref, bb_ref, ob_ref)]):
        qkv = q_ref[...]
        for h in range(GDN_HEADS):
            i = d * GDN_HEADS + h
            o, s_new = _gdn_chunk(qkv, gc_ref[:, i:i + 1], gr_ref[0, i:i + 1, :], b_ref[:, i:i + 1],
                                  st_ref[i], h, backward=(d == 1))
            o_ref[:, h * GDN_DV:(h + 1) * GDN_DV] = o
            st_ref[i] = s_new

    @pl.when(n == pl.num_programs(1) - 1)
    def _():
        sfin_ref[0] = st_ref[...]


def _gdn(qkv, g_col, g_row, beta_col, s0, n_seq, t):
    c = GDN_CHUNK
    nc = t // c
    nh = 2 * GDN_HEADS
    fwd = lambda s, n: (s * nc + n, 0)
    bwd = lambda s, n: (s * nc + nc - 1 - n, 0)
    fwd3 = lambda s, n: (s * nc + n, 0, 0)
    bwd3 = lambda s, n: (s * nc + nc - 1 - n, 0, 0)
    in_specs = [pl.BlockSpec((c, GDN_CONV_CH), fwd), pl.BlockSpec((c, GDN_CONV_CH), bwd),
                pl.BlockSpec((c, nh), fwd), pl.BlockSpec((c, nh), bwd),
                pl.BlockSpec((1, nh, c), fwd3), pl.BlockSpec((1, nh, c), bwd3),
                pl.BlockSpec((c, nh), fwd), pl.BlockSpec((c, nh), bwd)]
    args = [qkv, qkv, g_col, g_col, g_row, g_row, beta_col, beta_col]
    if s0 is not None:
        in_specs.append(pl.BlockSpec((1, nh, GDN_DK, GDN_DV), lambda s, n: (s, 0, 0, 0)))
        args.append(s0)
    return pl.pallas_call(
        functools.partial(_gdn_kernel, zero_init=s0 is None),
        grid=(n_seq, nc),
        in_specs=in_specs,
        out_specs=[pl.BlockSpec((c, GDN_V_DIM), fwd), pl.BlockSpec((c, GDN_V_DIM), bwd),
                   pl.BlockSpec((1, nh, GDN_DK, GDN_DV), lambda s, n: (s, 0, 0, 0))],
        out_shape=[jax.ShapeDtypeStruct((n_seq * t, GDN_V_DIM), F32),
                   jax.ShapeDtypeStruct((n_seq * t, GDN_V_DIM), F32),
                   jax.ShapeDtypeStruct((n_seq, nh, GDN_DK, GDN_DV), F32)],
        scratch_shapes=[pltpu.VMEM((nh, GDN_DK, GDN_DV), F32)],
        compiler_params=_params("parallel", "arbitrary"),
        name="gdn",
    )(*args)


def _out_kernel(x_ref, mod_ref, om_ref, os_ref, gf_ref, gb_ref, gz_ref, gn_ref, w_ref, o_ref):
    gate = mod_ref[0][:, 2 * D_MODEL:]
    s = gf_ref[...] + gb_ref[...]
    gz = gz_ref[...]
    cols = []
    for h in range(GDN_HEADS):
        sh = s[:, h * GDN_DV:(h + 1) * GDN_DV]
        cols.append(_rms(sh, gn_ref[...]) * _silu(gz[:, h * GDN_DV:(h + 1) * GDN_DV]))
    og = jnp.concatenate(cols, axis=1)
    n_m = MLA_HEADS * MLA_V
    n_s = SWA_HEADS * SWA_HD
    y = (jnp.dot(om_ref[...], w_ref[:n_m, :], preferred_element_type=F32)
         + jnp.dot(os_ref[...], w_ref[n_m:n_m + n_s, :], preferred_element_type=F32)
         + jnp.dot(og.astype(BF), w_ref[n_m + n_s:, :], preferred_element_type=F32))
    o_ref[...] = x_ref[...] + gate * y


def _out_proj(x, mods, o_mla, o_swa, o_gf, o_gb, gz, gdn_norm, w_out, tm=512):
    n = x.shape[0]
    row = lambda i: (i, 0)
    const = lambda i: (0, 0)
    return pl.pallas_call(
        _out_kernel,
        grid=(n // tm,),
        in_specs=[pl.BlockSpec((tm, D_MODEL), row),
                  pl.BlockSpec((1, 1, 3 * D_MODEL), lambda i: (_group_of_row(i * tm), 0, 1)),
                  pl.BlockSpec((tm, MLA_HEADS * MLA_V), row),
                  pl.BlockSpec((tm, SWA_HEADS * SWA_HD), row),
                  pl.BlockSpec((tm, GDN_V_DIM), row),
                  pl.BlockSpec((tm, GDN_V_DIM), row),
                  pl.BlockSpec((tm, GDN_V_DIM), row),
                  pl.BlockSpec((1, GDN_DV), const),
                  pl.BlockSpec(w_out.shape, const)],
        out_specs=pl.BlockSpec((tm, D_MODEL), row),
        out_shape=jax.ShapeDtypeStruct((n, D_MODEL), F32),
        compiler_params=_params("parallel"),
        name="out_proj",
    )(x, mods, o_mla, o_swa, o_gf, o_gb, gz, gdn_norm.reshape(1, GDN_DV), w_out)


def _final_norm_kernel(x_ref, g_ref, o_ref):
    o_ref[...] = _rms(x_ref[...], g_ref[...])


def _final_norm(x, gain, row0, n, tm=512):
    b0 = row0 // tm
    return pl.pallas_call(
        _final_norm_kernel,
        grid=(n // tm,),
        in_specs=[pl.BlockSpec((tm, D_MODEL), lambda i: (b0 + i, 0)),
                  pl.BlockSpec((1, D_MODEL), lambda i: (0, 0))],
        out_specs=pl.BlockSpec((tm, D_MODEL), lambda i: (i, 0)),
        out_shape=jax.ShapeDtypeStruct((n, D_MODEL), F32),
        compiler_params=_params("parallel"),
        name="final_norm",
    )(x, gain.reshape(1, D_MODEL))


def _rot_columns(dim, n_heads):
    quarter = dim // 4
    j = np.arange(dim)
    even = (j // quarter) % 2 == 0
    src = np.where(even, j + quarter, j - quarter)
    sign = np.where(even, -1.0, 1.0).astype(np.float32)
    idx = (np.arange(n_heads)[:, None] * dim + src[None, :]).reshape(-1)
    return idx, np.tile(sign, n_heads)


def _axial_rope(n_tokens, dim):
    rows = n_tokens // GRID_W
    row = jnp.repeat(jnp.arange(rows, dtype=F32), GRID_W)
    col = jnp.tile(jnp.arange(GRID_W, dtype=F32), rows)
    axis_dim = dim // 2
    inv_freq = 1.0 / (ROPE_BASE ** (jnp.arange(0, axis_dim, 2, dtype=F32) / axis_dim))
    ang_r = row[:, None] * inv_freq[None, :]
    ang_c = col[:, None] * inv_freq[None, :]
    ang = jnp.concatenate([ang_r, ang_r, ang_c, ang_c], axis=-1)
    return jnp.cos(ang), jnp.sin(ang)


def _rope_table(tm):
    cos_m, sin_m = _axial_rope(DEC_SEQ, MLA_ROPE)
    cos_s, sin_s = _axial_rope(DEC_SEQ, SWA_HD)
    t = DEC_SEQ
    one = lambda w: jnp.ones((t, w), F32)
    zero = lambda w: jnp.zeros((t, w), F32)
    lat = jnp.concatenate([
        cos_s, cos_s, sin_s, sin_s,
        cos_m, one(LANES - MLA_ROPE), sin_m, zero(LANES - MLA_ROPE),
        one(MLA_NOPE), cos_m, one(LANES - MLA_NOPE - MLA_ROPE),
        zero(MLA_NOPE), sin_m, zero(LANES - MLA_NOPE - MLA_ROPE)], axis=1)
    ident_row = np.concatenate([np.ones(LANES), np.zeros(LANES)] * 3).astype(np.float32)
    ident = jnp.broadcast_to(jnp.asarray(ident_row)[None, :], (tm, _T_END))
    return jnp.concatenate([ident, lat], axis=0)


def _layer_weights(l, w_in, mla_w_qb, mla_w_kvb):
    w = w_in[l]
    offs = np.cumsum([0, MLA_Q_LORA, MLA_KV_LORA, MLA_ROPE, SWA_HEADS * SWA_HD, SWA_KV_HEADS * SWA_HD,
                      SWA_KV_HEADS * SWA_HD, GDN_CONV_CH, GDN_V_DIM, 4 * GDN_HEADS])
    cq, ckv, krope, sq, sk, sv, g3, gz, gates = [w[:, offs[i]:offs[i + 1]] for i in range(9)]
    idx_s8, sgn_s8 = _rot_columns(SWA_HD, SWA_HEADS)
    idx_s2, sgn_s2 = _rot_columns(SWA_HD, SWA_KV_HEADS)
    idx_m, sgn_m = _rot_columns(MLA_ROPE, 1)
    zeros = lambda n: jnp.zeros((D_MODEL, n), F32)
    misc_a = jnp.concatenate([krope, gates, zeros(LANES - _M_END)], axis=1)
    misc_b = jnp.concatenate([krope[:, idx_m] * sgn_m, zeros(LANES - MLA_ROPE)], axis=1)
    win = jnp.concatenate([cq, ckv, sq, sq[:, idx_s8] * sgn_s8, sk, sk[:, idx_s2] * sgn_s2, sv, g3, gz,
                           misc_a, misc_b], axis=1).astype(BF)

    r = MLA_Q_LORA
    wq = mla_w_qb[l].reshape(r, MLA_HEADS, MLA_NOPE + MLA_ROPE)
    nope, rope = wq[:, :, :MLA_NOPE], wq[:, :, MLA_NOPE:]
    pad = LANES - MLA_NOPE - MLA_ROPE
    z = lambda n: jnp.zeros((r, MLA_HEADS, n), F32)
    qa = jnp.concatenate([nope, rope, z(pad)], axis=-1).reshape(r, MLA_HEADS * LANES)
    qb = jnp.concatenate([z(MLA_NOPE), rope[:, :, idx_m] * sgn_m, z(pad)], axis=-1).reshape(r, MLA_HEADS * LANES)
    wqb = jnp.concatenate([qa, qb], axis=1).astype(BF)

    kvb = mla_w_kvb[l].reshape(MLA_KV_LORA, MLA_HEADS, MLA_NOPE + MLA_V)
    k_nope = jnp.concatenate([kvb[:, :, :MLA_NOPE], jnp.zeros((MLA_KV_LORA, MLA_HEADS, LANES - MLA_NOPE), F32)],
                             axis=-1).reshape(MLA_KV_LORA, MLA_HEADS * LANES)
    place = np.zeros((LANES, MLA_HEADS, LANES), np.float32)
    for i in range(MLA_ROPE):
        place[i, :, MLA_NOPE + i] = 1.0
    wk = jnp.concatenate([k_nope, jnp.asarray(place.reshape(LANES, MLA_HEADS * LANES))], axis=0).astype(BF)
    wv = kvb[:, :, MLA_NOPE:].reshape(MLA_KV_LORA, MLA_HEADS * MLA_V).astype(BF)
    return win, wqb, wk, wv


def _misc_row(vals):
    row = jnp.zeros((1, LANES), F32)
    return row.at[0, _M_G:_M_B].set(vals.reshape(-1).astype(F32))


def kernel(x_prompt, x_sample, cache_mla_ckv, cache_mla_krope, cache_swa_k, cache_swa_v, state_gdn, c, c_ctx,
           w_ada, b_ada, norm_ffn1, ffn1_w1, ffn1_w2, norm_mix, w_in, mla_q_norm, mla_w_qb, mla_kv_norm,
           mla_w_kvb, swa_sink, gdn_conv_w, gdn_a_log, gdn_dt_bias, gdn_norm, w_out, norm_ffn2, ffn2_w1,
           ffn2_w2, final_norm):
    tm_proj = 256
    x = jnp.concatenate([x_prompt.reshape(N_CTX, D_MODEL), x_sample.reshape(N_LAT, D_MODEL)], axis=0)
    cond = jnp.concatenate([c_ctx[None, :], c, jnp.zeros((COND_ROWS - N_GROUPS, D_MODEL), F32)], axis=0)
    mods_all = _adaln(cond, w_ada, b_ada)
    tab = _rope_table(tm_proj)
    nh = 2 * GDN_HEADS
    c_ = GDN_CHUNK

    new_ckv, new_krope, new_sk, new_sv, new_st = [], [], [], [], []
    for l in range(DEPTH):
        mods = mods_all[l, :N_GROUPS].reshape(N_GROUPS, 1, N_MOD * D_MODEL)
        win, wqb, wk, wv = _layer_weights(l, w_in, mla_w_qb, mla_w_kvb)
        x = _ffn(x, mods, 0, norm_ffn1[l], ffn1_w1[l].astype(BF), ffn1_w2[l].astype(BF))

        (q_mla, ckv_n, misc, k_mla, v_mla, sq, sk, sv, g3, gz) = _proj(
            x, mods, norm_mix[l], win, mla_q_norm[l], wqb, mla_kv_norm[l], wk, wv, tab,
            _misc_row(gdn_a_log[l]), _misc_row(gdn_dt_bias[l]), tm=tm_proj)

        misc_c = jnp.pad(cache_mla_krope[:, l].reshape(DEC_BATCH * PAST_LEN, MLA_ROPE),
                         ((0, 0), (0, LANES - MLA_ROPE)))
        k_c, v_c = _kv_cache(cache_mla_ckv[:, l].reshape(DEC_BATCH * PAST_LEN, MLA_KV_LORA), misc_c, wk, wv)
        o_mla_c = _mla_attend(q_mla, [k_mla], [v_mla], BATCH, SEQ, 0, [(0, SEQ)])
        o_mla_l = _mla_attend(q_mla, [k_mla, k_c], [v_mla, v_c], DEC_BATCH, DEC_SEQ, N_CTX,
                              [(N_CTX // DEC_SEQ, DEC_SEQ), (0, PAST_LEN)])
        sink_rows = jnp.broadcast_to(swa_sink[l][:, None], (SWA_HEADS, LANES))
        o_swa_c = _swa_ctx(sq, sk, sv, sink_rows)
        o_swa_l = _swa_lat(sq, sk, sv, cache_swa_k[:, l].reshape(DEC_BATCH * PAST_LEN, LANES),
                           cache_swa_v[:, l].reshape(DEC_BATCH * PAST_LEN, LANES), sink_rows)
        g_col = misc[:, _M_G:_M_B]
        beta_col = misc[:, _M_B:_M_END]
        g_row = g_col.reshape(N_TOK // c_, c_, nh).transpose(0, 2, 1)
        qkv_c = _gdn_conv(g3, gdn_conv_w[l], 0, BATCH, SEQ)
        qkv_l = _gdn_conv(g3, gdn_conv_w[l], N_CTX, DEC_BATCH, DEC_SEQ)
        of_c, ob_c, st_c = _gdn(qkv_c, g_col[:N_CTX], g_row[:N_CTX // c_], beta_col[:N_CTX], None, BATCH, SEQ)
        of_l, ob_l, _ = _gdn(qkv_l, g_col[N_CTX:], g_row[N_CTX // c_:], beta_col[N_CTX:],
                             state_gdn[:, l].reshape(DEC_BATCH, nh, GDN_DK, GDN_DV), DEC_BATCH, DEC_SEQ)

        x = _out_proj(x, mods, jnp.concatenate([o_mla_c, o_mla_l], axis=0),
                      jnp.concatenate([o_swa_c, o_swa_l], axis=0),
                      jnp.concatenate([of_c, of_l], axis=0), jnp.concatenate([ob_c, ob_l], axis=0),
                      gz, gdn_norm[l], w_out[l].astype(BF))
        x = _ffn(x, mods, 2, norm_ffn2[l], ffn2_w1[l].astype(BF), ffn2_w2[l].astype(BF))

        new_ckv.append(ckv_n[:N_CTX].reshape(BATCH, SEQ, MLA_KV_LORA))
        new_krope.append(misc[:N_CTX, :MLA_ROPE].reshape(BATCH, SEQ, MLA_ROPE))
        new_sk.append(sk[:N_CTX].reshape(BATCH, SEQ, SWA_KV_HEADS, SWA_HD))
        new_sv.append(sv[:N_CTX].reshape(BATCH, SEQ, SWA_KV_HEADS, SWA_HD))
        new_st.append(st_c.reshape(BATCH, 2, GDN_HEADS, GDN_DK, GDN_DV))

    y_prompt = _final_norm(x, final_norm, 0, N_CTX).reshape(BATCH, SEQ, D_MODEL)
    y_sample = _final_norm(x, final_norm, N_CTX, N_LAT).reshape(DEC_BATCH, DEC_SEQ, D_MODEL)
    return (y_prompt, y_sample, jnp.stack(new_ckv, axis=1), jnp.stack(new_krope, axis=1),
            jnp.stack(new_sk, axis=1), jnp.stack(new_sv, axis=1), jnp.stack(new_st, axis=1))
```

```python
import functools

import numpy as np
import jax
import jax.numpy as jnp
from jax import lax
from jax.experimental import pallas as pl
from jax.experimental.pallas import tpu as pltpu

D_MODEL = 1024
BATCH = 16
SEQ = 256
DEPTH = 2
DEC_BATCH = 2
DEC_SEQ = 2048
PAST_LEN = 512
GRID_W = 64
ROPE_BASE = 10000.0
NORM_EPS = 1e-6
N_MOD = 9
D_FF = 2816
MLA_HEADS = 8
MLA_Q_LORA = 384
MLA_KV_LORA = 256
MLA_NOPE = 64
MLA_ROPE = 32
MLA_V = 64
SWA_HEADS = 8
SWA_KV_HEADS = 2
SWA_GROUP = SWA_HEADS // SWA_KV_HEADS
SWA_HD = 64
SWA_WINDOW = 128
GDN_HEADS = 4
GDN_DK = 128
GDN_DV = 128
GDN_CONV = 5
GDN_CHUNK = 64
GDN_QK_DIM = GDN_HEADS * GDN_DK
GDN_V_DIM = GDN_HEADS * GDN_DV
GDN_CONV_CH = 2 * GDN_QK_DIM + GDN_V_DIM

N_CTX = BATCH * SEQ
N_LAT = DEC_BATCH * DEC_SEQ
N_TOK = N_CTX + N_LAT
N_GROUPS = 1 + DEC_BATCH
COND_ROWS = 8

LANES = 128
VMEM_LIMIT_BYTES = 56 * 1024 * 1024

BF = jnp.bfloat16
F32 = jnp.float32

_C_CQ = 0
_C_CKV = _C_CQ + MLA_Q_LORA
_C_SQ = _C_CKV + MLA_KV_LORA
_C_SQR = _C_SQ + SWA_HEADS * SWA_HD
_C_SK = _C_SQR + SWA_HEADS * SWA_HD
_C_SKR = _C_SK + SWA_KV_HEADS * SWA_HD
_C_SV = _C_SKR + SWA_KV_HEADS * SWA_HD
_C_G3 = _C_SV + SWA_KV_HEADS * SWA_HD
_C_GZ = _C_G3 + GDN_CONV_CH
_C_MA = _C_GZ + GDN_V_DIM
_C_MB = _C_MA + LANES
_C_END = _C_MB + LANES
_M_G = MLA_ROPE
_M_B = MLA_ROPE + 2 * GDN_HEADS
_M_END = MLA_ROPE + 4 * GDN_HEADS
_T_CS, _T_SS, _T_CA, _T_SA, _T_CQ, _T_SQ = (i * LANES for i in range(6))
_T_END = 6 * LANES


def _params(*sem):
    return pltpu.CompilerParams(dimension_semantics=sem, vmem_limit_bytes=VMEM_LIMIT_BYTES)


def _bdot(a, b):
    return jnp.dot(a.astype(BF), b.astype(BF), preferred_element_type=F32)


def _bdot_nt(a, b):
    return lax.dot_general(a.astype(BF), b.astype(BF), (((1,), (1,)), ((), ())),
                           preferred_element_type=F32)


def _bdot_tn(a, b):
    return lax.dot_general(a.astype(BF), b.astype(BF), (((0,), (0,)), ((), ())),
                           preferred_element_type=F32)


def _split2(a):
    hi = a.astype(BF)
    lo = (a - hi.astype(F32)).astype(BF)
    return hi, lo


def _dot_3pass(a, b):
    ah, al = _split2(a)
    bh, bl = _split2(b)
    return (jnp.dot(ah, bh, preferred_element_type=F32)
            + (jnp.dot(ah, bl, preferred_element_type=F32)
               + jnp.dot(al, bh, preferred_element_type=F32)))


def _silu(x):
    return x / (1.0 + jnp.exp(-x))


def _rms(x, gain, eps=NORM_EPS):
    return x * lax.rsqrt(jnp.mean(x * x, axis=-1, keepdims=True) + eps) * gain


def _group_of_row(r):
    return jnp.where(r < N_CTX, 0, 1 + (r - N_CTX) // DEC_SEQ)


def _adaln_kernel(c_ref, w_ref, b_ref, o_ref):
    o_ref[0] = _bdot(_silu(c_ref[...]), w_ref[0]) + b_ref[0]


def _adaln(cond, w_ada, b_ada, tn=1536):
    n = N_MOD * D_MODEL
    return pl.pallas_call(
        _adaln_kernel,
        grid=(DEPTH, n // tn),
        in_specs=[pl.BlockSpec((COND_ROWS, D_MODEL), lambda l, j: (0, 0)),
                  pl.BlockSpec((1, D_MODEL, tn), lambda l, j: (l, 0, j)),
                  pl.BlockSpec((1, 1, tn), lambda l, j: (l, 0, j))],
        out_specs=pl.BlockSpec((1, COND_ROWS, tn), lambda l, j: (l, 0, j)),
        out_shape=jax.ShapeDtypeStruct((DEPTH, COND_ROWS, n), F32),
        compiler_params=_params("parallel", "parallel"),
        name="adaln",
    )(cond, w_ada, b_ada.reshape(DEPTH, 1, n))


def _ffn_kernel(x_ref, mod_ref, gain_ref, w1_ref, w2_ref, o_ref):
    x = x_ref[...]
    mod = mod_ref[0]
    shift, scale, gate = mod[:, :D_MODEL], mod[:, D_MODEL:2 * D_MODEL], mod[:, 2 * D_MODEL:]
    h = _rms(x, gain_ref[...]) * (1.0 + scale) + shift
    gu = jnp.dot(h.astype(BF), w1_ref[...], preferred_element_type=F32)
    a = _silu(gu[:, :D_FF]) * gu[:, D_FF:]
    y = jnp.dot(a.astype(BF), w2_ref[...], preferred_element_type=F32)
    o_ref[...] = x + gate * (0.5 * y)


def _ffn(x, mods, which, gain, w1, w2, tm=256):
    n = x.shape[0]
    return pl.pallas_call(
        _ffn_kernel,
        grid=(n // tm,),
        in_specs=[pl.BlockSpec((tm, D_MODEL), lambda i: (i, 0)),
                  pl.BlockSpec((1, 1, 3 * D_MODEL), lambda i: (_group_of_row(i * tm), 0, which)),
                  pl.BlockSpec((1, D_MODEL), lambda i: (0, 0)),
                  pl.BlockSpec((D_MODEL, 2 * D_FF), lambda i: (0, 0)),
                  pl.BlockSpec((D_FF, D_MODEL), lambda i: (0, 0))],
        out_specs=pl.BlockSpec((tm, D_MODEL), lambda i: (i, 0)),
        out_shape=jax.ShapeDtypeStruct((n, D_MODEL), F32),
        compiler_params=_params("parallel"),
        name="ffn",
    )(x, mods, gain.reshape(1, D_MODEL), w1, w2)


def _kv_expand(ckv_n, misc, wk_ref, wv_ref):
    kin = jnp.concatenate([ckv_n, misc], axis=1).astype(BF)
    kk = jnp.dot(kin, wk_ref[...], preferred_element_type=F32)
    vv = jnp.dot(ckv_n.astype(BF), wv_ref[...], preferred_element_type=F32)
    return kk.astype(BF), vv.astype(BF)


def _proj_kernel(x_ref, mod_ref, gain_ref, win_ref, qg_ref, wqb_ref, kvg_ref, wk_ref, wv_ref, tab_ref,
                 alog_ref, dtb_ref,
                 q_ref, ckv_ref, misc_ref, kmla_ref, vmla_ref, sq_ref, sk_ref, sv_ref, g3_ref, gz_ref):
    x = x_ref[...]
    mod = mod_ref[0]
    shift, scale = mod[:, :D_MODEL], mod[:, D_MODEL:2 * D_MODEL]
    h = _rms(x, gain_ref[...]) * (1.0 + scale) + shift
    u = jnp.dot(h.astype(BF), win_ref[...], preferred_element_type=F32)
    tab = tab_ref[...]

    qn = _rms(u[:, _C_CQ:_C_CKV], qg_ref[...])
    q2 = jnp.dot(qn.astype(BF), wqb_ref[...], preferred_element_type=F32)
    nq = MLA_HEADS * LANES
    cosq = jnp.concatenate([tab[:, _T_CQ:_T_CQ + LANES]] * MLA_HEADS, axis=1)
    sinq = jnp.concatenate([tab[:, _T_SQ:_T_SQ + LANES]] * MLA_HEADS, axis=1)
    q_ref[...] = (q2[:, :nq] * cosq + q2[:, nq:] * sinq).astype(BF)

    ckv_n = _rms(u[:, _C_CKV:_C_SQ], kvg_ref[...])
    ckv_ref[...] = ckv_n

    m = (u[:, _C_MA:_C_MB] * tab[:, _T_CA:_T_CA + LANES] + u[:, _C_MB:_C_END] * tab[:, _T_SA:_T_SA + LANES])
    lane = lax.broadcasted_iota(jnp.int32, m.shape, 1)
    z = m + dtb_ref[...]
    softplus = jnp.maximum(z, 0.0) + jnp.log(1.0 + jnp.exp(-jnp.abs(z)))
    decay = -jnp.exp(alog_ref[...]) * softplus
    strength = 1.0 / (1.0 + jnp.exp(-m))
    misc = jnp.where((lane >= _M_G) & (lane < _M_B), decay,
                     jnp.where((lane >= _M_B) & (lane < _M_END), strength, m))
    misc_ref[...] = misc

    kk, vv = _kv_expand(ckv_n, misc, wk_ref, wv_ref)
    kmla_ref[...] = kk
    vmla_ref[...] = vv

    n_sq = SWA_HEADS * SWA_HD
    cos_s = tab[:, _T_CS:_T_CS + LANES]
    sin_s = tab[:, _T_SS:_T_SS + LANES]
    cos4 = jnp.concatenate([cos_s] * (n_sq // LANES), axis=1)
    sin4 = jnp.concatenate([sin_s] * (n_sq // LANES), axis=1)
    sq = u[:, _C_SQ:_C_SQR] * cos4 + u[:, _C_SQR:_C_SK] * sin4
    sq_ref[...] = (sq * (SWA_HD ** -0.5)).astype(BF)
    sk_ref[...] = u[:, _C_SK:_C_SKR] * cos_s + u[:, _C_SKR:_C_SV] * sin_s
    sv_ref[...] = u[:, _C_SV:_C_G3]
    g3_ref[...] = u[:, _C_G3:_C_GZ]
    gz_ref[...] = u[:, _C_GZ:_C_MA]


def _proj(x, mods, gain, win, qg, wqb, kvg, wk, wv, tab, alog, dtb, tm=256):
    n = x.shape[0]
    const = lambda i: (0, 0)
    n_ctx_tiles = N_CTX // tm
    lat_tiles = DEC_SEQ // tm

    def tab_map(i):
        return (jnp.where(i < n_ctx_tiles, 0, 1 + (i - n_ctx_tiles) % lat_tiles), 0)

    widths = [(MLA_HEADS * LANES, BF), (MLA_KV_LORA, F32), (LANES, F32), (MLA_HEADS * LANES, BF),
              (MLA_HEADS * MLA_V, BF), (SWA_HEADS * SWA_HD, BF), (SWA_KV_HEADS * SWA_HD, F32),
              (SWA_KV_HEADS * SWA_HD, F32), (GDN_CONV_CH, F32), (GDN_V_DIM, F32)]
    return pl.pallas_call(
        _proj_kernel,
        grid=(n // tm,),
        in_specs=[pl.BlockSpec((tm, D_MODEL), lambda i: (i, 0)),
                  pl.BlockSpec((1, 1, 3 * D_MODEL), lambda i: (_group_of_row(i * tm), 0, 1)),
                  pl.BlockSpec((1, D_MODEL), const),
                  pl.BlockSpec(win.shape, const),
                  pl.BlockSpec((1, MLA_Q_LORA), const),
                  pl.BlockSpec(wqb.shape, const),
                  pl.BlockSpec((1, MLA_KV_LORA), const),
                  pl.BlockSpec(wk.shape, const),
                  pl.BlockSpec(wv.shape, const),
                  pl.BlockSpec((tm, _T_END), tab_map),
                  pl.BlockSpec((1, LANES), const),
                  pl.BlockSpec((1, LANES), const)],
        out_specs=[pl.BlockSpec((tm, w), lambda i: (i, 0)) for w, _ in widths],
        out_shape=[jax.ShapeDtypeStruct((n, w), dt) for w, dt in widths],
        compiler_params=_params("parallel"),
        name="proj",
    )(x, mods, gain.reshape(1, D_MODEL), win, qg.reshape(1, -1), wqb, kvg.reshape(1, -1), wk, wv, tab,
      alog, dtb)


def _kv_cache_kernel(ckv_ref, misc_ref, wk_ref, wv_ref, k_ref, v_ref):
    kk, vv = _kv_expand(ckv_ref[...], misc_ref[...], wk_ref, wv_ref)
    k_ref[...] = kk
    v_ref[...] = vv


def _kv_cache(ckv, misc, wk, wv, tm=PAST_LEN):
    n = ckv.shape[0]
    const = lambda i: (0, 0)
    return pl.pallas_call(
        _kv_cache_kernel,
        grid=(n // tm,),
        in_specs=[pl.BlockSpec((tm, MLA_KV_LORA), lambda i: (i, 0)),
                  pl.BlockSpec((tm, LANES), lambda i: (i, 0)),
                  pl.BlockSpec(wk.shape, const),
                  pl.BlockSpec(wv.shape, const)],
        out_specs=[pl.BlockSpec((tm, MLA_HEADS * LANES), lambda i: (i, 0)),
                   pl.BlockSpec((tm, MLA_HEADS * MLA_V), lambda i: (i, 0))],
        out_shape=[jax.ShapeDtypeStruct((n, MLA_HEADS * LANES), BF),
                   jax.ShapeDtypeStruct((n, MLA_HEADS * MLA_V), BF)],
        compiler_params=_params("parallel"),
        name="kv_cache",
    )(ckv, misc, wk, wv)


def _mla_kernel(*refs, n_seg):
    q_ref = refs[0]
    k_refs = refs[1:1 + n_seg]
    v_refs = refs[1 + n_seg:1 + 2 * n_seg]
    o_ref = refs[1 + 2 * n_seg]
    scale = (MLA_NOPE + MLA_ROPE) ** -0.5
    tq = q_ref.shape[0]
    lane = lax.broadcasted_iota(jnp.int32, (tq, LANES), 1)
    for pair in range(MLA_HEADS // 2):
        outs = []
        for h in (2 * pair, 2 * pair + 1):
            qh = q_ref[:, h * LANES:(h + 1) * LANES]
            ss = [_bdot_nt(qh, k[:, h * LANES:(h + 1) * LANES]) * scale for k in k_refs]
            mx = functools.reduce(jnp.maximum, [jnp.max(s, axis=-1, keepdims=True) for s in ss])
            ps = [jnp.exp(s - mx) for s in ss]
            den = functools.reduce(jnp.add, [jnp.sum(p, axis=-1, keepdims=True) for p in ps])
            acc = functools.reduce(jnp.add, [_bdot(p, v[:, pair * LANES:(pair + 1) * LANES])
                                             for p, v in zip(ps, v_refs)])
            outs.append(acc / den)
        o_ref[:, pair * LANES:(pair + 1) * LANES] = jnp.where(lane < MLA_V, outs[0], outs[1]).astype(BF)


def _mla_attend(q, ks, vs, n_batch, t, row0, k_blocks, tq=256):
    n_seg = len(ks)
    qb0 = row0 // tq
    tiles = t // tq
    in_specs = [pl.BlockSpec((tq, MLA_HEADS * LANES), lambda b, i: (qb0 + b * tiles + i, 0))]
    for (b0, s), width in [(kb, MLA_HEADS * LANES) for kb in k_blocks] + [(kb, MLA_HEADS * MLA_V) for kb in k_blocks]:
        in_specs.append(pl.BlockSpec((s, width), functools.partial(lambda b, i, b0: (b0 + b, 0), b0=b0)))
    return pl.pallas_call(
        functools.partial(_mla_kernel, n_seg=n_seg),
        grid=(n_batch, tiles),
        in_specs=in_specs,
        out_specs=pl.BlockSpec((tq, MLA_HEADS * MLA_V), lambda b, i: (b * tiles + i, 0)),
        out_shape=jax.ShapeDtypeStruct((n_batch * t, MLA_HEADS * MLA_V), BF),
        compiler_params=_params("parallel", "parallel"),
        name="mla_attend",
    )(q, *ks, *vs)


def _gqa_heads(q_ref, k_segs, v_segs, masks, sink_ref, o_ref):
    tq = q_ref.shape[0]
    lane_q = lax.broadcasted_iota(jnp.int32, (tq, LANES), 1)
    lo_q = lane_q < SWA_HD
    k_roll = [pltpu.roll(k, SWA_HD, 1) for k in k_segs]
    v_roll = [pltpu.roll(v, SWA_HD, 1) for v in v_segs]
    for kvh in range(SWA_KV_HEADS):
        tiles = [q_ref[:, (kvh * 2 + j) * LANES:(kvh * 2 + j + 1) * LANES] for j in range(2)]
        zero = jnp.zeros_like(tiles[0])
        qs = jnp.concatenate([jnp.where(lo_q, tiles[0], zero), jnp.where(lo_q, zero, tiles[0]),
                              jnp.where(lo_q, tiles[1], zero), jnp.where(lo_q, zero, tiles[1])], axis=0)
        sink = jnp.concatenate(
            [jnp.broadcast_to(sink_ref[kvh * SWA_GROUP + g:kvh * SWA_GROUP + g + 1, 0:1], (tq, 1))
             for g in range(SWA_GROUP)], axis=0)
        mx = sink
        ss = []
        for k, kr, msk in zip(k_segs, k_roll, masks):
            lane_k = lax.broadcasted_iota(jnp.int32, k.shape, 1)
            first = (lane_k < SWA_HD) == (kvh == 0)
            kd = jnp.where(first, k, kr)
            s = _bdot_nt(qs, kd)
            if msk is not None:
                s = jnp.where(msk, s, -1e30)
            ss.append(s)
            mx = jnp.maximum(mx, jnp.max(s, axis=-1, keepdims=True))
        den = jnp.exp(sink - mx)
        acc = None
        for s, v, vr in zip(ss, v_segs, v_roll):
            lane_v = lax.broadcasted_iota(jnp.int32, v.shape, 1)
            first = (lane_v < SWA_HD) == (kvh == 0)
            vd = jnp.where(first, v, vr)
            p = jnp.exp(s - mx)
            den = den + jnp.sum(p, axis=-1, keepdims=True)
            pv = _bdot(p, vd)
            acc = pv if acc is None else acc + pv
        o = acc / den
        for j in range(2):
            o_ref[:, (kvh * 2 + j) * LANES:(kvh * 2 + j + 1) * LANES] = jnp.where(
                lo_q, o[(2 * j) * tq:(2 * j + 1) * tq], o[(2 * j + 1) * tq:(2 * j + 2) * tq]).astype(BF)


def _swa_ctx_kernel(q_ref, k_ref, v_ref, sink_ref, o_ref):
    _gqa_heads(q_ref, [k_ref[...]], [v_ref[...]], [None], sink_ref, o_ref)


def _swa_ctx(sq, sk, sv, sink_rows):
    return pl.pallas_call(
        _swa_ctx_kernel,
        grid=(BATCH,),
        in_specs=[pl.BlockSpec((SEQ, SWA_HEADS * SWA_HD), lambda b: (b, 0)),
                  pl.BlockSpec((SEQ, LANES), lambda b: (b, 0)),
                  pl.BlockSpec((SEQ, LANES), lambda b: (b, 0)),
                  pl.BlockSpec((SWA_HEADS, LANES), lambda b: (0, 0))],
        out_specs=pl.BlockSpec((SEQ, SWA_HEADS * SWA_HD), lambda b: (b, 0)),
        out_shape=jax.ShapeDtypeStruct((N_CTX, SWA_HEADS * SWA_HD), BF),
        compiler_params=_params("parallel"),
        name="swa_ctx",
    )(sq, sk, sv, sink_rows)


def _swa_lat_kernel(q_ref, kp_ref, kc_ref, kn_ref, vp_ref, vc_ref, vn_ref, kx_ref, vx_ref, sink_ref, o_ref):
    w = SWA_WINDOW
    n = pl.program_id(1)
    nb = pl.num_programs(1)
    k_band = jnp.concatenate([kp_ref[...], kc_ref[...], kn_ref[...]], axis=0)
    v_band = jnp.concatenate([vp_ref[...], vc_ref[...], vn_ref[...]], axis=0)
    rows = SWA_GROUP * w
    r = lax.broadcasted_iota(jnp.int32, (rows, 3 * w), 0) & (w - 1)
    c = lax.broadcasted_iota(jnp.int32, (rows, 3 * w), 1)
    valid = (c >= r) & (c <= r + 2 * w) & ((c >= w) | (n > 0)) & ((c < 2 * w) | (n < nb - 1))
    _gqa_heads(q_ref, [k_band, kx_ref[...]], [v_band, vx_ref[...]], [valid, None], sink_ref, o_ref)


def _swa_lat(sq, sk, sv, k_ctx, v_ctx, sink_rows):
    w = SWA_WINDOW
    nb = DEC_SEQ // w
    q0 = N_CTX // w

    def blk(d):
        return lambda b, n: (q0 + b * nb + jnp.clip(n + d, 0, nb - 1), 0)

    kv_specs = [pl.BlockSpec((w, LANES), blk(d)) for d in (-1, 0, 1)]
    return pl.pallas_call(
        _swa_lat_kernel,
        grid=(DEC_BATCH, nb),
        in_specs=[pl.BlockSpec((w, SWA_HEADS * SWA_HD), blk(0))] + kv_specs + kv_specs
        + [pl.BlockSpec((PAST_LEN, LANES), lambda b, n: (b, 0)),
           pl.BlockSpec((PAST_LEN, LANES), lambda b, n: (b, 0)),
           pl.BlockSpec((SWA_HEADS, LANES), lambda b, n: (0, 0))],
        out_specs=pl.BlockSpec((w, SWA_HEADS * SWA_HD), lambda b, n: (b * nb + n, 0)),
        out_shape=jax.ShapeDtypeStruct((N_LAT, SWA_HEADS * SWA_HD), BF),
        compiler_params=_params("parallel", "parallel"),
        name="swa_lat",
    )(sq, sk, sk, sk, sv, sv, sv, k_ctx, v_ctx, sink_rows)


def _gdn_conv_kernel(x_ref, w_ref, o_ref):
    x = x_ref[...]
    t = x.shape[0]
    w = w_ref[...]
    row = lax.broadcasted_iota(jnp.int32, x.shape, 0)
    half = GDN_CONV // 2
    acc = x * w[half:half + 1, :]
    for k in range(GDN_CONV):
        d = k - half
        if d == 0:
            continue
        xs = pltpu.roll(x, (-d) % t, 0)
        ok = (row + d >= 0) & (row + d < t)
        acc = acc + jnp.where(ok, xs, 0.0) * w[k:k + 1, :]
    y = _silu(acc)
    is_qk = pl.program_id(1) < 2
    cols = []
    for h in range(x.shape[1] // LANES):
        yh = y[:, h * LANES:(h + 1) * LANES]
        nrm = lax.rsqrt(jnp.sum(yh * yh, axis=-1, keepdims=True) + 1e-6)
        cols.append(yh * jnp.where(is_qk, nrm, 1.0))
    o_ref[...] = jnp.concatenate(cols, axis=1)


def _gdn_conv(g3, conv_w, row0, n_seq, t):
    blk0 = row0 // t
    return pl.pallas_call(
        _gdn_conv_kernel,
        grid=(n_seq, 3),
        in_specs=[pl.BlockSpec((t, GDN_QK_DIM), lambda s, j: (blk0 + s, j)),
                  pl.BlockSpec((GDN_CONV, GDN_QK_DIM), lambda s, j: (0, j))],
        out_specs=pl.BlockSpec((t, GDN_QK_DIM), lambda s, j: (s, j)),
        out_shape=jax.ShapeDtypeStruct((n_seq * t, GDN_CONV_CH), F32),
        compiler_params=_params("parallel", "parallel"),
        name="gdn_conv",
    )(g3, conv_w)


def _gdn_chunks(probs):
    c = GDN_CHUNK
    ri = lax.broadcasted_iota(jnp.int32, (c, c), 0)
    ci = lax.broadcasted_iota(jnp.int32, (c, c), 1)
    lower_incl, upper_incl = ri >= ci, ri <= ci
    eye = (ri == ci).astype(F32)
    n = len(probs)
    incl = [upper_incl if p["backward"] else lower_incl for p in probs]
    incl_t = [lower_incl if p["backward"] else upper_incl for p in probs]
    strict = [(ri < ci) if p["backward"] else (ri > ci) for p in probs]
    gc_col = [jnp.sum(jnp.where(incl[i], probs[i]["g_row"], 0.0), axis=1, keepdims=True) for i in range(n)]
    gc_row = [jnp.sum(jnp.where(incl_t[i], probs[i]["g_col"], 0.0), axis=0, keepdims=True) for i in range(n)]
    decay = [jnp.where(incl[i], jnp.exp(gc_col[i] - gc_row[i]), 0.0) for i in range(n)]
    q = [p["q"] * (GDN_DK ** -0.5) for p in probs]
    kb = [p["k"] * p["beta"] for p in probs]
    kk = [_bdot_nt(kb[i], probs[i]["k"]) for i in range(n)]
    qk = [_bdot_nt(q[i], probs[i]["k"]) for i in range(n)]
    pw = [jnp.where(strict[i], -(kk[i] * decay[i]), 0.0) for i in range(n)]
    inv = [eye + m for m in pw]
    for _ in range(int(np.log2(c)) - 1):
        pw = [_dot_3pass(m, m) for m in pw]
        inv = [inv[i] + _dot_3pass(inv[i], pw[i]) for i in range(n)]
    e_col = [jnp.exp(g) for g in gc_col]
    uw = [_bdot(inv[i], jnp.concatenate([probs[i]["v"] * probs[i]["beta"], kb[i] * e_col[i]], axis=1))
          for i in range(n)]
    a = [jnp.where(incl[i], qk[i] * decay[i], 0.0) for i in range(n)]
    g_last = [gc_col[i][0:1, :] if probs[i]["backward"] else gc_col[i][c - 1:c, :] for i in range(n)]
    k_dec = [probs[i]["k"] * jnp.exp(g_last[i] - gc_col[i]) for i in range(n)]
    wq = [_bdot(jnp.concatenate([uw[i][:, GDN_DV:], q[i] * e_col[i]], axis=0), probs[i]["state"])
          for i in range(n)]
    v_new = [uw[i][:, :GDN_DV] - wq[i][:c] for i in range(n)]
    o = [wq[i][c:] + _bdot(a[i], v_new[i]) for i in range(n)]
    s_new = [probs[i]["state"] * jnp.exp(g_last[i]) + _bdot_tn(k_dec[i], v_new[i]) for i in range(n)]
    return list(zip(o, s_new))


def _gdn_kernel(*refs, zero_init, n_par):
    if zero_init:
        (qf_ref, qb_ref, gcf_ref, gcb_ref, grf_ref, grb_ref, bf_ref, bb_ref,
         of_ref, ob_ref, sfin_ref, st_ref) = refs
        s0_ref = None
    else:
        (qf_ref, qb_ref, gcf_ref, gcb_ref, grf_ref, grb_ref, bf_ref, bb_ref, s0_ref,
         of_ref, ob_ref, sfin_ref, st_ref) = refs
    n = pl.program_id(1)

    @pl.when(n == 0)
    def _():
        if zero_init:
            st_ref[...] = jnp.zeros_like(st_ref)
        else:
            st_ref[...] = s0_ref[...]

    probs = []
    for s in range(n_par):
        for d, (q_ref, gc_ref, gr_ref, b_ref) in enumerate(
                [(qf_ref, gcf_ref, grf_ref, bf_ref), (qb_ref, gcb_ref, grb_ref, bb_ref)]):
            qkv = q_ref[s]
            g_cols, g_rows, betas = gc_ref[s], gr_ref[s, 0], b_ref[s]
            for h in range(GDN_HEADS):
                i = d * GDN_HEADS + h
                probs.append(dict(
                    q=qkv[:, h * GDN_DK:(h + 1) * GDN_DK],
                    k=qkv[:, GDN_QK_DIM + h * GDN_DK:GDN_QK_DIM + (h + 1) * GDN_DK],
                    v=qkv[:, 2 * GDN_QK_DIM + h * GDN_DV:2 * GDN_QK_DIM + (h + 1) * GDN_DV],
                    g_col=g_cols[:, i:i + 1], g_row=g_rows[i:i + 1, :], beta=betas[:, i:i + 1],
                    state=st_ref[s, i], backward=(d == 1)))
    results = _gdn_chunks(probs)
    for s in range(n_par):
        for d, o_ref in enumerate([of_ref, ob_ref]):
            for h in range(GDN_HEADS):
                i = d * GDN_HEADS + h
                o, s_new = results[s * 2 * GDN_HEADS + i]
                o_ref[s, :, h * GDN_DV:(h + 1) * GDN_DV] = o
                st_ref[s, i] = s_new

    @pl.when(n == pl.num_programs(1) - 1)
    def _():
        sfin_ref[...] = st_ref[...]


def _gdn(qkv, g_col, g_row, beta_col, s0, n_seq, t, n_par):
    c = GDN_CHUNK
    nc = t // c
    nh = 2 * GDN_HEADS
    fwd = lambda s, n: (s, n, 0)
    bwd = lambda s, n: (s, nc - 1 - n, 0)
    fwd4 = lambda s, n: (s, n, 0, 0)
    bwd4 = lambda s, n: (s, nc - 1 - n, 0, 0)
    st_spec = pl.BlockSpec((n_par, nh, GDN_DK, GDN_DV), lambda s, n: (s, 0, 0, 0))
    in_specs = [pl.BlockSpec((n_par, c, GDN_CONV_CH), fwd), pl.BlockSpec((n_par, c, GDN_CONV_CH), bwd),
                pl.BlockSpec((n_par, c, nh), fwd), pl.BlockSpec((n_par, c, nh), bwd),
                pl.BlockSpec((n_par, 1, nh, c), fwd4), pl.BlockSpec((n_par, 1, nh, c), bwd4),
                pl.BlockSpec((n_par, c, nh), fwd), pl.BlockSpec((n_par, c, nh), bwd)]
    args = [qkv, qkv, g_col, g_col, g_row, g_row, beta_col, beta_col]
    if s0 is not None:
        in_specs.append(st_spec)
        args.append(s0)
    return pl.pallas_call(
        functools.partial(_gdn_kernel, zero_init=s0 is None, n_par=n_par),
        grid=(n_seq // n_par, nc),
        in_specs=in_specs,
        out_specs=[pl.BlockSpec((n_par, c, GDN_V_DIM), fwd), pl.BlockSpec((n_par, c, GDN_V_DIM), bwd), st_spec],
        out_shape=[jax.ShapeDtypeStruct((n_seq, t, GDN_V_DIM), F32),
                   jax.ShapeDtypeStruct((n_seq, t, GDN_V_DIM), F32),
                   jax.ShapeDtypeStruct((n_seq, nh, GDN_DK, GDN_DV), F32)],
        scratch_shapes=[pltpu.VMEM((n_par, nh, GDN_DK, GDN_DV), F32)],
        compiler_params=_params("parallel", "arbitrary"),
        name="gdn",
    )(*args)


def _out_kernel(x_ref, mod_ref, om_ref, os_ref, gf_ref, gb_ref, gz_ref, gn_ref, w_ref, o_ref):
    gate = mod_ref[0][:, 2 * D_MODEL:]
    s = gf_ref[...] + gb_ref[...]
    gz = gz_ref[...]
    cols = []
    for h in range(GDN_HEADS):
        sh = s[:, h * GDN_DV:(h + 1) * GDN_DV]
        cols.append(_rms(sh, gn_ref[...]) * _silu(gz[:, h * GDN_DV:(h + 1) * GDN_DV]))
    og = jnp.concatenate(cols, axis=1)
    n_m = MLA_HEADS * MLA_V
    n_s = SWA_HEADS * SWA_HD
    y = (jnp.dot(om_ref[...], w_ref[:n_m, :], preferred_element_type=F32)
         + jnp.dot(os_ref[...], w_ref[n_m:n_m + n_s, :], preferred_element_type=F32)
         + jnp.dot(og.astype(BF), w_ref[n_m + n_s:, :], preferred_element_type=F32))
    o_ref[...] = x_ref[...] + gate * y


def _out_proj(x, mods, o_mla, o_swa, o_gf, o_gb, gz, gdn_norm, w_out, tm=512):
    n = x.shape[0]
    row = lambda i: (i, 0)
    const = lambda i: (0, 0)
    return pl.pallas_call(
        _out_kernel,
        grid=(n // tm,),
        in_specs=[pl.BlockSpec((tm, D_MODEL), row),
                  pl.BlockSpec((1, 1, 3 * D_MODEL), lambda i: (_group_of_row(i * tm), 0, 1)),
                  pl.BlockSpec((tm, MLA_HEADS * MLA_V), row),
                  pl.BlockSpec((tm, SWA_HEADS * SWA_HD), row),
                  pl.BlockSpec((tm, GDN_V_DIM), row),
                  pl.BlockSpec((tm, GDN_V_DIM), row),
                  pl.BlockSpec((tm, GDN_V_DIM), row),
                  pl.BlockSpec((1, GDN_DV), const),
                  pl.BlockSpec(w_out.shape, const)],
        out_specs=pl.BlockSpec((tm, D_MODEL), row),
        out_shape=jax.ShapeDtypeStruct((n, D_MODEL), F32),
        compiler_params=_params("parallel"),
        name="out_proj",
    )(x, mods, o_mla, o_swa, o_gf, o_gb, gz, gdn_norm.reshape(1, GDN_DV), w_out)


def _final_norm_kernel(x_ref, g_ref, o_ref):
    o_ref[...] = _rms(x_ref[...], g_ref[...])


def _final_norm(x, gain, row0, n, tm=512):
    b0 = row0 // tm
    return pl.pallas_call(
        _final_norm_kernel,
        grid=(n // tm,),
        in_specs=[pl.BlockSpec((tm, D_MODEL), lambda i: (b0 + i, 0)),
                  pl.BlockSpec((1, D_MODEL), lambda i: (0, 0))],
        out_specs=pl.BlockSpec((tm, D_MODEL), lambda i: (i, 0)),
        out_shape=jax.ShapeDtypeStruct((n, D_MODEL), F32),
        compiler_params=_params("parallel"),
        name="final_norm",
    )(x, gain.reshape(1, D_MODEL))


def _rot_columns(dim, n_heads):
    quarter = dim // 4
    j = np.arange(dim)
    even = (j // quarter) % 2 == 0
    src = np.where(even, j + quarter, j - quarter)
    sign = np.where(even, -1.0, 1.0).astype(np.float32)
    idx = (np.arange(n_heads)[:, None] * dim + src[None, :]).reshape(-1)
    return idx, np.tile(sign, n_heads)


def _axial_rope(n_tokens, dim):
    rows = n_tokens // GRID_W
    row = jnp.repeat(jnp.arange(rows, dtype=F32), GRID_W)
    col = jnp.tile(jnp.arange(GRID_W, dtype=F32), rows)
    axis_dim = dim // 2
    inv_freq = 1.0 / (ROPE_BASE ** (jnp.arange(0, axis_dim, 2, dtype=F32) / axis_dim))
    ang_r = row[:, None] * inv_freq[None, :]
    ang_c = col[:, None] * inv_freq[None, :]
    ang = jnp.concatenate([ang_r, ang_r, ang_c, ang_c], axis=-1)
    return jnp.cos(ang), jnp.sin(ang)


def _rope_table(tm):
    cos_m, sin_m = _axial_rope(DEC_SEQ, MLA_ROPE)
    cos_s, sin_s = _axial_rope(DEC_SEQ, SWA_HD)
    t = DEC_SEQ
    one = lambda w: jnp.ones((t, w), F32)
    zero = lambda w: jnp.zeros((t, w), F32)
    lat = jnp.concatenate([
        cos_s, cos_s, sin_s, sin_s,
        cos_m, one(LANES - MLA_ROPE), sin_m, zero(LANES - MLA_ROPE),
        one(MLA_NOPE), cos_m, one(LANES - MLA_NOPE - MLA_ROPE),
        zero(MLA_NOPE), sin_m, zero(LANES - MLA_NOPE - MLA_ROPE)], axis=1)
    ident_row = np.concatenate([np.ones(LANES), np.zeros(LANES)] * 3).astype(np.float32)
    ident = jnp.broadcast_to(jnp.asarray(ident_row)[None, :], (tm, _T_END))
    return jnp.concatenate([ident, lat], axis=0)


def _layer_weights(l, w_in, mla_w_qb, mla_w_kvb):
    w = w_in[l]
    offs = np.cumsum([0, MLA_Q_LORA, MLA_KV_LORA, MLA_ROPE, SWA_HEADS * SWA_HD, SWA_KV_HEADS * SWA_HD,
                      SWA_KV_HEADS * SWA_HD, GDN_CONV_CH, GDN_V_DIM, 4 * GDN_HEADS])
    cq, ckv, krope, sq, sk, sv, g3, gz, gates = [w[:, offs[i]:offs[i + 1]] for i in range(9)]
    idx_s8, sgn_s8 = _rot_columns(SWA_HD, SWA_HEADS)
    idx_s2, sgn_s2 = _rot_columns(SWA_HD, SWA_KV_HEADS)
    idx_m, sgn_m = _rot_columns(MLA_ROPE, 1)
    zeros = lambda n: jnp.zeros((D_MODEL, n), F32)
    misc_a = jnp.concatenate([krope, gates, zeros(LANES - _M_END)], axis=1)
    misc_b = jnp.concatenate([krope[:, idx_m] * sgn_m, zeros(LANES - MLA_ROPE)], axis=1)
    win = jnp.concatenate([cq, ckv, sq, sq[:, idx_s8] * sgn_s8, sk, sk[:, idx_s2] * sgn_s2, sv, g3, gz,
                           misc_a, misc_b], axis=1).astype(BF)

    r = MLA_Q_LORA
    wq = mla_w_qb[l].reshape(r, MLA_HEADS, MLA_NOPE + MLA_ROPE)
    nope, rope = wq[:, :, :MLA_NOPE], wq[:, :, MLA_NOPE:]
    pad = LANES - MLA_NOPE - MLA_ROPE
    z = lambda n: jnp.zeros((r, MLA_HEADS, n), F32)
    qa = jnp.concatenate([nope, rope, z(pad)], axis=-1).reshape(r, MLA_HEADS * LANES)
    qb = jnp.concatenate([z(MLA_NOPE), rope[:, :, idx_m] * sgn_m, z(pad)], axis=-1).reshape(r, MLA_HEADS * LANES)
    wqb = jnp.concatenate([qa, qb], axis=1).astype(BF)

    kvb = mla_w_kvb[l].reshape(MLA_KV_LORA, MLA_HEADS, MLA_NOPE + MLA_V)
    k_nope = jnp.concatenate([kvb[:, :, :MLA_NOPE], jnp.zeros((MLA_KV_LORA, MLA_HEADS, LANES - MLA_NOPE), F32)],
                             axis=-1).reshape(MLA_KV_LORA, MLA_HEADS * LANES)
    place = np.zeros((LANES, MLA_HEADS, LANES), np.float32)
    for i in range(MLA_ROPE):
        place[i, :, MLA_NOPE + i] = 1.0
    wk = jnp.concatenate([k_nope, jnp.asarray(place.reshape(LANES, MLA_HEADS * LANES))], axis=0).astype(BF)
    wv = kvb[:, :, MLA_NOPE:].reshape(MLA_KV_LORA, MLA_HEADS * MLA_V).astype(BF)
    return win, wqb, wk, wv


def _misc_row(vals):
    row = jnp.zeros((1, LANES), F32)
    return row.at[0, _M_G:_M_B].set(vals.reshape(-1).astype(F32))


def kernel(x_prompt, x_sample, cache_mla_ckv, cache_mla_krope, cache_swa_k, cache_swa_v, state_gdn, c, c_ctx,
           w_ada, b_ada, norm_ffn1, ffn1_w1, ffn1_w2, norm_mix, w_in, mla_q_norm, mla_w_qb, mla_kv_norm,
           mla_w_kvb, swa_sink, gdn_conv_w, gdn_a_log, gdn_dt_bias, gdn_norm, w_out, norm_ffn2, ffn2_w1,
           ffn2_w2, final_norm):
    tm_proj = 256
    x = jnp.concatenate([x_prompt.reshape(N_CTX, D_MODEL), x_sample.reshape(N_LAT, D_MODEL)], axis=0)
    cond = jnp.concatenate([c_ctx[None, :], c, jnp.zeros((COND_ROWS - N_GROUPS, D_MODEL), F32)], axis=0)
    mods_all = _adaln(cond, w_ada, b_ada)
    tab = _rope_table(tm_proj)
    nh = 2 * GDN_HEADS
    c_ = GDN_CHUNK

    new_ckv, new_krope, new_sk, new_sv, new_st = [], [], [], [], []
    for l in range(DEPTH):
        mods = mods_all[l, :N_GROUPS].reshape(N_GROUPS, 1, N_MOD * D_MODEL)
        win, wqb, wk, wv = _layer_weights(l, w_in, mla_w_qb, mla_w_kvb)
        x = _ffn(x, mods, 0, norm_ffn1[l], ffn1_w1[l].astype(BF), ffn1_w2[l].astype(BF))

        (q_mla, ckv_n, misc, k_mla, v_mla, sq, sk, sv, g3, gz) = _proj(
            x, mods, norm_mix[l], win, mla_q_norm[l], wqb, mla_kv_norm[l], wk, wv, tab,
            _misc_row(gdn_a_log[l]), _misc_row(gdn_dt_bias[l]), tm=tm_proj)

        misc_c = jnp.pad(cache_mla_krope[:, l].reshape(DEC_BATCH * PAST_LEN, MLA_ROPE),
                         ((0, 0), (0, LANES - MLA_ROPE)))
        k_c, v_c = _kv_cache(cache_mla_ckv[:, l].reshape(DEC_BATCH * PAST_LEN, MLA_KV_LORA), misc_c, wk, wv)
        o_mla_c = _mla_attend(q_mla, [k_mla], [v_mla], BATCH, SEQ, 0, [(0, SEQ)])
        o_mla_l = _mla_attend(q_mla, [k_mla, k_c], [v_mla, v_c], DEC_BATCH, DEC_SEQ, N_CTX,
                              [(N_CTX // DEC_SEQ, DEC_SEQ), (0, PAST_LEN)])
        sink_rows = jnp.broadcast_to(swa_sink[l][:, None], (SWA_HEADS, LANES))
        o_swa_c = _swa_ctx(sq, sk, sv, sink_rows)
        o_swa_l = _swa_lat(sq, sk, sv, cache_swa_k[:, l].reshape(DEC_BATCH * PAST_LEN, LANES),
                           cache_swa_v[:, l].reshape(DEC_BATCH * PAST_LEN, LANES), sink_rows)
        g_col = misc[:, _M_G:_M_B]
        beta_col = misc[:, _M_B:_M_END]
        g_row = g_col.reshape(N_TOK // c_, c_, nh).transpose(0, 2, 1)
        qkv_c = _gdn_conv(g3, gdn_conv_w[l], 0, BATCH, SEQ)
        qkv_l = _gdn_conv(g3, gdn_conv_w[l], N_CTX, DEC_BATCH, DEC_SEQ)
        of_c, ob_c, st_c = _gdn(
            qkv_c.reshape(BATCH, SEQ, GDN_CONV_CH), g_col[:N_CTX].reshape(BATCH, SEQ, nh),
            g_row[:N_CTX // c_].reshape(BATCH, SEQ // c_, nh, c_), beta_col[:N_CTX].reshape(BATCH, SEQ, nh),
            None, BATCH, SEQ, n_par=2)
        of_l, ob_l, _ = _gdn(
            qkv_l.reshape(DEC_BATCH, DEC_SEQ, GDN_CONV_CH), g_col[N_CTX:].reshape(DEC_BATCH, DEC_SEQ, nh),
            g_row[N_CTX // c_:].reshape(DEC_BATCH, DEC_SEQ // c_, nh, c_),
            beta_col[N_CTX:].reshape(DEC_BATCH, DEC_SEQ, nh),
            state_gdn[:, l].reshape(DEC_BATCH, nh, GDN_DK, GDN_DV), DEC_BATCH, DEC_SEQ, n_par=2)

        x = _out_proj(x, mods, jnp.concatenate([o_mla_c, o_mla_l], axis=0),
                      jnp.concatenate([o_swa_c, o_swa_l], axis=0),
                      jnp.concatenate([of_c.reshape(N_CTX, GDN_V_DIM), of_l.reshape(N_LAT, GDN_V_DIM)], axis=0),
                      jnp.concatenate([ob_c.reshape(N_CTX, GDN_V_DIM), ob_l.reshape(N_LAT, GDN_V_DIM)], axis=0),
                      gz, gdn_norm[l], w_out[l].astype(BF))
        x = _ffn(x, mods, 2, norm_ffn2[l], ffn2_w1[l].astype(BF), ffn2_w2[l].astype(BF))

        new_ckv.append(ckv_n[:N_CTX].reshape(BATCH, SEQ, MLA_KV_LORA))
        new_krope.append(misc[:N_CTX, :MLA_ROPE].reshape(BATCH, SEQ, MLA_ROPE))
        new_sk.append(sk[:N_CTX].reshape(BATCH, SEQ, SWA_KV_HEADS, SWA_HD))
        new_sv.append(sv[:N_CTX].reshape(BATCH, SEQ, SWA_KV_HEADS, SWA_HD))
        new_st.append(st_c.reshape(BATCH, 2, GDN_HEADS, GDN_DK, GDN_DV))

    y_prompt = _final_norm(x, final_norm, 0, N_CTX).reshape(BATCH, SEQ, D_MODEL)
    y_sample = _final_norm(x, final_norm, N_CTX, N_LAT).reshape(DEC_BATCH, DEC_SEQ, D_MODEL)
    return (y_prompt, y_sample, jnp.stack(new_ckv, axis=1), jnp.stack(new_krope, axis=1),
            jnp.stack(new_sk, axis=1), jnp.stack(new_sv, axis=1), jnp.stack(new_st, axis=1))
```

```python
import functools

import numpy as np
import jax
import jax.numpy as jnp
from jax import lax
from jax.experimental import pallas as pl
from jax.experimental.pallas import tpu as pltpu

D_MODEL = 1024
BATCH = 16
SEQ = 256
DEPTH = 2
DEC_BATCH = 2
DEC_SEQ = 2048
PAST_LEN = 512
GRID_W = 64
ROPE_BASE = 10000.0
NORM_EPS = 1e-6
N_MOD = 9
D_FF = 2816
MLA_HEADS = 8
MLA_Q_LORA = 384
MLA_KV_LORA = 256
MLA_NOPE = 64
MLA_ROPE = 32
MLA_V = 64
SWA_HEADS = 8
SWA_KV_HEADS = 2
SWA_GROUP = SWA_HEADS // SWA_KV_HEADS
SWA_HD = 64
SWA_WINDOW = 128
GDN_HEADS = 4
GDN_DK = 128
GDN_DV = 128
GDN_CONV = 5
GDN_CHUNK = 64
GDN_QK_DIM = GDN_HEADS * GDN_DK
GDN_V_DIM = GDN_HEADS * GDN_DV
GDN_CONV_CH = 2 * GDN_QK_DIM + GDN_V_DIM
GDN_NH = 2 * GDN_HEADS

N_CTX = BATCH * SEQ
N_LAT = DEC_BATCH * DEC_SEQ
N_TOK = N_CTX + N_LAT
N_GROUPS = 1 + DEC_BATCH
COND_ROWS = 8

LANES = 128
VMEM_LIMIT_BYTES = 56 * 1024 * 1024

TM_FFN = 256
TM_PROJ = SEQ
TM_OUT = 512
TQ_MLA = 256
GDN_PAR = 2

BF = jnp.bfloat16
F32 = jnp.float32

_C_CQ = 0
_C_CKV = _C_CQ + MLA_Q_LORA
_C_SQ = _C_CKV + MLA_KV_LORA
_C_SQR = _C_SQ + SWA_HEADS * SWA_HD
_C_SK = _C_SQR + SWA_HEADS * SWA_HD
_C_SKR = _C_SK + SWA_KV_HEADS * SWA_HD
_C_SV = _C_SKR + SWA_KV_HEADS * SWA_HD
_C_G3 = _C_SV + SWA_KV_HEADS * SWA_HD
_C_GZ = _C_G3 + GDN_CONV_CH
_C_MA = _C_GZ + GDN_V_DIM
_C_MB = _C_MA + LANES
_C_END = _C_MB + LANES
_M_G = MLA_ROPE
_M_B = MLA_ROPE + GDN_NH
_M_END = MLA_ROPE + 2 * GDN_NH
_T_CS, _T_SS, _T_CA, _T_SA, _T_CQ, _T_SQ = (i * LANES for i in range(6))
_T_END = 6 * LANES


def _params(*sem):
    return pltpu.CompilerParams(dimension_semantics=sem, vmem_limit_bytes=VMEM_LIMIT_BYTES)


def _bdot(a, b):
    return jnp.dot(a.astype(BF), b.astype(BF), preferred_element_type=F32)


def _nt(a, b):
    return lax.dot_general(a, b, (((1,), (1,)), ((), ())), preferred_element_type=F32)


def _bdot_nt(a, b):
    return _nt(a.astype(BF), b.astype(BF))


def _bdot_tn(a, b):
    return lax.dot_general(a.astype(BF), b.astype(BF), (((0,), (0,)), ((), ())),
                           preferred_element_type=F32)


def _split2(a):
    hi = a.astype(BF)
    lo = (a - hi.astype(F32)).astype(BF)
    return hi, lo


def _split3(a):
    b1 = a.astype(BF)
    r = a - b1.astype(F32)
    b2 = r.astype(BF)
    b3 = (r - b2.astype(F32)).astype(BF)
    return b1, b2, b3


def _dot_3pass(a, b):
    ah, al = _split2(a)
    bh, bl = _split2(b)
    return (jnp.dot(ah, bh, preferred_element_type=F32)
            + (jnp.dot(ah, bl, preferred_element_type=F32)
               + jnp.dot(al, bh, preferred_element_type=F32)))


def _silu(x):
    return x / (1.0 + jnp.exp(-x))


def _rms(x, gain, eps=NORM_EPS):
    return x * lax.rsqrt(jnp.mean(x * x, axis=-1, keepdims=True) + eps) * gain


def _group_of_row(r):
    return jnp.where(r < N_CTX, 0, 1 + (r - N_CTX) // DEC_SEQ)


def _any_spec():
    return pl.BlockSpec(memory_space=pl.ANY)


def _adaln_kernel(c_ref, w_ref, b_ref, o_ref):
    o_ref[0] = _bdot(_silu(c_ref[...]), w_ref[0]) + b_ref[0]


def _adaln(cond, w_ada, b_ada, tn=1536):
    n = N_MOD * D_MODEL
    return pl.pallas_call(
        _adaln_kernel,
        grid=(DEPTH, n // tn),
        in_specs=[pl.BlockSpec((COND_ROWS, D_MODEL), lambda l, j: (0, 0)),
                  pl.BlockSpec((1, D_MODEL, tn), lambda l, j: (l, 0, j)),
                  pl.BlockSpec((1, 1, tn), lambda l, j: (l, 0, j))],
        out_specs=pl.BlockSpec((1, COND_ROWS, tn), lambda l, j: (l, 0, j)),
        out_shape=jax.ShapeDtypeStruct((DEPTH, COND_ROWS, n), F32),
        compiler_params=_params("parallel", "parallel"),
        name="adaln",
    )(cond, w_ada, b_ada.reshape(DEPTH, 1, n))


def _mod_spec(l, which, tm):
    return pl.BlockSpec((1, 1, 1, 3 * D_MODEL), lambda i: (l, _group_of_row(i * tm), 0, which))


def _ffn_kernel(*refs, n_in, n_a_tiles, final):
    x_refs = refs[:n_in]
    mod_ref, gain_ref, w1_ref, w2_ref = refs[n_in:n_in + 4]
    rest = refs[n_in + 4:]
    i = pl.program_id(0)
    if n_in == 2:
        x = jnp.where(i < n_a_tiles, x_refs[0][...], x_refs[1][...])
    else:
        x = x_refs[0][...]
    mod = mod_ref[0, 0]
    shift, scale, gate = mod[:, :D_MODEL], mod[:, D_MODEL:2 * D_MODEL], mod[:, 2 * D_MODEL:]
    h = _rms(x, gain_ref[0]) * (1.0 + scale) + shift
    gu = jnp.dot(h.astype(BF), w1_ref[0], preferred_element_type=F32)
    a = _silu(gu[:, :D_FF]) * gu[:, D_FF:]
    y = x + gate * (0.5 * jnp.dot(a.astype(BF), w2_ref[0], preferred_element_type=F32))
    if final:
        fg_ref, oa_ref, ob_ref = rest
        yn = _rms(y, fg_ref[...])

        @pl.when(i < n_a_tiles)
        def _():
            oa_ref[...] = yn

        @pl.when(i >= n_a_tiles)
        def _():
            ob_ref[...] = yn
    else:
        rest[0][...] = y


def _ffn(xs, mods, l, which, gain, w1, w2, final_gain=None, tm=TM_FFN):
    n_a = N_CTX // tm
    row = lambda i: (i, 0)
    first = lambda i: (jnp.minimum(i, n_a - 1), 0)
    second = lambda i: (jnp.maximum(i - n_a, 0), 0)
    lay3 = lambda i: (l, 0, 0)
    x_specs = ([pl.BlockSpec((tm, D_MODEL), row)] if len(xs) == 1
               else [pl.BlockSpec((tm, D_MODEL), first), pl.BlockSpec((tm, D_MODEL), second)])
    in_specs = x_specs + [_mod_spec(l, which, tm),
                          pl.BlockSpec((1, 1, D_MODEL), lay3),
                          pl.BlockSpec((1, D_MODEL, 2 * D_FF), lay3),
                          pl.BlockSpec((1, D_FF, D_MODEL), lay3)]
    args = list(xs) + [mods, gain.reshape(DEPTH, 1, D_MODEL), w1, w2]
    if final_gain is None:
        out_specs = pl.BlockSpec((tm, D_MODEL), row)
        out_shape = jax.ShapeDtypeStruct((N_TOK, D_MODEL), F32)
    else:
        in_specs.append(pl.BlockSpec((1, D_MODEL), lambda i: (0, 0)))
        args.append(final_gain.reshape(1, D_MODEL))
        out_specs = [pl.BlockSpec((tm, D_MODEL), first), pl.BlockSpec((tm, D_MODEL), second)]
        out_shape = [jax.ShapeDtypeStruct((N_CTX, D_MODEL), F32), jax.ShapeDtypeStruct((N_LAT, D_MODEL), F32)]
    return pl.pallas_call(
        functools.partial(_ffn_kernel, n_in=len(xs), n_a_tiles=n_a, final=final_gain is not None),
        grid=(N_TOK // tm,),
        in_specs=in_specs, out_specs=out_specs, out_shape=out_shape,
        compiler_params=_params("arbitrary" if final_gain is not None else "parallel"),
        name="ffn",
    )(*args)


def _kv_expand(ckv_n, misc, wk, wv):
    kin = jnp.concatenate([ckv_n, misc], axis=1).astype(BF)
    kk = jnp.dot(kin, wk, preferred_element_type=F32)
    vv = jnp.dot(ckv_n.astype(BF), wv, preferred_element_type=F32)
    return kk.astype(BF), vv.astype(BF)


def _proj_kernel(*refs, n_ctx_tiles):
    (x_ref, mod_ref, gain_ref, win_ref, qg_ref, wqb_ref, kvg_ref, wk_ref, wv_ref, tab_ref,
     alog_ref, dtb_ref) = refs[:12]
    q_ref, ckv_ref, misc_ref, kmla_ref, vmla_ref, sq_ref, sk_ref, sv_ref, g3_ref, gz_ref = refs[-10:]
    i = pl.program_id(0)
    x = x_ref[...]
    mod = mod_ref[0, 0]
    shift, scale = mod[:, :D_MODEL], mod[:, D_MODEL:2 * D_MODEL]
    h = _rms(x, gain_ref[0]) * (1.0 + scale) + shift
    u = jnp.dot(h.astype(BF), win_ref[0], preferred_element_type=F32)
    tab = tab_ref[...]

    qn = _rms(u[:, _C_CQ:_C_CKV], qg_ref[0])
    q2 = jnp.dot(qn.astype(BF), wqb_ref[0], preferred_element_type=F32)
    nq = MLA_HEADS * LANES
    cosq = jnp.concatenate([tab[:, _T_CQ:_T_CQ + LANES]] * MLA_HEADS, axis=1)
    sinq = jnp.concatenate([tab[:, _T_SQ:_T_SQ + LANES]] * MLA_HEADS, axis=1)
    q_ref[...] = (q2[:, :nq] * cosq + q2[:, nq:] * sinq).astype(BF)

    ckv_n = _rms(u[:, _C_CKV:_C_SQ], kvg_ref[0])

    @pl.when(i < n_ctx_tiles)
    def _():
        ckv_ref[0, 0] = ckv_n

    m = (u[:, _C_MA:_C_MB] * tab[:, _T_CA:_T_CA + LANES] + u[:, _C_MB:_C_END] * tab[:, _T_SA:_T_SA + LANES])
    lane = lax.broadcasted_iota(jnp.int32, m.shape, 1)
    z = m + dtb_ref[0]
    softplus = jnp.maximum(z, 0.0) + jnp.log(1.0 + jnp.exp(-jnp.abs(z)))
    decay = -jnp.exp(alog_ref[0]) * softplus
    strength = 1.0 / (1.0 + jnp.exp(-m))
    misc = jnp.where((lane >= _M_G) & (lane < _M_B), decay,
                     jnp.where((lane >= _M_B) & (lane < _M_END), strength, m))
    misc_ref[...] = misc

    kk, vv = _kv_expand(ckv_n, misc, wk_ref[0], wv_ref[0])
    kmla_ref[...] = kk
    vmla_ref[...] = vv

    n_sq = SWA_HEADS * SWA_HD
    cos_s = tab[:, _T_CS:_T_CS + LANES]
    sin_s = tab[:, _T_SS:_T_SS + LANES]
    cos4 = jnp.concatenate([cos_s] * (n_sq // LANES), axis=1)
    sin4 = jnp.concatenate([sin_s] * (n_sq // LANES), axis=1)
    sq = u[:, _C_SQ:_C_SQR] * cos4 + u[:, _C_SQR:_C_SK] * sin4
    sq_ref[...] = (sq * (SWA_HD ** -0.5)).astype(BF)
    sk_ref[...] = u[:, _C_SK:_C_SKR] * cos_s + u[:, _C_SKR:_C_SV] * sin_s
    sv_ref[...] = u[:, _C_SV:_C_G3]
    g3_ref[...] = u[:, _C_G3:_C_GZ]
    gz_ref[...] = u[:, _C_GZ:_C_MA]


def _proj(x, mods, l, gain, win, qg, wqb, kvg, wk, wv, tab, alog, dtb, ckv_prev, tm=TM_PROJ):
    assert tm == SEQ
    n_ctx_tiles = N_CTX // tm
    lat_tiles = DEC_SEQ // tm
    lay3 = lambda i: (l, 0, 0)
    row = lambda i: (i, 0)

    def tab_map(i):
        return (jnp.where(i < n_ctx_tiles, 0, 1 + (i - n_ctx_tiles) % lat_tiles), 0)

    def lay_spec(a):
        return pl.BlockSpec((1,) + a.shape[1:], lay3)

    widths = [(MLA_HEADS * LANES, BF), None, (LANES, F32), (MLA_HEADS * LANES, BF),
              (MLA_HEADS * MLA_V, BF), (SWA_HEADS * SWA_HD, BF), (SWA_KV_HEADS * SWA_HD, F32),
              (SWA_KV_HEADS * SWA_HD, F32), (GDN_CONV_CH, F32), (GDN_V_DIM, F32)]
    out_specs = [pl.BlockSpec((tm, w[0]), row) if w else
                 pl.BlockSpec((1, 1, SEQ, MLA_KV_LORA), lambda i: (jnp.minimum(i, n_ctx_tiles - 1), l, 0, 0))
                 for w in widths]
    out_shape = [jax.ShapeDtypeStruct((N_TOK, w[0]), w[1]) if w else
                 jax.ShapeDtypeStruct((BATCH, DEPTH, SEQ, MLA_KV_LORA), F32) for w in widths]
    qg, kvg = qg.reshape(DEPTH, 1, -1), kvg.reshape(DEPTH, 1, -1)
    args = [x, mods, gain.reshape(DEPTH, 1, D_MODEL), win, qg, wqb, kvg, wk, wv, tab, alog, dtb]
    in_specs = [pl.BlockSpec((tm, D_MODEL), row), _mod_spec(l, 1, tm),
                pl.BlockSpec((1, 1, D_MODEL), lay3), lay_spec(win), lay_spec(qg), lay_spec(wqb),
                lay_spec(kvg), lay_spec(wk), lay_spec(wv), pl.BlockSpec((tm, _T_END), tab_map),
                lay_spec(alog), lay_spec(dtb)]
    aliases = {}
    if ckv_prev is not None:
        args.append(ckv_prev)
        in_specs.append(_any_spec())
        aliases = {len(args) - 1: 1}
    return pl.pallas_call(
        functools.partial(_proj_kernel, n_ctx_tiles=n_ctx_tiles),
        grid=(N_TOK // tm,),
        in_specs=in_specs, out_specs=out_specs, out_shape=out_shape,
        input_output_aliases=aliases,
        compiler_params=_params("arbitrary"),
        name="proj",
    )(*args)


def _kv_cache_kernel(ckv_ref, misc_ref, wk_ref, wv_ref, k_ref, v_ref):
    kk, vv = _kv_expand(ckv_ref[0, 0], misc_ref[0, 0], wk_ref[0], wv_ref[0])
    k_ref[...] = kk
    v_ref[...] = vv


def _kv_cache(cache_ckv, cache_misc, l, wk, wv):
    lay3 = lambda b: (l, 0, 0)
    n = DEC_BATCH * PAST_LEN
    return pl.pallas_call(
        _kv_cache_kernel,
        grid=(DEC_BATCH,),
        in_specs=[pl.BlockSpec((1, 1, PAST_LEN, MLA_KV_LORA), lambda b: (b, l, 0, 0)),
                  pl.BlockSpec((1, 1, PAST_LEN, LANES), lambda b: (b, l, 0, 0)),
                  pl.BlockSpec((1,) + wk.shape[1:], lay3),
                  pl.BlockSpec((1,) + wv.shape[1:], lay3)],
        out_specs=[pl.BlockSpec((PAST_LEN, MLA_HEADS * LANES), lambda b: (b, 0)),
                   pl.BlockSpec((PAST_LEN, MLA_HEADS * MLA_V), lambda b: (b, 0))],
        out_shape=[jax.ShapeDtypeStruct((n, MLA_HEADS * LANES), BF),
                   jax.ShapeDtypeStruct((n, MLA_HEADS * MLA_V), BF)],
        compiler_params=_params("parallel"),
        name="kv_cache",
    )(cache_ckv, cache_misc, wk, wv)


def _mla_kernel(*refs, n_seg):
    q_ref = refs[0]
    k_refs = refs[1:1 + n_seg]
    v_refs = refs[1 + n_seg:1 + 2 * n_seg]
    o_ref = refs[-1]
    scale = (MLA_NOPE + MLA_ROPE) ** -0.5
    tq = q_ref.shape[0]
    lane = lax.broadcasted_iota(jnp.int32, (tq, LANES), 1)
    for pair in range(MLA_HEADS // 2):
        outs = []
        for h in (2 * pair, 2 * pair + 1):
            qh = q_ref[:, h * LANES:(h + 1) * LANES]
            ss = [_bdot_nt(qh, k[:, h * LANES:(h + 1) * LANES]) * scale for k in k_refs]
            mx = functools.reduce(jnp.maximum, [jnp.max(s, axis=-1, keepdims=True) for s in ss])
            ps = [jnp.exp(s - mx) for s in ss]
            den = functools.reduce(jnp.add, [jnp.sum(p, axis=-1, keepdims=True) for p in ps])
            acc = functools.reduce(jnp.add, [_bdot(p, v[:, pair * LANES:(pair + 1) * LANES])
                                             for p, v in zip(ps, v_refs)])
            outs.append(acc / den)
        o_ref[:, pair * LANES:(pair + 1) * LANES] = jnp.where(lane < MLA_V, outs[0], outs[1]).astype(BF)


def _mla_attend(q, ks, vs, n_batch, t, row0, k_blocks, prev, tq=TQ_MLA):
    n_seg = len(ks)
    qb0 = row0 // tq
    tiles = t // tq
    q_map = lambda b, i: (qb0 + b * tiles + i, 0)
    in_specs = [pl.BlockSpec((tq, MLA_HEADS * LANES), q_map)]
    for (b0, s), width in ([(kb, MLA_HEADS * LANES) for kb in k_blocks]
                           + [(kb, MLA_HEADS * MLA_V) for kb in k_blocks]):
        in_specs.append(pl.BlockSpec((s, width), functools.partial(lambda b, i, b0: (b0 + b, 0), b0=b0)))
    args = [q, *ks, *vs]
    aliases = {}
    if prev is not None:
        args.append(prev)
        in_specs.append(_any_spec())
        aliases = {len(args) - 1: 0}
    return pl.pallas_call(
        functools.partial(_mla_kernel, n_seg=n_seg),
        grid=(n_batch, tiles),
        in_specs=in_specs,
        out_specs=pl.BlockSpec((tq, MLA_HEADS * MLA_V), q_map),
        out_shape=jax.ShapeDtypeStruct((N_TOK, MLA_HEADS * MLA_V), BF),
        input_output_aliases=aliases,
        compiler_params=_params("parallel", "parallel"),
        name="mla_attend",
    )(*args)


def _gqa_heads(q_ref, k_segs, v_segs, masks, sink_ref, o_ref):
    tq = q_ref.shape[0]
    lane_q = lax.broadcasted_iota(jnp.int32, (tq, LANES), 1)
    lo_q = lane_q < SWA_HD
    k_roll = [pltpu.roll(k, SWA_HD, 1) for k in k_segs]
    v_roll = [pltpu.roll(v, SWA_HD, 1) for v in v_segs]
    for kvh in range(SWA_KV_HEADS):
        tiles = [q_ref[:, (kvh * 2 + j) * LANES:(kvh * 2 + j + 1) * LANES] for j in range(2)]
        zero = jnp.zeros_like(tiles[0])
        qs = jnp.concatenate([jnp.where(lo_q, tiles[0], zero), jnp.where(lo_q, zero, tiles[0]),
                              jnp.where(lo_q, tiles[1], zero), jnp.where(lo_q, zero, tiles[1])], axis=0)
        sink = jnp.concatenate(
            [jnp.broadcast_to(sink_ref[0, kvh * SWA_GROUP + g:kvh * SWA_GROUP + g + 1, 0:1], (tq, 1))
             for g in range(SWA_GROUP)], axis=0)
        mx = sink
        ss = []
        for k, kr, msk in zip(k_segs, k_roll, masks):
            lane_k = lax.broadcasted_iota(jnp.int32, k.shape, 1)
            first = (lane_k < SWA_HD) == (kvh == 0)
            kd = jnp.where(first, k, kr)
            s = _bdot_nt(qs, kd)
            if msk is not None:
                s = jnp.where(msk, s, -1e30)
            ss.append(s)
            mx = jnp.maximum(mx, jnp.max(s, axis=-1, keepdims=True))
        den = jnp.exp(sink - mx)
        acc = None
        for s, v, vr in zip(ss, v_segs, v_roll):
            lane_v = lax.broadcasted_iota(jnp.int32, v.shape, 1)
            first = (lane_v < SWA_HD) == (kvh == 0)
            vd = jnp.where(first, v, vr)
            p = jnp.exp(s - mx)
            den = den + jnp.sum(p, axis=-1, keepdims=True)
            pv = _bdot(p, vd)
            acc = pv if acc is None else acc + pv
        o = acc / den
        for j in range(2):
            o_ref[:, (kvh * 2 + j) * LANES:(kvh * 2 + j + 1) * LANES] = jnp.where(
                lo_q, o[(2 * j) * tq:(2 * j + 1) * tq], o[(2 * j + 1) * tq:(2 * j + 2) * tq]).astype(BF)


def _swa_ctx_kernel(q_ref, k_ref, v_ref, sink_ref, o_ref):
    _gqa_heads(q_ref, [k_ref[...]], [v_ref[...]], [None], sink_ref, o_ref)


def _swa_ctx(sq, sk, sv, sink_rows, l):
    return pl.pallas_call(
        _swa_ctx_kernel,
        grid=(BATCH,),
        in_specs=[pl.BlockSpec((SEQ, SWA_HEADS * SWA_HD), lambda b: (b, 0)),
                  pl.BlockSpec((SEQ, LANES), lambda b: (b, 0)),
                  pl.BlockSpec((SEQ, LANES), lambda b: (b, 0)),
                  pl.BlockSpec((1, SWA_HEADS, LANES), lambda b: (l, 0, 0))],
        out_specs=pl.BlockSpec((SEQ, SWA_HEADS * SWA_HD), lambda b: (b, 0)),
        out_shape=jax.ShapeDtypeStruct((N_TOK, SWA_HEADS * SWA_HD), BF),
        compiler_params=_params("parallel"),
        name="swa_ctx",
    )(sq, sk, sv, sink_rows)


def _swa_lat_kernel(q_ref, kp_ref, kc_ref, kn_ref, vp_ref, vc_ref, vn_ref, kx_ref, vx_ref, sink_ref, prev_ref,
                    o_ref):
    w = SWA_WINDOW
    n = pl.program_id(1)
    nb = pl.num_programs(1)
    k_band = jnp.concatenate([kp_ref[...], kc_ref[...], kn_ref[...]], axis=0)
    v_band = jnp.concatenate([vp_ref[...], vc_ref[...], vn_ref[...]], axis=0)
    rows = SWA_GROUP * w
    r = lax.broadcasted_iota(jnp.int32, (rows, 3 * w), 0) & (w - 1)
    c = lax.broadcasted_iota(jnp.int32, (rows, 3 * w), 1)
    valid = (c >= r) & (c <= r + 2 * w) & ((c >= w) | (n > 0)) & ((c < 2 * w) | (n < nb - 1))
    _gqa_heads(q_ref, [k_band, kx_ref[0, 0]], [v_band, vx_ref[0, 0]], [valid, None], sink_ref, o_ref)


def _swa_lat(sq, sk, sv, k_cache, v_cache, sink_rows, l, prev):
    w = SWA_WINDOW
    nb = DEC_SEQ // w
    q0 = N_CTX // w

    def blk(d):
        return lambda b, n: (q0 + b * nb + jnp.clip(n + d, 0, nb - 1), 0)

    kv_specs = [pl.BlockSpec((w, LANES), blk(d)) for d in (-1, 0, 1)]
    cache_spec = pl.BlockSpec((1, 1, PAST_LEN, LANES), lambda b, n: (b, l, 0, 0))
    return pl.pallas_call(
        _swa_lat_kernel,
        grid=(DEC_BATCH, nb),
        in_specs=[pl.BlockSpec((w, SWA_HEADS * SWA_HD), blk(0))] + kv_specs + kv_specs
        + [cache_spec, cache_spec, pl.BlockSpec((1, SWA_HEADS, LANES), lambda b, n: (l, 0, 0)), _any_spec()],
        out_specs=pl.BlockSpec((w, SWA_HEADS * SWA_HD), blk(0)),
        out_shape=jax.ShapeDtypeStruct((N_TOK, SWA_HEADS * SWA_HD), BF),
        input_output_aliases={10: 0},
        compiler_params=_params("parallel", "parallel"),
        name="swa_lat",
    )(sq, sk, sk, sk, sv, sv, sv, k_cache, v_cache, sink_rows, prev)


def _gdn_conv_kernel(*refs):
    x_ref, w_ref, o_ref = refs[0], refs[1], refs[-1]
    x = x_ref[...]
    t = x.shape[0]
    w = w_ref[0]
    row = lax.broadcasted_iota(jnp.int32, x.shape, 0)
    half = GDN_CONV // 2
    acc = x * w[half:half + 1, :]
    for k in range(GDN_CONV):
        d = k - half
        if d == 0:
            continue
        xs = pltpu.roll(x, (-d) % t, 0)
        ok = (row + d >= 0) & (row + d < t)
        acc = acc + jnp.where(ok, xs, 0.0) * w[k:k + 1, :]
    y = _silu(acc)
    is_qk = pl.program_id(1) < 2
    cols = []
    for h in range(x.shape[1] // LANES):
        yh = y[:, h * LANES:(h + 1) * LANES]
        nrm = lax.rsqrt(jnp.sum(yh * yh, axis=-1, keepdims=True) + 1e-6)
        cols.append(yh * jnp.where(is_qk, nrm, 1.0))
    o_ref[...] = jnp.concatenate(cols, axis=1)


def _gdn_conv(g3, conv_w, l, row0, n_seq, t, prev):
    blk0 = row0 // t
    args = [g3, conv_w]
    in_specs = [pl.BlockSpec((t, GDN_QK_DIM), lambda s, j: (blk0 + s, j)),
                pl.BlockSpec((1, GDN_CONV, GDN_QK_DIM), lambda s, j: (l, 0, j))]
    aliases = {}
    if prev is not None:
        args.append(prev)
        in_specs.append(_any_spec())
        aliases = {2: 0}
    return pl.pallas_call(
        _gdn_conv_kernel,
        grid=(n_seq, 3),
        in_specs=in_specs,
        out_specs=pl.BlockSpec((t, GDN_QK_DIM), lambda s, j: (blk0 + s, j)),
        out_shape=jax.ShapeDtypeStruct((N_TOK, GDN_CONV_CH), F32),
        input_output_aliases=aliases,
        compiler_params=_params("parallel", "parallel"),
        name="gdn_conv",
    )(*args)


def _gdn_chunks(probs):
    c = GDN_CHUNK
    ri = lax.broadcasted_iota(jnp.int32, (c, c), 0)
    ci = lax.broadcasted_iota(jnp.int32, (c, c), 1)
    lower_incl, upper_incl = ri >= ci, ri <= ci
    eye = (ri == ci).astype(F32)
    n = len(probs)
    incl = [upper_incl if p["backward"] else lower_incl for p in probs]
    incl_t = [lower_incl if p["backward"] else upper_incl for p in probs]
    strict = [(ri < ci) if p["backward"] else (ri > ci) for p in probs]
    gc_col = [jnp.sum(jnp.where(incl[i], probs[i]["g_row"], 0.0), axis=1, keepdims=True) for i in range(n)]
    gc_row = [jnp.sum(jnp.where(incl_t[i], probs[i]["g_col"], 0.0), axis=0, keepdims=True) for i in range(n)]
    decay = [jnp.where(incl[i], jnp.exp(gc_col[i] - gc_row[i]), 0.0) for i in range(n)]
    q = [p["q"] * (GDN_DK ** -0.5) for p in probs]
    kb = [p["k"] * p["beta"] for p in probs]
    kk = [_bdot_nt(kb[i], probs[i]["k"]) for i in range(n)]
    qk = [_bdot_nt(q[i], probs[i]["k"]) for i in range(n)]
    pw = [jnp.where(strict[i], -(kk[i] * decay[i]), 0.0) for i in range(n)]
    inv = [eye + m for m in pw]
    for _ in range(int(np.log2(c)) - 1):
        pw = [_dot_3pass(m, m) for m in pw]
        inv = [inv[i] + _dot_3pass(inv[i], pw[i]) for i in range(n)]
    e_col = [jnp.exp(g) for g in gc_col]
    uw = [_bdot(inv[i], jnp.concatenate([probs[i]["v"] * probs[i]["beta"], kb[i] * e_col[i]], axis=1))
          for i in range(n)]
    a = [jnp.where(incl[i], qk[i] * decay[i], 0.0) for i in range(n)]
    g_last = [gc_col[i][0:1, :] if probs[i]["backward"] else gc_col[i][c - 1:c, :] for i in range(n)]
    k_dec = [probs[i]["k"] * jnp.exp(g_last[i] - gc_col[i]) for i in range(n)]
    wq = [_bdot(jnp.concatenate([uw[i][:, GDN_DV:], q[i] * e_col[i]], axis=0), probs[i]["state"])
          for i in range(n)]
    v_new = [uw[i][:, :GDN_DV] - wq[i][:c] for i in range(n)]
    o = [wq[i][c:] + _bdot(a[i], v_new[i]) for i in range(n)]
    s_new = [probs[i]["state"] * jnp.exp(g_last[i]) + _bdot_tn(k_dec[i], v_new[i]) for i in range(n)]
    return list(zip(o, s_new))


def _gate_rows(misc):
    shape = (2 * GDN_NH, LANES)
    sel = (lax.broadcasted_iota(jnp.int32, shape, 1)
           == lax.broadcasted_iota(jnp.int32, shape, 0) + _M_G).astype(BF)
    b1, b2, b3 = _split3(misc)
    return _nt(sel, b1) + (_nt(sel, b2) + _nt(sel, b3))


def _gdn_kernel(*refs, context, n_par):
    qf_ref, qb_ref, mf_ref, mb_ref = refs[:4]
    st_ref = refs[-1]
    if context:
        of_ref, ob_ref, sfin_ref = refs[-4:-1]
    else:
        s0_ref = refs[4]
        of_ref, ob_ref = refs[-3:-1]
    n = pl.program_id(1)

    @pl.when(n == 0)
    def _():
        if context:
            st_ref[...] = jnp.zeros_like(st_ref)
        else:
            st_ref[...] = s0_ref[:, 0]

    probs = []
    for s in range(n_par):
        for d, (q_ref, m_ref) in enumerate([(qf_ref, mf_ref), (qb_ref, mb_ref)]):
            qkv = q_ref[s]
            misc = m_ref[s]
            rows = _gate_rows(misc)
            for h in range(GDN_HEADS):
                i = d * GDN_HEADS + h
                probs.append(dict(
                    q=qkv[:, h * GDN_DK:(h + 1) * GDN_DK],
                    k=qkv[:, GDN_QK_DIM + h * GDN_DK:GDN_QK_DIM + (h + 1) * GDN_DK],
                    v=qkv[:, 2 * GDN_QK_DIM + h * GDN_DV:2 * GDN_QK_DIM + (h + 1) * GDN_DV],
                    g_col=misc[:, _M_G + i:_M_G + i + 1], g_row=rows[i:i + 1, :],
                    beta=misc[:, _M_B + i:_M_B + i + 1],
                    state=st_ref[s, i], backward=(d == 1)))
    results = _gdn_chunks(probs)
    for s in range(n_par):
        for d, o_ref in enumerate([of_ref, ob_ref]):
            for h in range(GDN_HEADS):
                i = d * GDN_HEADS + h
                o, s_new = results[s * GDN_NH + i]
                o_ref[s, :, h * GDN_DV:(h + 1) * GDN_DV] = o
                st_ref[s, i] = s_new

    if context:
        @pl.when(n == pl.num_programs(1) - 1)
        def _():
            sfin_ref[:, 0] = st_ref[...]


def _gdn(qkv, misc, l, t, n_seq, seq0, s0, prevs, n_par=GDN_PAR):
    c = GDN_CHUNK
    nc = t // c
    context = s0 is None
    sb0 = seq0 // n_par
    n_all = N_TOK // t
    fwd = lambda s, n: (sb0 + s, n, 0)
    bwd = lambda s, n: (sb0 + s, nc - 1 - n, 0)
    st_spec = pl.BlockSpec((n_par, 1, GDN_NH, GDN_DK, GDN_DV), lambda s, n: (s, l, 0, 0, 0))
    qkv3 = qkv.reshape(n_all, t, GDN_CONV_CH)
    misc3 = misc.reshape(n_all, t, LANES)
    in_specs = [pl.BlockSpec((n_par, c, GDN_CONV_CH), fwd), pl.BlockSpec((n_par, c, GDN_CONV_CH), bwd),
                pl.BlockSpec((n_par, c, LANES), fwd), pl.BlockSpec((n_par, c, LANES), bwd)]
    args = [qkv3, qkv3, misc3, misc3]
    if not context:
        in_specs.append(st_spec)
        args.append(s0)
    out_specs = [pl.BlockSpec((n_par, c, GDN_V_DIM), fwd), pl.BlockSpec((n_par, c, GDN_V_DIM), bwd)]
    out_shape = [jax.ShapeDtypeStruct((n_all, t, GDN_V_DIM), F32)] * 2
    if context:
        out_specs.append(st_spec)
        out_shape.append(jax.ShapeDtypeStruct((n_seq, DEPTH, GDN_NH, GDN_DK, GDN_DV), F32))
    aliases = {}
    for j, p in enumerate(prevs):
        if p is not None:
            aliases[len(args)] = j
            args.append(p.reshape(out_shape[j].shape))
            in_specs.append(_any_spec())
    return pl.pallas_call(
        functools.partial(_gdn_kernel, context=context, n_par=n_par),
        grid=(n_seq // n_par, nc),
        in_specs=in_specs, out_specs=out_specs, out_shape=out_shape,
        input_output_aliases=aliases,
        scratch_shapes=[pltpu.VMEM((n_par, GDN_NH, GDN_DK, GDN_DV), F32)],
        compiler_params=_params("parallel", "arbitrary"),
        name="gdn",
    )(*args)


def _out_kernel(x_ref, mod_ref, om_ref, os_ref, gf_ref, gb_ref, gz_ref, gn_ref, w_ref, o_ref):
    gate = mod_ref[0, 0][:, 2 * D_MODEL:]
    s = gf_ref[...] + gb_ref[...]
    gz = gz_ref[...]
    cols = []
    for h in range(GDN_HEADS):
        sh = s[:, h * GDN_DV:(h + 1) * GDN_DV]
        cols.append(_rms(sh, gn_ref[0]) * _silu(gz[:, h * GDN_DV:(h + 1) * GDN_DV]))
    og = jnp.concatenate(cols, axis=1)
    n_m = MLA_HEADS * MLA_V
    n_s = SWA_HEADS * SWA_HD
    y = (jnp.dot(om_ref[...], w_ref[0, :n_m, :], preferred_element_type=F32)
         + jnp.dot(os_ref[...], w_ref[0, n_m:n_m + n_s, :], preferred_element_type=F32)
         + jnp.dot(og.astype(BF), w_ref[0, n_m + n_s:, :], preferred_element_type=F32))
    o_ref[...] = x_ref[...] + gate * y


def _out_proj(x, mods, l, o_mla, o_swa, o_gf, o_gb, gz, gdn_norm, w_out, tm=TM_OUT):
    row = lambda i: (i, 0)
    lay3 = lambda i: (l, 0, 0)
    return pl.pallas_call(
        _out_kernel,
        grid=(N_TOK // tm,),
        in_specs=[pl.BlockSpec((tm, D_MODEL), row),
                  _mod_spec(l, 1, tm),
                  pl.BlockSpec((tm, MLA_HEADS * MLA_V), row),
                  pl.BlockSpec((tm, SWA_HEADS * SWA_HD), row),
                  pl.BlockSpec((tm, GDN_V_DIM), row),
                  pl.BlockSpec((tm, GDN_V_DIM), row),
                  pl.BlockSpec((tm, GDN_V_DIM), row),
                  pl.BlockSpec((1, 1, GDN_DV), lay3),
                  pl.BlockSpec((1,) + w_out.shape[1:], lay3)],
        out_specs=pl.BlockSpec((tm, D_MODEL), row),
        out_shape=jax.ShapeDtypeStruct((N_TOK, D_MODEL), F32),
        compiler_params=_params("parallel"),
        name="out_proj",
    )(x, mods, o_mla, o_swa, o_gf, o_gb, gz, gdn_norm.reshape(DEPTH, 1, GDN_DV), w_out)


def _rot_columns(dim, n_heads):
    quarter = dim // 4
    j = np.arange(dim)
    even = (j // quarter) % 2 == 0
    src = np.where(even, j + quarter, j - quarter)
    sign = np.where(even, -1.0, 1.0).astype(np.float32)
    idx = (np.arange(n_heads)[:, None] * dim + src[None, :]).reshape(-1)
    return idx, np.tile(sign, n_heads)


def _axial_rope(n_tokens, dim):
    f32 = np.float32
    rows = n_tokens // GRID_W
    row = np.repeat(np.arange(rows, dtype=f32), GRID_W)
    col = np.tile(np.arange(GRID_W, dtype=f32), rows)
    axis_dim = dim // 2
    inv_freq = (f32(1.0) / (f32(ROPE_BASE) ** (np.arange(0, axis_dim, 2, dtype=f32) / f32(axis_dim)))).astype(f32)
    ang_r = row[:, None] * inv_freq[None, :]
    ang_c = col[:, None] * inv_freq[None, :]
    ang = np.concatenate([ang_r, ang_r, ang_c, ang_c], axis=-1).astype(f32)
    return np.cos(ang).astype(f32), np.sin(ang).astype(f32)


def _rope_table(tm):
    cos_m, sin_m = _axial_rope(DEC_SEQ, MLA_ROPE)
    cos_s, sin_s = _axial_rope(DEC_SEQ, SWA_HD)
    t = DEC_SEQ
    one = lambda w: np.ones((t, w), np.float32)
    zero = lambda w: np.zeros((t, w), np.float32)
    lat = np.concatenate([
        cos_s, cos_s, sin_s, sin_s,
        cos_m, one(LANES - MLA_ROPE), sin_m, zero(LANES - MLA_ROPE),
        one(MLA_NOPE), cos_m, one(LANES - MLA_NOPE - MLA_ROPE),
        zero(MLA_NOPE), sin_m, zero(LANES - MLA_NOPE - MLA_ROPE)], axis=1)
    ident_row = np.concatenate([np.ones(LANES), np.zeros(LANES)] * 3).astype(np.float32)
    ident = np.broadcast_to(ident_row[None, :], (tm, _T_END))
    return jnp.asarray(np.concatenate([ident, lat], axis=0))


def _mixer_weights(w_in, mla_w_qb, mla_w_kvb):
    nl = DEPTH
    offs = np.cumsum([0, MLA_Q_LORA, MLA_KV_LORA, MLA_ROPE, SWA_HEADS * SWA_HD, SWA_KV_HEADS * SWA_HD,
                      SWA_KV_HEADS * SWA_HD, GDN_CONV_CH, GDN_V_DIM, 2 * GDN_NH])
    cq, ckv, krope, sq, sk, sv, g3, gz, gates = [w_in[:, :, offs[i]:offs[i + 1]] for i in range(9)]
    idx_s8, sgn_s8 = _rot_columns(SWA_HD, SWA_HEADS)
    idx_s2, sgn_s2 = _rot_columns(SWA_HD, SWA_KV_HEADS)
    idx_m, sgn_m = _rot_columns(MLA_ROPE, 1)
    zeros = lambda n: jnp.zeros((nl, D_MODEL, n), F32)
    misc_a = jnp.concatenate([krope, gates, zeros(LANES - _M_END)], axis=2)
    misc_b = jnp.concatenate([krope[:, :, idx_m] * sgn_m, zeros(LANES - MLA_ROPE)], axis=2)
    win = jnp.concatenate([cq, ckv, sq, sq[:, :, idx_s8] * sgn_s8, sk, sk[:, :, idx_s2] * sgn_s2, sv, g3, gz,
                           misc_a, misc_b], axis=2).astype(BF)

    r = MLA_Q_LORA
    wq = mla_w_qb.reshape(nl, r, MLA_HEADS, MLA_NOPE + MLA_ROPE)
    nope, rope = wq[..., :MLA_NOPE], wq[..., MLA_NOPE:]
    pad = LANES - MLA_NOPE - MLA_ROPE
    z = lambda n: jnp.zeros((nl, r, MLA_HEADS, n), F32)
    qa = jnp.concatenate([nope, rope, z(pad)], axis=-1).reshape(nl, r, MLA_HEADS * LANES)
    qb = jnp.concatenate([z(MLA_NOPE), rope[..., idx_m] * sgn_m, z(pad)], axis=-1).reshape(nl, r, MLA_HEADS * LANES)
    wqb = jnp.concatenate([qa, qb], axis=2).astype(BF)

    kvb = mla_w_kvb.reshape(nl, MLA_KV_LORA, MLA_HEADS, MLA_NOPE + MLA_V)
    k_nope = jnp.concatenate([kvb[..., :MLA_NOPE],
                              jnp.zeros((nl, MLA_KV_LORA, MLA_HEADS, LANES - MLA_NOPE), F32)],
                             axis=-1).reshape(nl, MLA_KV_LORA, MLA_HEADS * LANES)
    place = np.zeros((LANES, MLA_HEADS, LANES), np.float32)
    for i in range(MLA_ROPE):
        place[i, :, MLA_NOPE + i] = 1.0
    place = jnp.broadcast_to(jnp.asarray(place.reshape(1, LANES, MLA_HEADS * LANES)),
                             (nl, LANES, MLA_HEADS * LANES))
    wk = jnp.concatenate([k_nope, place], axis=1).astype(BF)
    wv = kvb[..., MLA_NOPE:].reshape(nl, MLA_KV_LORA, MLA_HEADS * MLA_V).astype(BF)
    return win, wqb, wk, wv


def _misc_rows(vals):
    rows = jnp.zeros((DEPTH, 1, LANES), F32)
    return rows.at[:, 0, _M_G:_M_B].set(vals.reshape(DEPTH, GDN_NH).astype(F32))


def kernel(x_prompt, x_sample, cache_mla_ckv, cache_mla_krope, cache_swa_k, cache_swa_v, state_gdn, c, c_ctx,
           w_ada, b_ada, norm_ffn1, ffn1_w1, ffn1_w2, norm_mix, w_in, mla_q_norm, mla_w_qb, mla_kv_norm,
           mla_w_kvb, swa_sink, gdn_conv_w, gdn_a_log, gdn_dt_bias, gdn_norm, w_out, norm_ffn2, ffn2_w1,
           ffn2_w2, final_norm):
    cond = jnp.concatenate([c_ctx[None, :], c, jnp.zeros((COND_ROWS - N_GROUPS, D_MODEL), F32)], axis=0)
    mods = _adaln(cond, w_ada, b_ada)[:, :N_GROUPS].reshape(DEPTH, N_GROUPS, 1, N_MOD * D_MODEL)
    tab = _rope_table(TM_PROJ)
    win, wqb, wk, wv = _mixer_weights(w_in, mla_w_qb, mla_w_kvb)
    w11, w12, w21, w22, wo = (w.astype(BF) for w in (ffn1_w1, ffn1_w2, ffn2_w1, ffn2_w2, w_out))
    alog, dtb = _misc_rows(gdn_a_log), _misc_rows(gdn_dt_bias)
    sink_rows = jnp.broadcast_to(swa_sink[:, :, None], (DEPTH, SWA_HEADS, LANES))
    cache_misc = jnp.pad(cache_mla_krope, ((0, 0), (0, 0), (0, 0), (0, LANES - MLA_ROPE)))
    cache_k = cache_swa_k.reshape(DEC_BATCH, DEPTH, PAST_LEN, LANES)
    cache_v = cache_swa_v.reshape(DEC_BATCH, DEPTH, PAST_LEN, LANES)
    s0 = state_gdn.reshape(DEC_BATCH, DEPTH, GDN_NH, GDN_DK, GDN_DV)

    xs = [x_prompt.reshape(N_CTX, D_MODEL), x_sample.reshape(N_LAT, D_MODEL)]
    new_ckv = new_st = None
    new_krope, new_sk, new_sv = [], [], []
    for l in range(DEPTH):
        x = _ffn(xs, mods, l, 0, norm_ffn1, w11, w12)
        (q_mla, new_ckv, misc, k_mla, v_mla, sq, sk, sv, g3, gz) = _proj(
            x, mods, l, norm_mix, win, mla_q_norm, wqb, mla_kv_norm, wk, wv, tab, alog, dtb, new_ckv)

        k_c, v_c = _kv_cache(cache_mla_ckv, cache_misc, l, wk, wv)
        o_mla = _mla_attend(q_mla, [k_mla], [v_mla], BATCH, SEQ, 0, [(0, SEQ)], None)
        o_mla = _mla_attend(q_mla, [k_mla, k_c], [v_mla, v_c], DEC_BATCH, DEC_SEQ, N_CTX,
                            [(N_CTX // DEC_SEQ, DEC_SEQ), (0, PAST_LEN)], o_mla)

        o_swa = _swa_ctx(sq, sk, sv, sink_rows, l)
        o_swa = _swa_lat(sq, sk, sv, cache_k, cache_v, sink_rows, l, o_swa)

        qkv = _gdn_conv(g3, gdn_conv_w, l, 0, BATCH, SEQ, None)
        qkv = _gdn_conv(g3, gdn_conv_w, l, N_CTX, DEC_BATCH, DEC_SEQ, qkv)
        o_gf, o_gb, new_st = _gdn(qkv, misc, l, SEQ, BATCH, 0, None, [None, None, new_st])
        o_gf, o_gb = _gdn(qkv, misc, l, DEC_SEQ, DEC_BATCH, N_CTX // DEC_SEQ, s0, [o_gf, o_gb])

        x = _out_proj(x, mods, l, o_mla, o_swa, o_gf.reshape(N_TOK, GDN_V_DIM), o_gb.reshape(N_TOK, GDN_V_DIM),
                      gz, gdn_norm, wo)
        if l + 1 < DEPTH:
            xs = [_ffn([x], mods, l, 2, norm_ffn2, w21, w22)]
        else:
            y_prompt, y_sample = _ffn([x], mods, l, 2, norm_ffn2, w21, w22, final_gain=final_norm)

        new_krope.append(misc[:N_CTX, :MLA_ROPE].reshape(BATCH, SEQ, MLA_ROPE))
        new_sk.append(sk[:N_CTX].reshape(BATCH, SEQ, SWA_KV_HEADS, SWA_HD))
        new_sv.append(sv[:N_CTX].reshape(BATCH, SEQ, SWA_KV_HEADS, SWA_HD))

    return (y_prompt.reshape(BATCH, SEQ, D_MODEL), y_sample.reshape(DEC_BATCH, DEC_SEQ, D_MODEL), new_ckv,
            jnp.stack(new_krope, axis=1), jnp.stack(new_sk, axis=1), jnp.stack(new_sv, axis=1),
            new_st.reshape(BATCH, DEPTH, 2, GDN_HEADS, GDN_DK, GDN_DV))
```

```python
import functools

import numpy as np
import jax
import jax.numpy as jnp
from jax import lax
from jax.experimental import pallas as pl
from jax.experimental.pallas import tpu as pltpu

D_MODEL = 1024
BATCH = 16
SEQ = 256
DEPTH = 2
DEC_BATCH = 2
DEC_SEQ = 2048
PAST_LEN = 512
GRID_W = 64
ROPE_BASE = 10000.0
NORM_EPS = 1e-6
N_MOD = 9
D_FF = 2816
MLA_HEADS = 8
MLA_Q_LORA = 384
MLA_KV_LORA = 256
MLA_NOPE = 64
MLA_ROPE = 32
MLA_V = 64
SWA_HEADS = 8
SWA_KV_HEADS = 2
SWA_GROUP = SWA_HEADS // SWA_KV_HEADS
SWA_HD = 64
SWA_WINDOW = 128
GDN_HEADS = 4
GDN_DK = 128
GDN_DV = 128
GDN_CONV = 5
GDN_CHUNK = 64
GDN_QK_DIM = GDN_HEADS * GDN_DK
GDN_V_DIM = GDN_HEADS * GDN_DV
GDN_CONV_CH = 2 * GDN_QK_DIM + GDN_V_DIM
GDN_NH = 2 * GDN_HEADS

N_CTX = BATCH * SEQ
N_LAT = DEC_BATCH * DEC_SEQ
N_TOK = N_CTX + N_LAT
N_GROUPS = 1 + DEC_BATCH
COND_ROWS = 8

LANES = 128
VMEM_LIMIT_BYTES = 56 * 1024 * 1024

TM_FFN = 256
TM_PROJ = SEQ
TM_OUT = 512
TQ_MLA = 256
GDN_PAR = 2

BF = jnp.bfloat16
F32 = jnp.float32
LOG2_E = 1.4426950408889634
MLA_Q_SCALE = (MLA_NOPE + MLA_ROPE) ** -0.5 * LOG2_E

_C_CQ = 0
_C_CKV = _C_CQ + MLA_Q_LORA
_C_SQ = _C_CKV + MLA_KV_LORA
_C_SQR = _C_SQ + SWA_HEADS * SWA_HD
_C_SK = _C_SQR + SWA_HEADS * SWA_HD
_C_SKR = _C_SK + SWA_KV_HEADS * SWA_HD
_C_SV = _C_SKR + SWA_KV_HEADS * SWA_HD
_C_G3 = _C_SV + SWA_KV_HEADS * SWA_HD
_C_GZ = _C_G3 + GDN_CONV_CH
_C_MA = _C_GZ + GDN_V_DIM
_C_MB = _C_MA + LANES
_C_END = _C_MB + LANES
_M_G = MLA_ROPE
_M_B = MLA_ROPE + GDN_NH
_M_END = MLA_ROPE + 2 * GDN_NH
_T_CS, _T_SS, _T_CA, _T_SA, _T_CQ, _T_SQ = (i * LANES for i in range(6))
_T_END = 6 * LANES


def _params(*sem):
    return pltpu.CompilerParams(dimension_semantics=sem, vmem_limit_bytes=VMEM_LIMIT_BYTES)


def _bdot(a, b):
    return jnp.dot(a.astype(BF), b.astype(BF), preferred_element_type=F32)


def _nt(a, b):
    return lax.dot_general(a, b, (((1,), (1,)), ((), ())), preferred_element_type=F32)


def _bdot_nt(a, b):
    return _nt(a.astype(BF), b.astype(BF))


def _bdot_tn(a, b):
    return lax.dot_general(a.astype(BF), b.astype(BF), (((0,), (0,)), ((), ())),
                           preferred_element_type=F32)


def _split2(a):
    hi = a.astype(BF)
    lo = (a - hi.astype(F32)).astype(BF)
    return hi, lo


def _split3(a):
    b1 = a.astype(BF)
    r = a - b1.astype(F32)
    b2 = r.astype(BF)
    b3 = (r - b2.astype(F32)).astype(BF)
    return b1, b2, b3


def _dot_3pass(a, b):
    ah, al = _split2(a)
    bh, bl = _split2(b)
    return (jnp.dot(ah, bh, preferred_element_type=F32)
            + (jnp.dot(ah, bl, preferred_element_type=F32)
               + jnp.dot(al, bh, preferred_element_type=F32)))


def _silu(x):
    return x / (1.0 + jnp.exp(-x))


def _rms(x, gain, eps=NORM_EPS):
    return x * lax.rsqrt(jnp.mean(x * x, axis=-1, keepdims=True) + eps) * gain


def _group_of_row(r):
    return jnp.where(r < N_CTX, 0, 1 + (r - N_CTX) // DEC_SEQ)


def _any_spec():
    return pl.BlockSpec(memory_space=pl.ANY)


def _adaln_kernel(c_ref, w_ref, b_ref, o_ref):
    o_ref[0] = _bdot(_silu(c_ref[...]), w_ref[0]) + b_ref[0]


def _adaln(cond, w_ada, b_ada, tn=1536):
    n = N_MOD * D_MODEL
    return pl.pallas_call(
        _adaln_kernel,
        grid=(DEPTH, n // tn),
        in_specs=[pl.BlockSpec((COND_ROWS, D_MODEL), lambda l, j: (0, 0)),
                  pl.BlockSpec((1, D_MODEL, tn), lambda l, j: (l, 0, j)),
                  pl.BlockSpec((1, 1, tn), lambda l, j: (l, 0, j))],
        out_specs=pl.BlockSpec((1, COND_ROWS, tn), lambda l, j: (l, 0, j)),
        out_shape=jax.ShapeDtypeStruct((DEPTH, COND_ROWS, n), F32),
        compiler_params=_params("parallel", "parallel"),
        name="adaln",
    )(cond, w_ada, b_ada.reshape(DEPTH, 1, n))


def _mod_spec(l, which, tm):
    return pl.BlockSpec((1, 1, 1, 3 * D_MODEL), lambda i: (l, _group_of_row(i * tm), 0, which))


def _ffn_kernel(*refs, n_in, n_a_tiles, final):
    x_refs = refs[:n_in]
    mod_ref, gain_ref, w1_ref, w2_ref = refs[n_in:n_in + 4]
    rest = refs[n_in + 4:]
    i = pl.program_id(0)
    if n_in == 2:
        x = jnp.where(i < n_a_tiles, x_refs[0][...], x_refs[1][...])
    else:
        x = x_refs[0][...]
    mod = mod_ref[0, 0]
    shift, scale, gate = mod[:, :D_MODEL], mod[:, D_MODEL:2 * D_MODEL], mod[:, 2 * D_MODEL:]
    h = _rms(x, gain_ref[0]) * (1.0 + scale) + shift
    gu = jnp.dot(h.astype(BF), w1_ref[0], preferred_element_type=F32)
    a = _silu(gu[:, :D_FF]) * gu[:, D_FF:]
    y = x + gate * (0.5 * jnp.dot(a.astype(BF), w2_ref[0], preferred_element_type=F32))
    if final:
        fg_ref, oa_ref, ob_ref = rest
        yn = _rms(y, fg_ref[...])

        @pl.when(i < n_a_tiles)
        def _():
            oa_ref[...] = yn

        @pl.when(i >= n_a_tiles)
        def _():
            ob_ref[...] = yn
    else:
        rest[0][...] = y


def _ffn(xs, mods, l, which, gain, w1, w2, final_gain=None, tm=TM_FFN):
    n_a = N_CTX // tm
    row = lambda i: (i, 0)
    first = lambda i: (jnp.minimum(i, n_a - 1), 0)
    second = lambda i: (jnp.maximum(i - n_a, 0), 0)
    lay3 = lambda i: (l, 0, 0)
    x_specs = ([pl.BlockSpec((tm, D_MODEL), row)] if len(xs) == 1
               else [pl.BlockSpec((tm, D_MODEL), first), pl.BlockSpec((tm, D_MODEL), second)])
    in_specs = x_specs + [_mod_spec(l, which, tm),
                          pl.BlockSpec((1, 1, D_MODEL), lay3),
                          pl.BlockSpec((1, D_MODEL, 2 * D_FF), lay3),
                          pl.BlockSpec((1, D_FF, D_MODEL), lay3)]
    args = list(xs) + [mods, gain.reshape(DEPTH, 1, D_MODEL), w1, w2]
    if final_gain is None:
        out_specs = pl.BlockSpec((tm, D_MODEL), row)
        out_shape = jax.ShapeDtypeStruct((N_TOK, D_MODEL), F32)
    else:
        in_specs.append(pl.BlockSpec((1, D_MODEL), lambda i: (0, 0)))
        args.append(final_gain.reshape(1, D_MODEL))
        out_specs = [pl.BlockSpec((tm, D_MODEL), first), pl.BlockSpec((tm, D_MODEL), second)]
        out_shape = [jax.ShapeDtypeStruct((N_CTX, D_MODEL), F32), jax.ShapeDtypeStruct((N_LAT, D_MODEL), F32)]
    return pl.pallas_call(
        functools.partial(_ffn_kernel, n_in=len(xs), n_a_tiles=n_a, final=final_gain is not None),
        grid=(N_TOK // tm,),
        in_specs=in_specs, out_specs=out_specs, out_shape=out_shape,
        compiler_params=_params("arbitrary" if final_gain is not None else "parallel"),
        name="ffn",
    )(*args)


def _kv_expand(ckv_n, misc, wk, wv):
    kin = jnp.concatenate([ckv_n, misc], axis=1).astype(BF)
    kk = jnp.dot(kin, wk, preferred_element_type=F32)
    vv = jnp.dot(ckv_n.astype(BF), wv, preferred_element_type=F32)
    return kk.astype(BF), vv.astype(BF)


def _proj_kernel(*refs, n_ctx_tiles):
    (x_ref, mod_ref, gain_ref, win_ref, qg_ref, wqb_ref, kvg_ref, wk_ref, wv_ref, tab_ref,
     alog_ref, dtb_ref) = refs[:12]
    q_ref, ckv_ref, misc_ref, kmla_ref, vmla_ref, sq_ref, sk_ref, sv_ref, g3_ref, gz_ref = refs[-10:]
    i = pl.program_id(0)
    x = x_ref[...]
    mod = mod_ref[0, 0]
    shift, scale = mod[:, :D_MODEL], mod[:, D_MODEL:2 * D_MODEL]
    h = _rms(x, gain_ref[0]) * (1.0 + scale) + shift
    u = jnp.dot(h.astype(BF), win_ref[0], preferred_element_type=F32)
    tab = tab_ref[...]

    qn = _rms(u[:, _C_CQ:_C_CKV], qg_ref[0])
    q2 = jnp.dot(qn.astype(BF), wqb_ref[0], preferred_element_type=F32)
    nq = MLA_HEADS * LANES
    cosq = jnp.concatenate([tab[:, _T_CQ:_T_CQ + LANES]] * MLA_HEADS, axis=1)
    sinq = jnp.concatenate([tab[:, _T_SQ:_T_SQ + LANES]] * MLA_HEADS, axis=1)
    q_ref[...] = ((q2[:, :nq] * cosq + q2[:, nq:] * sinq) * MLA_Q_SCALE).astype(BF)

    ckv_n = _rms(u[:, _C_CKV:_C_SQ], kvg_ref[0])

    @pl.when(i < n_ctx_tiles)
    def _():
        ckv_ref[0, 0] = ckv_n

    m = (u[:, _C_MA:_C_MB] * tab[:, _T_CA:_T_CA + LANES] + u[:, _C_MB:_C_END] * tab[:, _T_SA:_T_SA + LANES])
    lane = lax.broadcasted_iota(jnp.int32, m.shape, 1)
    z = m + dtb_ref[0]
    softplus = jnp.maximum(z, 0.0) + jnp.log(1.0 + jnp.exp(-jnp.abs(z)))
    decay = -jnp.exp(alog_ref[0]) * softplus
    strength = 1.0 / (1.0 + jnp.exp(-m))
    misc = jnp.where((lane >= _M_G) & (lane < _M_B), decay,
                     jnp.where((lane >= _M_B) & (lane < _M_END), strength, m))
    misc_ref[...] = misc

    kk, vv = _kv_expand(ckv_n, misc, wk_ref[0], wv_ref[0])
    kmla_ref[...] = kk
    vmla_ref[...] = vv

    n_sq = SWA_HEADS * SWA_HD
    cos_s = tab[:, _T_CS:_T_CS + LANES]
    sin_s = tab[:, _T_SS:_T_SS + LANES]
    cos4 = jnp.concatenate([cos_s] * (n_sq // LANES), axis=1)
    sin4 = jnp.concatenate([sin_s] * (n_sq // LANES), axis=1)
    sq = u[:, _C_SQ:_C_SQR] * cos4 + u[:, _C_SQR:_C_SK] * sin4
    sq_ref[...] = (sq * (SWA_HD ** -0.5 * LOG2_E)).astype(BF)
    sk_ref[...] = u[:, _C_SK:_C_SKR] * cos_s + u[:, _C_SKR:_C_SV] * sin_s
    sv_ref[...] = u[:, _C_SV:_C_G3]
    g3_ref[...] = u[:, _C_G3:_C_GZ]
    gz_ref[...] = u[:, _C_GZ:_C_MA]


def _proj(x, mods, l, gain, win, qg, wqb, kvg, wk, wv, tab, alog, dtb, ckv_prev, tm=TM_PROJ):
    assert tm == SEQ
    n_ctx_tiles = N_CTX // tm
    lat_tiles = DEC_SEQ // tm
    lay3 = lambda i: (l, 0, 0)
    row = lambda i: (i, 0)

    def tab_map(i):
        return (jnp.where(i < n_ctx_tiles, 0, 1 + (i - n_ctx_tiles) % lat_tiles), 0)

    def lay_spec(a):
        return pl.BlockSpec((1,) + a.shape[1:], lay3)

    widths = [(MLA_HEADS * LANES, BF), None, (LANES, F32), (MLA_HEADS * LANES, BF),
              (MLA_HEADS * MLA_V, BF), (SWA_HEADS * SWA_HD, BF), (SWA_KV_HEADS * SWA_HD, F32),
              (SWA_KV_HEADS * SWA_HD, F32), (GDN_CONV_CH, F32), (GDN_V_DIM, F32)]
    out_specs = [pl.BlockSpec((tm, w[0]), row) if w else
                 pl.BlockSpec((1, 1, SEQ, MLA_KV_LORA), lambda i: (jnp.minimum(i, n_ctx_tiles - 1), l, 0, 0))
                 for w in widths]
    out_shape = [jax.ShapeDtypeStruct((N_TOK, w[0]), w[1]) if w else
                 jax.ShapeDtypeStruct((BATCH, DEPTH, SEQ, MLA_KV_LORA), F32) for w in widths]
    qg, kvg = qg.reshape(DEPTH, 1, -1), kvg.reshape(DEPTH, 1, -1)
    args = [x, mods, gain.reshape(DEPTH, 1, D_MODEL), win, qg, wqb, kvg, wk, wv, tab, alog, dtb]
    in_specs = [pl.BlockSpec((tm, D_MODEL), row), _mod_spec(l, 1, tm),
                pl.BlockSpec((1, 1, D_MODEL), lay3), lay_spec(win), lay_spec(qg), lay_spec(wqb),
                lay_spec(kvg), lay_spec(wk), lay_spec(wv), pl.BlockSpec((tm, _T_END), tab_map),
                lay_spec(alog), lay_spec(dtb)]
    aliases = {}
    if ckv_prev is not None:
        args.append(ckv_prev)
        in_specs.append(_any_spec())
        aliases = {len(args) - 1: 1}
    return pl.pallas_call(
        functools.partial(_proj_kernel, n_ctx_tiles=n_ctx_tiles),
        grid=(N_TOK // tm,),
        in_specs=in_specs, out_specs=out_specs, out_shape=out_shape,
        input_output_aliases=aliases,
        compiler_params=_params("arbitrary"),
        name="proj",
    )(*args)


def _kv_cache_kernel(ckv_ref, misc_ref, wk_ref, wv_ref, k_ref, v_ref):
    kk, vv = _kv_expand(ckv_ref[0, 0], misc_ref[0, 0], wk_ref[0], wv_ref[0])
    k_ref[...] = kk
    v_ref[...] = vv


def _kv_cache(cache_ckv, cache_misc, l, wk, wv):
    lay3 = lambda b: (l, 0, 0)
    n = DEC_BATCH * PAST_LEN
    return pl.pallas_call(
        _kv_cache_kernel,
        grid=(DEC_BATCH,),
        in_specs=[pl.BlockSpec((1, 1, PAST_LEN, MLA_KV_LORA), lambda b: (b, l, 0, 0)),
                  pl.BlockSpec((1, 1, PAST_LEN, LANES), lambda b: (b, l, 0, 0)),
                  pl.BlockSpec((1,) + wk.shape[1:], lay3),
                  pl.BlockSpec((1,) + wv.shape[1:], lay3)],
        out_specs=[pl.BlockSpec((PAST_LEN, MLA_HEADS * LANES), lambda b: (b, 0)),
                   pl.BlockSpec((PAST_LEN, MLA_HEADS * MLA_V), lambda b: (b, 0))],
        out_shape=[jax.ShapeDtypeStruct((n, MLA_HEADS * LANES), BF),
                   jax.ShapeDtypeStruct((n, MLA_HEADS * MLA_V), BF)],
        compiler_params=_params("parallel"),
        name="kv_cache",
    )(cache_ckv, cache_misc, wk, wv)


def _softmax_numerator(s_ref, p_ref, sink=None, row_block=128):
    rows, cols = s_ref.shape
    mx = jnp.max(s_ref[...], axis=-1, keepdims=True)
    if sink is not None:
        mx = jnp.maximum(mx, sink)
    sums = []
    for r0 in range(0, rows, row_block):
        mb = jnp.broadcast_to(mx[r0:r0 + row_block], (row_block, LANES))
        part = jnp.zeros((row_block, LANES), F32)
        for c in range(0, cols, LANES):
            p = jnp.exp2(s_ref[r0:r0 + row_block, c:c + LANES] - mb)
            part = part + p
            p_ref[r0:r0 + row_block, c:c + LANES] = p.astype(BF)
        sums.append(jnp.sum(part, axis=-1, keepdims=True))
    den = jnp.concatenate(sums, axis=0)
    return den if sink is None else den + jnp.exp2(sink - mx)


def _mla_kernel(*refs, n_seg):
    q_ref = refs[0]
    k_refs = refs[1:1 + n_seg]
    v_refs = refs[1 + n_seg:1 + 2 * n_seg]
    o_ref = refs[-5]
    s_refs, p_refs = refs[-4:-2], refs[-2:]
    tq = q_ref.shape[0]
    lane = lax.broadcasted_iota(jnp.int32, (tq, LANES), 1)

    def scores(h):
        off = 0
        for k in k_refs:
            s_refs[h % 2][:, off:off + k.shape[0]] = _nt(q_ref[:, h * LANES:(h + 1) * LANES],
                                                         k[:, h * LANES:(h + 1) * LANES])
            off += k.shape[0]

    scores(0)
    outs = []
    for h in range(MLA_HEADS):
        if h + 1 < MLA_HEADS:
            scores(h + 1)
        den = _softmax_numerator(s_refs[h % 2], p_refs[h % 2])
        pair = h // 2
        acc, off = None, 0
        for v in v_refs:
            pv = jnp.dot(p_refs[h % 2][:, off:off + v.shape[0]], v[:, pair * LANES:(pair + 1) * LANES],
                         preferred_element_type=F32)
            acc = pv if acc is None else acc + pv
            off += v.shape[0]
        outs.append(acc / den)
        if h % 2 == 1:
            o_ref[:, pair * LANES:(pair + 1) * LANES] = jnp.where(lane < MLA_V, outs[-2], outs[-1]).astype(BF)


def _mla_attend(q, ks, vs, n_batch, t, row0, k_blocks, prev, tq=TQ_MLA):
    n_seg = len(ks)
    n_keys = sum(s for _, s in k_blocks)
    qb0 = row0 // tq
    tiles = t // tq
    q_map = lambda b, i: (qb0 + b * tiles + i, 0)
    in_specs = [pl.BlockSpec((tq, MLA_HEADS * LANES), q_map)]
    for (b0, s), width in ([(kb, MLA_HEADS * LANES) for kb in k_blocks]
                           + [(kb, MLA_HEADS * MLA_V) for kb in k_blocks]):
        in_specs.append(pl.BlockSpec((s, width), functools.partial(lambda b, i, b0: (b0 + b, 0), b0=b0)))
    args = [q, *ks, *vs]
    aliases = {}
    if prev is not None:
        args.append(prev)
        in_specs.append(_any_spec())
        aliases = {len(args) - 1: 0}
    return pl.pallas_call(
        functools.partial(_mla_kernel, n_seg=n_seg),
        grid=(n_batch, tiles),
        in_specs=in_specs,
        out_specs=pl.BlockSpec((tq, MLA_HEADS * MLA_V), q_map),
        out_shape=jax.ShapeDtypeStruct((N_TOK, MLA_HEADS * MLA_V), BF),
        input_output_aliases=aliases,
        scratch_shapes=[pltpu.VMEM((tq, n_keys), F32)] * 2 + [pltpu.VMEM((tq, n_keys), BF)] * 2,
        compiler_params=_params("parallel", "parallel"),
        name="mla_attend",
    )(*args)


def _gqa_heads(q_ref, k_segs, v_segs, masks, sink_ref, o_ref, s_refs, p_refs):
    tq = q_ref.shape[0]
    lane_q = lax.broadcasted_iota(jnp.int32, (tq, LANES), 1)
    lo_q = lane_q < SWA_HD
    k_roll = [pltpu.roll(k, SWA_HD, 1) for k in k_segs]
    v_roll = [pltpu.roll(v, SWA_HD, 1) for v in v_segs]
    heads = range(SWA_KV_HEADS)
    sinks = []
    for kvh in heads:
        tiles = [q_ref[:, (kvh * 2 + j) * LANES:(kvh * 2 + j + 1) * LANES] for j in range(2)]
        zero = jnp.zeros_like(tiles[0])
        qs = jnp.concatenate([jnp.where(lo_q, tiles[0], zero), jnp.where(lo_q, zero, tiles[0]),
                              jnp.where(lo_q, tiles[1], zero), jnp.where(lo_q, zero, tiles[1])], axis=0)
        sinks.append(jnp.concatenate(
            [jnp.broadcast_to(sink_ref[0, kvh * SWA_GROUP + g:kvh * SWA_GROUP + g + 1, 0:1], (tq, 1))
             for g in range(SWA_GROUP)], axis=0) * LOG2_E)
        off = 0
        for k, kr, msk in zip(k_segs, k_roll, masks):
            lane_k = lax.broadcasted_iota(jnp.int32, k.shape, 1)
            first = (lane_k < SWA_HD) == (kvh == 0)
            kd = jnp.where(first, k, kr)
            s = _bdot_nt(qs, kd)
            if msk is not None:
                s = jnp.where(msk, s, -1e30)
            s_refs[kvh][:, off:off + k.shape[0]] = s
            off += k.shape[0]
    dens = [_softmax_numerator(s_refs[kvh], p_refs[kvh], sinks[kvh]) for kvh in heads]
    for kvh in heads:
        acc, off = None, 0
        for v, vr in zip(v_segs, v_roll):
            lane_v = lax.broadcasted_iota(jnp.int32, v.shape, 1)
            first = (lane_v < SWA_HD) == (kvh == 0)
            vd = jnp.where(first, v, vr).astype(BF)
            pv = jnp.dot(p_refs[kvh][:, off:off + v.shape[0]], vd, preferred_element_type=F32)
            acc = pv if acc is None else acc + pv
            off += v.shape[0]
        o = acc / dens[kvh]
        for j in range(2):
            o_ref[:, (kvh * 2 + j) * LANES:(kvh * 2 + j + 1) * LANES] = jnp.where(
                lo_q, o[(2 * j) * tq:(2 * j + 1) * tq], o[(2 * j + 1) * tq:(2 * j + 2) * tq]).astype(BF)


def _swa_scratch(rows, cols):
    return ([pltpu.VMEM((rows, cols), F32)] * SWA_KV_HEADS + [pltpu.VMEM((rows, cols), BF)] * SWA_KV_HEADS)


def _swa_ctx_kernel(q_ref, k_ref, v_ref, sink_ref, o_ref, *scratch):
    _gqa_heads(q_ref, [k_ref[...]], [v_ref[...]], [None], sink_ref, o_ref,
               scratch[:SWA_KV_HEADS], scratch[SWA_KV_HEADS:])


def _swa_ctx(sq, sk, sv, sink_rows, l):
    return pl.pallas_call(
        _swa_ctx_kernel,
        grid=(BATCH,),
        in_specs=[pl.BlockSpec((SEQ, SWA_HEADS * SWA_HD), lambda b: (b, 0)),
                  pl.BlockSpec((SEQ, LANES), lambda b: (b, 0)),
                  pl.BlockSpec((SEQ, LANES), lambda b: (b, 0)),
                  pl.BlockSpec((1, SWA_HEADS, LANES), lambda b: (l, 0, 0))],
        out_specs=pl.BlockSpec((SEQ, SWA_HEADS * SWA_HD), lambda b: (b, 0)),
        out_shape=jax.ShapeDtypeStruct((N_TOK, SWA_HEADS * SWA_HD), BF),
        scratch_shapes=_swa_scratch(SWA_GROUP * SEQ, SEQ),
        compiler_params=_params("parallel"),
        name="swa_ctx",
    )(sq, sk, sv, sink_rows)


def _swa_lat_kernel(q_ref, kp_ref, kc_ref, kn_ref, vp_ref, vc_ref, vn_ref, kx_ref, vx_ref, sink_ref, prev_ref,
                    o_ref, *scratch):
    w = SWA_WINDOW
    n = pl.program_id(1)
    nb = pl.num_programs(1)
    k_band = jnp.concatenate([kp_ref[...], kc_ref[...], kn_ref[...]], axis=0)
    v_band = jnp.concatenate([vp_ref[...], vc_ref[...], vn_ref[...]], axis=0)
    rows = SWA_GROUP * w
    r = lax.broadcasted_iota(jnp.int32, (rows, 3 * w), 0) & (w - 1)
    c = lax.broadcasted_iota(jnp.int32, (rows, 3 * w), 1)
    valid = (c >= r) & (c <= r + 2 * w) & ((c >= w) | (n > 0)) & ((c < 2 * w) | (n < nb - 1))
    _gqa_heads(q_ref, [k_band, kx_ref[0, 0]], [v_band, vx_ref[0, 0]], [valid, None], sink_ref, o_ref,
               scratch[:SWA_KV_HEADS], scratch[SWA_KV_HEADS:])


def _swa_lat(sq, sk, sv, k_cache, v_cache, sink_rows, l, prev):
    w = SWA_WINDOW
    nb = DEC_SEQ // w
    q0 = N_CTX // w

    def blk(d):
        return lambda b, n: (q0 + b * nb + jnp.clip(n + d, 0, nb - 1), 0)

    kv_specs = [pl.BlockSpec((w, LANES), blk(d)) for d in (-1, 0, 1)]
    cache_spec = pl.BlockSpec((1, 1, PAST_LEN, LANES), lambda b, n: (b, l, 0, 0))
    return pl.pallas_call(
        _swa_lat_kernel,
        grid=(DEC_BATCH, nb),
        in_specs=[pl.BlockSpec((w, SWA_HEADS * SWA_HD), blk(0))] + kv_specs + kv_specs
        + [cache_spec, cache_spec, pl.BlockSpec((1, SWA_HEADS, LANES), lambda b, n: (l, 0, 0)), _any_spec()],
        out_specs=pl.BlockSpec((w, SWA_HEADS * SWA_HD), blk(0)),
        out_shape=jax.ShapeDtypeStruct((N_TOK, SWA_HEADS * SWA_HD), BF),
        input_output_aliases={10: 0},
        scratch_shapes=_swa_scratch(SWA_GROUP * w, 3 * w + PAST_LEN),
        compiler_params=_params("parallel", "parallel"),
        name="swa_lat",
    )(sq, sk, sk, sk, sv, sv, sv, k_cache, v_cache, sink_rows, prev)


def _gdn_conv_kernel(*refs):
    x_ref, w_ref, o_ref = refs[0], refs[1], refs[-1]
    x = x_ref[...]
    t = x.shape[0]
    w = w_ref[0]
    row = lax.broadcasted_iota(jnp.int32, x.shape, 0)
    half = GDN_CONV // 2
    acc = x * w[half:half + 1, :]
    for k in range(GDN_CONV):
        d = k - half
        if d == 0:
            continue
        xs = pltpu.roll(x, (-d) % t, 0)
        ok = (row + d >= 0) & (row + d < t)
        acc = acc + jnp.where(ok, xs, 0.0) * w[k:k + 1, :]
    y = _silu(acc)
    is_qk = pl.program_id(1) < 2
    cols = []
    for h in range(x.shape[1] // LANES):
        yh = y[:, h * LANES:(h + 1) * LANES]
        nrm = lax.rsqrt(jnp.sum(yh * yh, axis=-1, keepdims=True) + 1e-6)
        cols.append(yh * jnp.where(is_qk, nrm, 1.0))
    o_ref[...] = jnp.concatenate(cols, axis=1)


def _gdn_conv(g3, conv_w, l, row0, n_seq, t, prev):
    blk0 = row0 // t
    args = [g3, conv_w]
    in_specs = [pl.BlockSpec((t, GDN_QK_DIM), lambda s, j: (blk0 + s, j)),
                pl.BlockSpec((1, GDN_CONV, GDN_QK_DIM), lambda s, j: (l, 0, j))]
    aliases = {}
    if prev is not None:
        args.append(prev)
        in_specs.append(_any_spec())
        aliases = {2: 0}
    return pl.pallas_call(
        _gdn_conv_kernel,
        grid=(n_seq, 3),
        in_specs=in_specs,
        out_specs=pl.BlockSpec((t, GDN_QK_DIM), lambda s, j: (blk0 + s, j)),
        out_shape=jax.ShapeDtypeStruct((N_TOK, GDN_CONV_CH), F32),
        input_output_aliases=aliases,
        compiler_params=_params("parallel", "parallel"),
        name="gdn_conv",
    )(*args)


def _gdn_chunks(probs):
    c = GDN_CHUNK
    ri = lax.broadcasted_iota(jnp.int32, (c, c), 0)
    ci = lax.broadcasted_iota(jnp.int32, (c, c), 1)
    lower_incl, upper_incl = ri >= ci, ri <= ci
    eye = (ri == ci).astype(F32)
    n = len(probs)
    incl = [upper_incl if p["backward"] else lower_incl for p in probs]
    incl_t = [lower_incl if p["backward"] else upper_incl for p in probs]
    strict = [(ri < ci) if p["backward"] else (ri > ci) for p in probs]
    gc_col = [jnp.sum(jnp.where(incl[i], probs[i]["g_row"], 0.0), axis=1, keepdims=True) for i in range(n)]
    gc_row = [jnp.sum(jnp.where(incl_t[i], probs[i]["g_col"], 0.0), axis=0, keepdims=True) for i in range(n)]
    decay = [jnp.where(incl[i], jnp.exp(gc_col[i] - gc_row[i]), 0.0) for i in range(n)]
    q = [p["q"] * (GDN_DK ** -0.5) for p in probs]
    kb = [p["k"] * p["beta"] for p in probs]
    kk = [_bdot_nt(kb[i], probs[i]["k"]) for i in range(n)]
    qk = [_bdot_nt(q[i], probs[i]["k"]) for i in range(n)]
    pw = [jnp.where(strict[i], -(kk[i] * decay[i]), 0.0) for i in range(n)]
    inv = [eye + m for m in pw]
    for _ in range(int(np.log2(c)) - 1):
        pw = [_dot_3pass(m, m) for m in pw]
        inv = [inv[i] + _dot_3pass(inv[i], pw[i]) for i in range(n)]
    e_col = [jnp.exp(g) for g in gc_col]
    uw = [_bdot(inv[i], jnp.concatenate([probs[i]["v"] * probs[i]["beta"], kb[i] * e_col[i]], axis=1))
          for i in range(n)]
    a = [jnp.where(incl[i], qk[i] * decay[i], 0.0) for i in range(n)]
    g_last = [gc_col[i][0:1, :] if probs[i]["backward"] else gc_col[i][c - 1:c, :] for i in range(n)]
    k_dec = [probs[i]["k"] * jnp.exp(g_last[i] - gc_col[i]) for i in range(n)]
    wq = [_bdot(jnp.concatenate([uw[i][:, GDN_DV:], q[i] * e_col[i]], axis=0), probs[i]["state"])
          for i in range(n)]
    v_new = [uw[i][:, :GDN_DV] - wq[i][:c] for i in range(n)]
    o = [wq[i][c:] + _bdot(a[i], v_new[i]) for i in range(n)]
    s_new = [probs[i]["state"] * jnp.exp(g_last[i]) + _bdot_tn(k_dec[i], v_new[i]) for i in range(n)]
    return list(zip(o, s_new))


def _gate_rows(misc):
    shape = (2 * GDN_NH, LANES)
    sel = (lax.broadcasted_iota(jnp.int32, shape, 1)
           == lax.broadcasted_iota(jnp.int32, shape, 0) + _M_G).astype(BF)
    b1, b2, b3 = _split3(misc)
    return _nt(sel, b1) + (_nt(sel, b2) + _nt(sel, b3))


def _gdn_kernel(*refs, context, n_par):
    qf_ref, qb_ref, mf_ref, mb_ref = refs[:4]
    st_ref = refs[-1]
    if context:
        of_ref, ob_ref, sfin_ref = refs[-4:-1]
    else:
        s0_ref = refs[4]
        of_ref, ob_ref = refs[-3:-1]
    n = pl.program_id(1)

    @pl.when(n == 0)
    def _():
        if context:
            st_ref[...] = jnp.zeros_like(st_ref)
        else:
            st_ref[...] = s0_ref[:, 0]

    probs = []
    for s in range(n_par):
        for d, (q_ref, m_ref) in enumerate([(qf_ref, mf_ref), (qb_ref, mb_ref)]):
            qkv = q_ref[s]
            misc = m_ref[s]
            rows = _gate_rows(misc)
            for h in range(GDN_HEADS):
                i = d * GDN_HEADS + h
                probs.append(dict(
                    q=qkv[:, h * GDN_DK:(h + 1) * GDN_DK],
                    k=qkv[:, GDN_QK_DIM + h * GDN_DK:GDN_QK_DIM + (h + 1) * GDN_DK],
                    v=qkv[:, 2 * GDN_QK_DIM + h * GDN_DV:2 * GDN_QK_DIM + (h + 1) * GDN_DV],
                    g_col=misc[:, _M_G + i:_M_G + i + 1], g_row=rows[i:i + 1, :],
                    beta=misc[:, _M_B + i:_M_B + i + 1],
                    state=st_ref[s, i], backward=(d == 1)))
    results = _gdn_chunks(probs)
    for s in range(n_par):
        for d, o_ref in enumerate([of_ref, ob_ref]):
            for h in range(GDN_HEADS):
                i = d * GDN_HEADS + h
                o, s_new = results[s * GDN_NH + i]
                o_ref[s, :, h * GDN_DV:(h + 1) * GDN_DV] = o
                st_ref[s, i] = s_new

    if context:
        @pl.when(n == pl.num_programs(1) - 1)
        def _():
            sfin_ref[:, 0] = st_ref[...]


def _gdn(qkv, misc, l, t, n_seq, seq0, s0, prevs, n_par=GDN_PAR):
    c = GDN_CHUNK
    nc = t // c
    context = s0 is None
    sb0 = seq0 // n_par
    n_all = N_TOK // t
    fwd = lambda s, n: (sb0 + s, n, 0)
    bwd = lambda s, n: (sb0 + s, nc - 1 - n, 0)
    st_spec = pl.BlockSpec((n_par, 1, GDN_NH, GDN_DK, GDN_DV), lambda s, n: (s, l, 0, 0, 0))
    qkv3 = qkv.reshape(n_all, t, GDN_CONV_CH)
    misc3 = misc.reshape(n_all, t, LANES)
    in_specs = [pl.BlockSpec((n_par, c, GDN_CONV_CH), fwd), pl.BlockSpec((n_par, c, GDN_CONV_CH), bwd),
                pl.BlockSpec((n_par, c, LANES), fwd), pl.BlockSpec((n_par, c, LANES), bwd)]
    args = [qkv3, qkv3, misc3, misc3]
    if not context:
        in_specs.append(st_spec)
        args.append(s0)
    out_specs = [pl.BlockSpec((n_par, c, GDN_V_DIM), fwd), pl.BlockSpec((n_par, c, GDN_V_DIM), bwd)]
    out_shape = [jax.ShapeDtypeStruct((n_all, t, GDN_V_DIM), F32)] * 2
    if context:
        out_specs.append(st_spec)
        out_shape.append(jax.ShapeDtypeStruct((n_seq, DEPTH, GDN_NH, GDN_DK, GDN_DV), F32))
    aliases = {}
    for j, p in enumerate(prevs):
        if p is not None:
            aliases[len(args)] = j
            args.append(p.reshape(out_shape[j].shape))
            in_specs.append(_any_spec())
    return pl.pallas_call(
        functools.partial(_gdn_kernel, context=context, n_par=n_par),
        grid=(n_seq // n_par, nc),
        in_specs=in_specs, out_specs=out_specs, out_shape=out_shape,
        input_output_aliases=aliases,
        scratch_shapes=[pltpu.VMEM((n_par, GDN_NH, GDN_DK, GDN_DV), F32)],
        compiler_params=_params("parallel", "arbitrary"),
        name="gdn",
    )(*args)


def _out_kernel(x_ref, mod_ref, om_ref, os_ref, gf_ref, gb_ref, gz_ref, gn_ref, w_ref, o_ref):
    gate = mod_ref[0, 0][:, 2 * D_MODEL:]
    s = gf_ref[...] + gb_ref[...]
    gz = gz_ref[...]
    cols = []
    for h in range(GDN_HEADS):
        sh = s[:, h * GDN_DV:(h + 1) * GDN_DV]
        cols.append(_rms(sh, gn_ref[0]) * _silu(gz[:, h * GDN_DV:(h + 1) * GDN_DV]))
    og = jnp.concatenate(cols, axis=1)
    n_m = MLA_HEADS * MLA_V
    n_s = SWA_HEADS * SWA_HD
    y = (jnp.dot(om_ref[...], w_ref[0, :n_m, :], preferred_element_type=F32)
         + jnp.dot(os_ref[...], w_ref[0, n_m:n_m + n_s, :], preferred_element_type=F32)
         + jnp.dot(og.astype(BF), w_ref[0, n_m + n_s:, :], preferred_element_type=F32))
    o_ref[...] = x_ref[...] + gate * y


def _out_proj(x, mods, l, o_mla, o_swa, o_gf, o_gb, gz, gdn_norm, w_out, tm=TM_OUT):
    row = lambda i: (i, 0)
    lay3 = lambda i: (l, 0, 0)
    return pl.pallas_call(
        _out_kernel,
        grid=(N_TOK // tm,),
        in_specs=[pl.BlockSpec((tm, D_MODEL), row),
                  _mod_spec(l, 1, tm),
                  pl.BlockSpec((tm, MLA_HEADS * MLA_V), row),
                  pl.BlockSpec((tm, SWA_HEADS * SWA_HD), row),
                  pl.BlockSpec((tm, GDN_V_DIM), row),
                  pl.BlockSpec((tm, GDN_V_DIM), row),
                  pl.BlockSpec((tm, GDN_V_DIM), row),
                  pl.BlockSpec((1, 1, GDN_DV), lay3),
                  pl.BlockSpec((1,) + w_out.shape[1:], lay3)],
        out_specs=pl.BlockSpec((tm, D_MODEL), row),
        out_shape=jax.ShapeDtypeStruct((N_TOK, D_MODEL), F32),
        compiler_params=_params("parallel"),
        name="out_proj",
    )(x, mods, o_mla, o_swa, o_gf, o_gb, gz, gdn_norm.reshape(DEPTH, 1, GDN_DV), w_out)


def _rot_columns(w, dim):
    shp = w.shape
    w6 = w.reshape(shp[:-1] + (shp[-1] // dim, 2, 2, dim // 4))
    sign = jnp.asarray([-1.0, 1.0], F32).reshape(2, 1)
    return (jnp.flip(w6, axis=-2) * sign).reshape(shp)


def _axial_rope(n_tokens, dim):
    f32 = np.float32
    rows = n_tokens // GRID_W
    row = np.repeat(np.arange(rows, dtype=f32), GRID_W)
    col = np.tile(np.arange(GRID_W, dtype=f32), rows)
    axis_dim = dim // 2
    inv_freq = (f32(1.0) / (f32(ROPE_BASE) ** (np.arange(0, axis_dim, 2, dtype=f32) / f32(axis_dim)))).astype(f32)
    ang_r = row[:, None] * inv_freq[None, :]
    ang_c = col[:, None] * inv_freq[None, :]
    ang = np.concatenate([ang_r, ang_r, ang_c, ang_c], axis=-1).astype(f32)
    return np.cos(ang).astype(f32), np.sin(ang).astype(f32)


def _rope_table(tm):
    cos_m, sin_m = _axial_rope(DEC_SEQ, MLA_ROPE)
    cos_s, sin_s = _axial_rope(DEC_SEQ, SWA_HD)
    t = DEC_SEQ
    one = lambda w: np.ones((t, w), np.float32)
    zero = lambda w: np.zeros((t, w), np.float32)
    lat = np.concatenate([
        cos_s, cos_s, sin_s, sin_s,
        cos_m, one(LANES - MLA_ROPE), sin_m, zero(LANES - MLA_ROPE),
        one(MLA_NOPE), cos_m, one(LANES - MLA_NOPE - MLA_ROPE),
        zero(MLA_NOPE), sin_m, zero(LANES - MLA_NOPE - MLA_ROPE)], axis=1)
    ident_row = np.concatenate([np.ones(LANES), np.zeros(LANES)] * 3).astype(np.float32)
    ident = np.broadcast_to(ident_row[None, :], (tm, _T_END))
    return jnp.asarray(np.concatenate([ident, lat], axis=0))


def _mixer_weights(w_in, mla_w_qb, mla_w_kvb):
    nl = DEPTH
    offs = np.cumsum([0, MLA_Q_LORA, MLA_KV_LORA, MLA_ROPE, SWA_HEADS * SWA_HD, SWA_KV_HEADS * SWA_HD,
                      SWA_KV_HEADS * SWA_HD, GDN_CONV_CH, GDN_V_DIM, 2 * GDN_NH])
    cq, ckv, krope, sq, sk, sv, g3, gz, gates = [w_in[:, :, offs[i]:offs[i + 1]] for i in range(9)]
    zeros = lambda n: jnp.zeros((nl, D_MODEL, n), F32)
    misc_a = jnp.concatenate([krope, gates, zeros(LANES - _M_END)], axis=2)
    misc_b = jnp.concatenate([_rot_columns(krope, MLA_ROPE), zeros(LANES - MLA_ROPE)], axis=2)
    win = jnp.concatenate([cq, ckv, sq, _rot_columns(sq, SWA_HD), sk, _rot_columns(sk, SWA_HD), sv, g3, gz,
                           misc_a, misc_b], axis=2).astype(BF)

    r = MLA_Q_LORA
    wq = mla_w_qb.reshape(nl, r, MLA_HEADS, MLA_NOPE + MLA_ROPE)
    nope, rope = wq[..., :MLA_NOPE], wq[..., MLA_NOPE:]
    pad = LANES - MLA_NOPE - MLA_ROPE
    z = lambda n: jnp.zeros((nl, r, MLA_HEADS, n), F32)
    qa = jnp.concatenate([nope, rope, z(pad)], axis=-1).reshape(nl, r, MLA_HEADS * LANES)
    qb = jnp.concatenate([z(MLA_NOPE), _rot_columns(rope, MLA_ROPE), z(pad)],
                         axis=-1).reshape(nl, r, MLA_HEADS * LANES)
    wqb = jnp.concatenate([qa, qb], axis=2).astype(BF)

    kvb = mla_w_kvb.reshape(nl, MLA_KV_LORA, MLA_HEADS, MLA_NOPE + MLA_V)
    k_nope = jnp.concatenate([kvb[..., :MLA_NOPE],
                              jnp.zeros((nl, MLA_KV_LORA, MLA_HEADS, LANES - MLA_NOPE), F32)],
                             axis=-1).reshape(nl, MLA_KV_LORA, MLA_HEADS * LANES)
    place = np.zeros((LANES, MLA_HEADS, LANES), np.float32)
    for i in range(MLA_ROPE):
        place[i, :, MLA_NOPE + i] = 1.0
    place = jnp.broadcast_to(jnp.asarray(place.reshape(1, LANES, MLA_HEADS * LANES)),
                             (nl, LANES, MLA_HEADS * LANES))
    wk = jnp.concatenate([k_nope, place], axis=1).astype(BF)
    wv = kvb[..., MLA_NOPE:].reshape(nl, MLA_KV_LORA, MLA_HEADS * MLA_V).astype(BF)
    return win, wqb, wk, wv


def _misc_rows(vals):
    rows = jnp.zeros((DEPTH, 1, LANES), F32)
    return rows.at[:, 0, _M_G:_M_B].set(vals.reshape(DEPTH, GDN_NH).astype(F32))


def kernel(x_prompt, x_sample, cache_mla_ckv, cache_mla_krope, cache_swa_k, cache_swa_v, state_gdn, c, c_ctx,
           w_ada, b_ada, norm_ffn1, ffn1_w1, ffn1_w2, norm_mix, w_in, mla_q_norm, mla_w_qb, mla_kv_norm,
           mla_w_kvb, swa_sink, gdn_conv_w, gdn_a_log, gdn_dt_bias, gdn_norm, w_out, norm_ffn2, ffn2_w1,
           ffn2_w2, final_norm):
    cond = jnp.concatenate([c_ctx[None, :], c, jnp.zeros((COND_ROWS - N_GROUPS, D_MODEL), F32)], axis=0)
    mods = _adaln(cond, w_ada, b_ada)[:, :N_GROUPS].reshape(DEPTH, N_GROUPS, 1, N_MOD * D_MODEL)
    tab = _rope_table(TM_PROJ)
    win, wqb, wk, wv = _mixer_weights(w_in, mla_w_qb, mla_w_kvb)
    w11, w12, w21, w22, wo = (w.astype(BF) for w in (ffn1_w1, ffn1_w2, ffn2_w1, ffn2_w2, w_out))
    alog, dtb = _misc_rows(gdn_a_log), _misc_rows(gdn_dt_bias)
    sink_rows = jnp.broadcast_to(swa_sink[:, :, None], (DEPTH, SWA_HEADS, LANES))
    cache_misc = jnp.pad(cache_mla_krope, ((0, 0), (0, 0), (0, 0), (0, LANES - MLA_ROPE)))
    cache_k = cache_swa_k.reshape(DEC_BATCH, DEPTH, PAST_LEN, LANES)
    cache_v = cache_swa_v.reshape(DEC_BATCH, DEPTH, PAST_LEN, LANES)
    s0 = state_gdn.reshape(DEC_BATCH, DEPTH, GDN_NH, GDN_DK, GDN_DV)

    xs = [x_prompt.reshape(N_CTX, D_MODEL), x_sample.reshape(N_LAT, D_MODEL)]
    new_ckv = new_st = None
    new_krope, new_sk, new_sv = [], [], []
    for l in range(DEPTH):
        x = _ffn(xs, mods, l, 0, norm_ffn1, w11, w12)
        (q_mla, new_ckv, misc, k_mla, v_mla, sq, sk, sv, g3, gz) = _proj(
            x, mods, l, norm_mix, win, mla_q_norm, wqb, mla_kv_norm, wk, wv, tab, alog, dtb, new_ckv)

        k_c, v_c = _kv_cache(cache_mla_ckv, cache_misc, l, wk, wv)
        o_mla = _mla_attend(q_mla, [k_mla], [v_mla], BATCH, SEQ, 0, [(0, SEQ)], None)
        o_mla = _mla_attend(q_mla, [k_mla, k_c], [v_mla, v_c], DEC_BATCH, DEC_SEQ, N_CTX,
                            [(N_CTX // DEC_SEQ, DEC_SEQ), (0, PAST_LEN)], o_mla)

        o_swa = _swa_ctx(sq, sk, sv, sink_rows, l)
        o_swa = _swa_lat(sq, sk, sv, cache_k, cache_v, sink_rows, l, o_swa)

        qkv = _gdn_conv(g3, gdn_conv_w, l, 0, BATCH, SEQ, None)
        qkv = _gdn_conv(g3, gdn_conv_w, l, N_CTX, DEC_BATCH, DEC_SEQ, qkv)
        o_gf, o_gb, new_st = _gdn(qkv, misc, l, SEQ, BATCH, 0, None, [None, None, new_st])
        o_gf, o_gb = _gdn(qkv, misc, l, DEC_SEQ, DEC_BATCH, N_CTX // DEC_SEQ, s0, [o_gf, o_gb])

        x = _out_proj(x, mods, l, o_mla, o_swa, o_gf.reshape(N_TOK, GDN_V_DIM), o_gb.reshape(N_TOK, GDN_V_DIM),
                      gz, gdn_norm, wo)
        if l + 1 < DEPTH:
            xs = [_ffn([x], mods, l, 2, norm_ffn2, w21, w22)]
        else:
            y_prompt, y_sample = _ffn([x], mods, l, 2, norm_ffn2, w21, w22, final_gain=final_norm)

        new_krope.append(misc[:N_CTX, :MLA_ROPE].reshape(BATCH, SEQ, MLA_ROPE))
        new_sk.append(sk[:N_CTX].reshape(BATCH, SEQ, SWA_KV_HEADS, SWA_HD))
        new_sv.append(sv[:N_CTX].reshape(BATCH, SEQ, SWA_KV_HEADS, SWA_HD))

    return (y_prompt.reshape(BATCH, SEQ, D_MODEL), y_sample.reshape(DEC_BATCH, DEC_SEQ, D_MODEL), new_ckv,
            jnp.stack(new_krope, axis=1), jnp.stack(new_sk, axis=1), jnp.stack(new_sv, axis=1),
            new_st.reshape(BATCH, DEPTH, 2, GDN_HEADS, GDN_DK, GDN_DV))
```

```python
import functools

import numpy as np
import jax
import jax.numpy as jnp
from jax import lax
from jax.experimental import pallas as pl
from jax.experimental.pallas import tpu as pltpu

D_MODEL = 1024
BATCH = 16
SEQ = 256
DEPTH = 2
DEC_BATCH = 2
DEC_SEQ = 2048
PAST_LEN = 512
GRID_W = 64
ROPE_BASE = 10000.0
NORM_EPS = 1e-6
N_MOD = 9
D_FF = 2816
MLA_HEADS = 8
MLA_Q_LORA = 384
MLA_KV_LORA = 256
MLA_NOPE = 64
MLA_ROPE = 32
MLA_V = 64
SWA_HEADS = 8
SWA_KV_HEADS = 2
SWA_GROUP = SWA_HEADS // SWA_KV_HEADS
SWA_HD = 64
SWA_WINDOW = 128
GDN_HEADS = 4
GDN_DK = 128
GDN_DV = 128
GDN_CONV = 5
GDN_CHUNK = 64
GDN_QK_DIM = GDN_HEADS * GDN_DK
GDN_V_DIM = GDN_HEADS * GDN_DV
GDN_CONV_CH = 2 * GDN_QK_DIM + GDN_V_DIM
GDN_NH = 2 * GDN_HEADS

N_CTX = BATCH * SEQ
N_LAT = DEC_BATCH * DEC_SEQ
N_TOK = N_CTX + N_LAT
N_GROUPS = 1 + DEC_BATCH
COND_ROWS = 8

LANES = 128
VMEM_LIMIT_BYTES = 56 * 1024 * 1024

TM_FFN = 256
TM_PROJ = SEQ
TM_OUT = 512
TQ_MLA = 256
GDN_PAR = 2

BF = jnp.bfloat16
F32 = jnp.float32
LOG2_E = 1.4426950408889634
MLA_Q_SCALE = (MLA_NOPE + MLA_ROPE) ** -0.5 * LOG2_E

_C_CQ = 0
_C_CKV = _C_CQ + MLA_Q_LORA
_C_SQ = _C_CKV + MLA_KV_LORA
_C_SQR = _C_SQ + SWA_HEADS * SWA_HD
_C_SK = _C_SQR + SWA_HEADS * SWA_HD
_C_SKR = _C_SK + SWA_KV_HEADS * SWA_HD
_C_SV = _C_SKR + SWA_KV_HEADS * SWA_HD
_C_G3 = _C_SV + SWA_KV_HEADS * SWA_HD
_C_GZ = _C_G3 + GDN_CONV_CH
_C_MA = _C_GZ + GDN_V_DIM
_C_MB = _C_MA + LANES
_C_END = _C_MB + LANES
_M_G = MLA_ROPE
_M_B = MLA_ROPE + GDN_NH
_M_END = MLA_ROPE + 2 * GDN_NH
_T_CS, _T_SS, _T_CA, _T_SA, _T_CQ, _T_SQ = (i * LANES for i in range(6))
_T_END = 6 * LANES


def _params(*sem):
    return pltpu.CompilerParams(dimension_semantics=sem, vmem_limit_bytes=VMEM_LIMIT_BYTES)


def _bdot(a, b):
    return jnp.dot(a.astype(BF), b.astype(BF), preferred_element_type=F32)


def _nt(a, b):
    return lax.dot_general(a, b, (((1,), (1,)), ((), ())), preferred_element_type=F32)


def _bdot_nt(a, b):
    return _nt(a.astype(BF), b.astype(BF))


def _bdot_tn(a, b):
    return lax.dot_general(a.astype(BF), b.astype(BF), (((0,), (0,)), ((), ())),
                           preferred_element_type=F32)


def _split2(a):
    hi = a.astype(BF)
    lo = (a - hi.astype(F32)).astype(BF)
    return hi, lo


def _split3(a):
    b1 = a.astype(BF)
    r = a - b1.astype(F32)
    b2 = r.astype(BF)
    b3 = (r - b2.astype(F32)).astype(BF)
    return b1, b2, b3


def _dot_3pass(a, b):
    ah, al = _split2(a)
    bh, bl = _split2(b)
    return (jnp.dot(ah, bh, preferred_element_type=F32)
            + (jnp.dot(ah, bl, preferred_element_type=F32)
               + jnp.dot(al, bh, preferred_element_type=F32)))


def _silu(x):
    return x / (1.0 + jnp.exp(-x))


def _rms(x, gain, eps=NORM_EPS):
    return x * lax.rsqrt(jnp.mean(x * x, axis=-1, keepdims=True) + eps) * gain


def _group_of_row(r):
    return jnp.where(r < N_CTX, 0, 1 + (r - N_CTX) // DEC_SEQ)


def _any_spec():
    return pl.BlockSpec(memory_space=pl.ANY)


def _adaln_kernel(c_ref, w_ref, b_ref, o_ref):
    o_ref[0] = _bdot(_silu(c_ref[...]), w_ref[0]) + b_ref[0]


def _adaln(cond, w_ada, b_ada, tn=1536):
    n = N_MOD * D_MODEL
    return pl.pallas_call(
        _adaln_kernel,
        grid=(DEPTH, n // tn),
        in_specs=[pl.BlockSpec((COND_ROWS, D_MODEL), lambda l, j: (0, 0)),
                  pl.BlockSpec((1, D_MODEL, tn), lambda l, j: (l, 0, j)),
                  pl.BlockSpec((1, 1, tn), lambda l, j: (l, 0, j))],
        out_specs=pl.BlockSpec((1, COND_ROWS, tn), lambda l, j: (l, 0, j)),
        out_shape=jax.ShapeDtypeStruct((DEPTH, COND_ROWS, n), F32),
        compiler_params=_params("parallel", "parallel"),
        name="adaln",
    )(cond, w_ada, b_ada.reshape(DEPTH, 1, n))


def _mod_spec(l, which, tm):
    return pl.BlockSpec((1, 1, 1, 3 * D_MODEL), lambda i: (l, _group_of_row(i * tm), 0, which))


def _ffn_kernel(*refs, n_in, n_a_tiles, final):
    x_refs = refs[:n_in]
    mod_ref, gain_ref, w1_ref, w2_ref = refs[n_in:n_in + 4]
    rest = refs[n_in + 4:]
    i = pl.program_id(0)
    if n_in == 2:
        x = jnp.where(i < n_a_tiles, x_refs[0][...], x_refs[1][...])
    else:
        x = x_refs[0][...]
    mod = mod_ref[0, 0]
    shift, scale, gate = mod[:, :D_MODEL], mod[:, D_MODEL:2 * D_MODEL], mod[:, 2 * D_MODEL:]
    h = _rms(x, gain_ref[0]) * (1.0 + scale) + shift
    gu = jnp.dot(h.astype(BF), w1_ref[0], preferred_element_type=F32)
    a = _silu(gu[:, :D_FF]) * gu[:, D_FF:]
    y = x + gate * (0.5 * jnp.dot(a.astype(BF), w2_ref[0], preferred_element_type=F32))
    if final:
        fg_ref, oa_ref, ob_ref = rest
        yn = _rms(y, fg_ref[...])

        @pl.when(i < n_a_tiles)
        def _():
            oa_ref[...] = yn

        @pl.when(i >= n_a_tiles)
        def _():
            ob_ref[...] = yn
    else:
        rest[0][...] = y


def _ffn(xs, mods, l, which, gain, w1, w2, final_gain=None, tm=TM_FFN):
    n_a = N_CTX // tm
    row = lambda i: (i, 0)
    first = lambda i: (jnp.minimum(i, n_a - 1), 0)
    second = lambda i: (jnp.maximum(i - n_a, 0), 0)
    lay3 = lambda i: (l, 0, 0)
    x_specs = ([pl.BlockSpec((tm, D_MODEL), row)] if len(xs) == 1
               else [pl.BlockSpec((tm, D_MODEL), first), pl.BlockSpec((tm, D_MODEL), second)])
    in_specs = x_specs + [_mod_spec(l, which, tm),
                          pl.BlockSpec((1, 1, D_MODEL), lay3),
                          pl.BlockSpec((1, D_MODEL, 2 * D_FF), lay3),
                          pl.BlockSpec((1, D_FF, D_MODEL), lay3)]
    args = list(xs) + [mods, gain.reshape(DEPTH, 1, D_MODEL), w1, w2]
    if final_gain is None:
        out_specs = pl.BlockSpec((tm, D_MODEL), row)
        out_shape = jax.ShapeDtypeStruct((N_TOK, D_MODEL), F32)
    else:
        in_specs.append(pl.BlockSpec((1, D_MODEL), lambda i: (0, 0)))
        args.append(final_gain.reshape(1, D_MODEL))
        out_specs = [pl.BlockSpec((tm, D_MODEL), first), pl.BlockSpec((tm, D_MODEL), second)]
        out_shape = [jax.ShapeDtypeStruct((N_CTX, D_MODEL), F32), jax.ShapeDtypeStruct((N_LAT, D_MODEL), F32)]
    return pl.pallas_call(
        functools.partial(_ffn_kernel, n_in=len(xs), n_a_tiles=n_a, final=final_gain is not None),
        grid=(N_TOK // tm,),
        in_specs=in_specs, out_specs=out_specs, out_shape=out_shape,
        compiler_params=_params("arbitrary" if final_gain is not None else "parallel"),
        name="ffn",
    )(*args)


def _kv_expand(ckv_n, misc, wk, wv):
    kin = jnp.concatenate([ckv_n, misc], axis=1).astype(BF)
    kk = jnp.dot(kin, wk, preferred_element_type=F32)
    vv = jnp.dot(ckv_n.astype(BF), wv, preferred_element_type=F32)
    return kk.astype(BF), vv.astype(BF)


def _proj_kernel(*refs, n_ctx_tiles):
    (x_ref, mod_ref, gain_ref, win_ref, qg_ref, wqb_ref, kvg_ref, wk_ref, wv_ref, tab_ref,
     alog_ref, dtb_ref) = refs[:12]
    q_ref, ckv_ref, misc_ref, kmla_ref, vmla_ref, sq_ref, sk_ref, sv_ref, g3_ref, gz_ref = refs[-10:]
    i = pl.program_id(0)
    x = x_ref[...]
    mod = mod_ref[0, 0]
    shift, scale = mod[:, :D_MODEL], mod[:, D_MODEL:2 * D_MODEL]
    h = _rms(x, gain_ref[0]) * (1.0 + scale) + shift
    u = _nt(h.astype(BF), win_ref[0])
    tab = tab_ref[...]

    qn = _rms(u[:, _C_CQ:_C_CKV], qg_ref[0])
    q2 = jnp.dot(qn.astype(BF), wqb_ref[0], preferred_element_type=F32)
    nq = MLA_HEADS * LANES
    cosq = jnp.concatenate([tab[:, _T_CQ:_T_CQ + LANES]] * MLA_HEADS, axis=1)
    sinq = jnp.concatenate([tab[:, _T_SQ:_T_SQ + LANES]] * MLA_HEADS, axis=1)
    q_ref[...] = ((q2[:, :nq] * cosq + q2[:, nq:] * sinq) * MLA_Q_SCALE).astype(BF)

    ckv_n = _rms(u[:, _C_CKV:_C_SQ], kvg_ref[0])

    @pl.when(i < n_ctx_tiles)
    def _():
        ckv_ref[0, 0] = ckv_n

    m = (u[:, _C_MA:_C_MB] * tab[:, _T_CA:_T_CA + LANES] + u[:, _C_MB:_C_END] * tab[:, _T_SA:_T_SA + LANES])
    lane = lax.broadcasted_iota(jnp.int32, m.shape, 1)
    z = m + dtb_ref[0]
    softplus = jnp.maximum(z, 0.0) + jnp.log(1.0 + jnp.exp(-jnp.abs(z)))
    decay = -jnp.exp(alog_ref[0]) * softplus
    strength = 1.0 / (1.0 + jnp.exp(-m))
    misc = jnp.where((lane >= _M_G) & (lane < _M_B), decay,
                     jnp.where((lane >= _M_B) & (lane < _M_END), strength, m))
    misc_ref[...] = misc

    kk, vv = _kv_expand(ckv_n, misc, wk_ref[0], wv_ref[0])
    kmla_ref[...] = kk
    vmla_ref[...] = vv

    n_sq = SWA_HEADS * SWA_HD
    cos_s = tab[:, _T_CS:_T_CS + LANES]
    sin_s = tab[:, _T_SS:_T_SS + LANES]
    cos4 = jnp.concatenate([cos_s] * (n_sq // LANES), axis=1)
    sin4 = jnp.concatenate([sin_s] * (n_sq // LANES), axis=1)
    sq = u[:, _C_SQ:_C_SQR] * cos4 + u[:, _C_SQR:_C_SK] * sin4
    sq_ref[...] = (sq * (SWA_HD ** -0.5 * LOG2_E)).astype(BF)
    sk_ref[...] = u[:, _C_SK:_C_SKR] * cos_s + u[:, _C_SKR:_C_SV] * sin_s
    sv_ref[...] = u[:, _C_SV:_C_G3]
    g3_ref[...] = u[:, _C_G3:_C_GZ]
    gz_ref[...] = u[:, _C_GZ:_C_MA]


def _proj(x, mods, l, gain, win, qg, wqb, kvg, wk, wv, tab, alog, dtb, ckv_prev, tm=TM_PROJ):
    assert tm == SEQ
    n_ctx_tiles = N_CTX // tm
    lat_tiles = DEC_SEQ // tm
    lay3 = lambda i: (l, 0, 0)
    row = lambda i: (i, 0)

    def tab_map(i):
        return (jnp.where(i < n_ctx_tiles, 0, 1 + (i - n_ctx_tiles) % lat_tiles), 0)

    def lay_spec(a):
        return pl.BlockSpec((1,) + a.shape[1:], lay3)

    widths = [(MLA_HEADS * LANES, BF), None, (LANES, F32), (MLA_HEADS * LANES, BF),
              (MLA_HEADS * MLA_V, BF), (SWA_HEADS * SWA_HD, BF), (SWA_KV_HEADS * SWA_HD, F32),
              (SWA_KV_HEADS * SWA_HD, F32), (GDN_CONV_CH, F32), (GDN_V_DIM, F32)]
    out_specs = [pl.BlockSpec((tm, w[0]), row) if w else
                 pl.BlockSpec((1, 1, SEQ, MLA_KV_LORA), lambda i: (jnp.minimum(i, n_ctx_tiles - 1), l, 0, 0))
                 for w in widths]
    out_shape = [jax.ShapeDtypeStruct((N_TOK, w[0]), w[1]) if w else
                 jax.ShapeDtypeStruct((BATCH, DEPTH, SEQ, MLA_KV_LORA), F32) for w in widths]
    qg, kvg = qg.reshape(DEPTH, 1, -1), kvg.reshape(DEPTH, 1, -1)
    args = [x, mods, gain.reshape(DEPTH, 1, D_MODEL), win, qg, wqb, kvg, wk, wv, tab, alog, dtb]
    in_specs = [pl.BlockSpec((tm, D_MODEL), row), _mod_spec(l, 1, tm),
                pl.BlockSpec((1, 1, D_MODEL), lay3), lay_spec(win), lay_spec(qg), lay_spec(wqb),
                lay_spec(kvg), lay_spec(wk), lay_spec(wv), pl.BlockSpec((tm, _T_END), tab_map),
                lay_spec(alog), lay_spec(dtb)]
    aliases = {}
    if ckv_prev is not None:
        args.append(ckv_prev)
        in_specs.append(_any_spec())
        aliases = {len(args) - 1: 1}
    return pl.pallas_call(
        functools.partial(_proj_kernel, n_ctx_tiles=n_ctx_tiles),
        grid=(N_TOK // tm,),
        in_specs=in_specs, out_specs=out_specs, out_shape=out_shape,
        input_output_aliases=aliases,
        compiler_params=_params("arbitrary"),
        name="proj",
    )(*args)


def _kv_cache_kernel(ckv_ref, misc_ref, wk_ref, wv_ref, k_ref, v_ref):
    kk, vv = _kv_expand(ckv_ref[0, 0], misc_ref[0, 0], wk_ref[0], wv_ref[0])
    k_ref[...] = kk
    v_ref[...] = vv


def _kv_cache(cache_ckv, cache_misc, l, wk, wv):
    lay3 = lambda b: (l, 0, 0)
    n = DEC_BATCH * PAST_LEN
    return pl.pallas_call(
        _kv_cache_kernel,
        grid=(DEC_BATCH,),
        in_specs=[pl.BlockSpec((1, 1, PAST_LEN, MLA_KV_LORA), lambda b: (b, l, 0, 0)),
                  pl.BlockSpec((1, 1, PAST_LEN, LANES), lambda b: (b, l, 0, 0)),
                  pl.BlockSpec((1,) + wk.shape[1:], lay3),
                  pl.BlockSpec((1,) + wv.shape[1:], lay3)],
        out_specs=[pl.BlockSpec((PAST_LEN, MLA_HEADS * LANES), lambda b: (b, 0)),
                   pl.BlockSpec((PAST_LEN, MLA_HEADS * MLA_V), lambda b: (b, 0))],
        out_shape=[jax.ShapeDtypeStruct((n, MLA_HEADS * LANES), BF),
                   jax.ShapeDtypeStruct((n, MLA_HEADS * MLA_V), BF)],
        compiler_params=_params("parallel"),
        name="kv_cache",
    )(cache_ckv, cache_misc, wk, wv)


def _softmax_numerator(s_ref, p_ref, sink=None, row_block=128):
    rows, cols = s_ref.shape
    mx = jnp.max(s_ref[...], axis=-1, keepdims=True)
    if sink is not None:
        mx = jnp.maximum(mx, sink)
    sums = []
    for r0 in range(0, rows, row_block):
        mb = jnp.broadcast_to(mx[r0:r0 + row_block], (row_block, LANES))
        part = jnp.zeros((row_block, LANES), F32)
        for c in range(0, cols, LANES):
            p = jnp.exp2(s_ref[r0:r0 + row_block, c:c + LANES] - mb)
            part = part + p
            p_ref[r0:r0 + row_block, c:c + LANES] = p.astype(BF)
        sums.append(jnp.sum(part, axis=-1, keepdims=True))
    den = jnp.concatenate(sums, axis=0)
    return den if sink is None else den + jnp.exp2(sink - mx)


def _mla_kernel(*refs, n_seg):
    q_ref = refs[0]
    k_refs = refs[1:1 + n_seg]
    v_refs = refs[1 + n_seg:1 + 2 * n_seg]
    o_ref = refs[-5]
    s_refs, p_refs = refs[-4:-2], refs[-2:]
    tq = q_ref.shape[0]
    lane = lax.broadcasted_iota(jnp.int32, (tq, LANES), 1)

    def scores(h):
        off = 0
        for k in k_refs:
            s_refs[h % 2][:, off:off + k.shape[0]] = _nt(q_ref[:, h * LANES:(h + 1) * LANES],
                                                         k[:, h * LANES:(h + 1) * LANES])
            off += k.shape[0]

    scores(0)
    outs = []
    for h in range(MLA_HEADS):
        if h + 1 < MLA_HEADS:
            scores(h + 1)
        den = _softmax_numerator(s_refs[h % 2], p_refs[h % 2])
        pair = h // 2
        acc, off = None, 0
        for v in v_refs:
            pv = jnp.dot(p_refs[h % 2][:, off:off + v.shape[0]], v[:, pair * LANES:(pair + 1) * LANES],
                         preferred_element_type=F32)
            acc = pv if acc is None else acc + pv
            off += v.shape[0]
        outs.append(acc / den)
        if h % 2 == 1:
            o_ref[:, pair * LANES:(pair + 1) * LANES] = jnp.where(lane < MLA_V, outs[-2], outs[-1]).astype(BF)


def _mla_attend(q, ks, vs, n_batch, t, row0, k_blocks, prev, tq=TQ_MLA):
    n_seg = len(ks)
    n_keys = sum(s for _, s in k_blocks)
    qb0 = row0 // tq
    tiles = t // tq
    q_map = lambda b, i: (qb0 + b * tiles + i, 0)
    in_specs = [pl.BlockSpec((tq, MLA_HEADS * LANES), q_map)]
    for (b0, s), width in ([(kb, MLA_HEADS * LANES) for kb in k_blocks]
                           + [(kb, MLA_HEADS * MLA_V) for kb in k_blocks]):
        in_specs.append(pl.BlockSpec((s, width), functools.partial(lambda b, i, b0: (b0 + b, 0), b0=b0)))
    args = [q, *ks, *vs]
    aliases = {}
    if prev is not None:
        args.append(prev)
        in_specs.append(_any_spec())
        aliases = {len(args) - 1: 0}
    return pl.pallas_call(
        functools.partial(_mla_kernel, n_seg=n_seg),
        grid=(n_batch, tiles),
        in_specs=in_specs,
        out_specs=pl.BlockSpec((tq, MLA_HEADS * MLA_V), q_map),
        out_shape=jax.ShapeDtypeStruct((N_TOK, MLA_HEADS * MLA_V), BF),
        input_output_aliases=aliases,
        scratch_shapes=[pltpu.VMEM((tq, n_keys), F32)] * 2 + [pltpu.VMEM((tq, n_keys), BF)] * 2,
        compiler_params=_params("parallel", "parallel"),
        name="mla_attend",
    )(*args)


def _gqa_heads(q_ref, k_segs, v_segs, masks, sink_ref, o_ref, s_refs, p_refs):
    tq = q_ref.shape[0]
    lane_q = lax.broadcasted_iota(jnp.int32, (tq, LANES), 1)
    lo_q = lane_q < SWA_HD
    k_roll = [pltpu.roll(k, SWA_HD, 1) for k in k_segs]
    v_roll = [pltpu.roll(v, SWA_HD, 1) for v in v_segs]
    heads = range(SWA_KV_HEADS)
    sinks = []
    for kvh in heads:
        tiles = [q_ref[:, (kvh * 2 + j) * LANES:(kvh * 2 + j + 1) * LANES] for j in range(2)]
        zero = jnp.zeros_like(tiles[0])
        qs = jnp.concatenate([jnp.where(lo_q, tiles[0], zero), jnp.where(lo_q, zero, tiles[0]),
                              jnp.where(lo_q, tiles[1], zero), jnp.where(lo_q, zero, tiles[1])], axis=0)
        sinks.append(jnp.concatenate(
            [sink_ref[0, kvh * SWA_GROUP + g:kvh * SWA_GROUP + g + 1, :]
             for g in range(SWA_GROUP) for _ in range(tq // LANES)], axis=1) * LOG2_E)
        off = 0
        for k, kr, msk in zip(k_segs, k_roll, masks):
            lane_k = lax.broadcasted_iota(jnp.int32, k.shape, 1)
            first = (lane_k < SWA_HD) == (kvh == 0)
            kd = jnp.where(first, k, kr)
            s = _bdot_nt(kd, qs)
            if msk is not None:
                s = jnp.where(msk, s, -1e30)
            s_refs[kvh][off:off + k.shape[0], :] = s
            off += k.shape[0]
    dens = [_softmax_numerator_t(s_refs[kvh], p_refs[kvh], sinks[kvh]) for kvh in heads]
    for kvh in heads:
        acc, off = None, 0
        for v, vr in zip(v_segs, v_roll):
            lane_v = lax.broadcasted_iota(jnp.int32, v.shape, 1)
            first = (lane_v < SWA_HD) == (kvh == 0)
            vd = jnp.where(first, v, vr)
            pv = lax.dot_general(vd.astype(BF), p_refs[kvh][off:off + v.shape[0], :],
                                 (((0,), (0,)), ((), ())), preferred_element_type=F32)
            acc = pv if acc is None else acc + pv
            off += v.shape[0]
        o = (acc / dens[kvh]).T
        for j in range(2):
            o_ref[:, (kvh * 2 + j) * LANES:(kvh * 2 + j + 1) * LANES] = jnp.where(
                lo_q, o[(2 * j) * tq:(2 * j + 1) * tq], o[(2 * j + 1) * tq:(2 * j + 2) * tq]).astype(BF)


def _softmax_numerator_t(s_ref, p_ref, sink, row_block=128):
    keys, cols = s_ref.shape
    dens = []
    for c in range(0, cols, LANES):
        mx = s_ref[0:row_block, c:c + LANES]
        for r0 in range(row_block, keys, row_block):
            mx = jnp.maximum(mx, s_ref[r0:r0 + row_block, c:c + LANES])
        snk = sink[:, c:c + LANES]
        m = jnp.maximum(jnp.max(mx, axis=0, keepdims=True), snk)
        mb = jnp.broadcast_to(m, (row_block, LANES))
        part = jnp.zeros((row_block, LANES), F32)
        for r0 in range(0, keys, row_block):
            p = jnp.exp2(s_ref[r0:r0 + row_block, c:c + LANES] - mb)
            part = part + p
            p_ref[r0:r0 + row_block, c:c + LANES] = p.astype(BF)
        dens.append(jnp.sum(part, axis=0, keepdims=True) + jnp.exp2(snk - m))
    return jnp.concatenate(dens, axis=1)


def _swa_scratch(keys, rows):
    return ([pltpu.VMEM((keys, rows), F32)] * SWA_KV_HEADS + [pltpu.VMEM((keys, rows), BF)] * SWA_KV_HEADS)


def _swa_ctx_kernel(q_ref, k_ref, v_ref, sink_ref, o_ref, *scratch):
    _gqa_heads(q_ref, [k_ref[...]], [v_ref[...]], [None], sink_ref, o_ref,
               scratch[:SWA_KV_HEADS], scratch[SWA_KV_HEADS:])


def _swa_ctx(sq, sk, sv, sink_rows, l):
    return pl.pallas_call(
        _swa_ctx_kernel,
        grid=(BATCH,),
        in_specs=[pl.BlockSpec((SEQ, SWA_HEADS * SWA_HD), lambda b: (b, 0)),
                  pl.BlockSpec((SEQ, LANES), lambda b: (b, 0)),
                  pl.BlockSpec((SEQ, LANES), lambda b: (b, 0)),
                  pl.BlockSpec((1, SWA_HEADS, LANES), lambda b: (l, 0, 0))],
        out_specs=pl.BlockSpec((SEQ, SWA_HEADS * SWA_HD), lambda b: (b, 0)),
        out_shape=jax.ShapeDtypeStruct((N_TOK, SWA_HEADS * SWA_HD), BF),
        scratch_shapes=_swa_scratch(SEQ, SWA_GROUP * SEQ),
        compiler_params=_params("parallel"),
        name="swa_ctx",
    )(sq, sk, sv, sink_rows)


def _swa_lat_kernel(q_ref, kp_ref, kc_ref, kn_ref, vp_ref, vc_ref, vn_ref, kx_ref, vx_ref, sink_ref, prev_ref,
                    o_ref, *scratch):
    w = SWA_WINDOW
    n = pl.program_id(1)
    nb = pl.num_programs(1)
    k_band = jnp.concatenate([kp_ref[...], kc_ref[...], kn_ref[...]], axis=0)
    v_band = jnp.concatenate([vp_ref[...], vc_ref[...], vn_ref[...]], axis=0)
    rows = SWA_GROUP * w
    r = lax.broadcasted_iota(jnp.int32, (3 * w, rows), 1) & (w - 1)
    c = lax.broadcasted_iota(jnp.int32, (3 * w, rows), 0)
    valid = (c >= r) & (c <= r + 2 * w) & ((c >= w) | (n > 0)) & ((c < 2 * w) | (n < nb - 1))
    _gqa_heads(q_ref, [k_band, kx_ref[0, 0]], [v_band, vx_ref[0, 0]], [valid, None], sink_ref, o_ref,
               scratch[:SWA_KV_HEADS], scratch[SWA_KV_HEADS:])


def _swa_lat(sq, sk, sv, k_cache, v_cache, sink_rows, l, prev):
    w = SWA_WINDOW
    nb = DEC_SEQ // w
    q0 = N_CTX // w

    def blk(d):
        return lambda b, n: (q0 + b * nb + jnp.clip(n + d, 0, nb - 1), 0)

    kv_specs = [pl.BlockSpec((w, LANES), blk(d)) for d in (-1, 0, 1)]
    cache_spec = pl.BlockSpec((1, 1, PAST_LEN, LANES), lambda b, n: (b, l, 0, 0))
    return pl.pallas_call(
        _swa_lat_kernel,
        grid=(DEC_BATCH, nb),
        in_specs=[pl.BlockSpec((w, SWA_HEADS * SWA_HD), blk(0))] + kv_specs + kv_specs
        + [cache_spec, cache_spec, pl.BlockSpec((1, SWA_HEADS, LANES), lambda b, n: (l, 0, 0)), _any_spec()],
        out_specs=pl.BlockSpec((w, SWA_HEADS * SWA_HD), blk(0)),
        out_shape=jax.ShapeDtypeStruct((N_TOK, SWA_HEADS * SWA_HD), BF),
        input_output_aliases={10: 0},
        scratch_shapes=_swa_scratch(3 * w + PAST_LEN, SWA_GROUP * w),
        compiler_params=_params("parallel", "parallel"),
        name="swa_lat",
    )(sq, sk, sk, sk, sv, sv, sv, k_cache, v_cache, sink_rows, prev)


def _gdn_conv_kernel(*refs):
    x_ref, w_ref, o_ref = refs[0], refs[1], refs[-1]
    x = x_ref[...]
    t = x.shape[0]
    w = w_ref[0]
    row = lax.broadcasted_iota(jnp.int32, x.shape, 0)
    half = GDN_CONV // 2
    acc = x * w[half:half + 1, :]
    for k in range(GDN_CONV):
        d = k - half
        if d == 0:
            continue
        xs = pltpu.roll(x, (-d) % t, 0)
        ok = (row + d >= 0) & (row + d < t)
        acc = acc + jnp.where(ok, xs, 0.0) * w[k:k + 1, :]
    y = _silu(acc)
    is_qk = pl.program_id(1) < 2
    cols = []
    for h in range(x.shape[1] // LANES):
        yh = y[:, h * LANES:(h + 1) * LANES]
        nrm = lax.rsqrt(jnp.sum(yh * yh, axis=-1, keepdims=True) + 1e-6)
        cols.append(yh * jnp.where(is_qk, nrm, 1.0))
    o_ref[...] = jnp.concatenate(cols, axis=1)


def _gdn_conv(g3, conv_w, l, row0, n_seq, t, prev):
    blk0 = row0 // t
    args = [g3, conv_w]
    in_specs = [pl.BlockSpec((t, GDN_QK_DIM), lambda s, j: (blk0 + s, j)),
                pl.BlockSpec((1, GDN_CONV, GDN_QK_DIM), lambda s, j: (l, 0, j))]
    aliases = {}
    if prev is not None:
        args.append(prev)
        in_specs.append(_any_spec())
        aliases = {2: 0}
    return pl.pallas_call(
        _gdn_conv_kernel,
        grid=(n_seq, 3),
        in_specs=in_specs,
        out_specs=pl.BlockSpec((t, GDN_QK_DIM), lambda s, j: (blk0 + s, j)),
        out_shape=jax.ShapeDtypeStruct((N_TOK, GDN_CONV_CH), F32),
        input_output_aliases=aliases,
        compiler_params=_params("parallel", "parallel"),
        name="gdn_conv",
    )(*args)


def _gdn_chunks(probs):
    c = GDN_CHUNK
    ri = lax.broadcasted_iota(jnp.int32, (c, c), 0)
    ci = lax.broadcasted_iota(jnp.int32, (c, c), 1)
    lower_incl, upper_incl = ri >= ci, ri <= ci
    eye = (ri == ci).astype(F32)
    n = len(probs)
    incl = [upper_incl if p["backward"] else lower_incl for p in probs]
    incl_t = [lower_incl if p["backward"] else upper_incl for p in probs]
    strict = [(ri < ci) if p["backward"] else (ri > ci) for p in probs]
    gc_col = [jnp.sum(jnp.where(incl[i], probs[i]["g_row"], 0.0), axis=1, keepdims=True) for i in range(n)]
    gc_row = [jnp.sum(jnp.where(incl_t[i], probs[i]["g_col"], 0.0), axis=0, keepdims=True) for i in range(n)]
    decay = [jnp.where(incl[i], jnp.exp(gc_col[i] - gc_row[i]), 0.0) for i in range(n)]
    q = [p["q"] * (GDN_DK ** -0.5) for p in probs]
    kb = [p["k"] * p["beta"] for p in probs]
    kk = [_bdot_nt(kb[i], probs[i]["k"]) for i in range(n)]
    qk = [_bdot_nt(q[i], probs[i]["k"]) for i in range(n)]
    pw = [jnp.where(strict[i], -(kk[i] * decay[i]), 0.0) for i in range(n)]
    inv = [eye + m for m in pw]
    for _ in range(int(np.log2(c)) - 1):
        pw = [_dot_3pass(m, m) for m in pw]
        inv = [inv[i] + _dot_3pass(inv[i], pw[i]) for i in range(n)]
    e_col = [jnp.exp(g) for g in gc_col]
    uw = [_bdot(inv[i], jnp.concatenate([probs[i]["v"] * probs[i]["beta"], kb[i] * e_col[i]], axis=1))
          for i in range(n)]
    a = [jnp.where(incl[i], qk[i] * decay[i], 0.0) for i in range(n)]
    g_last = [gc_col[i][0:1, :] if probs[i]["backward"] else gc_col[i][c - 1:c, :] for i in range(n)]
    k_dec = [probs[i]["k"] * jnp.exp(g_last[i] - gc_col[i]) for i in range(n)]
    wq = [_bdot(jnp.concatenate([uw[i][:, GDN_DV:], q[i] * e_col[i]], axis=0), probs[i]["state"])
          for i in range(n)]
    v_new = [uw[i][:, :GDN_DV] - wq[i][:c] for i in range(n)]
    o = [wq[i][c:] + _bdot(a[i], v_new[i]) for i in range(n)]
    s_new = [probs[i]["state"] * jnp.exp(g_last[i]) + _bdot_tn(k_dec[i], v_new[i]) for i in range(n)]
    return list(zip(o, s_new))


def _gate_rows(misc):
    shape = (2 * GDN_NH, LANES)
    sel = (lax.broadcasted_iota(jnp.int32, shape, 1)
           == lax.broadcasted_iota(jnp.int32, shape, 0) + _M_G).astype(BF)
    b1, b2, b3 = _split3(misc)
    return _nt(sel, b1) + (_nt(sel, b2) + _nt(sel, b3))


def _gdn_kernel(*refs, context, n_par):
    qf_ref, qb_ref, mf_ref, mb_ref = refs[:4]
    st_ref = refs[-1]
    if context:
        of_ref, ob_ref, sfin_ref = refs[-4:-1]
    else:
        s0_ref = refs[4]
        of_ref, ob_ref = refs[-3:-1]
    n = pl.program_id(1)

    @pl.when(n == 0)
    def _():
        if context:
            st_ref[...] = jnp.zeros_like(st_ref)
        else:
            st_ref[...] = s0_ref[:, 0]

    probs = []
    for s in range(n_par):
        for d, (q_ref, m_ref) in enumerate([(qf_ref, mf_ref), (qb_ref, mb_ref)]):
            qkv = q_ref[s]
            misc = m_ref[s]
            rows = _gate_rows(misc)
            for h in range(GDN_HEADS):
                i = d * GDN_HEADS + h
                probs.append(dict(
                    q=qkv[:, h * GDN_DK:(h + 1) * GDN_DK],
                    k=qkv[:, GDN_QK_DIM + h * GDN_DK:GDN_QK_DIM + (h + 1) * GDN_DK],
                    v=qkv[:, 2 * GDN_QK_DIM + h * GDN_DV:2 * GDN_QK_DIM + (h + 1) * GDN_DV],
                    g_col=misc[:, _M_G + i:_M_G + i + 1], g_row=rows[i:i + 1, :],
                    beta=misc[:, _M_B + i:_M_B + i + 1],
                    state=st_ref[s, i], backward=(d == 1)))
    results = _gdn_chunks(probs)
    for s in range(n_par):
        for d, o_ref in enumerate([of_ref, ob_ref]):
            for h in range(GDN_HEADS):
                i = d * GDN_HEADS + h
                o, s_new = results[s * GDN_NH + i]
                o_ref[s, :, h * GDN_DV:(h + 1) * GDN_DV] = o
                st_ref[s, i] = s_new

    if context:
        @pl.when(n == pl.num_programs(1) - 1)
        def _():
            sfin_ref[:, 0] = st_ref[...]


def _gdn(qkv, misc, l, t, n_seq, seq0, s0, prevs, n_par=GDN_PAR):
    c = GDN_CHUNK
    nc = t // c
    context = s0 is None
    sb0 = seq0 // n_par
    n_all = N_TOK // t
    fwd = lambda s, n: (sb0 + s, n, 0)
    bwd = lambda s, n: (sb0 + s, nc - 1 - n, 0)
    st_spec = pl.BlockSpec((n_par, 1, GDN_NH, GDN_DK, GDN_DV), lambda s, n: (s, l, 0, 0, 0))
    qkv3 = qkv.reshape(n_all, t, GDN_CONV_CH)
    misc3 = misc.reshape(n_all, t, LANES)
    in_specs = [pl.BlockSpec((n_par, c, GDN_CONV_CH), fwd), pl.BlockSpec((n_par, c, GDN_CONV_CH), bwd),
                pl.BlockSpec((n_par, c, LANES), fwd), pl.BlockSpec((n_par, c, LANES), bwd)]
    args = [qkv3, qkv3, misc3, misc3]
    if not context:
        in_specs.append(st_spec)
        args.append(s0)
    out_specs = [pl.BlockSpec((n_par, c, GDN_V_DIM), fwd), pl.BlockSpec((n_par, c, GDN_V_DIM), bwd)]
    out_shape = [jax.ShapeDtypeStruct((n_all, t, GDN_V_DIM), F32)] * 2
    if context:
        out_specs.append(st_spec)
        out_shape.append(jax.ShapeDtypeStruct((n_seq, DEPTH, GDN_NH, GDN_DK, GDN_DV), F32))
    aliases = {}
    for j, p in enumerate(prevs):
        if p is not None:
            aliases[len(args)] = j
            args.append(p.reshape(out_shape[j].shape))
            in_specs.append(_any_spec())
    return pl.pallas_call(
        functools.partial(_gdn_kernel, context=context, n_par=n_par),
        grid=(n_seq // n_par, nc),
        in_specs=in_specs, out_specs=out_specs, out_shape=out_shape,
        input_output_aliases=aliases,
        scratch_shapes=[pltpu.VMEM((n_par, GDN_NH, GDN_DK, GDN_DV), F32)],
        compiler_params=_params("parallel", "arbitrary"),
        name="gdn",
    )(*args)


def _out_kernel(x_ref, mod_ref, om_ref, os_ref, gf_ref, gb_ref, gz_ref, gn_ref, w_ref, o_ref):
    gate = mod_ref[0, 0][:, 2 * D_MODEL:]
    s = gf_ref[...] + gb_ref[...]
    gz = gz_ref[...]
    cols = []
    for h in range(GDN_HEADS):
        sh = s[:, h * GDN_DV:(h + 1) * GDN_DV]
        cols.append(_rms(sh, gn_ref[0]) * _silu(gz[:, h * GDN_DV:(h + 1) * GDN_DV]))
    og = jnp.concatenate(cols, axis=1)
    n_m = MLA_HEADS * MLA_V
    n_s = SWA_HEADS * SWA_HD
    y = (jnp.dot(om_ref[...], w_ref[0, :n_m, :], preferred_element_type=F32)
         + jnp.dot(os_ref[...], w_ref[0, n_m:n_m + n_s, :], preferred_element_type=F32)
         + jnp.dot(og.astype(BF), w_ref[0, n_m + n_s:, :], preferred_element_type=F32))
    o_ref[...] = x_ref[...] + gate * y


def _out_proj(x, mods, l, o_mla, o_swa, o_gf, o_gb, gz, gdn_norm, w_out, tm=TM_OUT):
    row = lambda i: (i, 0)
    lay3 = lambda i: (l, 0, 0)
    return pl.pallas_call(
        _out_kernel,
        grid=(N_TOK // tm,),
        in_specs=[pl.BlockSpec((tm, D_MODEL), row),
                  _mod_spec(l, 1, tm),
                  pl.BlockSpec((tm, MLA_HEADS * MLA_V), row),
                  pl.BlockSpec((tm, SWA_HEADS * SWA_HD), row),
                  pl.BlockSpec((tm, GDN_V_DIM), row),
                  pl.BlockSpec((tm, GDN_V_DIM), row),
                  pl.BlockSpec((tm, GDN_V_DIM), row),
                  pl.BlockSpec((1, 1, GDN_DV), lay3),
                  pl.BlockSpec((1,) + w_out.shape[1:], lay3)],
        out_specs=pl.BlockSpec((tm, D_MODEL), row),
        out_shape=jax.ShapeDtypeStruct((N_TOK, D_MODEL), F32),
        compiler_params=_params("parallel"),
        name="out_proj",
    )(x, mods, o_mla, o_swa, o_gf, o_gb, gz, gdn_norm.reshape(DEPTH, 1, GDN_DV), w_out)


def _rot_columns(w, dim):
    shp = w.shape
    w6 = w.reshape(shp[:-1] + (shp[-1] // dim, 2, 2, dim // 4))
    sign = jnp.asarray([-1.0, 1.0], F32).reshape(2, 1)
    return (jnp.flip(w6, axis=-2) * sign).reshape(shp)


def _axial_rope(n_tokens, dim):
    f32 = np.float32
    rows = n_tokens // GRID_W
    row = np.repeat(np.arange(rows, dtype=f32), GRID_W)
    col = np.tile(np.arange(GRID_W, dtype=f32), rows)
    axis_dim = dim // 2
    inv_freq = (f32(1.0) / (f32(ROPE_BASE) ** (np.arange(0, axis_dim, 2, dtype=f32) / f32(axis_dim)))).astype(f32)
    ang_r = row[:, None] * inv_freq[None, :]
    ang_c = col[:, None] * inv_freq[None, :]
    ang = np.concatenate([ang_r, ang_r, ang_c, ang_c], axis=-1).astype(f32)
    return np.cos(ang).astype(f32), np.sin(ang).astype(f32)


def _rope_table(tm):
    cos_m, sin_m = _axial_rope(DEC_SEQ, MLA_ROPE)
    cos_s, sin_s = _axial_rope(DEC_SEQ, SWA_HD)
    t = DEC_SEQ
    one = lambda w: np.ones((t, w), np.float32)
    zero = lambda w: np.zeros((t, w), np.float32)
    lat = np.concatenate([
        cos_s, cos_s, sin_s, sin_s,
        cos_m, one(LANES - MLA_ROPE), sin_m, zero(LANES - MLA_ROPE),
        one(MLA_NOPE), cos_m, one(LANES - MLA_NOPE - MLA_ROPE),
        zero(MLA_NOPE), sin_m, zero(LANES - MLA_NOPE - MLA_ROPE)], axis=1)
    ident_row = np.concatenate([np.ones(LANES), np.zeros(LANES)] * 3).astype(np.float32)
    ident = np.broadcast_to(ident_row[None, :], (tm, _T_END))
    return jnp.asarray(np.concatenate([ident, lat], axis=0))


def _mixer_weights(w_in, mla_w_qb, mla_w_kvb):
    nl = DEPTH
    offs = np.cumsum([0, MLA_Q_LORA, MLA_KV_LORA, MLA_ROPE, SWA_HEADS * SWA_HD, SWA_KV_HEADS * SWA_HD,
                      SWA_KV_HEADS * SWA_HD, GDN_CONV_CH, GDN_V_DIM, 2 * GDN_NH])
    w_t = jnp.swapaxes(w_in, 1, 2)
    cq, ckv, krope, sq, sk, sv, g3, gz, gates = [w_t[:, offs[i]:offs[i + 1], :] for i in range(9)]

    def rot(w, dim):
        w6 = w.reshape(nl, w.shape[1] // dim, 2, 2, dim // 4, D_MODEL)
        return (jnp.flip(w6, axis=3) * jnp.asarray([-1.0, 1.0], F32).reshape(2, 1, 1)).reshape(w.shape)

    zeros = lambda n: jnp.zeros((nl, n, D_MODEL), F32)
    misc_a = jnp.concatenate([krope, gates, zeros(LANES - _M_END)], axis=1)
    misc_b = jnp.concatenate([rot(krope, MLA_ROPE), zeros(LANES - MLA_ROPE)], axis=1)
    win = jnp.concatenate([cq, ckv, sq, rot(sq, SWA_HD), sk, rot(sk, SWA_HD), sv, g3, gz, misc_a, misc_b],
                          axis=1).astype(BF)

    r = MLA_Q_LORA
    wq = mla_w_qb.reshape(nl, r, MLA_HEADS, MLA_NOPE + MLA_ROPE)
    nope, rope = wq[..., :MLA_NOPE], wq[..., MLA_NOPE:]
    pad = LANES - MLA_NOPE - MLA_ROPE
    z = lambda n: jnp.zeros((nl, r, MLA_HEADS, n), F32)
    qa = jnp.concatenate([nope, rope, z(pad)], axis=-1).reshape(nl, r, MLA_HEADS * LANES)
    qb = jnp.concatenate([z(MLA_NOPE), _rot_columns(rope, MLA_ROPE), z(pad)],
                         axis=-1).reshape(nl, r, MLA_HEADS * LANES)
    wqb = jnp.concatenate([qa, qb], axis=2).astype(BF)

    kvb = mla_w_kvb.reshape(nl, MLA_KV_LORA, MLA_HEADS, MLA_NOPE + MLA_V)
    k_nope = jnp.concatenate([kvb[..., :MLA_NOPE],
                              jnp.zeros((nl, MLA_KV_LORA, MLA_HEADS, LANES - MLA_NOPE), F32)],
                             axis=-1).reshape(nl, MLA_KV_LORA, MLA_HEADS * LANES)
    place = np.zeros((LANES, MLA_HEADS, LANES), np.float32)
    for i in range(MLA_ROPE):
        place[i, :, MLA_NOPE + i] = 1.0
    place = jnp.broadcast_to(jnp.asarray(place.reshape(1, LANES, MLA_HEADS * LANES)),
                             (nl, LANES, MLA_HEADS * LANES))
    wk = jnp.concatenate([k_nope, place], axis=1).astype(BF)
    wv = kvb[..., MLA_NOPE:].reshape(nl, MLA_KV_LORA, MLA_HEADS * MLA_V).astype(BF)
    return win, wqb, wk, wv


def _misc_rows(vals):
    rows = jnp.zeros((DEPTH, 1, LANES), F32)
    return rows.at[:, 0, _M_G:_M_B].set(vals.reshape(DEPTH, GDN_NH).astype(F32))


def kernel(x_prompt, x_sample, cache_mla_ckv, cache_mla_krope, cache_swa_k, cache_swa_v, state_gdn, c, c_ctx,
           w_ada, b_ada, norm_ffn1, ffn1_w1, ffn1_w2, norm_mix, w_in, mla_q_norm, mla_w_qb, mla_kv_norm,
           mla_w_kvb, swa_sink, gdn_conv_w, gdn_a_log, gdn_dt_bias, gdn_norm, w_out, norm_ffn2, ffn2_w1,
           ffn2_w2, final_norm):
    cond = jnp.concatenate([c_ctx[None, :], c, jnp.zeros((COND_ROWS - N_GROUPS, D_MODEL), F32)], axis=0)
    mods = _adaln(cond, w_ada, b_ada)[:, :N_GROUPS].reshape(DEPTH, N_GROUPS, 1, N_MOD * D_MODEL)
    tab = _rope_table(TM_PROJ)
    win, wqb, wk, wv = _mixer_weights(w_in, mla_w_qb, mla_w_kvb)
    w11, w12, w21, w22, wo = (w.astype(BF) for w in (ffn1_w1, ffn1_w2, ffn2_w1, ffn2_w2, w_out))
    alog, dtb = _misc_rows(gdn_a_log), _misc_rows(gdn_dt_bias)
    sink_rows = jnp.broadcast_to(swa_sink[:, :, None], (DEPTH, SWA_HEADS, LANES))
    cache_misc = jnp.pad(cache_mla_krope, ((0, 0), (0, 0), (0, 0), (0, LANES - MLA_ROPE)))
    cache_k = cache_swa_k.reshape(DEC_BATCH, DEPTH, PAST_LEN, LANES)
    cache_v = cache_swa_v.reshape(DEC_BATCH, DEPTH, PAST_LEN, LANES)
    s0 = state_gdn.reshape(DEC_BATCH, DEPTH, GDN_NH, GDN_DK, GDN_DV)

    xs = [x_prompt.reshape(N_CTX, D_MODEL), x_sample.reshape(N_LAT, D_MODEL)]
    new_ckv = new_st = None
    new_krope, new_sk, new_sv = [], [], []
    for l in range(DEPTH):
        x = _ffn(xs, mods, l, 0, norm_ffn1, w11, w12)
        (q_mla, new_ckv, misc, k_mla, v_mla, sq, sk, sv, g3, gz) = _proj(
            x, mods, l, norm_mix, win, mla_q_norm, wqb, mla_kv_norm, wk, wv, tab, alog, dtb, new_ckv)

        k_c, v_c = _kv_cache(cache_mla_ckv, cache_misc, l, wk, wv)
        o_mla = _mla_attend(q_mla, [k_mla], [v_mla], BATCH, SEQ, 0, [(0, SEQ)], None)
        o_mla = _mla_attend(q_mla, [k_mla, k_c], [v_mla, v_c], DEC_BATCH, DEC_SEQ, N_CTX,
                            [(N_CTX // DEC_SEQ, DEC_SEQ), (0, PAST_LEN)], o_mla)

        o_swa = _swa_ctx(sq, sk, sv, sink_rows, l)
        o_swa = _swa_lat(sq, sk, sv, cache_k, cache_v, sink_rows, l, o_swa)

        qkv = _gdn_conv(g3, gdn_conv_w, l, 0, BATCH, SEQ, None)
        qkv = _gdn_conv(g3, gdn_conv_w, l, N_CTX, DEC_BATCH, DEC_SEQ, qkv)
        o_gf, o_gb, new_st = _gdn(qkv, misc, l, SEQ, BATCH, 0, None, [None, None, new_st])
        o_gf, o_gb = _gdn(qkv, misc, l, DEC_SEQ, DEC_BATCH, N_CTX // DEC_SEQ, s0, [o_gf, o_gb])

        x = _out_proj(x, mods, l, o_mla, o_swa, o_gf.reshape(N_TOK, GDN_V_DIM), o_gb.reshape(N_TOK, GDN_V_DIM),
                      gz, gdn_norm, wo)
        if l + 1 < DEPTH:
            xs = [_ffn([x], mods, l, 2, norm_ffn2, w21, w22)]
        else:
            y_prompt, y_sample = _ffn([x], mods, l, 2, norm_ffn2, w21, w22, final_gain=final_norm)

        new_krope.append(misc[:N_CTX, :MLA_ROPE].reshape(BATCH, SEQ, MLA_ROPE))
        new_sk.append(sk[:N_CTX].reshape(BATCH, SEQ, SWA_KV_HEADS, SWA_HD))
        new_sv.append(sv[:N_CTX].reshape(BATCH, SEQ, SWA_KV_HEADS, SWA_HD))

    return (y_prompt.reshape(BATCH, SEQ, D_MODEL), y_sample.reshape(DEC_BATCH, DEC_SEQ, D_MODEL), new_ckv,
            jnp.stack(new_krope, axis=1), jnp.stack(new_sk, axis=1), jnp.stack(new_sv, axis=1),
            new_st.reshape(BATCH, DEPTH, 2, GDN_HEADS, GDN_DK, GDN_DV))
```

```python
import functools

import numpy as np
import jax
import jax.numpy as jnp
from jax import lax
from jax.experimental import pallas as pl
from jax.experimental.pallas import tpu as pltpu

D_MODEL = 1024
BATCH = 16
SEQ = 256
DEPTH = 2
DEC_BATCH = 2
DEC_SEQ = 2048
PAST_LEN = 512
GRID_W = 64
ROPE_BASE = 10000.0
NORM_EPS = 1e-6
N_MOD = 9
D_FF = 2816
MLA_HEADS = 8
MLA_Q_LORA = 384
MLA_KV_LORA = 256
MLA_NOPE = 64
MLA_ROPE = 32
MLA_V = 64
SWA_HEADS = 8
SWA_KV_HEADS = 2
SWA_GROUP = SWA_HEADS // SWA_KV_HEADS
SWA_HD = 64
SWA_WINDOW = 128
GDN_HEADS = 4
GDN_DK = 128
GDN_DV = 128
GDN_CONV = 5
GDN_CHUNK = 64
GDN_QK_DIM = GDN_HEADS * GDN_DK
GDN_V_DIM = GDN_HEADS * GDN_DV
GDN_CONV_CH = 2 * GDN_QK_DIM + GDN_V_DIM
GDN_NH = 2 * GDN_HEADS

N_CTX = BATCH * SEQ
N_LAT = DEC_BATCH * DEC_SEQ
N_TOK = N_CTX + N_LAT
N_GROUPS = 1 + DEC_BATCH
COND_ROWS = 8

LANES = 128
VMEM_LIMIT_BYTES = 56 * 1024 * 1024

TM_FFN = 256
TM_PROJ = 2 * SEQ
TM_OUT = 512
TQ_MLA = 256
GDN_PAR = 2

BF = jnp.bfloat16
F32 = jnp.float32
LOG2_E = 1.4426950408889634
MLA_Q_SCALE = (MLA_NOPE + MLA_ROPE) ** -0.5 * LOG2_E

_C_CQ = 0
_C_CKV = _C_CQ + MLA_Q_LORA
_C_SQ = _C_CKV + MLA_KV_LORA
_C_SQR = _C_SQ + SWA_HEADS * SWA_HD
_C_SK = _C_SQR + SWA_HEADS * SWA_HD
_C_SKR = _C_SK + SWA_KV_HEADS * SWA_HD
_C_SV = _C_SKR + SWA_KV_HEADS * SWA_HD
_C_G3 = _C_SV + SWA_KV_HEADS * SWA_HD
_C_GZ = _C_G3 + GDN_CONV_CH
_C_MA = _C_GZ + GDN_V_DIM
_C_MB = _C_MA + LANES
_C_END = _C_MB + LANES
_M_G = MLA_ROPE
_M_B = MLA_ROPE + GDN_NH
_M_END = MLA_ROPE + 2 * GDN_NH
_T_CS, _T_SS, _T_CA, _T_SA, _T_CQ, _T_SQ = (i * LANES for i in range(6))
_T_END = 6 * LANES


def _params(*sem):
    return pltpu.CompilerParams(dimension_semantics=sem, vmem_limit_bytes=VMEM_LIMIT_BYTES)


def _bdot(a, b):
    return jnp.dot(a.astype(BF), b.astype(BF), preferred_element_type=F32)


def _nt(a, b):
    return lax.dot_general(a, b, (((1,), (1,)), ((), ())), preferred_element_type=F32)


def _bdot_nt(a, b):
    return _nt(a.astype(BF), b.astype(BF))


def _bdot_tn(a, b):
    return lax.dot_general(a.astype(BF), b.astype(BF), (((0,), (0,)), ((), ())),
                           preferred_element_type=F32)


def _split3(a):
    b1 = a.astype(BF)
    r = a - b1.astype(F32)
    b2 = r.astype(BF)
    b3 = (r - b2.astype(F32)).astype(BF)
    return b1, b2, b3


def _silu(x):
    return x / (1.0 + jnp.exp(-x))


def _rms(x, gain, eps=NORM_EPS):
    return x * lax.rsqrt(jnp.mean(x * x, axis=-1, keepdims=True) + eps) * gain


def _group_of_row(r):
    return jnp.where(r < N_CTX, 0, 1 + (r - N_CTX) // DEC_SEQ)


def _any_spec():
    return pl.BlockSpec(memory_space=pl.ANY)


def _adaln_kernel(c_ref, w_ref, b_ref, o_ref):
    o_ref[0] = _bdot(_silu(c_ref[...]), w_ref[0]) + b_ref[0]


def _adaln(cond, w_ada, b_ada, tn=1536):
    n = N_MOD * D_MODEL
    return pl.pallas_call(
        _adaln_kernel,
        grid=(DEPTH, n // tn),
        in_specs=[pl.BlockSpec((COND_ROWS, D_MODEL), lambda l, j: (0, 0)),
                  pl.BlockSpec((1, D_MODEL, tn), lambda l, j: (l, 0, j)),
                  pl.BlockSpec((1, 1, tn), lambda l, j: (l, 0, j))],
        out_specs=pl.BlockSpec((1, COND_ROWS, tn), lambda l, j: (l, 0, j)),
        out_shape=jax.ShapeDtypeStruct((DEPTH, COND_ROWS, n), F32),
        compiler_params=_params("parallel", "parallel"),
        name="adaln",
    )(cond, w_ada, b_ada.reshape(DEPTH, 1, n))


def _mod_spec(l, which, tm):
    return pl.BlockSpec((1, 1, 1, 3 * D_MODEL), lambda i: (l, _group_of_row(i * tm), 0, which))


def _ffn_kernel(*refs, n_in, n_a_tiles, final):
    x_refs = refs[:n_in]
    mod_ref, gain_ref, w1_ref, w2_ref = refs[n_in:n_in + 4]
    rest = refs[n_in + 4:]
    i = pl.program_id(0)
    if n_in == 2:
        x = jnp.where(i < n_a_tiles, x_refs[0][...], x_refs[1][...])
    else:
        x = x_refs[0][...]
    mod = mod_ref[0, 0]
    shift, scale, gate = mod[:, :D_MODEL], mod[:, D_MODEL:2 * D_MODEL], mod[:, 2 * D_MODEL:]
    h = _rms(x, gain_ref[0]) * (1.0 + scale) + shift
    gu = jnp.dot(h.astype(BF), w1_ref[0], preferred_element_type=F32)
    a = _silu(gu[:, :D_FF]) * gu[:, D_FF:]
    y = x + gate * (0.5 * jnp.dot(a.astype(BF), w2_ref[0], preferred_element_type=F32))
    if final:
        fg_ref, oa_ref, ob_ref = rest
        yn = _rms(y, fg_ref[...])

        @pl.when(i < n_a_tiles)
        def _():
            oa_ref[...] = yn

        @pl.when(i >= n_a_tiles)
        def _():
            ob_ref[...] = yn
    else:
        rest[0][...] = y


def _ffn(xs, mods, l, which, gain, w1, w2, final_gain=None, tm=TM_FFN):
    n_a = N_CTX // tm
    row = lambda i: (i, 0)
    first = lambda i: (jnp.minimum(i, n_a - 1), 0)
    second = lambda i: (jnp.maximum(i - n_a, 0), 0)
    lay3 = lambda i: (l, 0, 0)
    x_specs = ([pl.BlockSpec((tm, D_MODEL), row)] if len(xs) == 1
               else [pl.BlockSpec((tm, D_MODEL), first), pl.BlockSpec((tm, D_MODEL), second)])
    in_specs = x_specs + [_mod_spec(l, which, tm),
                          pl.BlockSpec((1, 1, D_MODEL), lay3),
                          pl.BlockSpec((1, D_MODEL, 2 * D_FF), lay3),
                          pl.BlockSpec((1, D_FF, D_MODEL), lay3)]
    args = list(xs) + [mods, gain.reshape(DEPTH, 1, D_MODEL), w1, w2]
    if final_gain is None:
        out_specs = pl.BlockSpec((tm, D_MODEL), row)
        out_shape = jax.ShapeDtypeStruct((N_TOK, D_MODEL), F32)
    else:
        in_specs.append(pl.BlockSpec((1, D_MODEL), lambda i: (0, 0)))
        args.append(final_gain.reshape(1, D_MODEL))
        out_specs = [pl.BlockSpec((tm, D_MODEL), first), pl.BlockSpec((tm, D_MODEL), second)]
        out_shape = [jax.ShapeDtypeStruct((N_CTX, D_MODEL), F32), jax.ShapeDtypeStruct((N_LAT, D_MODEL), F32)]
    return pl.pallas_call(
        functools.partial(_ffn_kernel, n_in=len(xs), n_a_tiles=n_a, final=final_gain is not None),
        grid=(N_TOK // tm,),
        in_specs=in_specs, out_specs=out_specs, out_shape=out_shape,
        compiler_params=_params("arbitrary" if final_gain is not None else "parallel"),
        name="ffn",
    )(*args)


def _kv_expand(ckv_n, misc, wk, wv):
    kin = jnp.concatenate([ckv_n, misc], axis=1).astype(BF)
    kk = jnp.dot(kin, wk, preferred_element_type=F32)
    vv = jnp.dot(ckv_n.astype(BF), wv, preferred_element_type=F32)
    return kk.astype(BF), vv.astype(BF)


def _proj_kernel(*refs, n_ctx_tiles):
    (x_ref, mod_ref, gain_ref, win_ref, qg_ref, wqb_ref, kvg_ref, wk_ref, wv_ref, tab_ref,
     alog_ref, dtb_ref) = refs[:12]
    q_ref, ckv_ref, misc_ref, kmla_ref, vmla_ref, sq_ref, sk_ref, sv_ref, g3_ref, gz_ref = refs[-10:]
    i = pl.program_id(0)
    mod = mod_ref[0, 0]
    shift, scale = mod[:, :D_MODEL], mod[:, D_MODEL:2 * D_MODEL]
    n_sub = x_ref.shape[0] // SEQ
    us = []
    for j in range(n_sub):
        h = _rms(x_ref[j * SEQ:(j + 1) * SEQ, :], gain_ref[0]) * (1.0 + scale) + shift
        us.append(_nt(h.astype(BF), win_ref[0]))
    for j, u in enumerate(us):
        rows = slice(j * SEQ, (j + 1) * SEQ)
        tab = tab_ref[rows, :]

        qn = _rms(u[:, _C_CQ:_C_CKV], qg_ref[0])
        q2 = jnp.dot(qn.astype(BF), wqb_ref[0], preferred_element_type=F32)
        nq = MLA_HEADS * LANES
        cosq = jnp.concatenate([tab[:, _T_CQ:_T_CQ + LANES]] * MLA_HEADS, axis=1)
        sinq = jnp.concatenate([tab[:, _T_SQ:_T_SQ + LANES]] * MLA_HEADS, axis=1)
        q_ref[rows, :] = ((q2[:, :nq] * cosq + q2[:, nq:] * sinq) * MLA_Q_SCALE).astype(BF)

        ckv_n = _rms(u[:, _C_CKV:_C_SQ], kvg_ref[0])

        @pl.when(i < n_ctx_tiles)
        def _():
            ckv_ref[j, 0] = ckv_n

        m = (u[:, _C_MA:_C_MB] * tab[:, _T_CA:_T_CA + LANES] + u[:, _C_MB:_C_END] * tab[:, _T_SA:_T_SA + LANES])
        lane = lax.broadcasted_iota(jnp.int32, m.shape, 1)
        z = m + dtb_ref[0]
        softplus = jnp.maximum(z, 0.0) + jnp.log(1.0 + jnp.exp(-jnp.abs(z)))
        decay = -jnp.exp(alog_ref[0]) * softplus
        strength = 1.0 / (1.0 + jnp.exp(-m))
        misc = jnp.where((lane >= _M_G) & (lane < _M_B), decay,
                         jnp.where((lane >= _M_B) & (lane < _M_END), strength, m))
        misc_ref[rows, :] = misc

        kk, vv = _kv_expand(ckv_n, misc, wk_ref[0], wv_ref[0])
        kmla_ref[rows, :] = kk
        vmla_ref[rows, :] = vv

        n_sq = SWA_HEADS * SWA_HD
        cos_s = tab[:, _T_CS:_T_CS + LANES]
        sin_s = tab[:, _T_SS:_T_SS + LANES]
        cos4 = jnp.concatenate([cos_s] * (n_sq // LANES), axis=1)
        sin4 = jnp.concatenate([sin_s] * (n_sq // LANES), axis=1)
        sq = u[:, _C_SQ:_C_SQR] * cos4 + u[:, _C_SQR:_C_SK] * sin4
        sq_ref[rows, :] = (sq * (SWA_HD ** -0.5 * LOG2_E)).astype(BF)
        sk_ref[rows, :] = u[:, _C_SK:_C_SKR] * cos_s + u[:, _C_SKR:_C_SV] * sin_s
        sv_ref[rows, :] = u[:, _C_SV:_C_G3]
        g3_ref[rows, :] = u[:, _C_G3:_C_GZ]
        gz_ref[rows, :] = u[:, _C_GZ:_C_MA]


def _proj(x, mods, l, gain, win, qg, wqb, kvg, wk, wv, tab, alog, dtb, ckv_prev, tm=TM_PROJ):
    assert tm % SEQ == 0
    n_ctx_tiles = N_CTX // tm
    lat_tiles = DEC_SEQ // tm
    lay3 = lambda i: (l, 0, 0)
    row = lambda i: (i, 0)

    def tab_map(i):
        return (jnp.where(i < n_ctx_tiles, 0, 1 + (i - n_ctx_tiles) % lat_tiles), 0)

    def lay_spec(a):
        return pl.BlockSpec((1,) + a.shape[1:], lay3)

    widths = [(MLA_HEADS * LANES, BF), None, (LANES, F32), (MLA_HEADS * LANES, BF),
              (MLA_HEADS * MLA_V, BF), (SWA_HEADS * SWA_HD, BF), (SWA_KV_HEADS * SWA_HD, F32),
              (SWA_KV_HEADS * SWA_HD, F32), (GDN_CONV_CH, F32), (GDN_V_DIM, F32)]
    out_specs = [pl.BlockSpec((tm, w[0]), row) if w else
                 pl.BlockSpec((tm // SEQ, 1, SEQ, MLA_KV_LORA),
                              lambda i: (jnp.minimum(i, n_ctx_tiles - 1), l, 0, 0))
                 for w in widths]
    out_shape = [jax.ShapeDtypeStruct((N_TOK, w[0]), w[1]) if w else
                 jax.ShapeDtypeStruct((BATCH, DEPTH, SEQ, MLA_KV_LORA), F32) for w in widths]
    qg, kvg = qg.reshape(DEPTH, 1, -1), kvg.reshape(DEPTH, 1, -1)
    args = [x, mods, gain.reshape(DEPTH, 1, D_MODEL), win, qg, wqb, kvg, wk, wv, tab, alog, dtb]
    in_specs = [pl.BlockSpec((tm, D_MODEL), row), _mod_spec(l, 1, tm),
                pl.BlockSpec((1, 1, D_MODEL), lay3), lay_spec(win), lay_spec(qg), lay_spec(wqb),
                lay_spec(kvg), lay_spec(wk), lay_spec(wv), pl.BlockSpec((tm, _T_END), tab_map),
                lay_spec(alog), lay_spec(dtb)]
    aliases = {}
    if ckv_prev is not None:
        args.append(ckv_prev)
        in_specs.append(_any_spec())
        aliases = {len(args) - 1: 1}
    return pl.pallas_call(
        functools.partial(_proj_kernel, n_ctx_tiles=n_ctx_tiles),
        grid=(N_TOK // tm,),
        in_specs=in_specs, out_specs=out_specs, out_shape=out_shape,
        input_output_aliases=aliases,
        compiler_params=_params("arbitrary"),
        name="proj",
    )(*args)


def _kv_cache_kernel(ckv_ref, misc_ref, wk_ref, wv_ref, k_ref, v_ref):
    kk, vv = _kv_expand(ckv_ref[0, 0], misc_ref[0, 0], wk_ref[0], wv_ref[0])
    k_ref[...] = kk
    v_ref[...] = vv


def _kv_cache(cache_ckv, cache_misc, l, wk, wv):
    lay3 = lambda b: (l, 0, 0)
    n = DEC_BATCH * PAST_LEN
    return pl.pallas_call(
        _kv_cache_kernel,
        grid=(DEC_BATCH,),
        in_specs=[pl.BlockSpec((1, 1, PAST_LEN, MLA_KV_LORA), lambda b: (b, l, 0, 0)),
                  pl.BlockSpec((1, 1, PAST_LEN, LANES), lambda b: (b, l, 0, 0)),
                  pl.BlockSpec((1,) + wk.shape[1:], lay3),
                  pl.BlockSpec((1,) + wv.shape[1:], lay3)],
        out_specs=[pl.BlockSpec((PAST_LEN, MLA_HEADS * LANES), lambda b: (b, 0)),
                   pl.BlockSpec((PAST_LEN, MLA_HEADS * MLA_V), lambda b: (b, 0))],
        out_shape=[jax.ShapeDtypeStruct((n, MLA_HEADS * LANES), BF),
                   jax.ShapeDtypeStruct((n, MLA_HEADS * MLA_V), BF)],
        compiler_params=_params("parallel"),
        name="kv_cache",
    )(cache_ckv, cache_misc, wk, wv)


def _softmax_numerator(s_ref, p_ref, sink=None, row_block=128):
    rows, cols = s_ref.shape
    mx = jnp.max(s_ref[...], axis=-1, keepdims=True)
    if sink is not None:
        mx = jnp.maximum(mx, sink)
    sums = []
    for r0 in range(0, rows, row_block):
        mb = jnp.broadcast_to(mx[r0:r0 + row_block], (row_block, LANES))
        part = jnp.zeros((row_block, LANES), F32)
        for c in range(0, cols, LANES):
            p = jnp.exp2(s_ref[r0:r0 + row_block, c:c + LANES] - mb)
            part = part + p
            p_ref[r0:r0 + row_block, c:c + LANES] = p.astype(BF)
        sums.append(jnp.sum(part, axis=-1, keepdims=True))
    den = jnp.concatenate(sums, axis=0)
    return den if sink is None else den + jnp.exp2(sink - mx)


def _mla_kernel(*refs, n_seg):
    q_ref = refs[0]
    k_refs = refs[1:1 + n_seg]
    v_refs = refs[1 + n_seg:1 + 2 * n_seg]
    o_ref = refs[-5]
    s_refs, p_refs = refs[-4:-2], refs[-2:]
    tq = q_ref.shape[0]
    lane = lax.broadcasted_iota(jnp.int32, (tq, LANES), 1)

    def scores(h):
        off = 0
        for k in k_refs:
            s_refs[h % 2][:, off:off + k.shape[0]] = _nt(q_ref[:, h * LANES:(h + 1) * LANES],
                                                         k[:, h * LANES:(h + 1) * LANES])
            off += k.shape[0]

    scores(0)
    outs = []
    for h in range(MLA_HEADS):
        if h + 1 < MLA_HEADS:
            scores(h + 1)
        den = _softmax_numerator(s_refs[h % 2], p_refs[h % 2])
        pair = h // 2
        acc, off = None, 0
        for v in v_refs:
            pv = jnp.dot(p_refs[h % 2][:, off:off + v.shape[0]], v[:, pair * LANES:(pair + 1) * LANES],
                         preferred_element_type=F32)
            acc = pv if acc is None else acc + pv
            off += v.shape[0]
        outs.append(acc / den)
        if h % 2 == 1:
            o_ref[:, pair * LANES:(pair + 1) * LANES] = jnp.where(lane < MLA_V, outs[-2], outs[-1]).astype(BF)


def _mla_attend(q, ks, vs, n_batch, t, row0, k_blocks, prev, tq=TQ_MLA):
    n_seg = len(ks)
    n_keys = sum(s for _, s in k_blocks)
    qb0 = row0 // tq
    tiles = t // tq
    q_map = lambda b, i: (qb0 + b * tiles + i, 0)
    in_specs = [pl.BlockSpec((tq, MLA_HEADS * LANES), q_map)]
    for (b0, s), width in ([(kb, MLA_HEADS * LANES) for kb in k_blocks]
                           + [(kb, MLA_HEADS * MLA_V) for kb in k_blocks]):
        in_specs.append(pl.BlockSpec((s, width), functools.partial(lambda b, i, b0: (b0 + b, 0), b0=b0)))
    args = [q, *ks, *vs]
    aliases = {}
    if prev is not None:
        args.append(prev)
        in_specs.append(_any_spec())
        aliases = {len(args) - 1: 0}
    return pl.pallas_call(
        functools.partial(_mla_kernel, n_seg=n_seg),
        grid=(n_batch, tiles),
        in_specs=in_specs,
        out_specs=pl.BlockSpec((tq, MLA_HEADS * MLA_V), q_map),
        out_shape=jax.ShapeDtypeStruct((N_TOK, MLA_HEADS * MLA_V), BF),
        input_output_aliases=aliases,
        scratch_shapes=[pltpu.VMEM((tq, n_keys), F32)] * 2 + [pltpu.VMEM((tq, n_keys), BF)] * 2,
        compiler_params=_params("parallel", "parallel"),
        name="mla_attend",
    )(*args)


def _gqa_heads(q_ref, k_segs, v_segs, masks, sink_ref, o_ref, s_refs, p_refs):
    tq = q_ref.shape[0]
    lane_q = lax.broadcasted_iota(jnp.int32, (tq, LANES), 1)
    lo_q = lane_q < SWA_HD
    k_roll = [pltpu.roll(k, SWA_HD, 1) for k in k_segs]
    v_roll = [pltpu.roll(v, SWA_HD, 1) for v in v_segs]
    heads = range(SWA_KV_HEADS)
    sinks = []
    for kvh in heads:
        tiles = [q_ref[:, (kvh * 2 + j) * LANES:(kvh * 2 + j + 1) * LANES] for j in range(2)]
        zero = jnp.zeros_like(tiles[0])
        qs = jnp.concatenate([jnp.where(lo_q, tiles[0], zero), jnp.where(lo_q, zero, tiles[0]),
                              jnp.where(lo_q, tiles[1], zero), jnp.where(lo_q, zero, tiles[1])], axis=0)
        sinks.append(jnp.concatenate(
            [sink_ref[0, kvh * SWA_GROUP + g:kvh * SWA_GROUP + g + 1, :]
             for g in range(SWA_GROUP) for _ in range(tq // LANES)], axis=1) * LOG2_E)
        off = 0
        for k, kr, msk in zip(k_segs, k_roll, masks):
            lane_k = lax.broadcasted_iota(jnp.int32, k.shape, 1)
            first = (lane_k < SWA_HD) == (kvh == 0)
            kd = jnp.where(first, k, kr)
            s = _bdot_nt(kd, qs)
            if msk is not None:
                s = jnp.where(msk, s, -1e30)
            s_refs[kvh][off:off + k.shape[0], :] = s
            off += k.shape[0]
    dens = [_softmax_numerator_t(s_refs[kvh], p_refs[kvh], sinks[kvh]) for kvh in heads]
    for kvh in heads:
        acc, off = None, 0
        for v, vr in zip(v_segs, v_roll):
            lane_v = lax.broadcasted_iota(jnp.int32, v.shape, 1)
            first = (lane_v < SWA_HD) == (kvh == 0)
            vd = jnp.where(first, v, vr)
            pv = lax.dot_general(vd.astype(BF), p_refs[kvh][off:off + v.shape[0], :],
                                 (((0,), (0,)), ((), ())), preferred_element_type=F32)
            acc = pv if acc is None else acc + pv
            off += v.shape[0]
        o = (acc / dens[kvh]).T
        for j in range(2):
            o_ref[:, (kvh * 2 + j) * LANES:(kvh * 2 + j + 1) * LANES] = jnp.where(
                lo_q, o[(2 * j) * tq:(2 * j + 1) * tq], o[(2 * j + 1) * tq:(2 * j + 2) * tq]).astype(BF)


def _softmax_numerator_t(s_ref, p_ref, sink, row_block=128):
    keys, cols = s_ref.shape
    dens = []
    for c in range(0, cols, LANES):
        mx = s_ref[0:row_block, c:c + LANES]
        for r0 in range(row_block, keys, row_block):
            mx = jnp.maximum(mx, s_ref[r0:r0 + row_block, c:c + LANES])
        snk = sink[:, c:c + LANES]
        m = jnp.maximum(jnp.max(mx, axis=0, keepdims=True), snk)
        mb = jnp.broadcast_to(m, (row_block, LANES))
        part = jnp.zeros((row_block, LANES), F32)
        for r0 in range(0, keys, row_block):
            p = jnp.exp2(s_ref[r0:r0 + row_block, c:c + LANES] - mb)
            part = part + p
            p_ref[r0:r0 + row_block, c:c + LANES] = p.astype(BF)
        dens.append(jnp.sum(part, axis=0, keepdims=True) + jnp.exp2(snk - m))
    return jnp.concatenate(dens, axis=1)


def _swa_scratch(keys, rows):
    return ([pltpu.VMEM((keys, rows), F32)] * SWA_KV_HEADS + [pltpu.VMEM((keys, rows), BF)] * SWA_KV_HEADS)


def _swa_ctx_kernel(q_ref, k_ref, v_ref, sink_ref, o_ref, *scratch):
    _gqa_heads(q_ref, [k_ref[...]], [v_ref[...]], [None], sink_ref, o_ref,
               scratch[:SWA_KV_HEADS], scratch[SWA_KV_HEADS:])


def _swa_ctx(sq, sk, sv, sink_rows, l):
    return pl.pallas_call(
        _swa_ctx_kernel,
        grid=(BATCH,),
        in_specs=[pl.BlockSpec((SEQ, SWA_HEADS * SWA_HD), lambda b: (b, 0)),
                  pl.BlockSpec((SEQ, LANES), lambda b: (b, 0)),
                  pl.BlockSpec((SEQ, LANES), lambda b: (b, 0)),
                  pl.BlockSpec((1, SWA_HEADS, LANES), lambda b: (l, 0, 0))],
        out_specs=pl.BlockSpec((SEQ, SWA_HEADS * SWA_HD), lambda b: (b, 0)),
        out_shape=jax.ShapeDtypeStruct((N_TOK, SWA_HEADS * SWA_HD), BF),
        scratch_shapes=_swa_scratch(SEQ, SWA_GROUP * SEQ),
        compiler_params=_params("parallel"),
        name="swa_ctx",
    )(sq, sk, sv, sink_rows)


def _swa_lat_kernel(q_ref, kp_ref, kc_ref, kn_ref, vp_ref, vc_ref, vn_ref, kx_ref, vx_ref, sink_ref, prev_ref,
                    o_ref, *scratch):
    w = SWA_WINDOW
    n = pl.program_id(1)
    nb = pl.num_programs(1)
    k_band = jnp.concatenate([kp_ref[...], kc_ref[...], kn_ref[...]], axis=0)
    v_band = jnp.concatenate([vp_ref[...], vc_ref[...], vn_ref[...]], axis=0)
    rows = SWA_GROUP * w
    r = lax.broadcasted_iota(jnp.int32, (3 * w, rows), 1) & (w - 1)
    c = lax.broadcasted_iota(jnp.int32, (3 * w, rows), 0)
    valid = (c >= r) & (c <= r + 2 * w) & ((c >= w) | (n > 0)) & ((c < 2 * w) | (n < nb - 1))
    _gqa_heads(q_ref, [k_band, kx_ref[0, 0]], [v_band, vx_ref[0, 0]], [valid, None], sink_ref, o_ref,
               scratch[:SWA_KV_HEADS], scratch[SWA_KV_HEADS:])


def _swa_lat(sq, sk, sv, k_cache, v_cache, sink_rows, l, prev):
    w = SWA_WINDOW
    nb = DEC_SEQ // w
    q0 = N_CTX // w

    def blk(d):
        return lambda b, n: (q0 + b * nb + jnp.clip(n + d, 0, nb - 1), 0)

    kv_specs = [pl.BlockSpec((w, LANES), blk(d)) for d in (-1, 0, 1)]
    cache_spec = pl.BlockSpec((1, 1, PAST_LEN, LANES), lambda b, n: (b, l, 0, 0))
    return pl.pallas_call(
        _swa_lat_kernel,
        grid=(DEC_BATCH, nb),
        in_specs=[pl.BlockSpec((w, SWA_HEADS * SWA_HD), blk(0))] + kv_specs + kv_specs
        + [cache_spec, cache_spec, pl.BlockSpec((1, SWA_HEADS, LANES), lambda b, n: (l, 0, 0)), _any_spec()],
        out_specs=pl.BlockSpec((w, SWA_HEADS * SWA_HD), blk(0)),
        out_shape=jax.ShapeDtypeStruct((N_TOK, SWA_HEADS * SWA_HD), BF),
        input_output_aliases={10: 0},
        scratch_shapes=_swa_scratch(3 * w + PAST_LEN, SWA_GROUP * w),
        compiler_params=_params("parallel", "parallel"),
        name="swa_lat",
    )(sq, sk, sk, sk, sv, sv, sv, k_cache, v_cache, sink_rows, prev)


def _gdn_conv_kernel(*refs):
    x_ref, w_ref, o_ref = refs[0], refs[1], refs[-1]
    x = x_ref[...]
    t = x.shape[0]
    w = w_ref[0]
    row = lax.broadcasted_iota(jnp.int32, x.shape, 0)
    half = GDN_CONV // 2
    acc = x * w[half:half + 1, :]
    for k in range(GDN_CONV):
        d = k - half
        if d == 0:
            continue
        xs = pltpu.roll(x, (-d) % t, 0)
        ok = (row + d >= 0) & (row + d < t)
        acc = acc + jnp.where(ok, xs, 0.0) * w[k:k + 1, :]
    y = _silu(acc)
    is_qk = pl.program_id(1) < 2
    cols = []
    for h in range(x.shape[1] // LANES):
        yh = y[:, h * LANES:(h + 1) * LANES]
        nrm = lax.rsqrt(jnp.sum(yh * yh, axis=-1, keepdims=True) + 1e-6)
        cols.append(yh * jnp.where(is_qk, nrm, 1.0))
    o_ref[...] = jnp.concatenate(cols, axis=1)


def _gdn_conv(g3, conv_w, l, row0, n_seq, t, prev):
    blk0 = row0 // t
    args = [g3, conv_w]
    in_specs = [pl.BlockSpec((t, GDN_QK_DIM), lambda s, j: (blk0 + s, j)),
                pl.BlockSpec((1, GDN_CONV, GDN_QK_DIM), lambda s, j: (l, 0, j))]
    aliases = {}
    if prev is not None:
        args.append(prev)
        in_specs.append(_any_spec())
        aliases = {2: 0}
    return pl.pallas_call(
        _gdn_conv_kernel,
        grid=(n_seq, 3),
        in_specs=in_specs,
        out_specs=pl.BlockSpec((t, GDN_QK_DIM), lambda s, j: (blk0 + s, j)),
        out_shape=jax.ShapeDtypeStruct((N_TOK, GDN_CONV_CH), F32),
        input_output_aliases=aliases,
        compiler_params=_params("parallel", "parallel"),
        name="gdn_conv",
    )(*args)


def _gate_rows(misc):
    shape = (2 * GDN_NH, LANES)
    sel = (lax.broadcasted_iota(jnp.int32, shape, 1)
           == lax.broadcasted_iota(jnp.int32, shape, 0) + _M_G).astype(BF)
    b1, b2, b3 = _split3(misc)
    return _nt(sel, b1) + (_nt(sel, b2) + _nt(sel, b3))


def _gdn_chunk_pairs(pairs):
    c = GDN_CHUNK
    shape = (c, 2 * c)
    ri = lax.broadcasted_iota(jnp.int32, shape, 0)
    lane = lax.broadcasted_iota(jnp.int32, shape, 1)
    cj = lane & (c - 1)
    fwd_half = lane < c
    lower, upper = ri >= cj, ri <= cj
    incl = (fwd_half & lower) | (~fwd_half & upper)
    incl_t = (fwd_half & upper) | (~fwd_half & lower)
    strict = incl & (ri != cj)
    incl_f, incl_b = incl & fwd_half, incl & ~fwd_half
    eye = (ri == cj).astype(F32)
    n = len(pairs)
    rng = range(n)
    fw = [p[0] for p in pairs]
    bw = [p[1] for p in pairs]

    def halves(a, b):
        return jnp.where(fwd_half, a, b)

    def split_rows(m):
        return jnp.concatenate([jnp.where(fwd_half, m, 0.0), jnp.where(fwd_half, 0.0, m)], axis=0)

    def pair_dot3(x, p):
        xh = x.astype(BF)
        xl = (x - xh.astype(F32)).astype(BF)
        phf = p.astype(BF).astype(F32)
        bd_hi = split_rows(phf).astype(BF)
        bd_lo = split_rows(p - phf).astype(BF)
        return jnp.dot(jnp.concatenate([xh, xl, xh], axis=1), jnp.concatenate([bd_hi, bd_hi, bd_lo], axis=0),
                       preferred_element_type=F32)

    g_col = [halves(fw[i]["g_col"], bw[i]["g_col"]) for i in rng]
    gc_row = [jnp.sum(jnp.where(incl_t, g_col[i], 0.0), axis=0, keepdims=True) for i in rng]
    gcf = [jnp.sum(jnp.where(incl_f, pairs[i][2], 0.0), axis=1, keepdims=True) for i in rng]
    gcb = [jnp.sum(jnp.where(incl_b, pairs[i][2], 0.0), axis=1, keepdims=True) for i in rng]
    decay = [jnp.where(incl, jnp.exp(halves(gcf[i], gcb[i]) - gc_row[i]), 0.0) for i in rng]
    qf = [p["q"] * (GDN_DK ** -0.5) for p in fw]
    qb = [p["q"] * (GDN_DK ** -0.5) for p in bw]
    kbf = [p["k"] * p["beta"] for p in fw]
    kbb = [p["k"] * p["beta"] for p in bw]
    z = jnp.zeros((c, GDN_DK), F32)
    kq = [_bdot_nt(jnp.concatenate([jnp.concatenate([kbf[i], kbb[i]], axis=1),
                                    jnp.concatenate([qf[i], qb[i]], axis=1)], axis=0),
                   jnp.concatenate([jnp.concatenate([fw[i]["k"], z], axis=1),
                                    jnp.concatenate([z, bw[i]["k"]], axis=1)], axis=0)) for i in rng]
    pw = [jnp.where(strict, -(kq[i][:c] * decay[i]), 0.0) for i in rng]
    inv = [eye + m for m in pw]
    for _ in range(int(np.log2(c)) - 1):
        pw = [pair_dot3(m, m) for m in pw]
        inv = [inv[i] + pair_dot3(inv[i], pw[i]) for i in rng]
    ef = [jnp.exp(g) for g in gcf]
    eb = [jnp.exp(g) for g in gcb]
    uw = [_bdot(split_rows(inv[i]),
                jnp.concatenate([jnp.concatenate([fw[i]["v"] * fw[i]["beta"], kbf[i] * ef[i]], axis=1),
                                 jnp.concatenate([bw[i]["v"] * bw[i]["beta"], kbb[i] * eb[i]], axis=1)], axis=0))
          for i in rng]
    a = [jnp.where(incl, kq[i][c:] * decay[i], 0.0) for i in rng]
    glf = [g[c - 1:c, :] for g in gcf]
    glb = [g[0:1, :] for g in gcb]
    wqf = [_bdot(jnp.concatenate([uw[i][:c, GDN_DV:], qf[i] * ef[i]], axis=0), fw[i]["state"]) for i in rng]
    wqb = [_bdot(jnp.concatenate([uw[i][c:, GDN_DV:], qb[i] * eb[i]], axis=0), bw[i]["state"]) for i in rng]
    vnf = [uw[i][:c, :GDN_DV] - wqf[i][:c] for i in rng]
    vnb = [uw[i][c:, :GDN_DV] - wqb[i][:c] for i in rng]
    av = [_bdot(split_rows(a[i]), jnp.concatenate([vnf[i], vnb[i]], axis=0)) for i in rng]
    sf = [fw[i]["state"] * jnp.exp(glf[i]) + _bdot_tn(fw[i]["k"] * jnp.exp(glf[i] - gcf[i]), vnf[i]) for i in rng]
    sb = [bw[i]["state"] * jnp.exp(glb[i]) + _bdot_tn(bw[i]["k"] * jnp.exp(glb[i] - gcb[i]), vnb[i]) for i in rng]
    return [(wqf[i][c:] + av[i][:c], sf[i], wqb[i][c:] + av[i][c:], sb[i]) for i in rng]


def _gdn_kernel(*refs, context, n_par):
    qf_ref, qb_ref, mf_ref, mb_ref = refs[:4]
    st_ref = refs[-1]
    if context:
        of_ref, ob_ref, sfin_ref = refs[-4:-1]
    else:
        s0_ref = refs[4]
        of_ref, ob_ref = refs[-3:-1]
    n = pl.program_id(1)

    @pl.when(n == 0)
    def _():
        if context:
            st_ref[...] = jnp.zeros_like(st_ref)
        else:
            st_ref[...] = s0_ref[:, 0]

    def problem(qkv, misc, s, h, d):
        i = d * GDN_HEADS + h
        return dict(q=qkv[:, h * GDN_DK:(h + 1) * GDN_DK],
                    k=qkv[:, GDN_QK_DIM + h * GDN_DK:GDN_QK_DIM + (h + 1) * GDN_DK],
                    v=qkv[:, 2 * GDN_QK_DIM + h * GDN_DV:2 * GDN_QK_DIM + (h + 1) * GDN_DV],
                    g_col=misc[:, _M_G + i:_M_G + i + 1], beta=misc[:, _M_B + i:_M_B + i + 1],
                    state=st_ref[s, i])

    lane = lax.broadcasted_iota(jnp.int32, (1, 2 * GDN_CHUNK), 1)
    pairs = []
    for s in range(n_par):
        qkv_f, qkv_b, misc_f, misc_b = qf_ref[s], qb_ref[s], mf_ref[s], mb_ref[s]
        rows = _gate_rows(jnp.concatenate([misc_f, misc_b], axis=0))
        for h in range(GDN_HEADS):
            g_row = jnp.where(lane < GDN_CHUNK, rows[h:h + 1, :], rows[GDN_HEADS + h:GDN_HEADS + h + 1, :])
            pairs.append((problem(qkv_f, misc_f, s, h, 0), problem(qkv_b, misc_b, s, h, 1), g_row))
    results = _gdn_chunk_pairs(pairs)
    for s in range(n_par):
        for h in range(GDN_HEADS):
            o_f, s_f, o_b, s_b = results[s * GDN_HEADS + h]
            of_ref[s, :, h * GDN_DV:(h + 1) * GDN_DV] = o_f
            ob_ref[s, :, h * GDN_DV:(h + 1) * GDN_DV] = o_b
            st_ref[s, h] = s_f
            st_ref[s, GDN_HEADS + h] = s_b

    if context:
        @pl.when(n == pl.num_programs(1) - 1)
        def _():
            sfin_ref[:, 0] = st_ref[...]


def _gdn(qkv, misc, l, t, n_seq, seq0, s0, prevs, n_par=GDN_PAR):
    c = GDN_CHUNK
    nc = t // c
    context = s0 is None
    sb0 = seq0 // n_par
    n_all = N_TOK // t
    fwd = lambda s, n: (sb0 + s, n, 0)
    bwd = lambda s, n: (sb0 + s, nc - 1 - n, 0)
    st_spec = pl.BlockSpec((n_par, 1, GDN_NH, GDN_DK, GDN_DV), lambda s, n: (s, l, 0, 0, 0))
    qkv3 = qkv.reshape(n_all, t, GDN_CONV_CH)
    misc3 = misc.reshape(n_all, t, LANES)
    in_specs = [pl.BlockSpec((n_par, c, GDN_CONV_CH), fwd), pl.BlockSpec((n_par, c, GDN_CONV_CH), bwd),
                pl.BlockSpec((n_par, c, LANES), fwd), pl.BlockSpec((n_par, c, LANES), bwd)]
    args = [qkv3, qkv3, misc3, misc3]
    if not context:
        in_specs.append(st_spec)
        args.append(s0)
    out_specs = [pl.BlockSpec((n_par, c, GDN_V_DIM), fwd), pl.BlockSpec((n_par, c, GDN_V_DIM), bwd)]
    out_shape = [jax.ShapeDtypeStruct((n_all, t, GDN_V_DIM), F32)] * 2
    if context:
        out_specs.append(st_spec)
        out_shape.append(jax.ShapeDtypeStruct((n_seq, DEPTH, GDN_NH, GDN_DK, GDN_DV), F32))
    aliases = {}
    for j, p in enumerate(prevs):
        if p is not None:
            aliases[len(args)] = j
            args.append(p.reshape(out_shape[j].shape))
            in_specs.append(_any_spec())
    return pl.pallas_call(
        functools.partial(_gdn_kernel, context=context, n_par=n_par),
        grid=(n_seq // n_par, nc),
        in_specs=in_specs, out_specs=out_specs, out_shape=out_shape,
        input_output_aliases=aliases,
        scratch_shapes=[pltpu.VMEM((n_par, GDN_NH, GDN_DK, GDN_DV), F32)],
        compiler_params=_params("parallel", "arbitrary"),
        name="gdn",
    )(*args)


def _out_kernel(x_ref, mod_ref, om_ref, os_ref, gf_ref, gb_ref, gz_ref, gn_ref, w_ref, o_ref):
    gate = mod_ref[0, 0][:, 2 * D_MODEL:]
    s = gf_ref[...] + gb_ref[...]
    gz = gz_ref[...]
    cols = []
    for h in range(GDN_HEADS):
        sh = s[:, h * GDN_DV:(h + 1) * GDN_DV]
        cols.append(_rms(sh, gn_ref[0]) * _silu(gz[:, h * GDN_DV:(h + 1) * GDN_DV]))
    og = jnp.concatenate(cols, axis=1)
    n_m = MLA_HEADS * MLA_V
    n_s = SWA_HEADS * SWA_HD
    y = (jnp.dot(om_ref[...], w_ref[0, :n_m, :], preferred_element_type=F32)
         + jnp.dot(os_ref[...], w_ref[0, n_m:n_m + n_s, :], preferred_element_type=F32)
         + jnp.dot(og.astype(BF), w_ref[0, n_m + n_s:, :], preferred_element_type=F32))
    o_ref[...] = x_ref[...] + gate * y


def _out_proj(x, mods, l, o_mla, o_swa, o_gf, o_gb, gz, gdn_norm, w_out, tm=TM_OUT):
    row = lambda i: (i, 0)
    lay3 = lambda i: (l, 0, 0)
    return pl.pallas_call(
        _out_kernel,
        grid=(N_TOK // tm,),
        in_specs=[pl.BlockSpec((tm, D_MODEL), row),
                  _mod_spec(l, 1, tm),
                  pl.BlockSpec((tm, MLA_HEADS * MLA_V), row),
                  pl.BlockSpec((tm, SWA_HEADS * SWA_HD), row),
                  pl.BlockSpec((tm, GDN_V_DIM), row),
                  pl.BlockSpec((tm, GDN_V_DIM), row),
                  pl.BlockSpec((tm, GDN_V_DIM), row),
                  pl.BlockSpec((1, 1, GDN_DV), lay3),
                  pl.BlockSpec((1,) + w_out.shape[1:], lay3)],
        out_specs=pl.BlockSpec((tm, D_MODEL), row),
        out_shape=jax.ShapeDtypeStruct((N_TOK, D_MODEL), F32),
        compiler_params=_params("parallel"),
        name="out_proj",
    )(x, mods, o_mla, o_swa, o_gf, o_gb, gz, gdn_norm.reshape(DEPTH, 1, GDN_DV), w_out)


def _rot_columns(w, dim):
    shp = w.shape
    w6 = w.reshape(shp[:-1] + (shp[-1] // dim, 2, 2, dim // 4))
    sign = jnp.asarray([-1.0, 1.0], F32).reshape(2, 1)
    return (jnp.flip(w6, axis=-2) * sign).reshape(shp)


def _axial_rope(n_tokens, dim):
    f32 = np.float32
    rows = n_tokens // GRID_W
    row = np.repeat(np.arange(rows, dtype=f32), GRID_W)
    col = np.tile(np.arange(GRID_W, dtype=f32), rows)
    axis_dim = dim // 2
    inv_freq = (f32(1.0) / (f32(ROPE_BASE) ** (np.arange(0, axis_dim, 2, dtype=f32) / f32(axis_dim)))).astype(f32)
    ang_r = row[:, None] * inv_freq[None, :]
    ang_c = col[:, None] * inv_freq[None, :]
    ang = np.concatenate([ang_r, ang_r, ang_c, ang_c], axis=-1).astype(f32)
    return np.cos(ang).astype(f32), np.sin(ang).astype(f32)


def _rope_table(tm):
    cos_m, sin_m = _axial_rope(DEC_SEQ, MLA_ROPE)
    cos_s, sin_s = _axial_rope(DEC_SEQ, SWA_HD)
    t = DEC_SEQ
    one = lambda w: np.ones((t, w), np.float32)
    zero = lambda w: np.zeros((t, w), np.float32)
    lat = np.concatenate([
        cos_s, cos_s, sin_s, sin_s,
        cos_m, one(LANES - MLA_ROPE), sin_m, zero(LANES - MLA_ROPE),
        one(MLA_NOPE), cos_m, one(LANES - MLA_NOPE - MLA_ROPE),
        zero(MLA_NOPE), sin_m, zero(LANES - MLA_NOPE - MLA_ROPE)], axis=1)
    ident_row = np.concatenate([np.ones(LANES), np.zeros(LANES)] * 3).astype(np.float32)
    ident = np.broadcast_to(ident_row[None, :], (tm, _T_END))
    return jnp.asarray(np.concatenate([ident, lat], axis=0))


def _mixer_weights(w_in, mla_w_qb, mla_w_kvb):
    nl = DEPTH
    offs = np.cumsum([0, MLA_Q_LORA, MLA_KV_LORA, MLA_ROPE, SWA_HEADS * SWA_HD, SWA_KV_HEADS * SWA_HD,
                      SWA_KV_HEADS * SWA_HD, GDN_CONV_CH, GDN_V_DIM, 2 * GDN_NH])
    w_t = jnp.swapaxes(w_in, 1, 2)
    cq, ckv, krope, sq, sk, sv, g3, gz, gates = [w_t[:, offs[i]:offs[i + 1], :] for i in range(9)]

    def rot(w, dim):
        w6 = w.reshape(nl, w.shape[1] // dim, 2, 2, dim // 4, D_MODEL)
        return (jnp.flip(w6, axis=3) * jnp.asarray([-1.0, 1.0], F32).reshape(2, 1, 1)).reshape(w.shape)

    zeros = lambda n: jnp.zeros((nl, n, D_MODEL), F32)
    misc_a = jnp.concatenate([krope, gates, zeros(LANES - _M_END)], axis=1)
    misc_b = jnp.concatenate([rot(krope, MLA_ROPE), zeros(LANES - MLA_ROPE)], axis=1)
    win = jnp.concatenate([cq, ckv, sq, rot(sq, SWA_HD), sk, rot(sk, SWA_HD), sv, g3, gz, misc_a, misc_b],
                          axis=1).astype(BF)

    r = MLA_Q_LORA
    wq = mla_w_qb.reshape(nl, r, MLA_HEADS, MLA_NOPE + MLA_ROPE)
    nope, rope = wq[..., :MLA_NOPE], wq[..., MLA_NOPE:]
    pad = LANES - MLA_NOPE - MLA_ROPE
    z = lambda n: jnp.zeros((nl, r, MLA_HEADS, n), F32)
    qa = jnp.concatenate([nope, rope, z(pad)], axis=-1).reshape(nl, r, MLA_HEADS * LANES)
    qb = jnp.concatenate([z(MLA_NOPE), _rot_columns(rope, MLA_ROPE), z(pad)],
                         axis=-1).reshape(nl, r, MLA_HEADS * LANES)
    wqb = jnp.concatenate([qa, qb], axis=2).astype(BF)

    kvb = mla_w_kvb.reshape(nl, MLA_KV_LORA, MLA_HEADS, MLA_NOPE + MLA_V)
    k_nope = jnp.concatenate([kvb[..., :MLA_NOPE],
                              jnp.zeros((nl, MLA_KV_LORA, MLA_HEADS, LANES - MLA_NOPE), F32)],
                             axis=-1).reshape(nl, MLA_KV_LORA, MLA_HEADS * LANES)
    place = np.zeros((LANES, MLA_HEADS, LANES), np.float32)
    for i in range(MLA_ROPE):
        place[i, :, MLA_NOPE + i] = 1.0
    place = jnp.broadcast_to(jnp.asarray(place.reshape(1, LANES, MLA_HEADS * LANES)),
                             (nl, LANES, MLA_HEADS * LANES))
    wk = jnp.concatenate([k_nope, place], axis=1).astype(BF)
    wv = kvb[..., MLA_NOPE:].reshape(nl, MLA_KV_LORA, MLA_HEADS * MLA_V).astype(BF)
    return win, wqb, wk, wv


def _misc_rows(vals):
    rows = jnp.zeros((DEPTH, 1, LANES), F32)
    return rows.at[:, 0, _M_G:_M_B].set(vals.reshape(DEPTH, GDN_NH).astype(F32))


def kernel(x_prompt, x_sample, cache_mla_ckv, cache_mla_krope, cache_swa_k, cache_swa_v, state_gdn, c, c_ctx,
           w_ada, b_ada, norm_ffn1, ffn1_w1, ffn1_w2, norm_mix, w_in, mla_q_norm, mla_w_qb, mla_kv_norm,
           mla_w_kvb, swa_sink, gdn_conv_w, gdn_a_log, gdn_dt_bias, gdn_norm, w_out, norm_ffn2, ffn2_w1,
           ffn2_w2, final_norm):
    cond = jnp.concatenate([c_ctx[None, :], c, jnp.zeros((COND_ROWS - N_GROUPS, D_MODEL), F32)], axis=0)
    mods = _adaln(cond, w_ada, b_ada)[:, :N_GROUPS].reshape(DEPTH, N_GROUPS, 1, N_MOD * D_MODEL)
    tab = _rope_table(TM_PROJ)
    win, wqb, wk, wv = _mixer_weights(w_in, mla_w_qb, mla_w_kvb)
    w11, w12, w21, w22, wo = (w.astype(BF) for w in (ffn1_w1, ffn1_w2, ffn2_w1, ffn2_w2, w_out))
    alog, dtb = _misc_rows(gdn_a_log), _misc_rows(gdn_dt_bias)
    sink_rows = jnp.broadcast_to(swa_sink[:, :, None], (DEPTH, SWA_HEADS, LANES))
    cache_misc = jnp.pad(cache_mla_krope, ((0, 0), (0, 0), (0, 0), (0, LANES - MLA_ROPE)))
    cache_k = cache_swa_k.reshape(DEC_BATCH, DEPTH, PAST_LEN, LANES)
    cache_v = cache_swa_v.reshape(DEC_BATCH, DEPTH, PAST_LEN, LANES)
    s0 = state_gdn.reshape(DEC_BATCH, DEPTH, GDN_NH, GDN_DK, GDN_DV)

    xs = [x_prompt.reshape(N_CTX, D_MODEL), x_sample.reshape(N_LAT, D_MODEL)]
    new_ckv = new_st = None
    new_krope, new_sk, new_sv = [], [], []
    for l in range(DEPTH):
        x = _ffn(xs, mods, l, 0, norm_ffn1, w11, w12)
        (q_mla, new_ckv, misc, k_mla, v_mla, sq, sk, sv, g3, gz) = _proj(
            x, mods, l, norm_mix, win, mla_q_norm, wqb, mla_kv_norm, wk, wv, tab, alog, dtb, new_ckv)

        k_c, v_c = _kv_cache(cache_mla_ckv, cache_misc, l, wk, wv)
        o_mla = _mla_attend(q_mla, [k_mla], [v_mla], BATCH, SEQ, 0, [(0, SEQ)], None)
        o_mla = _mla_attend(q_mla, [k_mla, k_c], [v_mla, v_c], DEC_BATCH, DEC_SEQ, N_CTX,
                            [(N_CTX // DEC_SEQ, DEC_SEQ), (0, PAST_LEN)], o_mla)

        o_swa = _swa_ctx(sq, sk, sv, sink_rows, l)
        o_swa = _swa_lat(sq, sk, sv, cache_k, cache_v, sink_rows, l, o_swa)

        qkv = _gdn_conv(g3, gdn_conv_w, l, 0, BATCH, SEQ, None)
        qkv = _gdn_conv(g3, gdn_conv_w, l, N_CTX, DEC_BATCH, DEC_SEQ, qkv)
        o_gf, o_gb, new_st = _gdn(qkv, misc, l, SEQ, BATCH, 0, None, [None, None, new_st])
        o_gf, o_gb = _gdn(qkv, misc, l, DEC_SEQ, DEC_BATCH, N_CTX // DEC_SEQ, s0, [o_gf, o_gb])

        x = _out_proj(x, mods, l, o_mla, o_swa, o_gf.reshape(N_TOK, GDN_V_DIM), o_gb.reshape(N_TOK, GDN_V_DIM),
                      gz, gdn_norm, wo)
        if l + 1 < DEPTH:
            xs = [_ffn([x], mods, l, 2, norm_ffn2, w21, w22)]
        else:
            y_prompt, y_sample = _ffn([x], mods, l, 2, norm_ffn2, w21, w22, final_gain=final_norm)

        new_krope.append(misc[:N_CTX, :MLA_ROPE].reshape(BATCH, SEQ, MLA_ROPE))
        new_sk.append(sk[:N_CTX].reshape(BATCH, SEQ, SWA_KV_HEADS, SWA_HD))
        new_sv.append(sv[:N_CTX].reshape(BATCH, SEQ, SWA_KV_HEADS, SWA_HD))

    return (y_prompt.reshape(BATCH, SEQ, D_MODEL), y_sample.reshape(DEC_BATCH, DEC_SEQ, D_MODEL), new_ckv,
            jnp.stack(new_krope, axis=1), jnp.stack(new_sk, axis=1), jnp.stack(new_sv, axis=1),
            new_st.reshape(BATCH, DEPTH, 2, GDN_HEADS, GDN_DK, GDN_DV))
```

```python
import functools

import numpy as np
import jax
import jax.numpy as jnp
from jax import lax
from jax.experimental import pallas as pl
from jax.experimental.pallas import tpu as pltpu

D_MODEL = 1024
BATCH = 16
SEQ = 256
DEPTH = 2
DEC_BATCH = 2
DEC_SEQ = 2048
PAST_LEN = 512
GRID_W = 64
ROPE_BASE = 10000.0
NORM_EPS = 1e-6
N_MOD = 9
D_FF = 2816
MLA_HEADS = 8
MLA_Q_LORA = 384
MLA_KV_LORA = 256
MLA_NOPE = 64
MLA_ROPE = 32
MLA_V = 64
SWA_HEADS = 8
SWA_KV_HEADS = 2
SWA_GROUP = SWA_HEADS // SWA_KV_HEADS
SWA_HD = 64
SWA_WINDOW = 128
GDN_HEADS = 4
GDN_DK = 128
GDN_DV = 128
GDN_CONV = 5
GDN_CHUNK = 64
GDN_QK_DIM = GDN_HEADS * GDN_DK
GDN_V_DIM = GDN_HEADS * GDN_DV
GDN_CONV_CH = 2 * GDN_QK_DIM + GDN_V_DIM
GDN_NH = 2 * GDN_HEADS

N_CTX = BATCH * SEQ
N_LAT = DEC_BATCH * DEC_SEQ
N_TOK = N_CTX + N_LAT
N_GROUPS = 1 + DEC_BATCH
COND_ROWS = 8

LANES = 128
SUBLANES = 8
VMEM_LIMIT_BYTES = 56 * 1024 * 1024

TM_FFN = 512
TM_PROJ = 2 * SEQ
TM_OUT = 512
TQ_MLA = 512
GDN_PAR = 2

BF = jnp.bfloat16
F32 = jnp.float32
LOG2_E = 1.4426950408889634
MLA_Q_SCALE = (MLA_NOPE + MLA_ROPE) ** -0.5 * LOG2_E

_C_CQ = 0
_C_CKV = _C_CQ + MLA_Q_LORA
_C_SQ = _C_CKV + MLA_KV_LORA
_C_SQR = _C_SQ + SWA_HEADS * SWA_HD
_C_SK = _C_SQR + SWA_HEADS * SWA_HD
_C_SKR = _C_SK + SWA_KV_HEADS * SWA_HD
_C_SV = _C_SKR + SWA_KV_HEADS * SWA_HD
_C_G3 = _C_SV + SWA_KV_HEADS * SWA_HD
_C_GZ = _C_G3 + GDN_CONV_CH
_C_MA = _C_GZ + GDN_V_DIM
_C_MB = _C_MA + LANES
_C_END = _C_MB + LANES
_M_G = MLA_ROPE
_M_B = MLA_ROPE + GDN_NH
_M_END = MLA_ROPE + 2 * GDN_NH
_CACHE_WIDTHS = (MLA_KV_LORA, MLA_ROPE, SWA_KV_HEADS * SWA_HD, SWA_KV_HEADS * SWA_HD)
_T_CS, _T_SS, _T_CA, _T_SA, _T_CQ, _T_SQ = (i * LANES for i in range(6))
_T_END = 6 * LANES


def _params(*sem):
    return pltpu.CompilerParams(dimension_semantics=sem, vmem_limit_bytes=VMEM_LIMIT_BYTES)


def _bdot(a, b):
    return jnp.dot(a.astype(BF), b.astype(BF), preferred_element_type=F32)


def _nt(a, b):
    return lax.dot_general(a, b, (((1,), (1,)), ((), ())), preferred_element_type=F32)


def _bdot_nt(a, b):
    return _nt(a.astype(BF), b.astype(BF))


def _bdot_tn(a, b):
    return lax.dot_general(a.astype(BF), b.astype(BF), (((0,), (0,)), ((), ())),
                           preferred_element_type=F32)


def _split3(a):
    b1 = a.astype(BF)
    r = a - b1.astype(F32)
    b2 = r.astype(BF)
    b3 = (r - b2.astype(F32)).astype(BF)
    return b1, b2, b3


def _silu(x):
    return x / (1.0 + jnp.exp(-x))


def _rms(x, gain, eps=NORM_EPS):
    return x * lax.rsqrt(jnp.mean(x * x, axis=-1, keepdims=True) + eps) * gain


def _group_of_row(r):
    return jnp.where(r < N_CTX, 0, 1 + (r - N_CTX) // DEC_SEQ)


def _any_spec():
    return pl.BlockSpec(memory_space=pl.ANY)


def _adaln_kernel(c_ref, w_ref, b_ref, o_ref):
    o_ref[0] = _bdot(_silu(c_ref[...]), w_ref[0]) + b_ref[0]


def _adaln(cond, w_ada, b_ada, tn=1536):
    n = N_MOD * D_MODEL
    return pl.pallas_call(
        _adaln_kernel,
        grid=(DEPTH, n // tn),
        in_specs=[pl.BlockSpec((COND_ROWS, D_MODEL), lambda l, j: (0, 0)),
                  pl.BlockSpec((1, D_MODEL, tn), lambda l, j: (l, 0, j)),
                  pl.BlockSpec((1, 1, tn), lambda l, j: (l, 0, j))],
        out_specs=pl.BlockSpec((1, COND_ROWS, tn), lambda l, j: (l, 0, j)),
        out_shape=jax.ShapeDtypeStruct((DEPTH, COND_ROWS, n), F32),
        compiler_params=_params("parallel", "parallel"),
        name="adaln",
    )(cond, w_ada, b_ada.reshape(DEPTH, 1, n))


def _mod_spec(l, which, tm):
    return pl.BlockSpec((1, 1, 1, 3 * D_MODEL), lambda i: (l, _group_of_row(i * tm), 0, which))


def _ffn_kernel(*refs, n_in, n_a_tiles, final):
    x_refs = refs[:n_in]
    mod_ref, gain_ref, w1_ref, w2_ref = refs[n_in:n_in + 4]
    rest = refs[n_in + 4:]
    i = pl.program_id(0)
    if n_in == 2:
        x = jnp.where(i < n_a_tiles, x_refs[0][...], x_refs[1][...])
    else:
        x = x_refs[0][...]
    mod = mod_ref[0, 0]
    shift, scale, gate = mod[:, :D_MODEL], mod[:, D_MODEL:2 * D_MODEL], mod[:, 2 * D_MODEL:]
    h = _rms(x, gain_ref[0]) * (1.0 + scale) + shift
    gu = jnp.dot(h.astype(BF), w1_ref[0], preferred_element_type=F32)
    a = _silu(gu[:, :D_FF]) * gu[:, D_FF:]
    y = x + gate * (0.5 * jnp.dot(a.astype(BF), w2_ref[0], preferred_element_type=F32))
    if final:
        fg_ref, oa_ref, ob_ref = rest
        yn = _rms(y, fg_ref[...])

        @pl.when(i < n_a_tiles)
        def _():
            oa_ref[...] = yn

        @pl.when(i >= n_a_tiles)
        def _():
            ob_ref[...] = yn
    else:
        rest[0][...] = y


def _ffn(xs, mods, l, which, gain, w1, w2, final_gain=None, tm=TM_FFN):
    n_a = N_CTX // tm
    row = lambda i: (i, 0)
    first = lambda i: (jnp.minimum(i, n_a - 1), 0)
    second = lambda i: (jnp.maximum(i - n_a, 0), 0)
    lay3 = lambda i: (l, 0, 0)
    x_specs = ([pl.BlockSpec((tm, D_MODEL), row)] if len(xs) == 1
               else [pl.BlockSpec((tm, D_MODEL), first), pl.BlockSpec((tm, D_MODEL), second)])
    in_specs = x_specs + [_mod_spec(l, which, tm),
                          pl.BlockSpec((1, 1, D_MODEL), lay3),
                          pl.BlockSpec((1, D_MODEL, 2 * D_FF), lay3, pipeline_mode=pl.Buffered(1)),
                          pl.BlockSpec((1, D_FF, D_MODEL), lay3, pipeline_mode=pl.Buffered(1))]
    args = list(xs) + [mods, gain.reshape(DEPTH, 1, D_MODEL), w1, w2]
    if final_gain is None:
        out_specs = pl.BlockSpec((tm, D_MODEL), row)
        out_shape = jax.ShapeDtypeStruct((N_TOK, D_MODEL), F32)
    else:
        in_specs.append(pl.BlockSpec((1, D_MODEL), lambda i: (0, 0)))
        args.append(final_gain.reshape(1, D_MODEL))
        out_specs = [pl.BlockSpec((tm, D_MODEL), first), pl.BlockSpec((tm, D_MODEL), second)]
        out_shape = [jax.ShapeDtypeStruct((N_CTX, D_MODEL), F32), jax.ShapeDtypeStruct((N_LAT, D_MODEL), F32)]
    return pl.pallas_call(
        functools.partial(_ffn_kernel, n_in=len(xs), n_a_tiles=n_a, final=final_gain is not None),
        grid=(N_TOK // tm,),
        in_specs=in_specs, out_specs=out_specs, out_shape=out_shape,
        compiler_params=_params("arbitrary" if final_gain is not None else "parallel"),
        name="ffn",
    )(*args)


def _kv_expand(ckv_n, misc, wk, wv):
    kin = jnp.concatenate([ckv_n, misc], axis=1).astype(BF)
    kk = jnp.dot(kin, wk, preferred_element_type=F32)
    vv = jnp.dot(ckv_n.astype(BF), wv, preferred_element_type=F32)
    return kk.astype(BF), vv.astype(BF)


def _proj_kernel(*refs, n_ctx_tiles):
    (x_ref, mod_ref, gain_ref, win_ref, qg_ref, wqb_ref, kvg_ref, wk_ref, wv_ref, tab_ref,
     alog_ref, dtb_ref) = refs[:12]
    (q_ref, misc_ref, kmla_ref, vmla_ref, sq_ref, sk_ref, sv_ref, g3_ref, gz_ref,
     c_ckv_ref, c_krope_ref, c_sk_ref, c_sv_ref) = refs[-13:]
    i = pl.program_id(0)
    mod = mod_ref[0, 0]
    shift, scale = mod[:, :D_MODEL], mod[:, D_MODEL:2 * D_MODEL]
    hb = (_rms(x_ref[...], gain_ref[0]) * (1.0 + scale) + shift).astype(BF)
    tab = tab_ref[...]

    def project(c0, c1):
        return _nt(hb, win_ref[0, c0:c1, :])

    u_lora = project(_C_CQ, _C_SQ)
    u_misc = project(_C_MA, _C_END)
    u_sq = project(_C_SQ, _C_SK)
    u_skv = project(_C_SK, _C_G3)

    qn = _rms(u_lora[:, :MLA_Q_LORA], qg_ref[0])
    ckv_n = _rms(u_lora[:, MLA_Q_LORA:], kvg_ref[0])

    m = u_misc[:, :LANES] * tab[:, _T_CA:_T_CA + LANES] + u_misc[:, LANES:] * tab[:, _T_SA:_T_SA + LANES]
    lane = lax.broadcasted_iota(jnp.int32, m.shape, 1)
    z = m + dtb_ref[0]
    softplus = jnp.maximum(z, 0.0) + jnp.log(1.0 + jnp.exp(-jnp.abs(z)))
    decay = -jnp.exp(alog_ref[0]) * softplus
    strength = 1.0 / (1.0 + jnp.exp(-m))
    misc = jnp.where((lane >= _M_G) & (lane < _M_B), decay,
                     jnp.where((lane >= _M_B) & (lane < _M_END), strength, m))
    misc_ref[...] = misc

    n_sq = SWA_HEADS * SWA_HD
    cos_s = tab[:, _T_CS:_T_CS + LANES]
    sin_s = tab[:, _T_SS:_T_SS + LANES]
    cos4 = jnp.concatenate([cos_s] * (n_sq // LANES), axis=1)
    sin4 = jnp.concatenate([sin_s] * (n_sq // LANES), axis=1)
    sq_ref[...] = ((u_sq[:, :n_sq] * cos4 + u_sq[:, n_sq:] * sin4) * (SWA_HD ** -0.5 * LOG2_E)).astype(BF)
    sk = u_skv[:, :LANES] * cos_s + u_skv[:, LANES:2 * LANES] * sin_s
    sv = u_skv[:, 2 * LANES:]
    sk_ref[...] = sk
    sv_ref[...] = sv

    @pl.when(i < n_ctx_tiles)
    def _():
        for j in range(c_ckv_ref.shape[0]):
            rows = slice(j * SEQ, (j + 1) * SEQ)
            c_ckv_ref[j, 0] = ckv_n[rows]
            c_krope_ref[j, 0] = misc[rows, :MLA_ROPE]
            c_sk_ref[j, 0] = sk[rows]
            c_sv_ref[j, 0] = sv[rows]

    g3_ref[...] = project(_C_G3, _C_GZ)
    gz_ref[...] = project(_C_GZ, _C_MA)

    q2 = jnp.dot(qn.astype(BF), wqb_ref[0], preferred_element_type=F32)
    nq = MLA_HEADS * LANES
    cosq = jnp.concatenate([tab[:, _T_CQ:_T_CQ + LANES]] * MLA_HEADS, axis=1)
    sinq = jnp.concatenate([tab[:, _T_SQ:_T_SQ + LANES]] * MLA_HEADS, axis=1)
    q_ref[...] = ((q2[:, :nq] * cosq + q2[:, nq:] * sinq) * MLA_Q_SCALE).astype(BF)
    kk, vv = _kv_expand(ckv_n, misc, wk_ref[0], wv_ref[0])
    kmla_ref[...] = kk
    vmla_ref[...] = vv


def _proj(x, mods, l, gain, win, qg, wqb, kvg, wk, wv, tab, alog, dtb, cache_prev, tm=TM_PROJ):
    assert tm % SEQ == 0
    n_ctx_tiles = N_CTX // tm
    lat_tiles = DEC_SEQ // tm
    lay3 = lambda i: (l, 0, 0)
    row = lambda i: (i, 0)

    def tab_map(i):
        return (jnp.where(i < n_ctx_tiles, 0, 1 + (i - n_ctx_tiles) % lat_tiles), 0)

    def lay_spec(a):
        return pl.BlockSpec((1,) + a.shape[1:], lay3)

    widths = [(MLA_HEADS * LANES, BF), (LANES, F32), (MLA_HEADS * LANES, BF),
              (MLA_HEADS * MLA_V, BF), (SWA_HEADS * SWA_HD, BF), (SWA_KV_HEADS * SWA_HD, F32),
              (SWA_KV_HEADS * SWA_HD, F32), (GDN_CONV_CH, F32), (GDN_V_DIM, F32)]
    out_specs = [pl.BlockSpec((tm, w), row) for w, _ in widths]
    out_shape = [jax.ShapeDtypeStruct((N_TOK, w), dt) for w, dt in widths]
    for w in _CACHE_WIDTHS:
        out_specs.append(pl.BlockSpec((tm // SEQ, 1, SEQ, w),
                                      lambda i: (jnp.minimum(i, n_ctx_tiles - 1), l, 0, 0)))
        out_shape.append(jax.ShapeDtypeStruct((BATCH, DEPTH, SEQ, w), F32))
    qg, kvg = qg.reshape(DEPTH, 1, -1), kvg.reshape(DEPTH, 1, -1)
    args = [x, mods, gain.reshape(DEPTH, 1, D_MODEL), win, qg, wqb, kvg, wk, wv, tab, alog, dtb]
    in_specs = [pl.BlockSpec((tm, D_MODEL), row), _mod_spec(l, 1, tm),
                pl.BlockSpec((1, 1, D_MODEL), lay3), lay_spec(win), lay_spec(qg), lay_spec(wqb),
                lay_spec(kvg), lay_spec(wk), lay_spec(wv), pl.BlockSpec((tm, _T_END), tab_map),
                lay_spec(alog), lay_spec(dtb)]
    aliases = {}
    if cache_prev is not None:
        for j, prev in enumerate(cache_prev):
            aliases[len(args)] = len(widths) + j
            args.append(prev)
            in_specs.append(_any_spec())
    return pl.pallas_call(
        functools.partial(_proj_kernel, n_ctx_tiles=n_ctx_tiles),
        grid=(N_TOK // tm,),
        in_specs=in_specs, out_specs=out_specs, out_shape=out_shape,
        input_output_aliases=aliases,
        compiler_params=_params("arbitrary"),
        name="proj",
    )(*args)


def _kv_cache_kernel(ckv_ref, misc_ref, wk_ref, wv_ref, k_ref, v_ref):
    kk, vv = _kv_expand(ckv_ref[0, 0], misc_ref[0, 0], wk_ref[0], wv_ref[0])
    k_ref[...] = kk
    v_ref[...] = vv


def _kv_cache(cache_ckv, cache_misc, l, wk, wv):
    lay3 = lambda b: (l, 0, 0)
    n = DEC_BATCH * PAST_LEN
    return pl.pallas_call(
        _kv_cache_kernel,
        grid=(DEC_BATCH,),
        in_specs=[pl.BlockSpec((1, 1, PAST_LEN, MLA_KV_LORA), lambda b: (b, l, 0, 0)),
                  pl.BlockSpec((1, 1, PAST_LEN, LANES), lambda b: (b, l, 0, 0)),
                  pl.BlockSpec((1,) + wk.shape[1:], lay3),
                  pl.BlockSpec((1,) + wv.shape[1:], lay3)],
        out_specs=[pl.BlockSpec((PAST_LEN, MLA_HEADS * LANES), lambda b: (b, 0)),
                   pl.BlockSpec((PAST_LEN, MLA_HEADS * MLA_V), lambda b: (b, 0))],
        out_shape=[jax.ShapeDtypeStruct((n, MLA_HEADS * LANES), BF),
                   jax.ShapeDtypeStruct((n, MLA_HEADS * MLA_V), BF)],
        compiler_params=_params("parallel"),
        name="kv_cache",
    )(cache_ckv, cache_misc, wk, wv)


def _softmax_numerator(s_ref, p_ref, sink=None, row_block=128):
    rows, cols = s_ref.shape
    mx = jnp.max(s_ref[...], axis=-1, keepdims=True)
    if sink is not None:
        mx = jnp.maximum(mx, sink)
    sums = []
    for r0 in range(0, rows, row_block):
        mb = jnp.broadcast_to(mx[r0:r0 + row_block], (row_block, LANES))
        part = jnp.zeros((row_block, LANES), F32)
        for c in range(0, cols, LANES):
            p = jnp.exp2(s_ref[r0:r0 + row_block, c:c + LANES] - mb)
            part = part + p
            p_ref[r0:r0 + row_block, c:c + LANES] = p.astype(BF)
        sums.append(jnp.sum(part, axis=-1, keepdims=True))
    den = jnp.concatenate(sums, axis=0)
    return den if sink is None else den + jnp.exp2(sink - mx)


def _mla_kernel(*refs, n_seg):
    q_ref = refs[0]
    k_refs = refs[1:1 + n_seg]
    v_refs = refs[1 + n_seg:1 + 2 * n_seg]
    o_ref = refs[-5]
    s_refs, p_refs = refs[-4:-2], refs[-2:]
    tq = q_ref.shape[0]
    lane = lax.broadcasted_iota(jnp.int32, (tq, LANES), 1)

    def scores(h):
        off = 0
        for k in k_refs:
            s_refs[h % 2][:, off:off + k.shape[0]] = _nt(q_ref[:, h * LANES:(h + 1) * LANES],
                                                         k[:, h * LANES:(h + 1) * LANES])
            off += k.shape[0]

    scores(0)
    outs = []
    for h in range(MLA_HEADS):
        if h + 1 < MLA_HEADS:
            scores(h + 1)
        den = _softmax_numerator(s_refs[h % 2], p_refs[h % 2])
        pair = h // 2
        acc, off = None, 0
        for v in v_refs:
            pv = jnp.dot(p_refs[h % 2][:, off:off + v.shape[0]], v[:, pair * LANES:(pair + 1) * LANES],
                         preferred_element_type=F32)
            acc = pv if acc is None else acc + pv
            off += v.shape[0]
        outs.append(acc / den)
        if h % 2 == 1:
            o_ref[:, pair * LANES:(pair + 1) * LANES] = jnp.where(lane < MLA_V, outs[-2], outs[-1]).astype(BF)


def _mla_attend(q, ks, vs, n_batch, t, row0, k_blocks, prev):
    n_seg = len(ks)
    tq = min(TQ_MLA, t)
    n_keys = sum(s for _, s in k_blocks)
    qb0 = row0 // tq
    tiles = t // tq
    q_map = lambda b, i: (qb0 + b * tiles + i, 0)
    in_specs = [pl.BlockSpec((tq, MLA_HEADS * LANES), q_map)]
    for (b0, s), width in ([(kb, MLA_HEADS * LANES) for kb in k_blocks]
                           + [(kb, MLA_HEADS * MLA_V) for kb in k_blocks]):
        in_specs.append(pl.BlockSpec((s, width), functools.partial(lambda b, i, b0: (b0 + b, 0), b0=b0)))
    args = [q, *ks, *vs]
    aliases = {}
    if prev is not None:
        args.append(prev)
        in_specs.append(_any_spec())
        aliases = {len(args) - 1: 0}
    return pl.pallas_call(
        functools.partial(_mla_kernel, n_seg=n_seg),
        grid=(n_batch, tiles),
        in_specs=in_specs,
        out_specs=pl.BlockSpec((tq, MLA_HEADS * MLA_V), q_map),
        out_shape=jax.ShapeDtypeStruct((N_TOK, MLA_HEADS * MLA_V), BF),
        input_output_aliases=aliases,
        scratch_shapes=[pltpu.VMEM((tq, n_keys), F32)] * 2 + [pltpu.VMEM((tq, n_keys), BF)] * 2,
        compiler_params=_params("parallel", "parallel"),
        name="mla_attend",
    )(*args)


def _gqa_heads(q_ref, k_segs, v_segs, masks, sink_ref, o_ref, s_refs, p_refs):
    tq = q_ref.shape[0]
    lane_q = lax.broadcasted_iota(jnp.int32, (tq, LANES), 1)
    lo_q = lane_q < SWA_HD
    k_roll = [pltpu.roll(k, SWA_HD, 1) for k in k_segs]
    v_roll = [pltpu.roll(v, SWA_HD, 1) for v in v_segs]
    heads = range(SWA_KV_HEADS)
    sinks = []
    for kvh in heads:
        tiles = [q_ref[:, (kvh * 2 + j) * LANES:(kvh * 2 + j + 1) * LANES] for j in range(2)]
        zero = jnp.zeros_like(tiles[0])
        qs = jnp.concatenate([jnp.where(lo_q, tiles[0], zero), jnp.where(lo_q, zero, tiles[0]),
                              jnp.where(lo_q, tiles[1], zero), jnp.where(lo_q, zero, tiles[1])], axis=0)
        sinks.append(jnp.concatenate(
            [sink_ref[0, kvh * SWA_GROUP + g:kvh * SWA_GROUP + g + 1, :]
             for g in range(SWA_GROUP) for _ in range(tq // LANES)], axis=1) * LOG2_E)
        off = 0
        for k, kr, msk in zip(k_segs, k_roll, masks):
            lane_k = lax.broadcasted_iota(jnp.int32, k.shape, 1)
            first = (lane_k < SWA_HD) == (kvh == 0)
            kd = jnp.where(first, k, kr)
            s = _bdot_nt(kd, qs)
            if msk is not None:
                s = jnp.where(msk, s, -1e30)
            s_refs[kvh][off:off + k.shape[0], :] = s
            off += k.shape[0]
    dens = [_softmax_numerator_t(s_refs[kvh], p_refs[kvh], sinks[kvh]) for kvh in heads]
    for kvh in heads:
        acc, off = None, 0
        for v, vr in zip(v_segs, v_roll):
            lane_v = lax.broadcasted_iota(jnp.int32, v.shape, 1)
            first = (lane_v < SWA_HD) == (kvh == 0)
            vd = jnp.where(first, v, vr)
            pv = lax.dot_general(vd.astype(BF), p_refs[kvh][off:off + v.shape[0], :],
                                 (((0,), (0,)), ((), ())), preferred_element_type=F32)
            acc = pv if acc is None else acc + pv
            off += v.shape[0]
        o = (acc / dens[kvh]).T
        for j in range(2):
            o_ref[:, (kvh * 2 + j) * LANES:(kvh * 2 + j + 1) * LANES] = jnp.where(
                lo_q, o[(2 * j) * tq:(2 * j + 1) * tq], o[(2 * j + 1) * tq:(2 * j + 2) * tq]).astype(BF)


def _softmax_numerator_t(s_ref, p_ref, sink, row_block=128):
    keys, cols = s_ref.shape
    dens = []
    for c in range(0, cols, LANES):
        mx = s_ref[0:row_block, c:c + LANES]
        for r0 in range(row_block, keys, row_block):
            mx = jnp.maximum(mx, s_ref[r0:r0 + row_block, c:c + LANES])
        snk = sink[:, c:c + LANES]
        m = jnp.maximum(jnp.max(mx, axis=0, keepdims=True), snk)
        mb = jnp.broadcast_to(m, (row_block, LANES))
        part = jnp.zeros((row_block, LANES), F32)
        for r0 in range(0, keys, row_block):
            p = jnp.exp2(s_ref[r0:r0 + row_block, c:c + LANES] - mb)
            part = part + p
            p_ref[r0:r0 + row_block, c:c + LANES] = p.astype(BF)
        dens.append(jnp.sum(part, axis=0, keepdims=True) + jnp.exp2(snk - m))
    return jnp.concatenate(dens, axis=1)


def _swa_scratch(keys, rows):
    return ([pltpu.VMEM((keys, rows), F32)] * SWA_KV_HEADS + [pltpu.VMEM((keys, rows), BF)] * SWA_KV_HEADS)


def _swa_ctx_kernel(q_ref, k_ref, v_ref, sink_ref, o_ref, *scratch):
    _gqa_heads(q_ref, [k_ref[...]], [v_ref[...]], [None], sink_ref, o_ref,
               scratch[:SWA_KV_HEADS], scratch[SWA_KV_HEADS:])


def _swa_ctx(sq, sk, sv, sink_rows, l):
    return pl.pallas_call(
        _swa_ctx_kernel,
        grid=(BATCH,),
        in_specs=[pl.BlockSpec((SEQ, SWA_HEADS * SWA_HD), lambda b: (b, 0)),
                  pl.BlockSpec((SEQ, LANES), lambda b: (b, 0)),
                  pl.BlockSpec((SEQ, LANES), lambda b: (b, 0)),
                  pl.BlockSpec((1, SWA_HEADS, LANES), lambda b: (l, 0, 0))],
        out_specs=pl.BlockSpec((SEQ, SWA_HEADS * SWA_HD), lambda b: (b, 0)),
        out_shape=jax.ShapeDtypeStruct((N_TOK, SWA_HEADS * SWA_HD), BF),
        scratch_shapes=_swa_scratch(SEQ, SWA_GROUP * SEQ),
        compiler_params=_params("parallel"),
        name="swa_ctx",
    )(sq, sk, sv, sink_rows)


def _swa_lat_kernel(q_ref, kp_ref, kc_ref, kn_ref, vp_ref, vc_ref, vn_ref, kx_ref, vx_ref, sink_ref, prev_ref,
                    o_ref, *scratch):
    w = SWA_WINDOW
    n = pl.program_id(1)
    nb = pl.num_programs(1)
    k_band = jnp.concatenate([kp_ref[...], kc_ref[...], kn_ref[...]], axis=0)
    v_band = jnp.concatenate([vp_ref[...], vc_ref[...], vn_ref[...]], axis=0)
    rows = SWA_GROUP * w
    r = lax.broadcasted_iota(jnp.int32, (3 * w, rows), 1) & (w - 1)
    c = lax.broadcasted_iota(jnp.int32, (3 * w, rows), 0)
    valid = (c >= r) & (c <= r + 2 * w) & ((c >= w) | (n > 0)) & ((c < 2 * w) | (n < nb - 1))
    _gqa_heads(q_ref, [k_band, kx_ref[0, 0]], [v_band, vx_ref[0, 0]], [valid, None], sink_ref, o_ref,
               scratch[:SWA_KV_HEADS], scratch[SWA_KV_HEADS:])


def _swa_lat(sq, sk, sv, k_cache, v_cache, sink_rows, l, prev):
    w = SWA_WINDOW
    nb = DEC_SEQ // w
    q0 = N_CTX // w

    def blk(d):
        return lambda b, n: (q0 + b * nb + jnp.clip(n + d, 0, nb - 1), 0)

    kv_specs = [pl.BlockSpec((w, LANES), blk(d)) for d in (-1, 0, 1)]
    cache_spec = pl.BlockSpec((1, 1, PAST_LEN, LANES), lambda b, n: (b, l, 0, 0))
    return pl.pallas_call(
        _swa_lat_kernel,
        grid=(DEC_BATCH, nb),
        in_specs=[pl.BlockSpec((w, SWA_HEADS * SWA_HD), blk(0))] + kv_specs + kv_specs
        + [cache_spec, cache_spec, pl.BlockSpec((1, SWA_HEADS, LANES), lambda b, n: (l, 0, 0)), _any_spec()],
        out_specs=pl.BlockSpec((w, SWA_HEADS * SWA_HD), blk(0)),
        out_shape=jax.ShapeDtypeStruct((N_TOK, SWA_HEADS * SWA_HD), BF),
        input_output_aliases={10: 0},
        scratch_shapes=_swa_scratch(3 * w + PAST_LEN, SWA_GROUP * w),
        compiler_params=_params("parallel", "parallel"),
        name="swa_lat",
    )(sq, sk, sk, sk, sv, sv, sv, k_cache, v_cache, sink_rows, prev)


def _gdn_conv_kernel(*refs):
    x_ref, w_ref, o_ref = refs[0], refs[1], refs[-1]
    x = x_ref[...]
    t = x.shape[0]
    w = w_ref[0]
    r8 = lax.broadcasted_iota(jnp.int32, (SUBLANES, x.shape[1]), 0)
    half = GDN_CONV // 2
    acc = x * w[half:half + 1, :]
    for k in range(GDN_CONV):
        d = k - half
        if d == 0:
            continue
        xs = pltpu.roll(x, (-d) % t, 0)
        top, bot = xs[:SUBLANES], xs[t - SUBLANES:]
        if d < 0:
            top = jnp.where(r8 + d >= 0, top, 0.0)
        else:
            bot = jnp.where(r8 + d < SUBLANES, bot, 0.0)
        xs = jnp.concatenate([top, xs[SUBLANES:t - SUBLANES], bot], axis=0)
        acc = acc + xs * w[k:k + 1, :]
    y = _silu(acc)
    is_qk = pl.program_id(1) < 2
    cols = []
    for h in range(y.shape[1] // LANES):
        yh = y[:, h * LANES:(h + 1) * LANES]
        nrm = lax.rsqrt(jnp.sum(yh * yh, axis=-1, keepdims=True) + 1e-6)
        cols.append(yh * jnp.where(is_qk, nrm, 1.0))
    o_ref[...] = jnp.concatenate(cols, axis=1)


def _gdn_conv(g3, conv_w, l, row0, n_seq, t, prev):
    blk0 = row0 // t
    args = [g3, conv_w]
    in_specs = [pl.BlockSpec((t, GDN_QK_DIM), lambda s, j: (blk0 + s, j)),
                pl.BlockSpec((1, GDN_CONV, GDN_QK_DIM), lambda s, j: (l, 0, j))]
    aliases = {}
    if prev is not None:
        args.append(prev)
        in_specs.append(_any_spec())
        aliases = {2: 0}
    return pl.pallas_call(
        _gdn_conv_kernel,
        grid=(n_seq, 3),
        in_specs=in_specs,
        out_specs=pl.BlockSpec((t, GDN_QK_DIM), lambda s, j: (blk0 + s, j)),
        out_shape=jax.ShapeDtypeStruct((N_TOK, GDN_CONV_CH), F32),
        input_output_aliases=aliases,
        compiler_params=_params("parallel", "parallel"),
        name="gdn_conv",
    )(*args)


def _gate_rows(misc):
    shape = (2 * GDN_NH, LANES)
    sel = (lax.broadcasted_iota(jnp.int32, shape, 1)
           == lax.broadcasted_iota(jnp.int32, shape, 0) + _M_G).astype(BF)
    b1, b2, b3 = _split3(misc)
    return _nt(sel, b1) + (_nt(sel, b2) + _nt(sel, b3))


def _gdn_chunk_pairs(pairs):
    c = GDN_CHUNK
    shape = (c, 2 * c)
    ri = lax.broadcasted_iota(jnp.int32, shape, 0)
    lane = lax.broadcasted_iota(jnp.int32, shape, 1)
    cj = lane & (c - 1)
    fwd_half = lane < c
    lower, upper = ri >= cj, ri <= cj
    incl = (fwd_half & lower) | (~fwd_half & upper)
    incl_t = (fwd_half & upper) | (~fwd_half & lower)
    strict = incl & (ri != cj)
    incl_f, incl_b = incl & fwd_half, incl & ~fwd_half
    eye = (ri == cj).astype(F32)
    n = len(pairs)
    rng = range(n)
    fw = [p[0] for p in pairs]
    bw = [p[1] for p in pairs]

    def halves(a, b):
        return jnp.where(fwd_half, a, b)

    def split_rows(m):
        return jnp.concatenate([jnp.where(fwd_half, m, 0.0), jnp.where(fwd_half, 0.0, m)], axis=0)

    def pair_dot3(x, p):
        xh = x.astype(BF)
        xl = (x - xh.astype(F32)).astype(BF)
        phf = p.astype(BF).astype(F32)
        bd_hi = split_rows(phf).astype(BF)
        bd_lo = split_rows(p - phf).astype(BF)
        return jnp.dot(jnp.concatenate([xh, xl, xh], axis=1), jnp.concatenate([bd_hi, bd_hi, bd_lo], axis=0),
                       preferred_element_type=F32)

    g_col = [halves(fw[i]["g_col"], bw[i]["g_col"]) for i in rng]
    gc_row = [jnp.sum(jnp.where(incl_t, g_col[i], 0.0), axis=0, keepdims=True) for i in rng]
    gcf = [jnp.sum(jnp.where(incl_f, pairs[i][2], 0.0), axis=1, keepdims=True) for i in rng]
    gcb = [jnp.sum(jnp.where(incl_b, pairs[i][2], 0.0), axis=1, keepdims=True) for i in rng]
    decay = [jnp.where(incl, jnp.exp(halves(gcf[i], gcb[i]) - gc_row[i]), 0.0) for i in rng]
    qf = [p["q"] * (GDN_DK ** -0.5) for p in fw]
    qb = [p["q"] * (GDN_DK ** -0.5) for p in bw]
    kbf = [p["k"] * p["beta"] for p in fw]
    kbb = [p["k"] * p["beta"] for p in bw]
    z = jnp.zeros((c, GDN_DK), F32)
    kq = [_bdot_nt(jnp.concatenate([jnp.concatenate([kbf[i], kbb[i]], axis=1),
                                    jnp.concatenate([qf[i], qb[i]], axis=1)], axis=0),
                   jnp.concatenate([jnp.concatenate([fw[i]["k"], z], axis=1),
                                    jnp.concatenate([z, bw[i]["k"]], axis=1)], axis=0)) for i in rng]
    pw = [jnp.where(strict, -(kq[i][:c] * decay[i]), 0.0) for i in rng]
    inv = [eye + m for m in pw]
    for _ in range(int(np.log2(c)) - 1):
        pw = [pair_dot3(m, m) for m in pw]
        inv = [inv[i] + pair_dot3(inv[i], pw[i]) for i in rng]
    ef = [jnp.exp(g) for g in gcf]
    eb = [jnp.exp(g) for g in gcb]
    uw = [_bdot(split_rows(inv[i]),
                jnp.concatenate([jnp.concatenate([fw[i]["v"] * fw[i]["beta"], kbf[i] * ef[i]], axis=1),
                                 jnp.concatenate([bw[i]["v"] * bw[i]["beta"], kbb[i] * eb[i]], axis=1)], axis=0))
          for i in rng]
    a = [jnp.where(incl, kq[i][c:] * decay[i], 0.0) for i in rng]
    glf = [g[c - 1:c, :] for g in gcf]
    glb = [g[0:1, :] for g in gcb]
    wqf = [_bdot(jnp.concatenate([uw[i][:c, GDN_DV:], qf[i] * ef[i]], axis=0), fw[i]["state"]) for i in rng]
    wqb = [_bdot(jnp.concatenate([uw[i][c:, GDN_DV:], qb[i] * eb[i]], axis=0), bw[i]["state"]) for i in rng]
    vnf = [uw[i][:c, :GDN_DV] - wqf[i][:c] for i in rng]
    vnb = [uw[i][c:, :GDN_DV] - wqb[i][:c] for i in rng]
    av = [_bdot(split_rows(a[i]), jnp.concatenate([vnf[i], vnb[i]], axis=0)) for i in rng]
    sf = [fw[i]["state"] * jnp.exp(glf[i]) + _bdot_tn(fw[i]["k"] * jnp.exp(glf[i] - gcf[i]), vnf[i]) for i in rng]
    sb = [bw[i]["state"] * jnp.exp(glb[i]) + _bdot_tn(bw[i]["k"] * jnp.exp(glb[i] - gcb[i]), vnb[i]) for i in rng]
    return [(wqf[i][c:] + av[i][:c], sf[i], wqb[i][c:] + av[i][c:], sb[i]) for i in rng]


def _gdn_kernel(*refs, context, n_par):
    qf_ref, qb_ref, mf_ref, mb_ref = refs[:4]
    st_ref = refs[-1]
    if context:
        of_ref, ob_ref, sfin_ref = refs[-4:-1]
    else:
        s0_ref = refs[4]
        of_ref, ob_ref = refs[-3:-1]
    n = pl.program_id(1)

    @pl.when(n == 0)
    def _():
        if context:
            st_ref[...] = jnp.zeros_like(st_ref)
        else:
            st_ref[...] = s0_ref[:, 0]

    def problem(qkv, misc, s, h, d):
        i = d * GDN_HEADS + h
        return dict(q=qkv[:, h * GDN_DK:(h + 1) * GDN_DK],
                    k=qkv[:, GDN_QK_DIM + h * GDN_DK:GDN_QK_DIM + (h + 1) * GDN_DK],
                    v=qkv[:, 2 * GDN_QK_DIM + h * GDN_DV:2 * GDN_QK_DIM + (h + 1) * GDN_DV],
                    g_col=misc[:, _M_G + i:_M_G + i + 1], beta=misc[:, _M_B + i:_M_B + i + 1],
                    state=st_ref[s, i])

    lane = lax.broadcasted_iota(jnp.int32, (1, 2 * GDN_CHUNK), 1)
    pairs = []
    for s in range(n_par):
        qkv_f, qkv_b, misc_f, misc_b = qf_ref[s], qb_ref[s], mf_ref[s], mb_ref[s]
        rows = _gate_rows(jnp.concatenate([misc_f, misc_b], axis=0))
        for h in range(GDN_HEADS):
            g_row = jnp.where(lane < GDN_CHUNK, rows[h:h + 1, :], rows[GDN_HEADS + h:GDN_HEADS + h + 1, :])
            pairs.append((problem(qkv_f, misc_f, s, h, 0), problem(qkv_b, misc_b, s, h, 1), g_row))
    results = _gdn_chunk_pairs(pairs)
    for s in range(n_par):
        for h in range(GDN_HEADS):
            o_f, s_f, o_b, s_b = results[s * GDN_HEADS + h]
            of_ref[s, :, h * GDN_DV:(h + 1) * GDN_DV] = o_f
            ob_ref[s, :, h * GDN_DV:(h + 1) * GDN_DV] = o_b
            st_ref[s, h] = s_f
            st_ref[s, GDN_HEADS + h] = s_b

    if context:
        @pl.when(n == pl.num_programs(1) - 1)
        def _():
            sfin_ref[:, 0] = st_ref[...]


def _gdn(qkv, misc, l, t, n_seq, seq0, s0, prevs, n_par=GDN_PAR):
    c = GDN_CHUNK
    nc = t // c
    context = s0 is None
    sb0 = seq0 // n_par
    n_all = N_TOK // t
    fwd = lambda s, n: (sb0 + s, n, 0)
    bwd = lambda s, n: (sb0 + s, nc - 1 - n, 0)
    st_spec = pl.BlockSpec((n_par, 1, GDN_NH, GDN_DK, GDN_DV), lambda s, n: (s, l, 0, 0, 0))
    qkv3 = qkv.reshape(n_all, t, GDN_CONV_CH)
    misc3 = misc.reshape(n_all, t, LANES)
    in_specs = [pl.BlockSpec((n_par, c, GDN_CONV_CH), fwd), pl.BlockSpec((n_par, c, GDN_CONV_CH), bwd),
                pl.BlockSpec((n_par, c, LANES), fwd), pl.BlockSpec((n_par, c, LANES), bwd)]
    args = [qkv3, qkv3, misc3, misc3]
    if not context:
        in_specs.append(st_spec)
        args.append(s0)
    out_specs = [pl.BlockSpec((n_par, c, GDN_V_DIM), fwd), pl.BlockSpec((n_par, c, GDN_V_DIM), bwd)]
    out_shape = [jax.ShapeDtypeStruct((n_all, t, GDN_V_DIM), F32)] * 2
    if context:
        out_specs.append(st_spec)
        out_shape.append(jax.ShapeDtypeStruct((n_seq, DEPTH, GDN_NH, GDN_DK, GDN_DV), F32))
    aliases = {}
    for j, p in enumerate(prevs):
        if p is not None:
            aliases[len(args)] = j
            args.append(p.reshape(out_shape[j].shape))
            in_specs.append(_any_spec())
    return pl.pallas_call(
        functools.partial(_gdn_kernel, context=context, n_par=n_par),
        grid=(n_seq // n_par, nc),
        in_specs=in_specs, out_specs=out_specs, out_shape=out_shape,
        input_output_aliases=aliases,
        scratch_shapes=[pltpu.VMEM((n_par, GDN_NH, GDN_DK, GDN_DV), F32)],
        compiler_params=_params("parallel", "arbitrary"),
        name="gdn",
    )(*args)


def _out_kernel(x_ref, mod_ref, om_ref, os_ref, gf_ref, gb_ref, gz_ref, gn_ref, w_ref, o_ref):
    gate = mod_ref[0, 0][:, 2 * D_MODEL:]
    s = gf_ref[...] + gb_ref[...]
    gz = gz_ref[...]
    cols = []
    for h in range(GDN_HEADS):
        sh = s[:, h * GDN_DV:(h + 1) * GDN_DV]
        cols.append(_rms(sh, gn_ref[0]) * _silu(gz[:, h * GDN_DV:(h + 1) * GDN_DV]))
    og = jnp.concatenate(cols, axis=1)
    n_m = MLA_HEADS * MLA_V
    n_s = SWA_HEADS * SWA_HD
    y = (jnp.dot(om_ref[...], w_ref[0, :n_m, :], preferred_element_type=F32)
         + jnp.dot(os_ref[...], w_ref[0, n_m:n_m + n_s, :], preferred_element_type=F32)
         + jnp.dot(og.astype(BF), w_ref[0, n_m + n_s:, :], preferred_element_type=F32))
    o_ref[...] = x_ref[...] + gate * y


def _out_proj(x, mods, l, o_mla, o_swa, o_gf, o_gb, gz, gdn_norm, w_out, tm=TM_OUT):
    row = lambda i: (i, 0)
    lay3 = lambda i: (l, 0, 0)
    return pl.pallas_call(
        _out_kernel,
        grid=(N_TOK // tm,),
        in_specs=[pl.BlockSpec((tm, D_MODEL), row),
                  _mod_spec(l, 1, tm),
                  pl.BlockSpec((tm, MLA_HEADS * MLA_V), row),
                  pl.BlockSpec((tm, SWA_HEADS * SWA_HD), row),
                  pl.BlockSpec((tm, GDN_V_DIM), row),
                  pl.BlockSpec((tm, GDN_V_DIM), row),
                  pl.BlockSpec((tm, GDN_V_DIM), row),
                  pl.BlockSpec((1, 1, GDN_DV), lay3),
                  pl.BlockSpec((1,) + w_out.shape[1:], lay3)],
        out_specs=pl.BlockSpec((tm, D_MODEL), row),
        out_shape=jax.ShapeDtypeStruct((N_TOK, D_MODEL), F32),
        compiler_params=_params("parallel"),
        name="out_proj",
    )(x, mods, o_mla, o_swa, o_gf, o_gb, gz, gdn_norm.reshape(DEPTH, 1, GDN_DV), w_out)


def _rot_columns(w, dim):
    shp = w.shape
    w6 = w.reshape(shp[:-1] + (shp[-1] // dim, 2, 2, dim // 4))
    sign = jnp.asarray([-1.0, 1.0], F32).reshape(2, 1)
    return (jnp.flip(w6, axis=-2) * sign).reshape(shp)


def _axial_rope(n_tokens, dim):
    f32 = np.float32
    rows = n_tokens // GRID_W
    row = np.repeat(np.arange(rows, dtype=f32), GRID_W)
    col = np.tile(np.arange(GRID_W, dtype=f32), rows)
    axis_dim = dim // 2
    inv_freq = (f32(1.0) / (f32(ROPE_BASE) ** (np.arange(0, axis_dim, 2, dtype=f32) / f32(axis_dim)))).astype(f32)
    ang_r = row[:, None] * inv_freq[None, :]
    ang_c = col[:, None] * inv_freq[None, :]
    ang = np.concatenate([ang_r, ang_r, ang_c, ang_c], axis=-1).astype(f32)
    return np.cos(ang).astype(f32), np.sin(ang).astype(f32)


def _rope_table(tm):
    cos_m, sin_m = _axial_rope(DEC_SEQ, MLA_ROPE)
    cos_s, sin_s = _axial_rope(DEC_SEQ, SWA_HD)
    t = DEC_SEQ
    one = lambda w: np.ones((t, w), np.float32)
    zero = lambda w: np.zeros((t, w), np.float32)
    lat = np.concatenate([
        cos_s, cos_s, sin_s, sin_s,
        cos_m, one(LANES - MLA_ROPE), sin_m, zero(LANES - MLA_ROPE),
        one(MLA_NOPE), cos_m, one(LANES - MLA_NOPE - MLA_ROPE),
        zero(MLA_NOPE), sin_m, zero(LANES - MLA_NOPE - MLA_ROPE)], axis=1)
    ident_row = np.concatenate([np.ones(LANES), np.zeros(LANES)] * 3).astype(np.float32)
    ident = np.broadcast_to(ident_row[None, :], (tm, _T_END))
    return jnp.asarray(np.concatenate([ident, lat], axis=0))


def _mixer_weights(w_in, mla_w_qb, mla_w_kvb):
    nl = DEPTH
    offs = np.cumsum([0, MLA_Q_LORA, MLA_KV_LORA, MLA_ROPE, SWA_HEADS * SWA_HD, SWA_KV_HEADS * SWA_HD,
                      SWA_KV_HEADS * SWA_HD, GDN_CONV_CH, GDN_V_DIM, 2 * GDN_NH])
    w_t = jnp.swapaxes(w_in, 1, 2)
    cq, ckv, krope, sq, sk, sv, g3, gz, gates = [w_t[:, offs[i]:offs[i + 1], :] for i in range(9)]

    def rot(w, dim):
        w6 = w.reshape(nl, w.shape[1] // dim, 2, 2, dim // 4, D_MODEL)
        return (jnp.flip(w6, axis=3) * jnp.asarray([-1.0, 1.0], F32).reshape(2, 1, 1)).reshape(w.shape)

    zeros = lambda n: jnp.zeros((nl, n, D_MODEL), F32)
    misc_a = jnp.concatenate([krope, gates, zeros(LANES - _M_END)], axis=1)
    misc_b = jnp.concatenate([rot(krope, MLA_ROPE), zeros(LANES - MLA_ROPE)], axis=1)
    win = jnp.concatenate([cq, ckv, sq, rot(sq, SWA_HD), sk, rot(sk, SWA_HD), sv, g3, gz, misc_a, misc_b],
                          axis=1).astype(BF)

    r = MLA_Q_LORA
    wq = mla_w_qb.reshape(nl, r, MLA_HEADS, MLA_NOPE + MLA_ROPE)
    nope, rope = wq[..., :MLA_NOPE], wq[..., MLA_NOPE:]
    pad = LANES - MLA_NOPE - MLA_ROPE
    z = lambda n: jnp.zeros((nl, r, MLA_HEADS, n), F32)
    qa = jnp.concatenate([nope, rope, z(pad)], axis=-1).reshape(nl, r, MLA_HEADS * LANES)
    qb = jnp.concatenate([z(MLA_NOPE), _rot_columns(rope, MLA_ROPE), z(pad)],
                         axis=-1).reshape(nl, r, MLA_HEADS * LANES)
    wqb = jnp.concatenate([qa, qb], axis=2).astype(BF)

    kvb = mla_w_kvb.reshape(nl, MLA_KV_LORA, MLA_HEADS, MLA_NOPE + MLA_V)
    k_nope = jnp.concatenate([kvb[..., :MLA_NOPE],
                              jnp.zeros((nl, MLA_KV_LORA, MLA_HEADS, LANES - MLA_NOPE), F32)],
                             axis=-1).reshape(nl, MLA_KV_LORA, MLA_HEADS * LANES)
    place = np.zeros((LANES, MLA_HEADS, LANES), np.float32)
    for i in range(MLA_ROPE):
        place[i, :, MLA_NOPE + i] = 1.0
    place = jnp.broadcast_to(jnp.asarray(place.reshape(1, LANES, MLA_HEADS * LANES)),
                             (nl, LANES, MLA_HEADS * LANES))
    wk = jnp.concatenate([k_nope, place], axis=1).astype(BF)
    wv = kvb[..., MLA_NOPE:].reshape(nl, MLA_KV_LORA, MLA_HEADS * MLA_V).astype(BF)
    return win, wqb, wk, wv


def _misc_rows(vals):
    rows = jnp.zeros((DEPTH, 1, LANES), F32)
    return rows.at[:, 0, _M_G:_M_B].set(vals.reshape(DEPTH, GDN_NH).astype(F32))


def kernel(x_prompt, x_sample, cache_mla_ckv, cache_mla_krope, cache_swa_k, cache_swa_v, state_gdn, c, c_ctx,
           w_ada, b_ada, norm_ffn1, ffn1_w1, ffn1_w2, norm_mix, w_in, mla_q_norm, mla_w_qb, mla_kv_norm,
           mla_w_kvb, swa_sink, gdn_conv_w, gdn_a_log, gdn_dt_bias, gdn_norm, w_out, norm_ffn2, ffn2_w1,
           ffn2_w2, final_norm):
    cond = jnp.concatenate([c_ctx[None, :], c, jnp.zeros((COND_ROWS - N_GROUPS, D_MODEL), F32)], axis=0)
    mods = _adaln(cond, w_ada, b_ada)[:, :N_GROUPS].reshape(DEPTH, N_GROUPS, 1, N_MOD * D_MODEL)
    tab = _rope_table(TM_PROJ)
    win, wqb, wk, wv = _mixer_weights(w_in, mla_w_qb, mla_w_kvb)
    w11, w12, w21, w22, wo = (w.astype(BF) for w in (ffn1_w1, ffn1_w2, ffn2_w1, ffn2_w2, w_out))
    alog, dtb = _misc_rows(gdn_a_log), _misc_rows(gdn_dt_bias)
    sink_rows = jnp.broadcast_to(swa_sink[:, :, None], (DEPTH, SWA_HEADS, LANES))
    cache_misc = jnp.pad(cache_mla_krope, ((0, 0), (0, 0), (0, 0), (0, LANES - MLA_ROPE)))
    cache_k = cache_swa_k.reshape(DEC_BATCH, DEPTH, PAST_LEN, LANES)
    cache_v = cache_swa_v.reshape(DEC_BATCH, DEPTH, PAST_LEN, LANES)
    s0 = state_gdn.reshape(DEC_BATCH, DEPTH, GDN_NH, GDN_DK, GDN_DV)

    xs = [x_prompt.reshape(N_CTX, D_MODEL), x_sample.reshape(N_LAT, D_MODEL)]
    caches = new_st = None
    for l in range(DEPTH):
        x = _ffn(xs, mods, l, 0, norm_ffn1, w11, w12)
        (q_mla, misc, k_mla, v_mla, sq, sk, sv, g3, gz, *caches) = _proj(
            x, mods, l, norm_mix, win, mla_q_norm, wqb, mla_kv_norm, wk, wv, tab, alog, dtb, caches)

        k_c, v_c = _kv_cache(cache_mla_ckv, cache_misc, l, wk, wv)
        o_mla = _mla_attend(q_mla, [k_mla], [v_mla], BATCH, SEQ, 0, [(0, SEQ)], None)
        o_mla = _mla_attend(q_mla, [k_mla, k_c], [v_mla, v_c], DEC_BATCH, DEC_SEQ, N_CTX,
                            [(N_CTX // DEC_SEQ, DEC_SEQ), (0, PAST_LEN)], o_mla)

        o_swa = _swa_ctx(sq, sk, sv, sink_rows, l)
        o_swa = _swa_lat(sq, sk, sv, cache_k, cache_v, sink_rows, l, o_swa)

        qkv = _gdn_conv(g3, gdn_conv_w, l, 0, BATCH, SEQ, None)
        qkv = _gdn_conv(g3, gdn_conv_w, l, N_CTX, DEC_BATCH, DEC_SEQ, qkv)
        o_gf, o_gb, new_st = _gdn(qkv, misc, l, SEQ, BATCH, 0, None, [None, None, new_st])
        o_gf, o_gb = _gdn(qkv, misc, l, DEC_SEQ, DEC_BATCH, N_CTX // DEC_SEQ, s0, [o_gf, o_gb])

        x = _out_proj(x, mods, l, o_mla, o_swa, o_gf.reshape(N_TOK, GDN_V_DIM), o_gb.reshape(N_TOK, GDN_V_DIM),
                      gz, gdn_norm, wo)
        if l + 1 < DEPTH:
            xs = [_ffn([x], mods, l, 2, norm_ffn2, w21, w22)]
        else:
            y_prompt, y_sample = _ffn([x], mods, l, 2, norm_ffn2, w21, w22, final_gain=final_norm)

    new_ckv, new_krope, new_sk, new_sv = caches
    kv_shape = (BATCH, DEPTH, SEQ, SWA_KV_HEADS, SWA_HD)
    return (y_prompt.reshape(BATCH, SEQ, D_MODEL), y_sample.reshape(DEC_BATCH, DEC_SEQ, D_MODEL), new_ckv,
            new_krope, new_sk.reshape(kv_shape), new_sv.reshape(kv_shape),
            new_st.reshape(BATCH, DEPTH, 2, GDN_HEADS, GDN_DK, GDN_DV))
```

```python
import functools

import numpy as np
import jax
import jax.numpy as jnp
from jax import lax
from jax.experimental import pallas as pl
from jax.experimental.pallas import tpu as pltpu

D_MODEL = 1024
BATCH = 16
SEQ = 256
DEPTH = 2
DEC_BATCH = 2
DEC_SEQ = 2048
PAST_LEN = 512
GRID_W = 64
ROPE_BASE = 10000.0
NORM_EPS = 1e-6
N_MOD = 9
D_FF = 2816
MLA_HEADS = 8
MLA_Q_LORA = 384
MLA_KV_LORA = 256
MLA_NOPE = 64
MLA_ROPE = 32
MLA_V = 64
SWA_HEADS = 8
SWA_KV_HEADS = 2
SWA_GROUP = SWA_HEADS // SWA_KV_HEADS
SWA_HD = 64
SWA_WINDOW = 128
GDN_HEADS = 4
GDN_DK = 128
GDN_DV = 128
GDN_CONV = 5
GDN_CHUNK = 64
GDN_QK_DIM = GDN_HEADS * GDN_DK
GDN_V_DIM = GDN_HEADS * GDN_DV
GDN_CONV_CH = 2 * GDN_QK_DIM + GDN_V_DIM
GDN_NH = 2 * GDN_HEADS

N_CTX = BATCH * SEQ
N_LAT = DEC_BATCH * DEC_SEQ
N_TOK = N_CTX + N_LAT
N_GROUPS = 1 + DEC_BATCH
COND_ROWS = 8

LANES = 128
SUBLANES = 8
VMEM_LIMIT_BYTES = 56 * 1024 * 1024

TM_FFN = 512
TM_PROJ = 2 * SEQ
TM_OUT = 512
TQ_MLA = 512
GDN_PAR = 2
GDN_PAR_CTX = 4

BF = jnp.bfloat16
F32 = jnp.float32
LOG2_E = 1.4426950408889634
MLA_Q_SCALE = (MLA_NOPE + MLA_ROPE) ** -0.5 * LOG2_E

_C_CQ = 0
_C_CKV = _C_CQ + MLA_Q_LORA
_C_SQ = _C_CKV + MLA_KV_LORA
_C_SQR = _C_SQ + SWA_HEADS * SWA_HD
_C_SK = _C_SQR + SWA_HEADS * SWA_HD
_C_SKR = _C_SK + SWA_KV_HEADS * SWA_HD
_C_SV = _C_SKR + SWA_KV_HEADS * SWA_HD
_C_G3 = _C_SV + SWA_KV_HEADS * SWA_HD
_C_GZ = _C_G3 + GDN_CONV_CH
_C_MA = _C_GZ + GDN_V_DIM
_C_MB = _C_MA + LANES
_C_END = _C_MB + LANES
_M_G = MLA_ROPE
_M_B = MLA_ROPE + GDN_NH
_M_END = MLA_ROPE + 2 * GDN_NH
_CACHE_WIDTHS = (MLA_KV_LORA, MLA_ROPE, SWA_KV_HEADS * SWA_HD, SWA_KV_HEADS * SWA_HD)
_T_CS, _T_SS, _T_CA, _T_SA, _T_CQ, _T_SQ = (i * LANES for i in range(6))
_T_END = 6 * LANES


def _params(*sem):
    return pltpu.CompilerParams(dimension_semantics=sem, vmem_limit_bytes=VMEM_LIMIT_BYTES)


def _bdot(a, b):
    return jnp.dot(a.astype(BF), b.astype(BF), preferred_element_type=F32)


def _nt(a, b):
    return lax.dot_general(a, b, (((1,), (1,)), ((), ())), preferred_element_type=F32)


def _bdot_nt(a, b):
    return _nt(a.astype(BF), b.astype(BF))


def _bdot_tn(a, b):
    return lax.dot_general(a.astype(BF), b.astype(BF), (((0,), (0,)), ((), ())),
                           preferred_element_type=F32)


def _split3(a):
    b1 = a.astype(BF)
    r = a - b1.astype(F32)
    b2 = r.astype(BF)
    b3 = (r - b2.astype(F32)).astype(BF)
    return b1, b2, b3


def _silu(x):
    return x / (1.0 + jnp.exp(-x))


def _rms(x, gain, eps=NORM_EPS):
    return x * lax.rsqrt(jnp.mean(x * x, axis=-1, keepdims=True) + eps) * gain


def _group_of_row(r):
    return jnp.where(r < N_CTX, 0, 1 + (r - N_CTX) // DEC_SEQ)


def _any_spec():
    return pl.BlockSpec(memory_space=pl.ANY)


def _adaln_kernel(c_ref, w_ref, b_ref, o_ref):
    o_ref[0] = _bdot(_silu(c_ref[...]), w_ref[0]) + b_ref[0]


def _adaln(cond, w_ada, b_ada, tn=1536):
    n = N_MOD * D_MODEL
    return pl.pallas_call(
        _adaln_kernel,
        grid=(DEPTH, n // tn),
        in_specs=[pl.BlockSpec((COND_ROWS, D_MODEL), lambda l, j: (0, 0)),
                  pl.BlockSpec((1, D_MODEL, tn), lambda l, j: (l, 0, j)),
                  pl.BlockSpec((1, 1, tn), lambda l, j: (l, 0, j))],
        out_specs=pl.BlockSpec((1, COND_ROWS, tn), lambda l, j: (l, 0, j)),
        out_shape=jax.ShapeDtypeStruct((DEPTH, COND_ROWS, n), F32),
        compiler_params=_params("parallel", "parallel"),
        name="adaln",
    )(cond, w_ada, b_ada.reshape(DEPTH, 1, n))


def _mod_spec(l, which, tm):
    return pl.BlockSpec((1, 1, 1, 3 * D_MODEL), lambda i: (l, _group_of_row(i * tm), 0, which))


def _ffn_kernel(*refs, n_in, n_a_tiles, final):
    x_refs = refs[:n_in]
    mod_ref, gain_ref, w1_ref, w2_ref = refs[n_in:n_in + 4]
    rest = refs[n_in + 4:]
    i = pl.program_id(0)
    if n_in == 2:
        x = jnp.where(i < n_a_tiles, x_refs[0][...], x_refs[1][...])
    else:
        x = x_refs[0][...]
    mod = mod_ref[0, 0]
    shift, scale, gate = mod[:, :D_MODEL], mod[:, D_MODEL:2 * D_MODEL], mod[:, 2 * D_MODEL:]
    h = _rms(x, gain_ref[0]) * (1.0 + scale) + shift
    gu = jnp.dot(h.astype(BF), w1_ref[0], preferred_element_type=F32)
    a = _silu(gu[:, :D_FF]) * gu[:, D_FF:]
    y = x + gate * (0.5 * jnp.dot(a.astype(BF), w2_ref[0], preferred_element_type=F32))
    if final:
        fg_ref, oa_ref, ob_ref = rest
        yn = _rms(y, fg_ref[...])

        @pl.when(i < n_a_tiles)
        def _():
            oa_ref[...] = yn

        @pl.when(i >= n_a_tiles)
        def _():
            ob_ref[...] = yn
    else:
        rest[0][...] = y


def _ffn(xs, mods, l, which, gain, w1, w2, final_gain=None, tm=TM_FFN):
    n_a = N_CTX // tm
    row = lambda i: (i, 0)
    first = lambda i: (jnp.minimum(i, n_a - 1), 0)
    second = lambda i: (jnp.maximum(i - n_a, 0), 0)
    lay3 = lambda i: (l, 0, 0)
    x_specs = ([pl.BlockSpec((tm, D_MODEL), row)] if len(xs) == 1
               else [pl.BlockSpec((tm, D_MODEL), first), pl.BlockSpec((tm, D_MODEL), second)])
    in_specs = x_specs + [_mod_spec(l, which, tm),
                          pl.BlockSpec((1, 1, D_MODEL), lay3),
                          pl.BlockSpec((1, D_MODEL, 2 * D_FF), lay3, pipeline_mode=pl.Buffered(1)),
                          pl.BlockSpec((1, D_FF, D_MODEL), lay3, pipeline_mode=pl.Buffered(1))]
    args = list(xs) + [mods, gain.reshape(DEPTH, 1, D_MODEL), w1, w2]
    if final_gain is None:
        out_specs = pl.BlockSpec((tm, D_MODEL), row)
        out_shape = jax.ShapeDtypeStruct((N_TOK, D_MODEL), F32)
    else:
        in_specs.append(pl.BlockSpec((1, D_MODEL), lambda i: (0, 0)))
        args.append(final_gain.reshape(1, D_MODEL))
        out_specs = [pl.BlockSpec((tm, D_MODEL), first), pl.BlockSpec((tm, D_MODEL), second)]
        out_shape = [jax.ShapeDtypeStruct((N_CTX, D_MODEL), F32), jax.ShapeDtypeStruct((N_LAT, D_MODEL), F32)]
    return pl.pallas_call(
        functools.partial(_ffn_kernel, n_in=len(xs), n_a_tiles=n_a, final=final_gain is not None),
        grid=(N_TOK // tm,),
        in_specs=in_specs, out_specs=out_specs, out_shape=out_shape,
        compiler_params=_params("arbitrary" if final_gain is not None else "parallel"),
        name="ffn",
    )(*args)


def _kv_expand(ckv_n, misc, wk, wv):
    kin = jnp.concatenate([ckv_n, misc], axis=1).astype(BF)
    kk = jnp.dot(kin, wk, preferred_element_type=F32)
    vv = jnp.dot(ckv_n.astype(BF), wv, preferred_element_type=F32)
    return kk.astype(BF), vv.astype(BF)


def _proj_kernel(*refs, n_ctx_tiles):
    (x_ref, mod_ref, gain_ref, win_ref, qg_ref, wqb_ref, kvg_ref, wk_ref, wv_ref, tab_ref,
     alog_ref, dtb_ref) = refs[:12]
    (q_ref, misc_ref, kmla_ref, vmla_ref, sq_ref, sk_ref, sv_ref, g3_ref, gz_ref,
     c_ckv_ref, c_krope_ref, c_sk_ref, c_sv_ref) = refs[-13:]
    i = pl.program_id(0)
    mod = mod_ref[0, 0]
    shift, scale = mod[:, :D_MODEL], mod[:, D_MODEL:2 * D_MODEL]
    hb = (_rms(x_ref[...], gain_ref[0]) * (1.0 + scale) + shift).astype(BF)
    tab = tab_ref[...]

    def project(c0, c1):
        return _nt(hb, win_ref[0, c0:c1, :])

    u_lora = project(_C_CQ, _C_SQ)
    u_misc = project(_C_MA, _C_END)
    u_sq = project(_C_SQ, _C_SK)
    u_skv = project(_C_SK, _C_G3)

    qn = _rms(u_lora[:, :MLA_Q_LORA], qg_ref[0])
    ckv_n = _rms(u_lora[:, MLA_Q_LORA:], kvg_ref[0])

    m = u_misc[:, :LANES] * tab[:, _T_CA:_T_CA + LANES] + u_misc[:, LANES:] * tab[:, _T_SA:_T_SA + LANES]
    lane = lax.broadcasted_iota(jnp.int32, m.shape, 1)
    z = m + dtb_ref[0]
    softplus = jnp.maximum(z, 0.0) + jnp.log(1.0 + jnp.exp(-jnp.abs(z)))
    decay = -jnp.exp(alog_ref[0]) * softplus
    strength = 1.0 / (1.0 + jnp.exp(-m))
    misc = jnp.where((lane >= _M_G) & (lane < _M_B), decay,
                     jnp.where((lane >= _M_B) & (lane < _M_END), strength, m))
    misc_ref[...] = misc

    n_sq = SWA_HEADS * SWA_HD
    cos_s = tab[:, _T_CS:_T_CS + LANES]
    sin_s = tab[:, _T_SS:_T_SS + LANES]
    cos4 = jnp.concatenate([cos_s] * (n_sq // LANES), axis=1)
    sin4 = jnp.concatenate([sin_s] * (n_sq // LANES), axis=1)
    sq_ref[...] = ((u_sq[:, :n_sq] * cos4 + u_sq[:, n_sq:] * sin4) * (SWA_HD ** -0.5 * LOG2_E)).astype(BF)
    sk = u_skv[:, :LANES] * cos_s + u_skv[:, LANES:2 * LANES] * sin_s
    sv = u_skv[:, 2 * LANES:]
    sk_ref[...] = sk
    sv_ref[...] = sv

    @pl.when(i < n_ctx_tiles)
    def _():
        for j in range(c_ckv_ref.shape[0]):
            rows = slice(j * SEQ, (j + 1) * SEQ)
            c_ckv_ref[j, 0] = ckv_n[rows]
            c_krope_ref[j, 0] = misc[rows, :MLA_ROPE]
            c_sk_ref[j, 0] = sk[rows]
            c_sv_ref[j, 0] = sv[rows]

    g3_ref[...] = project(_C_G3, _C_GZ)
    gz_ref[...] = project(_C_GZ, _C_MA)

    q2 = jnp.dot(qn.astype(BF), wqb_ref[0], preferred_element_type=F32)
    nq = MLA_HEADS * LANES
    cosq = jnp.concatenate([tab[:, _T_CQ:_T_CQ + LANES]] * MLA_HEADS, axis=1)
    sinq = jnp.concatenate([tab[:, _T_SQ:_T_SQ + LANES]] * MLA_HEADS, axis=1)
    q_ref[...] = ((q2[:, :nq] * cosq + q2[:, nq:] * sinq) * MLA_Q_SCALE).astype(BF)
    kk, vv = _kv_expand(ckv_n, misc, wk_ref[0], wv_ref[0])
    kmla_ref[...] = kk
    vmla_ref[...] = vv


def _proj(x, mods, l, gain, win, qg, wqb, kvg, wk, wv, tab, alog, dtb, cache_prev, tm=TM_PROJ):
    assert tm % SEQ == 0
    n_ctx_tiles = N_CTX // tm
    lat_tiles = DEC_SEQ // tm
    lay3 = lambda i: (l, 0, 0)
    row = lambda i: (i, 0)

    def tab_map(i):
        return (jnp.where(i < n_ctx_tiles, 0, 1 + (i - n_ctx_tiles) % lat_tiles), 0)

    def lay_spec(a):
        return pl.BlockSpec((1,) + a.shape[1:], lay3)

    widths = [(MLA_HEADS * LANES, BF), (LANES, F32), (MLA_HEADS * LANES, BF),
              (MLA_HEADS * MLA_V, BF), (SWA_HEADS * SWA_HD, BF), (SWA_KV_HEADS * SWA_HD, F32),
              (SWA_KV_HEADS * SWA_HD, F32), (GDN_CONV_CH, F32), (GDN_V_DIM, F32)]
    out_specs = [pl.BlockSpec((tm, w), row) for w, _ in widths]
    out_shape = [jax.ShapeDtypeStruct((N_TOK, w), dt) for w, dt in widths]
    for w in _CACHE_WIDTHS:
        out_specs.append(pl.BlockSpec((tm // SEQ, 1, SEQ, w),
                                      lambda i: (jnp.minimum(i, n_ctx_tiles - 1), l, 0, 0)))
        out_shape.append(jax.ShapeDtypeStruct((BATCH, DEPTH, SEQ, w), F32))
    qg, kvg = qg.reshape(DEPTH, 1, -1), kvg.reshape(DEPTH, 1, -1)
    args = [x, mods, gain.reshape(DEPTH, 1, D_MODEL), win, qg, wqb, kvg, wk, wv, tab, alog, dtb]
    in_specs = [pl.BlockSpec((tm, D_MODEL), row), _mod_spec(l, 1, tm),
                pl.BlockSpec((1, 1, D_MODEL), lay3), lay_spec(win), lay_spec(qg), lay_spec(wqb),
                lay_spec(kvg), lay_spec(wk), lay_spec(wv), pl.BlockSpec((tm, _T_END), tab_map),
                lay_spec(alog), lay_spec(dtb)]
    aliases = {}
    if cache_prev is not None:
        for j, prev in enumerate(cache_prev):
            aliases[len(args)] = len(widths) + j
            args.append(prev)
            in_specs.append(_any_spec())
    return pl.pallas_call(
        functools.partial(_proj_kernel, n_ctx_tiles=n_ctx_tiles),
        grid=(N_TOK // tm,),
        in_specs=in_specs, out_specs=out_specs, out_shape=out_shape,
        input_output_aliases=aliases,
        compiler_params=_params("arbitrary"),
        name="proj",
    )(*args)


def _kv_cache_kernel(ckv_ref, misc_ref, wk_ref, wv_ref, k_ref, v_ref):
    kk, vv = _kv_expand(ckv_ref[0, 0], misc_ref[0, 0], wk_ref[0], wv_ref[0])
    k_ref[...] = kk
    v_ref[...] = vv


def _kv_cache(cache_ckv, cache_misc, l, wk, wv):
    lay3 = lambda b: (l, 0, 0)
    n = DEC_BATCH * PAST_LEN
    return pl.pallas_call(
        _kv_cache_kernel,
        grid=(DEC_BATCH,),
        in_specs=[pl.BlockSpec((1, 1, PAST_LEN, MLA_KV_LORA), lambda b: (b, l, 0, 0)),
                  pl.BlockSpec((1, 1, PAST_LEN, LANES), lambda b: (b, l, 0, 0)),
                  pl.BlockSpec((1,) + wk.shape[1:], lay3),
                  pl.BlockSpec((1,) + wv.shape[1:], lay3)],
        out_specs=[pl.BlockSpec((PAST_LEN, MLA_HEADS * LANES), lambda b: (b, 0)),
                   pl.BlockSpec((PAST_LEN, MLA_HEADS * MLA_V), lambda b: (b, 0))],
        out_shape=[jax.ShapeDtypeStruct((n, MLA_HEADS * LANES), BF),
                   jax.ShapeDtypeStruct((n, MLA_HEADS * MLA_V), BF)],
        compiler_params=_params("parallel"),
        name="kv_cache",
    )(cache_ckv, cache_misc, wk, wv)


def _softmax_numerator(s_ref, p_ref, sink=None, row_block=128):
    rows, cols = s_ref.shape
    mx = jnp.max(s_ref[...], axis=-1, keepdims=True)
    if sink is not None:
        mx = jnp.maximum(mx, sink)
    sums = []
    for r0 in range(0, rows, row_block):
        mb = jnp.broadcast_to(mx[r0:r0 + row_block], (row_block, LANES))
        part = jnp.zeros((row_block, LANES), F32)
        for c in range(0, cols, LANES):
            p = jnp.exp2(s_ref[r0:r0 + row_block, c:c + LANES] - mb)
            part = part + p
            p_ref[r0:r0 + row_block, c:c + LANES] = p.astype(BF)
        sums.append(jnp.sum(part, axis=-1, keepdims=True))
    den = jnp.concatenate(sums, axis=0)
    return den if sink is None else den + jnp.exp2(sink - mx)


def _mla_kernel(*refs, n_seg):
    q_ref = refs[0]
    k_refs = refs[1:1 + n_seg]
    v_refs = refs[1 + n_seg:1 + 2 * n_seg]
    o_ref = refs[-5]
    s_refs, p_refs = refs[-4:-2], refs[-2:]
    tq = q_ref.shape[0]
    lane = lax.broadcasted_iota(jnp.int32, (tq, LANES), 1)

    def scores(h):
        off = 0
        for k in k_refs:
            s_refs[h % 2][:, off:off + k.shape[0]] = _nt(q_ref[:, h * LANES:(h + 1) * LANES],
                                                         k[:, h * LANES:(h + 1) * LANES])
            off += k.shape[0]

    scores(0)
    outs = []
    for h in range(MLA_HEADS):
        if h + 1 < MLA_HEADS:
            scores(h + 1)
        den = _softmax_numerator(s_refs[h % 2], p_refs[h % 2])
        pair = h // 2
        acc, off = None, 0
        for v in v_refs:
            pv = jnp.dot(p_refs[h % 2][:, off:off + v.shape[0]], v[:, pair * LANES:(pair + 1) * LANES],
                         preferred_element_type=F32)
            acc = pv if acc is None else acc + pv
            off += v.shape[0]
        outs.append(acc / den)
        if h % 2 == 1:
            o_ref[:, pair * LANES:(pair + 1) * LANES] = jnp.where(lane < MLA_V, outs[-2], outs[-1]).astype(BF)


def _mla_attend(q, ks, vs, n_batch, t, row0, k_blocks, prev):
    n_seg = len(ks)
    tq = min(TQ_MLA, t)
    n_keys = sum(s for _, s in k_blocks)
    qb0 = row0 // tq
    tiles = t // tq
    q_map = lambda b, i: (qb0 + b * tiles + i, 0)
    in_specs = [pl.BlockSpec((tq, MLA_HEADS * LANES), q_map)]
    for (b0, s), width in ([(kb, MLA_HEADS * LANES) for kb in k_blocks]
                           + [(kb, MLA_HEADS * MLA_V) for kb in k_blocks]):
        in_specs.append(pl.BlockSpec((s, width), functools.partial(lambda b, i, b0: (b0 + b, 0), b0=b0)))
    args = [q, *ks, *vs]
    aliases = {}
    if prev is not None:
        args.append(prev)
        in_specs.append(_any_spec())
        aliases = {len(args) - 1: 0}
    return pl.pallas_call(
        functools.partial(_mla_kernel, n_seg=n_seg),
        grid=(n_batch, tiles),
        in_specs=in_specs,
        out_specs=pl.BlockSpec((tq, MLA_HEADS * MLA_V), q_map),
        out_shape=jax.ShapeDtypeStruct((N_TOK, MLA_HEADS * MLA_V), BF),
        input_output_aliases=aliases,
        scratch_shapes=[pltpu.VMEM((tq, n_keys), F32)] * 2 + [pltpu.VMEM((tq, n_keys), BF)] * 2,
        compiler_params=_params("parallel", "parallel"),
        name="mla_attend",
    )(*args)


def _gqa_heads(q_ref, k_segs, v_segs, masks, sink_ref, o_ref, s_refs, p_refs):
    tq = q_ref.shape[0]
    lane_q = lax.broadcasted_iota(jnp.int32, (tq, LANES), 1)
    lo_q = lane_q < SWA_HD
    k_roll = [pltpu.roll(k, SWA_HD, 1) for k in k_segs]
    v_roll = [pltpu.roll(v, SWA_HD, 1) for v in v_segs]
    heads = range(SWA_KV_HEADS)
    sinks = []
    for kvh in heads:
        tiles = [q_ref[:, (kvh * 2 + j) * LANES:(kvh * 2 + j + 1) * LANES] for j in range(2)]
        zero = jnp.zeros_like(tiles[0])
        qs = jnp.concatenate([jnp.where(lo_q, tiles[0], zero), jnp.where(lo_q, zero, tiles[0]),
                              jnp.where(lo_q, tiles[1], zero), jnp.where(lo_q, zero, tiles[1])], axis=0)
        sinks.append(jnp.concatenate(
            [sink_ref[0, kvh * SWA_GROUP + g:kvh * SWA_GROUP + g + 1, :]
             for g in range(SWA_GROUP) for _ in range(tq // LANES)], axis=1) * LOG2_E)
        off = 0
        for k, kr, msk in zip(k_segs, k_roll, masks):
            lane_k = lax.broadcasted_iota(jnp.int32, k.shape, 1)
            first = (lane_k < SWA_HD) == (kvh == 0)
            kd = jnp.where(first, k, kr)
            s = _bdot_nt(kd, qs)
            if msk is not None:
                s = jnp.where(msk, s, -1e30)
            s_refs[kvh][off:off + k.shape[0], :] = s
            off += k.shape[0]
    dens = [_softmax_numerator_t(s_refs[kvh], p_refs[kvh], sinks[kvh]) for kvh in heads]
    for kvh in heads:
        acc, off = None, 0
        for v, vr in zip(v_segs, v_roll):
            lane_v = lax.broadcasted_iota(jnp.int32, v.shape, 1)
            first = (lane_v < SWA_HD) == (kvh == 0)
            vd = jnp.where(first, v, vr)
            pv = lax.dot_general(vd.astype(BF), p_refs[kvh][off:off + v.shape[0], :],
                                 (((0,), (0,)), ((), ())), preferred_element_type=F32)
            acc = pv if acc is None else acc + pv
            off += v.shape[0]
        o = (acc / dens[kvh]).T
        for j in range(2):
            o_ref[:, (kvh * 2 + j) * LANES:(kvh * 2 + j + 1) * LANES] = jnp.where(
                lo_q, o[(2 * j) * tq:(2 * j + 1) * tq], o[(2 * j + 1) * tq:(2 * j + 2) * tq]).astype(BF)


def _softmax_numerator_t(s_ref, p_ref, sink, row_block=128):
    keys, cols = s_ref.shape
    dens = []
    for c in range(0, cols, LANES):
        mx = s_ref[0:row_block, c:c + LANES]
        for r0 in range(row_block, keys, row_block):
            mx = jnp.maximum(mx, s_ref[r0:r0 + row_block, c:c + LANES])
        snk = sink[:, c:c + LANES]
        m = jnp.maximum(jnp.max(mx, axis=0, keepdims=True), snk)
        mb = jnp.broadcast_to(m, (row_block, LANES))
        part = jnp.zeros((row_block, LANES), F32)
        for r0 in range(0, keys, row_block):
            p = jnp.exp2(s_ref[r0:r0 + row_block, c:c + LANES] - mb)
            part = part + p
            p_ref[r0:r0 + row_block, c:c + LANES] = p.astype(BF)
        dens.append(jnp.sum(part, axis=0, keepdims=True) + jnp.exp2(snk - m))
    return jnp.concatenate(dens, axis=1)


def _swa_scratch(keys, rows):
    return ([pltpu.VMEM((keys, rows), F32)] * SWA_KV_HEADS + [pltpu.VMEM((keys, rows), BF)] * SWA_KV_HEADS)


def _swa_ctx_kernel(q_ref, k_ref, v_ref, sink_ref, o_ref, *scratch):
    _gqa_heads(q_ref, [k_ref[...]], [v_ref[...]], [None], sink_ref, o_ref,
               scratch[:SWA_KV_HEADS], scratch[SWA_KV_HEADS:])


def _swa_ctx(sq, sk, sv, sink_rows, l):
    return pl.pallas_call(
        _swa_ctx_kernel,
        grid=(BATCH,),
        in_specs=[pl.BlockSpec((SEQ, SWA_HEADS * SWA_HD), lambda b: (b, 0)),
                  pl.BlockSpec((SEQ, LANES), lambda b: (b, 0)),
                  pl.BlockSpec((SEQ, LANES), lambda b: (b, 0)),
                  pl.BlockSpec((1, SWA_HEADS, LANES), lambda b: (l, 0, 0))],
        out_specs=pl.BlockSpec((SEQ, SWA_HEADS * SWA_HD), lambda b: (b, 0)),
        out_shape=jax.ShapeDtypeStruct((N_TOK, SWA_HEADS * SWA_HD), BF),
        scratch_shapes=_swa_scratch(SEQ, SWA_GROUP * SEQ),
        compiler_params=_params("parallel"),
        name="swa_ctx",
    )(sq, sk, sv, sink_rows)


def _swa_lat_kernel(q_ref, kp_ref, kc_ref, kn_ref, vp_ref, vc_ref, vn_ref, kx_ref, vx_ref, sink_ref, prev_ref,
                    o_ref, *scratch):
    w = SWA_WINDOW
    n = pl.program_id(1)
    nb = pl.num_programs(1)
    k_band = jnp.concatenate([kp_ref[...], kc_ref[...], kn_ref[...]], axis=0)
    v_band = jnp.concatenate([vp_ref[...], vc_ref[...], vn_ref[...]], axis=0)
    rows = SWA_GROUP * w
    r = lax.broadcasted_iota(jnp.int32, (3 * w, rows), 1) & (w - 1)
    c = lax.broadcasted_iota(jnp.int32, (3 * w, rows), 0)
    valid = (c >= r) & (c <= r + 2 * w) & ((c >= w) | (n > 0)) & ((c < 2 * w) | (n < nb - 1))
    _gqa_heads(q_ref, [k_band, kx_ref[0, 0]], [v_band, vx_ref[0, 0]], [valid, None], sink_ref, o_ref,
               scratch[:SWA_KV_HEADS], scratch[SWA_KV_HEADS:])


def _swa_lat(sq, sk, sv, k_cache, v_cache, sink_rows, l, prev):
    w = SWA_WINDOW
    nb = DEC_SEQ // w
    q0 = N_CTX // w

    def blk(d):
        return lambda b, n: (q0 + b * nb + jnp.clip(n + d, 0, nb - 1), 0)

    kv_specs = [pl.BlockSpec((w, LANES), blk(d)) for d in (-1, 0, 1)]
    cache_spec = pl.BlockSpec((1, 1, PAST_LEN, LANES), lambda b, n: (b, l, 0, 0))
    return pl.pallas_call(
        _swa_lat_kernel,
        grid=(DEC_BATCH, nb),
        in_specs=[pl.BlockSpec((w, SWA_HEADS * SWA_HD), blk(0))] + kv_specs + kv_specs
        + [cache_spec, cache_spec, pl.BlockSpec((1, SWA_HEADS, LANES), lambda b, n: (l, 0, 0)), _any_spec()],
        out_specs=pl.BlockSpec((w, SWA_HEADS * SWA_HD), blk(0)),
        out_shape=jax.ShapeDtypeStruct((N_TOK, SWA_HEADS * SWA_HD), BF),
        input_output_aliases={10: 0},
        scratch_shapes=_swa_scratch(3 * w + PAST_LEN, SWA_GROUP * w),
        compiler_params=_params("parallel", "parallel"),
        name="swa_lat",
    )(sq, sk, sk, sk, sv, sv, sv, k_cache, v_cache, sink_rows, prev)


def _gdn_conv_kernel(*refs):
    x_ref, w_ref, o_ref = refs[0], refs[1], refs[-1]
    x = x_ref[...]
    t = x.shape[0]
    w = w_ref[0]
    r8 = lax.broadcasted_iota(jnp.int32, (SUBLANES, x.shape[1]), 0)
    half = GDN_CONV // 2
    acc = x * w[half:half + 1, :]
    for k in range(GDN_CONV):
        d = k - half
        if d == 0:
            continue
        xs = pltpu.roll(x, (-d) % t, 0)
        top, bot = xs[:SUBLANES], xs[t - SUBLANES:]
        if d < 0:
            top = jnp.where(r8 + d >= 0, top, 0.0)
        else:
            bot = jnp.where(r8 + d < SUBLANES, bot, 0.0)
        xs = jnp.concatenate([top, xs[SUBLANES:t - SUBLANES], bot], axis=0)
        acc = acc + xs * w[k:k + 1, :]
    y = _silu(acc)
    is_qk = pl.program_id(1) < 2
    cols = []
    for h in range(y.shape[1] // LANES):
        yh = y[:, h * LANES:(h + 1) * LANES]
        nrm = lax.rsqrt(jnp.sum(yh * yh, axis=-1, keepdims=True) + 1e-6)
        cols.append(yh * jnp.where(is_qk, nrm, 1.0))
    o_ref[...] = jnp.concatenate(cols, axis=1)


def _gdn_conv(g3, conv_w, l, row0, n_seq, t, prev):
    blk0 = row0 // t
    args = [g3, conv_w]
    in_specs = [pl.BlockSpec((t, GDN_QK_DIM), lambda s, j: (blk0 + s, j)),
                pl.BlockSpec((1, GDN_CONV, GDN_QK_DIM), lambda s, j: (l, 0, j))]
    aliases = {}
    if prev is not None:
        args.append(prev)
        in_specs.append(_any_spec())
        aliases = {2: 0}
    return pl.pallas_call(
        _gdn_conv_kernel,
        grid=(n_seq, 3),
        in_specs=in_specs,
        out_specs=pl.BlockSpec((t, GDN_QK_DIM), lambda s, j: (blk0 + s, j)),
        out_shape=jax.ShapeDtypeStruct((N_TOK, GDN_CONV_CH), F32),
        input_output_aliases=aliases,
        compiler_params=_params("parallel", "parallel"),
        name="gdn_conv",
    )(*args)


def _gate_rows(misc):
    shape = (2 * GDN_NH, LANES)
    sel = (lax.broadcasted_iota(jnp.int32, shape, 1)
           == lax.broadcasted_iota(jnp.int32, shape, 0) + _M_G).astype(BF)
    b1, b2, b3 = _split3(misc)
    return _nt(sel, b1) + (_nt(sel, b2) + _nt(sel, b3))


def _gdn_chunk_pairs(pairs):
    c = GDN_CHUNK
    shape = (c, 2 * c)
    ri = lax.broadcasted_iota(jnp.int32, shape, 0)
    lane = lax.broadcasted_iota(jnp.int32, shape, 1)
    cj = lane & (c - 1)
    fwd_half = lane < c
    lower, upper = ri >= cj, ri <= cj
    incl = (fwd_half & lower) | (~fwd_half & upper)
    incl_t = (fwd_half & upper) | (~fwd_half & lower)
    strict = incl & (ri != cj)
    incl_f, incl_b = incl & fwd_half, incl & ~fwd_half
    eye = (ri == cj).astype(F32)
    n = len(pairs)
    rng = range(n)
    fw = [p[0] for p in pairs]
    bw = [p[1] for p in pairs]

    def halves(a, b):
        return jnp.where(fwd_half, a, b)

    def split_rows(m):
        return jnp.concatenate([jnp.where(fwd_half, m, 0.0), jnp.where(fwd_half, 0.0, m)], axis=0)

    def pair_dot3(x, p):
        xh = x.astype(BF)
        xl = (x - xh.astype(F32)).astype(BF)
        phf = p.astype(BF).astype(F32)
        bd_hi = split_rows(phf).astype(BF)
        bd_lo = split_rows(p - phf).astype(BF)
        return jnp.dot(jnp.concatenate([xh, xl, xh], axis=1), jnp.concatenate([bd_hi, bd_hi, bd_lo], axis=0),
                       preferred_element_type=F32)

    g_col = [halves(fw[i]["g_col"], bw[i]["g_col"]) for i in rng]
    gc_row = [jnp.sum(jnp.where(incl_t, g_col[i], 0.0), axis=0, keepdims=True) for i in rng]
    gcf = [jnp.sum(jnp.where(incl_f, pairs[i][2], 0.0), axis=1, keepdims=True) for i in rng]
    gcb = [jnp.sum(jnp.where(incl_b, pairs[i][2], 0.0), axis=1, keepdims=True) for i in rng]
    decay = [jnp.where(incl, jnp.exp(halves(gcf[i], gcb[i]) - gc_row[i]), 0.0) for i in rng]
    qf = [p["q"] * (GDN_DK ** -0.5) for p in fw]
    qb = [p["q"] * (GDN_DK ** -0.5) for p in bw]
    kbf = [p["k"] * p["beta"] for p in fw]
    kbb = [p["k"] * p["beta"] for p in bw]
    z = jnp.zeros((c, GDN_DK), F32)
    kq = [_bdot_nt(jnp.concatenate([jnp.concatenate([kbf[i], kbb[i]], axis=1),
                                    jnp.concatenate([qf[i], qb[i]], axis=1)], axis=0),
                   jnp.concatenate([jnp.concatenate([fw[i]["k"], z], axis=1),
                                    jnp.concatenate([z, bw[i]["k"]], axis=1)], axis=0)) for i in rng]
    pw = [jnp.where(strict, -(kq[i][:c] * decay[i]), 0.0) for i in rng]
    inv = [eye + m for m in pw]
    levels = int(np.log2(c)) - 1
    pw = [pair_dot3(m, m) for m in pw]
    for level in range(levels):
        if level + 1 < levels:
            both = [pair_dot3(jnp.concatenate([inv[i], pw[i]], axis=0), pw[i]) for i in rng]
            inv = [inv[i] + both[i][:c] for i in rng]
            pw = [both[i][c:] for i in rng]
        else:
            inv = [inv[i] + pair_dot3(inv[i], pw[i]) for i in rng]
    ef = [jnp.exp(g) for g in gcf]
    eb = [jnp.exp(g) for g in gcb]
    uw = [_bdot(split_rows(inv[i]),
                jnp.concatenate([jnp.concatenate([fw[i]["v"] * fw[i]["beta"], kbf[i] * ef[i]], axis=1),
                                 jnp.concatenate([bw[i]["v"] * bw[i]["beta"], kbb[i] * eb[i]], axis=1)], axis=0))
          for i in rng]
    a = [jnp.where(incl, kq[i][c:] * decay[i], 0.0) for i in rng]
    glf = [g[c - 1:c, :] for g in gcf]
    glb = [g[0:1, :] for g in gcb]
    wqf = [_bdot(jnp.concatenate([uw[i][:c, GDN_DV:], qf[i] * ef[i]], axis=0), fw[i]["state"]) for i in rng]
    wqb = [_bdot(jnp.concatenate([uw[i][c:, GDN_DV:], qb[i] * eb[i]], axis=0), bw[i]["state"]) for i in rng]
    vnf = [uw[i][:c, :GDN_DV] - wqf[i][:c] for i in rng]
    vnb = [uw[i][c:, :GDN_DV] - wqb[i][:c] for i in rng]
    av = [_bdot(split_rows(a[i]), jnp.concatenate([vnf[i], vnb[i]], axis=0)) for i in rng]
    sf = [fw[i]["state"] * jnp.exp(glf[i]) + _bdot_tn(fw[i]["k"] * jnp.exp(glf[i] - gcf[i]), vnf[i]) for i in rng]
    sb = [bw[i]["state"] * jnp.exp(glb[i]) + _bdot_tn(bw[i]["k"] * jnp.exp(glb[i] - gcb[i]), vnb[i]) for i in rng]
    return [(wqf[i][c:] + av[i][:c], sf[i], wqb[i][c:] + av[i][c:], sb[i]) for i in rng]


def _gdn_kernel(*refs, context, n_par):
    qf_ref, qb_ref, mf_ref, mb_ref = refs[:4]
    st_ref = refs[-1]
    if context:
        of_ref, ob_ref, sfin_ref = refs[-4:-1]
    else:
        s0_ref = refs[4]
        of_ref, ob_ref = refs[-3:-1]
    n = pl.program_id(1)

    @pl.when(n == 0)
    def _():
        if context:
            st_ref[...] = jnp.zeros_like(st_ref)
        else:
            st_ref[...] = s0_ref[:, 0]

    def problem(qkv, misc, s, h, d):
        i = d * GDN_HEADS + h
        return dict(q=qkv[:, h * GDN_DK:(h + 1) * GDN_DK],
                    k=qkv[:, GDN_QK_DIM + h * GDN_DK:GDN_QK_DIM + (h + 1) * GDN_DK],
                    v=qkv[:, 2 * GDN_QK_DIM + h * GDN_DV:2 * GDN_QK_DIM + (h + 1) * GDN_DV],
                    g_col=misc[:, _M_G + i:_M_G + i + 1], beta=misc[:, _M_B + i:_M_B + i + 1],
                    state=st_ref[s, i])

    lane = lax.broadcasted_iota(jnp.int32, (1, 2 * GDN_CHUNK), 1)
    pairs = []
    for s in range(n_par):
        qkv_f, qkv_b, misc_f, misc_b = qf_ref[s], qb_ref[s], mf_ref[s], mb_ref[s]
        rows = _gate_rows(jnp.concatenate([misc_f, misc_b], axis=0))
        for h in range(GDN_HEADS):
            g_row = jnp.where(lane < GDN_CHUNK, rows[h:h + 1, :], rows[GDN_HEADS + h:GDN_HEADS + h + 1, :])
            pairs.append((problem(qkv_f, misc_f, s, h, 0), problem(qkv_b, misc_b, s, h, 1), g_row))
    results = _gdn_chunk_pairs(pairs)
    for s in range(n_par):
        for h in range(GDN_HEADS):
            o_f, s_f, o_b, s_b = results[s * GDN_HEADS + h]
            of_ref[s, :, h * GDN_DV:(h + 1) * GDN_DV] = o_f
            ob_ref[s, :, h * GDN_DV:(h + 1) * GDN_DV] = o_b
            st_ref[s, h] = s_f
            st_ref[s, GDN_HEADS + h] = s_b

    if context:
        @pl.when(n == pl.num_programs(1) - 1)
        def _():
            sfin_ref[:, 0] = st_ref[...]


def _gdn(qkv, misc, l, t, n_seq, seq0, s0, prevs, n_par=GDN_PAR):
    c = GDN_CHUNK
    nc = t // c
    context = s0 is None
    sb0 = seq0 // n_par
    n_all = N_TOK // t
    fwd = lambda s, n: (sb0 + s, n, 0)
    bwd = lambda s, n: (sb0 + s, nc - 1 - n, 0)
    st_spec = pl.BlockSpec((n_par, 1, GDN_NH, GDN_DK, GDN_DV), lambda s, n: (s, l, 0, 0, 0))
    qkv3 = qkv.reshape(n_all, t, GDN_CONV_CH)
    misc3 = misc.reshape(n_all, t, LANES)
    in_specs = [pl.BlockSpec((n_par, c, GDN_CONV_CH), fwd), pl.BlockSpec((n_par, c, GDN_CONV_CH), bwd),
                pl.BlockSpec((n_par, c, LANES), fwd), pl.BlockSpec((n_par, c, LANES), bwd)]
    args = [qkv3, qkv3, misc3, misc3]
    if not context:
        in_specs.append(st_spec)
        args.append(s0)
    out_specs = [pl.BlockSpec((n_par, c, GDN_V_DIM), fwd), pl.BlockSpec((n_par, c, GDN_V_DIM), bwd)]
    out_shape = [jax.ShapeDtypeStruct((n_all, t, GDN_V_DIM), F32)] * 2
    if context:
        out_specs.append(st_spec)
        out_shape.append(jax.ShapeDtypeStruct((n_seq, DEPTH, GDN_NH, GDN_DK, GDN_DV), F32))
    aliases = {}
    for j, p in enumerate(prevs):
        if p is not None:
            aliases[len(args)] = j
            args.append(p.reshape(out_shape[j].shape))
            in_specs.append(_any_spec())
    return pl.pallas_call(
        functools.partial(_gdn_kernel, context=context, n_par=n_par),
        grid=(n_seq // n_par, nc),
        in_specs=in_specs, out_specs=out_specs, out_shape=out_shape,
        input_output_aliases=aliases,
        scratch_shapes=[pltpu.VMEM((n_par, GDN_NH, GDN_DK, GDN_DV), F32)],
        compiler_params=_params("parallel", "arbitrary"),
        name="gdn",
    )(*args)


def _out_kernel(x_ref, mod_ref, om_ref, os_ref, gf_ref, gb_ref, gz_ref, gn_ref, w_ref, o_ref):
    gate = mod_ref[0, 0][:, 2 * D_MODEL:]
    s = gf_ref[...] + gb_ref[...]
    gz = gz_ref[...]
    cols = []
    for h in range(GDN_HEADS):
        sh = s[:, h * GDN_DV:(h + 1) * GDN_DV]
        cols.append(_rms(sh, gn_ref[0]) * _silu(gz[:, h * GDN_DV:(h + 1) * GDN_DV]))
    og = jnp.concatenate(cols, axis=1)
    n_m = MLA_HEADS * MLA_V
    n_s = SWA_HEADS * SWA_HD
    y = (jnp.dot(om_ref[...], w_ref[0, :n_m, :], preferred_element_type=F32)
         + jnp.dot(os_ref[...], w_ref[0, n_m:n_m + n_s, :], preferred_element_type=F32)
         + jnp.dot(og.astype(BF), w_ref[0, n_m + n_s:, :], preferred_element_type=F32))
    o_ref[...] = x_ref[...] + gate * y


def _out_proj(x, mods, l, o_mla, o_swa, o_gf, o_gb, gz, gdn_norm, w_out, tm=TM_OUT):
    row = lambda i: (i, 0)
    lay3 = lambda i: (l, 0, 0)
    return pl.pallas_call(
        _out_kernel,
        grid=(N_TOK // tm,),
        in_specs=[pl.BlockSpec((tm, D_MODEL), row),
                  _mod_spec(l, 1, tm),
                  pl.BlockSpec((tm, MLA_HEADS * MLA_V), row),
                  pl.BlockSpec((tm, SWA_HEADS * SWA_HD), row),
                  pl.BlockSpec((tm, GDN_V_DIM), row),
                  pl.BlockSpec((tm, GDN_V_DIM), row),
                  pl.BlockSpec((tm, GDN_V_DIM), row),
                  pl.BlockSpec((1, 1, GDN_DV), lay3),
                  pl.BlockSpec((1,) + w_out.shape[1:], lay3)],
        out_specs=pl.BlockSpec((tm, D_MODEL), row),
        out_shape=jax.ShapeDtypeStruct((N_TOK, D_MODEL), F32),
        compiler_params=_params("parallel"),
        name="out_proj",
    )(x, mods, o_mla, o_swa, o_gf, o_gb, gz, gdn_norm.reshape(DEPTH, 1, GDN_DV), w_out)


def _rot_columns(w, dim):
    shp = w.shape
    w6 = w.reshape(shp[:-1] + (shp[-1] // dim, 2, 2, dim // 4))
    sign = jnp.asarray([-1.0, 1.0], F32).reshape(2, 1)
    return (jnp.flip(w6, axis=-2) * sign).reshape(shp)


def _axial_rope(n_tokens, dim):
    f32 = np.float32
    rows = n_tokens // GRID_W
    row = np.repeat(np.arange(rows, dtype=f32), GRID_W)
    col = np.tile(np.arange(GRID_W, dtype=f32), rows)
    axis_dim = dim // 2
    inv_freq = (f32(1.0) / (f32(ROPE_BASE) ** (np.arange(0, axis_dim, 2, dtype=f32) / f32(axis_dim)))).astype(f32)
    ang_r = row[:, None] * inv_freq[None, :]
    ang_c = col[:, None] * inv_freq[None, :]
    ang = np.concatenate([ang_r, ang_r, ang_c, ang_c], axis=-1).astype(f32)
    return np.cos(ang).astype(f32), np.sin(ang).astype(f32)


def _rope_table(tm):
    cos_m, sin_m = _axial_rope(DEC_SEQ, MLA_ROPE)
    cos_s, sin_s = _axial_rope(DEC_SEQ, SWA_HD)
    t = DEC_SEQ
    one = lambda w: np.ones((t, w), np.float32)
    zero = lambda w: np.zeros((t, w), np.float32)
    lat = np.concatenate([
        cos_s, cos_s, sin_s, sin_s,
        cos_m, one(LANES - MLA_ROPE), sin_m, zero(LANES - MLA_ROPE),
        one(MLA_NOPE), cos_m, one(LANES - MLA_NOPE - MLA_ROPE),
        zero(MLA_NOPE), sin_m, zero(LANES - MLA_NOPE - MLA_ROPE)], axis=1)
    ident_row = np.concatenate([np.ones(LANES), np.zeros(LANES)] * 3).astype(np.float32)
    ident = np.broadcast_to(ident_row[None, :], (tm, _T_END))
    return jnp.asarray(np.concatenate([ident, lat], axis=0))


def _mixer_weights(w_in, mla_w_qb, mla_w_kvb):
    nl = DEPTH
    offs = np.cumsum([0, MLA_Q_LORA, MLA_KV_LORA, MLA_ROPE, SWA_HEADS * SWA_HD, SWA_KV_HEADS * SWA_HD,
                      SWA_KV_HEADS * SWA_HD, GDN_CONV_CH, GDN_V_DIM, 2 * GDN_NH])
    w_t = jnp.swapaxes(w_in, 1, 2)
    cq, ckv, krope, sq, sk, sv, g3, gz, gates = [w_t[:, offs[i]:offs[i + 1], :] for i in range(9)]

    def rot(w, dim):
        w6 = w.reshape(nl, w.shape[1] // dim, 2, 2, dim // 4, D_MODEL)
        return (jnp.flip(w6, axis=3) * jnp.asarray([-1.0, 1.0], F32).reshape(2, 1, 1)).reshape(w.shape)

    zeros = lambda n: jnp.zeros((nl, n, D_MODEL), F32)
    misc_a = jnp.concatenate([krope, gates, zeros(LANES - _M_END)], axis=1)
    misc_b = jnp.concatenate([rot(krope, MLA_ROPE), zeros(LANES - MLA_ROPE)], axis=1)
    win = jnp.concatenate([cq, ckv, sq, rot(sq, SWA_HD), sk, rot(sk, SWA_HD), sv, g3, gz, misc_a, misc_b],
                          axis=1).astype(BF)

    r = MLA_Q_LORA
    wq = mla_w_qb.reshape(nl, r, MLA_HEADS, MLA_NOPE + MLA_ROPE)
    nope, rope = wq[..., :MLA_NOPE], wq[..., MLA_NOPE:]
    pad = LANES - MLA_NOPE - MLA_ROPE
    z = lambda n: jnp.zeros((nl, r, MLA_HEADS, n), F32)
    qa = jnp.concatenate([nope, rope, z(pad)], axis=-1).reshape(nl, r, MLA_HEADS * LANES)
    qb = jnp.concatenate([z(MLA_NOPE), _rot_columns(rope, MLA_ROPE), z(pad)],
                         axis=-1).reshape(nl, r, MLA_HEADS * LANES)
    wqb = jnp.concatenate([qa, qb], axis=2).astype(BF)

    kvb = mla_w_kvb.reshape(nl, MLA_KV_LORA, MLA_HEADS, MLA_NOPE + MLA_V)
    k_nope = jnp.concatenate([kvb[..., :MLA_NOPE],
                              jnp.zeros((nl, MLA_KV_LORA, MLA_HEADS, LANES - MLA_NOPE), F32)],
                             axis=-1).reshape(nl, MLA_KV_LORA, MLA_HEADS * LANES)
    place = np.zeros((LANES, MLA_HEADS, LANES), np.float32)
    for i in range(MLA_ROPE):
        place[i, :, MLA_NOPE + i] = 1.0
    place = jnp.broadcast_to(jnp.asarray(place.reshape(1, LANES, MLA_HEADS * LANES)),
                             (nl, LANES, MLA_HEADS * LANES))
    wk = jnp.concatenate([k_nope, place], axis=1).astype(BF)
    wv = kvb[..., MLA_NOPE:].reshape(nl, MLA_KV_LORA, MLA_HEADS * MLA_V).astype(BF)
    return win, wqb, wk, wv


def _misc_rows(vals):
    rows = jnp.zeros((DEPTH, 1, LANES), F32)
    return rows.at[:, 0, _M_G:_M_B].set(vals.reshape(DEPTH, GDN_NH).astype(F32))


def kernel(x_prompt, x_sample, cache_mla_ckv, cache_mla_krope, cache_swa_k, cache_swa_v, state_gdn, c, c_ctx,
           w_ada, b_ada, norm_ffn1, ffn1_w1, ffn1_w2, norm_mix, w_in, mla_q_norm, mla_w_qb, mla_kv_norm,
           mla_w_kvb, swa_sink, gdn_conv_w, gdn_a_log, gdn_dt_bias, gdn_norm, w_out, norm_ffn2, ffn2_w1,
           ffn2_w2, final_norm):
    cond = jnp.concatenate([c_ctx[None, :], c, jnp.zeros((COND_ROWS - N_GROUPS, D_MODEL), F32)], axis=0)
    mods = _adaln(cond, w_ada, b_ada)[:, :N_GROUPS].reshape(DEPTH, N_GROUPS, 1, N_MOD * D_MODEL)
    tab = _rope_table(TM_PROJ)
    win, wqb, wk, wv = _mixer_weights(w_in, mla_w_qb, mla_w_kvb)
    w11, w12, w21, w22, wo = (w.astype(BF) for w in (ffn1_w1, ffn1_w2, ffn2_w1, ffn2_w2, w_out))
    alog, dtb = _misc_rows(gdn_a_log), _misc_rows(gdn_dt_bias)
    sink_rows = jnp.broadcast_to(swa_sink[:, :, None], (DEPTH, SWA_HEADS, LANES))
    cache_misc = jnp.pad(cache_mla_krope, ((0, 0), (0, 0), (0, 0), (0, LANES - MLA_ROPE)))
    cache_k = cache_swa_k.reshape(DEC_BATCH, DEPTH, PAST_LEN, LANES)
    cache_v = cache_swa_v.reshape(DEC_BATCH, DEPTH, PAST_LEN, LANES)
    s0 = state_gdn.reshape(DEC_BATCH, DEPTH, GDN_NH, GDN_DK, GDN_DV)

    xs = [x_prompt.reshape(N_CTX, D_MODEL), x_sample.reshape(N_LAT, D_MODEL)]
    caches = new_st = None
    for l in range(DEPTH):
        x = _ffn(xs, mods, l, 0, norm_ffn1, w11, w12)
        (q_mla, misc, k_mla, v_mla, sq, sk, sv, g3, gz, *caches) = _proj(
            x, mods, l, norm_mix, win, mla_q_norm, wqb, mla_kv_norm, wk, wv, tab, alog, dtb, caches)

        k_c, v_c = _kv_cache(cache_mla_ckv, cache_misc, l, wk, wv)
        o_mla = _mla_attend(q_mla, [k_mla], [v_mla], BATCH, SEQ, 0, [(0, SEQ)], None)
        o_mla = _mla_attend(q_mla, [k_mla, k_c], [v_mla, v_c], DEC_BATCH, DEC_SEQ, N_CTX,
                            [(N_CTX // DEC_SEQ, DEC_SEQ), (0, PAST_LEN)], o_mla)

        o_swa = _swa_ctx(sq, sk, sv, sink_rows, l)
        o_swa = _swa_lat(sq, sk, sv, cache_k, cache_v, sink_rows, l, o_swa)

        qkv = _gdn_conv(g3, gdn_conv_w, l, 0, BATCH, SEQ, None)
        qkv = _gdn_conv(g3, gdn_conv_w, l, N_CTX, DEC_BATCH, DEC_SEQ, qkv)
        o_gf, o_gb, new_st = _gdn(qkv, misc, l, SEQ, BATCH, 0, None, [None, None, new_st], n_par=GDN_PAR_CTX)
        o_gf, o_gb = _gdn(qkv, misc, l, DEC_SEQ, DEC_BATCH, N_CTX // DEC_SEQ, s0, [o_gf, o_gb])

        x = _out_proj(x, mods, l, o_mla, o_swa, o_gf.reshape(N_TOK, GDN_V_DIM), o_gb.reshape(N_TOK, GDN_V_DIM),
                      gz, gdn_norm, wo)
        if l + 1 < DEPTH:
            xs = [_ffn([x], mods, l, 2, norm_ffn2, w21, w22)]
        else:
            y_prompt, y_sample = _ffn([x], mods, l, 2, norm_ffn2, w21, w22, final_gain=final_norm)

    new_ckv, new_krope, new_sk, new_sv = caches
    kv_shape = (BATCH, DEPTH, SEQ, SWA_KV_HEADS, SWA_HD)
    return (y_prompt.reshape(BATCH, SEQ, D_MODEL), y_sample.reshape(DEC_BATCH, DEC_SEQ, D_MODEL), new_ckv,
            new_krope, new_sk.reshape(kv_shape), new_sv.reshape(kv_shape),
            new_st.reshape(BATCH, DEPTH, 2, GDN_HEADS, GDN_DK, GDN_DV))
```

```python
import functools

import numpy as np
import jax
import jax.numpy as jnp
from jax import lax
from jax.experimental import pallas as pl
from jax.experimental.pallas import tpu as pltpu

D_MODEL = 1024
BATCH = 16
SEQ = 256
DEPTH = 2
DEC_BATCH = 2
DEC_SEQ = 2048
PAST_LEN = 512
GRID_W = 64
ROPE_BASE = 10000.0
NORM_EPS = 1e-6
N_MOD = 9
D_FF = 2816
MLA_HEADS = 8
MLA_Q_LORA = 384
MLA_KV_LORA = 256
MLA_NOPE = 64
MLA_ROPE = 32
MLA_V = 64
SWA_HEADS = 8
SWA_KV_HEADS = 2
SWA_GROUP = SWA_HEADS // SWA_KV_HEADS
SWA_HD = 64
SWA_WINDOW = 128
GDN_HEADS = 4
GDN_DK = 128
GDN_DV = 128
GDN_CONV = 5
GDN_CHUNK = 64
GDN_QK_DIM = GDN_HEADS * GDN_DK
GDN_V_DIM = GDN_HEADS * GDN_DV
GDN_CONV_CH = 2 * GDN_QK_DIM + GDN_V_DIM
GDN_NH = 2 * GDN_HEADS

N_CTX = BATCH * SEQ
N_LAT = DEC_BATCH * DEC_SEQ
N_TOK = N_CTX + N_LAT
N_GROUPS = 1 + DEC_BATCH
COND_ROWS = 8

LANES = 128
SUBLANES = 8
VMEM_LIMIT_BYTES = 56 * 1024 * 1024

TM_FFN = 512
FFN_LOAD_CHUNKS = 8
TM_PROJ = 2 * SEQ
TM_OUT = 512
TQ_MLA = 512
GDN_PAR = 2
GDN_PAR_CTX = 4

BF = jnp.bfloat16
F32 = jnp.float32
LOG2_E = 1.4426950408889634
MLA_Q_SCALE = (MLA_NOPE + MLA_ROPE) ** -0.5 * LOG2_E

_C_CQ = 0
_C_CKV = _C_CQ + MLA_Q_LORA
_C_SQ = _C_CKV + MLA_KV_LORA
_C_SQR = _C_SQ + SWA_HEADS * SWA_HD
_C_SK = _C_SQR + SWA_HEADS * SWA_HD
_C_SKR = _C_SK + SWA_KV_HEADS * SWA_HD
_C_SV = _C_SKR + SWA_KV_HEADS * SWA_HD
_C_G3 = _C_SV + SWA_KV_HEADS * SWA_HD
_C_GZ = _C_G3 + GDN_CONV_CH
_C_MA = _C_GZ + GDN_V_DIM
_C_MB = _C_MA + LANES
_C_END = _C_MB + LANES
_M_G = MLA_ROPE
_M_B = MLA_ROPE + GDN_NH
_M_END = MLA_ROPE + 2 * GDN_NH
_CACHE_WIDTHS = (MLA_KV_LORA, MLA_ROPE, SWA_KV_HEADS * SWA_HD, SWA_KV_HEADS * SWA_HD)
_T_CS, _T_SS, _T_CA, _T_SA, _T_CQ, _T_SQ = (i * LANES for i in range(6))
_T_END = 6 * LANES


def _params(*sem):
    return pltpu.CompilerParams(dimension_semantics=sem, vmem_limit_bytes=VMEM_LIMIT_BYTES)


def _bdot(a, b):
    return jnp.dot(a.astype(BF), b.astype(BF), preferred_element_type=F32)


def _nt(a, b):
    return lax.dot_general(a, b, (((1,), (1,)), ((), ())), preferred_element_type=F32)


def _bdot_nt(a, b):
    return _nt(a.astype(BF), b.astype(BF))


def _bdot_tn(a, b):
    return lax.dot_general(a.astype(BF), b.astype(BF), (((0,), (0,)), ((), ())),
                           preferred_element_type=F32)


def _split3(a):
    b1 = a.astype(BF)
    r = a - b1.astype(F32)
    b2 = r.astype(BF)
    b3 = (r - b2.astype(F32)).astype(BF)
    return b1, b2, b3


def _silu(x):
    return x / (1.0 + jnp.exp(-x))


def _rms(x, gain, eps=NORM_EPS):
    return x * lax.rsqrt(jnp.mean(x * x, axis=-1, keepdims=True) + eps) * gain


def _group_of_row(r):
    return jnp.where(r < N_CTX, 0, 1 + (r - N_CTX) // DEC_SEQ)


def _any_spec():
    return pl.BlockSpec(memory_space=pl.ANY)


def _adaln_kernel(c_ref, w_ref, b_ref, o_ref):
    o_ref[0] = _bdot(_silu(c_ref[...]), w_ref[0]) + b_ref[0]


def _adaln(cond, w_ada, b_ada, tn=1536):
    n = N_MOD * D_MODEL
    return pl.pallas_call(
        _adaln_kernel,
        grid=(DEPTH, n // tn),
        in_specs=[pl.BlockSpec((COND_ROWS, D_MODEL), lambda l, j: (0, 0)),
                  pl.BlockSpec((1, D_MODEL, tn), lambda l, j: (l, 0, j)),
                  pl.BlockSpec((1, 1, tn), lambda l, j: (l, 0, j))],
        out_specs=pl.BlockSpec((1, COND_ROWS, tn), lambda l, j: (l, 0, j)),
        out_shape=jax.ShapeDtypeStruct((DEPTH, COND_ROWS, n), F32),
        compiler_params=_params("parallel", "parallel"),
        name="adaln",
    )(cond, w_ada, b_ada.reshape(DEPTH, 1, n))


def _mod_spec(l, which, tm):
    return pl.BlockSpec((1, 1, 1, 3 * D_MODEL), lambda i: (l, _group_of_row(i * tm), 0, which))


def _load_cast_rows(w_hbm, layer, dst_ref, stage_ref, sem_ref):
    rows = stage_ref.shape[1]
    n_chunks = dst_ref.shape[0] // rows

    def chunk_copy(c):
        slot = c % 2
        return pltpu.make_async_copy(w_hbm.at[layer, pl.ds(c * rows, rows), :], stage_ref.at[slot],
                                     sem_ref.at[slot])

    chunk_copy(0).start()
    for c in range(n_chunks):
        if c + 1 < n_chunks:
            chunk_copy(c + 1).start()
        chunk_copy(c).wait()
        dst_ref[c * rows:(c + 1) * rows, :] = stage_ref[c % 2].astype(BF)


def _ffn_kernel(*refs, n_in, n_a_tiles, final, layer):
    x_refs = refs[:n_in]
    mod_ref, gain_ref, w1_hbm, w2_hbm = refs[n_in:n_in + 4]
    w1_ref, w2_ref, stage1_ref, stage2_ref, sem1_ref, sem2_ref = refs[-6:]
    rest = refs[n_in + 4:-6]
    i = pl.program_id(0)

    @pl.when(i == 0)
    def _():
        _load_cast_rows(w1_hbm, layer, w1_ref, stage1_ref, sem1_ref)
        _load_cast_rows(w2_hbm, layer, w2_ref, stage2_ref, sem2_ref)

    if n_in == 2:
        x = jnp.where(i < n_a_tiles, x_refs[0][...], x_refs[1][...])
    else:
        x = x_refs[0][...]
    mod = mod_ref[0, 0]
    shift, scale, gate = mod[:, :D_MODEL], mod[:, D_MODEL:2 * D_MODEL], mod[:, 2 * D_MODEL:]
    h = _rms(x, gain_ref[0]) * (1.0 + scale) + shift
    gu = jnp.dot(h.astype(BF), w1_ref[...], preferred_element_type=F32)
    a = _silu(gu[:, :D_FF]) * gu[:, D_FF:]
    y = x + gate * (0.5 * jnp.dot(a.astype(BF), w2_ref[...], preferred_element_type=F32))
    if final:
        fg_ref, oa_ref, ob_ref = rest
        yn = _rms(y, fg_ref[...])

        @pl.when(i < n_a_tiles)
        def _():
            oa_ref[...] = yn

        @pl.when(i >= n_a_tiles)
        def _():
            ob_ref[...] = yn
    else:
        rest[0][...] = y


def _ffn(xs, mods, l, which, gain, w1, w2, final_gain=None, tm=TM_FFN):
    n_a = N_CTX // tm
    row = lambda i: (i, 0)
    first = lambda i: (jnp.minimum(i, n_a - 1), 0)
    second = lambda i: (jnp.maximum(i - n_a, 0), 0)
    lay3 = lambda i: (l, 0, 0)
    x_specs = ([pl.BlockSpec((tm, D_MODEL), row)] if len(xs) == 1
               else [pl.BlockSpec((tm, D_MODEL), first), pl.BlockSpec((tm, D_MODEL), second)])
    in_specs = x_specs + [_mod_spec(l, which, tm),
                          pl.BlockSpec((1, 1, D_MODEL), lay3),
                          _any_spec(), _any_spec()]
    args = list(xs) + [mods, gain.reshape(DEPTH, 1, D_MODEL), w1, w2]
    if final_gain is None:
        out_specs = pl.BlockSpec((tm, D_MODEL), row)
        out_shape = jax.ShapeDtypeStruct((N_TOK, D_MODEL), F32)
    else:
        in_specs.append(pl.BlockSpec((1, D_MODEL), lambda i: (0, 0)))
        args.append(final_gain.reshape(1, D_MODEL))
        out_specs = [pl.BlockSpec((tm, D_MODEL), first), pl.BlockSpec((tm, D_MODEL), second)]
        out_shape = [jax.ShapeDtypeStruct((N_CTX, D_MODEL), F32), jax.ShapeDtypeStruct((N_LAT, D_MODEL), F32)]
    return pl.pallas_call(
        functools.partial(_ffn_kernel, n_in=len(xs), n_a_tiles=n_a, final=final_gain is not None, layer=l),
        grid=(N_TOK // tm,),
        in_specs=in_specs, out_specs=out_specs, out_shape=out_shape,
        scratch_shapes=[pltpu.VMEM((D_MODEL, 2 * D_FF), BF), pltpu.VMEM((D_FF, D_MODEL), BF),
                        pltpu.VMEM((2, D_MODEL // FFN_LOAD_CHUNKS, 2 * D_FF), F32),
                        pltpu.VMEM((2, D_FF // FFN_LOAD_CHUNKS, D_MODEL), F32),
                        pltpu.SemaphoreType.DMA((2,)), pltpu.SemaphoreType.DMA((2,))],
        compiler_params=_params("arbitrary"),
        name="ffn",
    )(*args)


def _kv_expand(ckv_n, misc, wk, wv):
    kin = jnp.concatenate([ckv_n, misc], axis=1).astype(BF)
    kk = jnp.dot(kin, wk, preferred_element_type=F32)
    vv = jnp.dot(ckv_n.astype(BF), wv, preferred_element_type=F32)
    return kk.astype(BF), vv.astype(BF)


def _proj_kernel(*refs, n_ctx_tiles):
    (x_ref, mod_ref, gain_ref, win_ref, qg_ref, wqb_ref, kvg_ref, wk_ref, wv_ref, tab_ref,
     alog_ref, dtb_ref) = refs[:12]
    (q_ref, misc_ref, kmla_ref, vmla_ref, sq_ref, sk_ref, sv_ref, g3_ref, gz_ref,
     c_ckv_ref, c_krope_ref, c_sk_ref, c_sv_ref) = refs[-13:]
    i = pl.program_id(0)
    mod = mod_ref[0, 0]
    shift, scale = mod[:, :D_MODEL], mod[:, D_MODEL:2 * D_MODEL]
    hb = (_rms(x_ref[...], gain_ref[0]) * (1.0 + scale) + shift).astype(BF)
    tab = tab_ref[...]

    def project(c0, c1):
        return _nt(hb, win_ref[0, c0:c1, :])

    u_lora = project(_C_CQ, _C_SQ)
    u_misc = project(_C_MA, _C_END)
    u_sq = project(_C_SQ, _C_SK)
    u_skv = project(_C_SK, _C_G3)

    qn = _rms(u_lora[:, :MLA_Q_LORA], qg_ref[0])
    ckv_n = _rms(u_lora[:, MLA_Q_LORA:], kvg_ref[0])

    m = u_misc[:, :LANES] * tab[:, _T_CA:_T_CA + LANES] + u_misc[:, LANES:] * tab[:, _T_SA:_T_SA + LANES]
    lane = lax.broadcasted_iota(jnp.int32, m.shape, 1)
    z = m + dtb_ref[0]
    softplus = jnp.maximum(z, 0.0) + jnp.log(1.0 + jnp.exp(-jnp.abs(z)))
    decay = -jnp.exp(alog_ref[0]) * softplus
    strength = 1.0 / (1.0 + jnp.exp(-m))
    misc = jnp.where((lane >= _M_G) & (lane < _M_B), decay,
                     jnp.where((lane >= _M_B) & (lane < _M_END), strength, m))
    misc_ref[...] = misc

    n_sq = SWA_HEADS * SWA_HD
    cos_s = tab[:, _T_CS:_T_CS + LANES]
    sin_s = tab[:, _T_SS:_T_SS + LANES]
    cos4 = jnp.concatenate([cos_s] * (n_sq // LANES), axis=1)
    sin4 = jnp.concatenate([sin_s] * (n_sq // LANES), axis=1)
    sq_ref[...] = ((u_sq[:, :n_sq] * cos4 + u_sq[:, n_sq:] * sin4) * (SWA_HD ** -0.5 * LOG2_E)).astype(BF)
    sk = u_skv[:, :LANES] * cos_s + u_skv[:, LANES:2 * LANES] * sin_s
    sv = u_skv[:, 2 * LANES:]
    sk_ref[...] = sk
    sv_ref[...] = sv

    @pl.when(i < n_ctx_tiles)
    def _():
        for j in range(c_ckv_ref.shape[0]):
            rows = slice(j * SEQ, (j + 1) * SEQ)
            c_ckv_ref[j, 0] = ckv_n[rows]
            c_krope_ref[j, 0] = misc[rows, :MLA_ROPE]
            c_sk_ref[j, 0] = sk[rows]
            c_sv_ref[j, 0] = sv[rows]

    g3_ref[...] = project(_C_G3, _C_GZ)
    gz_ref[...] = project(_C_GZ, _C_MA)

    q2 = jnp.dot(qn.astype(BF), wqb_ref[0], preferred_element_type=F32)
    nq = MLA_HEADS * LANES
    cosq = jnp.concatenate([tab[:, _T_CQ:_T_CQ + LANES]] * MLA_HEADS, axis=1)
    sinq = jnp.concatenate([tab[:, _T_SQ:_T_SQ + LANES]] * MLA_HEADS, axis=1)
    q_ref[...] = ((q2[:, :nq] * cosq + q2[:, nq:] * sinq) * MLA_Q_SCALE).astype(BF)
    kk, vv = _kv_expand(ckv_n, misc, wk_ref[0], wv_ref[0])
    kmla_ref[...] = kk
    vmla_ref[...] = vv


def _proj(x, mods, l, gain, win, qg, wqb, kvg, wk, wv, tab, alog, dtb, cache_prev, tm=TM_PROJ):
    assert tm % SEQ == 0
    n_ctx_tiles = N_CTX // tm
    lat_tiles = DEC_SEQ // tm
    lay3 = lambda i: (l, 0, 0)
    row = lambda i: (i, 0)

    def tab_map(i):
        return (jnp.where(i < n_ctx_tiles, 0, 1 + (i - n_ctx_tiles) % lat_tiles), 0)

    def lay_spec(a):
        return pl.BlockSpec((1,) + a.shape[1:], lay3)

    widths = [(MLA_HEADS * LANES, BF), (LANES, F32), (MLA_HEADS * LANES, BF),
              (MLA_HEADS * MLA_V, BF), (SWA_HEADS * SWA_HD, BF), (SWA_KV_HEADS * SWA_HD, F32),
              (SWA_KV_HEADS * SWA_HD, F32), (GDN_CONV_CH, F32), (GDN_V_DIM, F32)]
    out_specs = [pl.BlockSpec((tm, w), row) for w, _ in widths]
    out_shape = [jax.ShapeDtypeStruct((N_TOK, w), dt) for w, dt in widths]
    for w in _CACHE_WIDTHS:
        out_specs.append(pl.BlockSpec((tm // SEQ, 1, SEQ, w),
                                      lambda i: (jnp.minimum(i, n_ctx_tiles - 1), l, 0, 0)))
        out_shape.append(jax.ShapeDtypeStruct((BATCH, DEPTH, SEQ, w), F32))
    qg, kvg = qg.reshape(DEPTH, 1, -1), kvg.reshape(DEPTH, 1, -1)
    args = [x, mods, gain.reshape(DEPTH, 1, D_MODEL), win, qg, wqb, kvg, wk, wv, tab, alog, dtb]
    in_specs = [pl.BlockSpec((tm, D_MODEL), row), _mod_spec(l, 1, tm),
                pl.BlockSpec((1, 1, D_MODEL), lay3), lay_spec(win), lay_spec(qg), lay_spec(wqb),
                lay_spec(kvg), lay_spec(wk), lay_spec(wv), pl.BlockSpec((tm, _T_END), tab_map),
                lay_spec(alog), lay_spec(dtb)]
    aliases = {}
    if cache_prev is not None:
        for j, prev in enumerate(cache_prev):
            aliases[len(args)] = len(widths) + j
            args.append(prev)
            in_specs.append(_any_spec())
    return pl.pallas_call(
        functools.partial(_proj_kernel, n_ctx_tiles=n_ctx_tiles),
        grid=(N_TOK // tm,),
        in_specs=in_specs, out_specs=out_specs, out_shape=out_shape,
        input_output_aliases=aliases,
        compiler_params=_params("arbitrary"),
        name="proj",
    )(*args)


def _kv_cache_kernel(ckv_ref, misc_ref, wk_ref, wv_ref, k_ref, v_ref):
    kk, vv = _kv_expand(ckv_ref[0, 0], misc_ref[0, 0], wk_ref[0], wv_ref[0])
    k_ref[...] = kk
    v_ref[...] = vv


def _kv_cache(cache_ckv, cache_misc, l, wk, wv):
    lay3 = lambda b: (l, 0, 0)
    n = DEC_BATCH * PAST_LEN
    return pl.pallas_call(
        _kv_cache_kernel,
        grid=(DEC_BATCH,),
        in_specs=[pl.BlockSpec((1, 1, PAST_LEN, MLA_KV_LORA), lambda b: (b, l, 0, 0)),
                  pl.BlockSpec((1, 1, PAST_LEN, LANES), lambda b: (b, l, 0, 0)),
                  pl.BlockSpec((1,) + wk.shape[1:], lay3),
                  pl.BlockSpec((1,) + wv.shape[1:], lay3)],
        out_specs=[pl.BlockSpec((PAST_LEN, MLA_HEADS * LANES), lambda b: (b, 0)),
                   pl.BlockSpec((PAST_LEN, MLA_HEADS * MLA_V), lambda b: (b, 0))],
        out_shape=[jax.ShapeDtypeStruct((n, MLA_HEADS * LANES), BF),
                   jax.ShapeDtypeStruct((n, MLA_HEADS * MLA_V), BF)],
        compiler_params=_params("parallel"),
        name="kv_cache",
    )(cache_ckv, cache_misc, wk, wv)


def _softmax_numerator(s_ref, p_ref, sink=None, row_block=128):
    rows, cols = s_ref.shape
    mx = jnp.max(s_ref[...], axis=-1, keepdims=True)
    if sink is not None:
        mx = jnp.maximum(mx, sink)
    sums = []
    for r0 in range(0, rows, row_block):
        mb = jnp.broadcast_to(mx[r0:r0 + row_block], (row_block, LANES))
        part = jnp.zeros((row_block, LANES), F32)
        for c in range(0, cols, LANES):
            p = jnp.exp2(s_ref[r0:r0 + row_block, c:c + LANES] - mb)
            part = part + p
            p_ref[r0:r0 + row_block, c:c + LANES] = p.astype(BF)
        sums.append(jnp.sum(part, axis=-1, keepdims=True))
    den = jnp.concatenate(sums, axis=0)
    return den if sink is None else den + jnp.exp2(sink - mx)


def _mla_kernel(*refs, n_seg):
    q_ref = refs[0]
    k_refs = refs[1:1 + n_seg]
    v_refs = refs[1 + n_seg:1 + 2 * n_seg]
    o_ref = refs[-5]
    s_refs, p_refs = refs[-4:-2], refs[-2:]
    tq = q_ref.shape[0]
    lane = lax.broadcasted_iota(jnp.int32, (tq, LANES), 1)

    def scores(h):
        off = 0
        for k in k_refs:
            s_refs[h % 2][:, off:off + k.shape[0]] = _nt(q_ref[:, h * LANES:(h + 1) * LANES],
                                                         k[:, h * LANES:(h + 1) * LANES])
            off += k.shape[0]

    scores(0)
    outs = []
    for h in range(MLA_HEADS):
        if h + 1 < MLA_HEADS:
            scores(h + 1)
        den = _softmax_numerator(s_refs[h % 2], p_refs[h % 2])
        pair = h // 2
        acc, off = None, 0
        for v in v_refs:
            pv = jnp.dot(p_refs[h % 2][:, off:off + v.shape[0]], v[:, pair * LANES:(pair + 1) * LANES],
                         preferred_element_type=F32)
            acc = pv if acc is None else acc + pv
            off += v.shape[0]
        outs.append(acc / den)
        if h % 2 == 1:
            o_ref[:, pair * LANES:(pair + 1) * LANES] = jnp.where(lane < MLA_V, outs[-2], outs[-1]).astype(BF)


def _mla_attend(q, ks, vs, n_batch, t, row0, k_blocks, prev):
    n_seg = len(ks)
    tq = min(TQ_MLA, t)
    n_keys = sum(s for _, s in k_blocks)
    qb0 = row0 // tq
    tiles = t // tq
    q_map = lambda b, i: (qb0 + b * tiles + i, 0)
    in_specs = [pl.BlockSpec((tq, MLA_HEADS * LANES), q_map)]
    for (b0, s), width in ([(kb, MLA_HEADS * LANES) for kb in k_blocks]
                           + [(kb, MLA_HEADS * MLA_V) for kb in k_blocks]):
        in_specs.append(pl.BlockSpec((s, width), functools.partial(lambda b, i, b0: (b0 + b, 0), b0=b0)))
    args = [q, *ks, *vs]
    aliases = {}
    if prev is not None:
        args.append(prev)
        in_specs.append(_any_spec())
        aliases = {len(args) - 1: 0}
    return pl.pallas_call(
        functools.partial(_mla_kernel, n_seg=n_seg),
        grid=(n_batch, tiles),
        in_specs=in_specs,
        out_specs=pl.BlockSpec((tq, MLA_HEADS * MLA_V), q_map),
        out_shape=jax.ShapeDtypeStruct((N_TOK, MLA_HEADS * MLA_V), BF),
        input_output_aliases=aliases,
        scratch_shapes=[pltpu.VMEM((tq, n_keys), F32)] * 2 + [pltpu.VMEM((tq, n_keys), BF)] * 2,
        compiler_params=_params("parallel", "parallel"),
        name="mla_attend",
    )(*args)


def _gqa_heads(q_ref, k_segs, v_segs, masks, sink_ref, o_ref, s_refs, p_refs):
    tq = q_ref.shape[0]
    lane_q = lax.broadcasted_iota(jnp.int32, (tq, LANES), 1)
    lo_q = lane_q < SWA_HD
    k_roll = [pltpu.roll(k, SWA_HD, 1) for k in k_segs]
    v_roll = [pltpu.roll(v, SWA_HD, 1) for v in v_segs]
    heads = range(SWA_KV_HEADS)
    sinks = []
    for kvh in heads:
        tiles = [q_ref[:, (kvh * 2 + j) * LANES:(kvh * 2 + j + 1) * LANES] for j in range(2)]
        zero = jnp.zeros_like(tiles[0])
        qs = jnp.concatenate([jnp.where(lo_q, tiles[0], zero), jnp.where(lo_q, zero, tiles[0]),
                              jnp.where(lo_q, tiles[1], zero), jnp.where(lo_q, zero, tiles[1])], axis=0)
        sinks.append(jnp.concatenate(
            [sink_ref[0, kvh * SWA_GROUP + g:kvh * SWA_GROUP + g + 1, :]
             for g in range(SWA_GROUP) for _ in range(tq // LANES)], axis=1) * LOG2_E)
        off = 0
        for k, kr, msk in zip(k_segs, k_roll, masks):
            lane_k = lax.broadcasted_iota(jnp.int32, k.shape, 1)
            first = (lane_k < SWA_HD) == (kvh == 0)
            kd = jnp.where(first, k, kr)
            s = _bdot_nt(kd, qs)
            if msk is not None:
                s = jnp.where(msk, s, -1e30)
            s_refs[kvh][off:off + k.shape[0], :] = s
            off += k.shape[0]
    dens = [_softmax_numerator_t(s_refs[kvh], p_refs[kvh], sinks[kvh]) for kvh in heads]
    for kvh in heads:
        acc, off = None, 0
        for v, vr in zip(v_segs, v_roll):
            lane_v = lax.broadcasted_iota(jnp.int32, v.shape, 1)
            first = (lane_v < SWA_HD) == (kvh == 0)
            vd = jnp.where(first, v, vr)
            pv = lax.dot_general(vd.astype(BF), p_refs[kvh][off:off + v.shape[0], :],
                                 (((0,), (0,)), ((), ())), preferred_element_type=F32)
            acc = pv if acc is None else acc + pv
            off += v.shape[0]
        o = (acc / dens[kvh]).T
        for j in range(2):
            o_ref[:, (kvh * 2 + j) * LANES:(kvh * 2 + j + 1) * LANES] = jnp.where(
                lo_q, o[(2 * j) * tq:(2 * j + 1) * tq], o[(2 * j + 1) * tq:(2 * j + 2) * tq]).astype(BF)


def _softmax_numerator_t(s_ref, p_ref, sink, row_block=128):
    keys, cols = s_ref.shape
    dens = []
    for c in range(0, cols, LANES):
        mx = s_ref[0:row_block, c:c + LANES]
        for r0 in range(row_block, keys, row_block):
            mx = jnp.maximum(mx, s_ref[r0:r0 + row_block, c:c + LANES])
        snk = sink[:, c:c + LANES]
        m = jnp.maximum(jnp.max(mx, axis=0, keepdims=True), snk)
        mb = jnp.broadcast_to(m, (row_block, LANES))
        part = jnp.zeros((row_block, LANES), F32)
        for r0 in range(0, keys, row_block):
            p = jnp.exp2(s_ref[r0:r0 + row_block, c:c + LANES] - mb)
            part = part + p
            p_ref[r0:r0 + row_block, c:c + LANES] = p.astype(BF)
        dens.append(jnp.sum(part, axis=0, keepdims=True) + jnp.exp2(snk - m))
    return jnp.concatenate(dens, axis=1)


def _swa_scratch(keys, rows):
    return ([pltpu.VMEM((keys, rows), F32)] * SWA_KV_HEADS + [pltpu.VMEM((keys, rows), BF)] * SWA_KV_HEADS)


def _swa_ctx_kernel(q_ref, k_ref, v_ref, sink_ref, o_ref, *scratch):
    _gqa_heads(q_ref, [k_ref[...]], [v_ref[...]], [None], sink_ref, o_ref,
               scratch[:SWA_KV_HEADS], scratch[SWA_KV_HEADS:])


def _swa_ctx(sq, sk, sv, sink_rows, l):
    return pl.pallas_call(
        _swa_ctx_kernel,
        grid=(BATCH,),
        in_specs=[pl.BlockSpec((SEQ, SWA_HEADS * SWA_HD), lambda b: (b, 0)),
                  pl.BlockSpec((SEQ, LANES), lambda b: (b, 0)),
                  pl.BlockSpec((SEQ, LANES), lambda b: (b, 0)),
                  pl.BlockSpec((1, SWA_HEADS, LANES), lambda b: (l, 0, 0))],
        out_specs=pl.BlockSpec((SEQ, SWA_HEADS * SWA_HD), lambda b: (b, 0)),
        out_shape=jax.ShapeDtypeStruct((N_TOK, SWA_HEADS * SWA_HD), BF),
        scratch_shapes=_swa_scratch(SEQ, SWA_GROUP * SEQ),
        compiler_params=_params("parallel"),
        name="swa_ctx",
    )(sq, sk, sv, sink_rows)


def _swa_lat_kernel(q_ref, kp_ref, kc_ref, kn_ref, vp_ref, vc_ref, vn_ref, kx_ref, vx_ref, sink_ref, prev_ref,
                    o_ref, *scratch):
    w = SWA_WINDOW
    n = pl.program_id(1)
    nb = pl.num_programs(1)
    k_band = jnp.concatenate([kp_ref[...], kc_ref[...], kn_ref[...]], axis=0)
    v_band = jnp.concatenate([vp_ref[...], vc_ref[...], vn_ref[...]], axis=0)
    rows = SWA_GROUP * w
    r = lax.broadcasted_iota(jnp.int32, (3 * w, rows), 1) & (w - 1)
    c = lax.broadcasted_iota(jnp.int32, (3 * w, rows), 0)
    valid = (c >= r) & (c <= r + 2 * w) & ((c >= w) | (n > 0)) & ((c < 2 * w) | (n < nb - 1))
    _gqa_heads(q_ref, [k_band, kx_ref[0, 0]], [v_band, vx_ref[0, 0]], [valid, None], sink_ref, o_ref,
               scratch[:SWA_KV_HEADS], scratch[SWA_KV_HEADS:])


def _swa_lat(sq, sk, sv, k_cache, v_cache, sink_rows, l, prev):
    w = SWA_WINDOW
    nb = DEC_SEQ // w
    q0 = N_CTX // w

    def blk(d):
        return lambda b, n: (q0 + b * nb + jnp.clip(n + d, 0, nb - 1), 0)

    kv_specs = [pl.BlockSpec((w, LANES), blk(d)) for d in (-1, 0, 1)]
    cache_spec = pl.BlockSpec((1, 1, PAST_LEN, LANES), lambda b, n: (b, l, 0, 0))
    return pl.pallas_call(
        _swa_lat_kernel,
        grid=(DEC_BATCH, nb),
        in_specs=[pl.BlockSpec((w, SWA_HEADS * SWA_HD), blk(0))] + kv_specs + kv_specs
        + [cache_spec, cache_spec, pl.BlockSpec((1, SWA_HEADS, LANES), lambda b, n: (l, 0, 0)), _any_spec()],
        out_specs=pl.BlockSpec((w, SWA_HEADS * SWA_HD), blk(0)),
        out_shape=jax.ShapeDtypeStruct((N_TOK, SWA_HEADS * SWA_HD), BF),
        input_output_aliases={10: 0},
        scratch_shapes=_swa_scratch(3 * w + PAST_LEN, SWA_GROUP * w),
        compiler_params=_params("parallel", "parallel"),
        name="swa_lat",
    )(sq, sk, sk, sk, sv, sv, sv, k_cache, v_cache, sink_rows, prev)


def _gdn_conv_kernel(*refs):
    x_ref, w_ref, o_ref = refs[0], refs[1], refs[-1]
    x = x_ref[...]
    t = x.shape[0]
    w = w_ref[0]
    r8 = lax.broadcasted_iota(jnp.int32, (SUBLANES, x.shape[1]), 0)
    half = GDN_CONV // 2
    acc = x * w[half:half + 1, :]
    for k in range(GDN_CONV):
        d = k - half
        if d == 0:
            continue
        xs = pltpu.roll(x, (-d) % t, 0)
        top, bot = xs[:SUBLANES], xs[t - SUBLANES:]
        if d < 0:
            top = jnp.where(r8 + d >= 0, top, 0.0)
        else:
            bot = jnp.where(r8 + d < SUBLANES, bot, 0.0)
        xs = jnp.concatenate([top, xs[SUBLANES:t - SUBLANES], bot], axis=0)
        acc = acc + xs * w[k:k + 1, :]
    y = _silu(acc)
    is_qk = pl.program_id(1) < 2
    cols = []
    for h in range(y.shape[1] // LANES):
        yh = y[:, h * LANES:(h + 1) * LANES]
        nrm = lax.rsqrt(jnp.sum(yh * yh, axis=-1, keepdims=True) + 1e-6)
        cols.append(yh * jnp.where(is_qk, nrm, 1.0))
    o_ref[...] = jnp.concatenate(cols, axis=1)


def _gdn_conv(g3, conv_w, l, row0, n_seq, t, prev):
    blk0 = row0 // t
    args = [g3, conv_w]
    in_specs = [pl.BlockSpec((t, GDN_QK_DIM), lambda s, j: (blk0 + s, j)),
                pl.BlockSpec((1, GDN_CONV, GDN_QK_DIM), lambda s, j: (l, 0, j))]
    aliases = {}
    if prev is not None:
        args.append(prev)
        in_specs.append(_any_spec())
        aliases = {2: 0}
    return pl.pallas_call(
        _gdn_conv_kernel,
        grid=(n_seq, 3),
        in_specs=in_specs,
        out_specs=pl.BlockSpec((t, GDN_QK_DIM), lambda s, j: (blk0 + s, j)),
        out_shape=jax.ShapeDtypeStruct((N_TOK, GDN_CONV_CH), F32),
        input_output_aliases=aliases,
        compiler_params=_params("parallel", "parallel"),
        name="gdn_conv",
    )(*args)


def _gate_rows(misc):
    shape = (2 * GDN_NH, LANES)
    sel = (lax.broadcasted_iota(jnp.int32, shape, 1)
           == lax.broadcasted_iota(jnp.int32, shape, 0) + _M_G).astype(BF)
    b1, b2, b3 = _split3(misc)
    return _nt(sel, b1) + (_nt(sel, b2) + _nt(sel, b3))


def _gdn_chunk_pairs(pairs):
    c = GDN_CHUNK
    shape = (c, 2 * c)
    ri = lax.broadcasted_iota(jnp.int32, shape, 0)
    lane = lax.broadcasted_iota(jnp.int32, shape, 1)
    cj = lane & (c - 1)
    fwd_half = lane < c
    lower, upper = ri >= cj, ri <= cj
    incl = (fwd_half & lower) | (~fwd_half & upper)
    incl_t = (fwd_half & upper) | (~fwd_half & lower)
    strict = incl & (ri != cj)
    incl_f, incl_b = incl & fwd_half, incl & ~fwd_half
    eye = (ri == cj).astype(F32)
    n = len(pairs)
    rng = range(n)
    fw = [p[0] for p in pairs]
    bw = [p[1] for p in pairs]

    def halves(a, b):
        return jnp.where(fwd_half, a, b)

    def split_rows(m):
        return jnp.concatenate([jnp.where(fwd_half, m, 0.0), jnp.where(fwd_half, 0.0, m)], axis=0)

    def pair_dot3(x, p):
        xh = x.astype(BF)
        xl = (x - xh.astype(F32)).astype(BF)
        phf = p.astype(BF).astype(F32)
        bd_hi = split_rows(phf).astype(BF)
        bd_lo = split_rows(p - phf).astype(BF)
        return jnp.dot(jnp.concatenate([xh, xl, xh], axis=1), jnp.concatenate([bd_hi, bd_hi, bd_lo], axis=0),
                       preferred_element_type=F32)

    g_col = [halves(fw[i]["g_col"], bw[i]["g_col"]) for i in rng]
    gc_row = [jnp.sum(jnp.where(incl_t, g_col[i], 0.0), axis=0, keepdims=True) for i in rng]
    gcf = [jnp.sum(jnp.where(incl_f, pairs[i][2], 0.0), axis=1, keepdims=True) for i in rng]
    gcb = [jnp.sum(jnp.where(incl_b, pairs[i][2], 0.0), axis=1, keepdims=True) for i in rng]
    decay = [jnp.where(incl, jnp.exp(halves(gcf[i], gcb[i]) - gc_row[i]), 0.0) for i in rng]
    qf = [p["q"] * (GDN_DK ** -0.5) for p in fw]
    qb = [p["q"] * (GDN_DK ** -0.5) for p in bw]
    kbf = [p["k"] * p["beta"] for p in fw]
    kbb = [p["k"] * p["beta"] for p in bw]
    z = jnp.zeros((c, GDN_DK), F32)
    kq = [_bdot_nt(jnp.concatenate([jnp.concatenate([kbf[i], kbb[i]], axis=1),
                                    jnp.concatenate([qf[i], qb[i]], axis=1)], axis=0),
                   jnp.concatenate([jnp.concatenate([fw[i]["k"], z], axis=1),
                                    jnp.concatenate([z, bw[i]["k"]], axis=1)], axis=0)) for i in rng]
    pw = [jnp.where(strict, -(kq[i][:c] * decay[i]), 0.0) for i in rng]
    inv = [eye + m for m in pw]
    levels = int(np.log2(c)) - 1
    pw = [pair_dot3(m, m) for m in pw]
    for level in range(levels):
        if level + 1 < levels:
            both = [pair_dot3(jnp.concatenate([inv[i], pw[i]], axis=0), pw[i]) for i in rng]
            inv = [inv[i] + both[i][:c] for i in rng]
            pw = [both[i][c:] for i in rng]
        else:
            inv = [inv[i] + pair_dot3(inv[i], pw[i]) for i in rng]
    ef = [jnp.exp(g) for g in gcf]
    eb = [jnp.exp(g) for g in gcb]
    uw = [_bdot(split_rows(inv[i]),
                jnp.concatenate([jnp.concatenate([fw[i]["v"] * fw[i]["beta"], kbf[i] * ef[i]], axis=1),
                                 jnp.concatenate([bw[i]["v"] * bw[i]["beta"], kbb[i] * eb[i]], axis=1)], axis=0))
          for i in rng]
    a = [jnp.where(incl, kq[i][c:] * decay[i], 0.0) for i in rng]
    glf = [g[c - 1:c, :] for g in gcf]
    glb = [g[0:1, :] for g in gcb]
    wqf = [_bdot(jnp.concatenate([uw[i][:c, GDN_DV:], qf[i] * ef[i]], axis=0), fw[i]["state"]) for i in rng]
    wqb = [_bdot(jnp.concatenate([uw[i][c:, GDN_DV:], qb[i] * eb[i]], axis=0), bw[i]["state"]) for i in rng]
    vnf = [uw[i][:c, :GDN_DV] - wqf[i][:c] for i in rng]
    vnb = [uw[i][c:, :GDN_DV] - wqb[i][:c] for i in rng]
    av = [_bdot(split_rows(a[i]), jnp.concatenate([vnf[i], vnb[i]], axis=0)) for i in rng]
    sf = [fw[i]["state"] * jnp.exp(glf[i]) + _bdot_tn(fw[i]["k"] * jnp.exp(glf[i] - gcf[i]), vnf[i]) for i in rng]
    sb = [bw[i]["state"] * jnp.exp(glb[i]) + _bdot_tn(bw[i]["k"] * jnp.exp(glb[i] - gcb[i]), vnb[i]) for i in rng]
    return [(wqf[i][c:] + av[i][:c], sf[i], wqb[i][c:] + av[i][c:], sb[i]) for i in rng]


def _gdn_kernel(*refs, context, n_par):
    qf_ref, qb_ref, mf_ref, mb_ref = refs[:4]
    st_ref = refs[-1]
    if context:
        of_ref, ob_ref, sfin_ref = refs[-4:-1]
    else:
        s0_ref = refs[4]
        of_ref, ob_ref = refs[-3:-1]
    n = pl.program_id(1)

    @pl.when(n == 0)
    def _():
        if context:
            st_ref[...] = jnp.zeros_like(st_ref)
        else:
            st_ref[...] = s0_ref[:, 0]

    def problem(qkv, misc, s, h, d):
        i = d * GDN_HEADS + h
        return dict(q=qkv[:, h * GDN_DK:(h + 1) * GDN_DK],
                    k=qkv[:, GDN_QK_DIM + h * GDN_DK:GDN_QK_DIM + (h + 1) * GDN_DK],
                    v=qkv[:, 2 * GDN_QK_DIM + h * GDN_DV:2 * GDN_QK_DIM + (h + 1) * GDN_DV],
                    g_col=misc[:, _M_G + i:_M_G + i + 1], beta=misc[:, _M_B + i:_M_B + i + 1],
                    state=st_ref[s, i])

    lane = lax.broadcasted_iota(jnp.int32, (1, 2 * GDN_CHUNK), 1)
    pairs = []
    for s in range(n_par):
        qkv_f, qkv_b, misc_f, misc_b = qf_ref[s], qb_ref[s], mf_ref[s], mb_ref[s]
        rows = _gate_rows(jnp.concatenate([misc_f, misc_b], axis=0))
        for h in range(GDN_HEADS):
            g_row = jnp.where(lane < GDN_CHUNK, rows[h:h + 1, :], rows[GDN_HEADS + h:GDN_HEADS + h + 1, :])
            pairs.append((problem(qkv_f, misc_f, s, h, 0), problem(qkv_b, misc_b, s, h, 1), g_row))
    results = _gdn_chunk_pairs(pairs)
    for s in range(n_par):
        for h in range(GDN_HEADS):
            o_f, s_f, o_b, s_b = results[s * GDN_HEADS + h]
            of_ref[s, :, h * GDN_DV:(h + 1) * GDN_DV] = o_f
            ob_ref[s, :, h * GDN_DV:(h + 1) * GDN_DV] = o_b
            st_ref[s, h] = s_f
            st_ref[s, GDN_HEADS + h] = s_b

    if context:
        @pl.when(n == pl.num_programs(1) - 1)
        def _():
            sfin_ref[:, 0] = st_ref[...]


def _gdn(qkv, misc, l, t, n_seq, seq0, s0, prevs, n_par=GDN_PAR):
    c = GDN_CHUNK
    nc = t // c
    context = s0 is None
    sb0 = seq0 // n_par
    n_all = N_TOK // t
    fwd = lambda s, n: (sb0 + s, n, 0)
    bwd = lambda s, n: (sb0 + s, nc - 1 - n, 0)
    st_spec = pl.BlockSpec((n_par, 1, GDN_NH, GDN_DK, GDN_DV), lambda s, n: (s, l, 0, 0, 0))
    qkv3 = qkv.reshape(n_all, t, GDN_CONV_CH)
    misc3 = misc.reshape(n_all, t, LANES)
    in_specs = [pl.BlockSpec((n_par, c, GDN_CONV_CH), fwd), pl.BlockSpec((n_par, c, GDN_CONV_CH), bwd),
                pl.BlockSpec((n_par, c, LANES), fwd), pl.BlockSpec((n_par, c, LANES), bwd)]
    args = [qkv3, qkv3, misc3, misc3]
    if not context:
        in_specs.append(st_spec)
        args.append(s0)
    out_specs = [pl.BlockSpec((n_par, c, GDN_V_DIM), fwd), pl.BlockSpec((n_par, c, GDN_V_DIM), bwd)]
    out_shape = [jax.ShapeDtypeStruct((n_all, t, GDN_V_DIM), F32)] * 2
    if context:
        out_specs.append(st_spec)
        out_shape.append(jax.ShapeDtypeStruct((n_seq, DEPTH, GDN_NH, GDN_DK, GDN_DV), F32))
    aliases = {}
    for j, p in enumerate(prevs):
        if p is not None:
            aliases[len(args)] = j
            args.append(p.reshape(out_shape[j].shape))
            in_specs.append(_any_spec())
    return pl.pallas_call(
        functools.partial(_gdn_kernel, context=context, n_par=n_par),
        grid=(n_seq // n_par, nc),
        in_specs=in_specs, out_specs=out_specs, out_shape=out_shape,
        input_output_aliases=aliases,
        scratch_shapes=[pltpu.VMEM((n_par, GDN_NH, GDN_DK, GDN_DV), F32)],
        compiler_params=_params("parallel", "arbitrary"),
        name="gdn",
    )(*args)


def _out_kernel(x_ref, mod_ref, om_ref, os_ref, gf_ref, gb_ref, gz_ref, gn_ref, w_ref, o_ref):
    gate = mod_ref[0, 0][:, 2 * D_MODEL:]
    s = gf_ref[...] + gb_ref[...]
    gz = gz_ref[...]
    cols = []
    for h in range(GDN_HEADS):
        sh = s[:, h * GDN_DV:(h + 1) * GDN_DV]
        cols.append(_rms(sh, gn_ref[0]) * _silu(gz[:, h * GDN_DV:(h + 1) * GDN_DV]))
    og = jnp.concatenate(cols, axis=1)
    n_m = MLA_HEADS * MLA_V
    n_s = SWA_HEADS * SWA_HD
    y = (jnp.dot(om_ref[...], w_ref[0, :n_m, :], preferred_element_type=F32)
         + jnp.dot(os_ref[...], w_ref[0, n_m:n_m + n_s, :], preferred_element_type=F32)
         + jnp.dot(og.astype(BF), w_ref[0, n_m + n_s:, :], preferred_element_type=F32))
    o_ref[...] = x_ref[...] + gate * y


def _out_proj(x, mods, l, o_mla, o_swa, o_gf, o_gb, gz, gdn_norm, w_out, tm=TM_OUT):
    row = lambda i: (i, 0)
    lay3 = lambda i: (l, 0, 0)
    return pl.pallas_call(
        _out_kernel,
        grid=(N_TOK // tm,),
        in_specs=[pl.BlockSpec((tm, D_MODEL), row),
                  _mod_spec(l, 1, tm),
                  pl.BlockSpec((tm, MLA_HEADS * MLA_V), row),
                  pl.BlockSpec((tm, SWA_HEADS * SWA_HD), row),
                  pl.BlockSpec((tm, GDN_V_DIM), row),
                  pl.BlockSpec((tm, GDN_V_DIM), row),
                  pl.BlockSpec((tm, GDN_V_DIM), row),
                  pl.BlockSpec((1, 1, GDN_DV), lay3),
                  pl.BlockSpec((1,) + w_out.shape[1:], lay3)],
        out_specs=pl.BlockSpec((tm, D_MODEL), row),
        out_shape=jax.ShapeDtypeStruct((N_TOK, D_MODEL), F32),
        compiler_params=_params("parallel"),
        name="out_proj",
    )(x, mods, o_mla, o_swa, o_gf, o_gb, gz, gdn_norm.reshape(DEPTH, 1, GDN_DV), w_out)


def _rot_columns(w, dim):
    shp = w.shape
    w6 = w.reshape(shp[:-1] + (shp[-1] // dim, 2, 2, dim // 4))
    sign = jnp.asarray([-1.0, 1.0], F32).reshape(2, 1)
    return (jnp.flip(w6, axis=-2) * sign).reshape(shp)


def _axial_rope(n_tokens, dim):
    f32 = np.float32
    rows = n_tokens // GRID_W
    row = np.repeat(np.arange(rows, dtype=f32), GRID_W)
    col = np.tile(np.arange(GRID_W, dtype=f32), rows)
    axis_dim = dim // 2
    inv_freq = (f32(1.0) / (f32(ROPE_BASE) ** (np.arange(0, axis_dim, 2, dtype=f32) / f32(axis_dim)))).astype(f32)
    ang_r = row[:, None] * inv_freq[None, :]
    ang_c = col[:, None] * inv_freq[None, :]
    ang = np.concatenate([ang_r, ang_r, ang_c, ang_c], axis=-1).astype(f32)
    return np.cos(ang).astype(f32), np.sin(ang).astype(f32)


def _rope_table(tm):
    cos_m, sin_m = _axial_rope(DEC_SEQ, MLA_ROPE)
    cos_s, sin_s = _axial_rope(DEC_SEQ, SWA_HD)
    t = DEC_SEQ
    one = lambda w: np.ones((t, w), np.float32)
    zero = lambda w: np.zeros((t, w), np.float32)
    lat = np.concatenate([
        cos_s, cos_s, sin_s, sin_s,
        cos_m, one(LANES - MLA_ROPE), sin_m, zero(LANES - MLA_ROPE),
        one(MLA_NOPE), cos_m, one(LANES - MLA_NOPE - MLA_ROPE),
        zero(MLA_NOPE), sin_m, zero(LANES - MLA_NOPE - MLA_ROPE)], axis=1)
    ident_row = np.concatenate([np.ones(LANES), np.zeros(LANES)] * 3).astype(np.float32)
    ident = np.broadcast_to(ident_row[None, :], (tm, _T_END))
    return jnp.asarray(np.concatenate([ident, lat], axis=0))


def _mixer_weights(w_in, mla_w_qb, mla_w_kvb):
    nl = DEPTH
    offs = np.cumsum([0, MLA_Q_LORA, MLA_KV_LORA, MLA_ROPE, SWA_HEADS * SWA_HD, SWA_KV_HEADS * SWA_HD,
                      SWA_KV_HEADS * SWA_HD, GDN_CONV_CH, GDN_V_DIM, 2 * GDN_NH])
    w_t = jnp.swapaxes(w_in, 1, 2)
    cq, ckv, krope, sq, sk, sv, g3, gz, gates = [w_t[:, offs[i]:offs[i + 1], :] for i in range(9)]

    def rot(w, dim):
        w6 = w.reshape(nl, w.shape[1] // dim, 2, 2, dim // 4, D_MODEL)
        return (jnp.flip(w6, axis=3) * jnp.asarray([-1.0, 1.0], F32).reshape(2, 1, 1)).reshape(w.shape)

    zeros = lambda n: jnp.zeros((nl, n, D_MODEL), F32)
    misc_a = jnp.concatenate([krope, gates, zeros(LANES - _M_END)], axis=1)
    misc_b = jnp.concatenate([rot(krope, MLA_ROPE), zeros(LANES - MLA_ROPE)], axis=1)
    win = jnp.concatenate([cq, ckv, sq, rot(sq, SWA_HD), sk, rot(sk, SWA_HD), sv, g3, gz, misc_a, misc_b],
                          axis=1).astype(BF)

    r = MLA_Q_LORA
    wq = mla_w_qb.reshape(nl, r, MLA_HEADS, MLA_NOPE + MLA_ROPE)
    nope, rope = wq[..., :MLA_NOPE], wq[..., MLA_NOPE:]
    pad = LANES - MLA_NOPE - MLA_ROPE
    z = lambda n: jnp.zeros((nl, r, MLA_HEADS, n), F32)
    qa = jnp.concatenate([nope, rope, z(pad)], axis=-1).reshape(nl, r, MLA_HEADS * LANES)
    qb = jnp.concatenate([z(MLA_NOPE), _rot_columns(rope, MLA_ROPE), z(pad)],
                         axis=-1).reshape(nl, r, MLA_HEADS * LANES)
    wqb = jnp.concatenate([qa, qb], axis=2).astype(BF)

    kvb = mla_w_kvb.reshape(nl, MLA_KV_LORA, MLA_HEADS, MLA_NOPE + MLA_V)
    k_nope = jnp.concatenate([kvb[..., :MLA_NOPE],
                              jnp.zeros((nl, MLA_KV_LORA, MLA_HEADS, LANES - MLA_NOPE), F32)],
                             axis=-1).reshape(nl, MLA_KV_LORA, MLA_HEADS * LANES)
    place = np.zeros((LANES, MLA_HEADS, LANES), np.float32)
    for i in range(MLA_ROPE):
        place[i, :, MLA_NOPE + i] = 1.0
    place = jnp.broadcast_to(jnp.asarray(place.reshape(1, LANES, MLA_HEADS * LANES)),
                             (nl, LANES, MLA_HEADS * LANES))
    wk = jnp.concatenate([k_nope, place], axis=1).astype(BF)
    wv = kvb[..., MLA_NOPE:].reshape(nl, MLA_KV_LORA, MLA_HEADS * MLA_V).astype(BF)
    return win, wqb, wk, wv


def _misc_rows(vals):
    rows = jnp.zeros((DEPTH, 1, LANES), F32)
    return rows.at[:, 0, _M_G:_M_B].set(vals.reshape(DEPTH, GDN_NH).astype(F32))


def kernel(x_prompt, x_sample, cache_mla_ckv, cache_mla_krope, cache_swa_k, cache_swa_v, state_gdn, c, c_ctx,
           w_ada, b_ada, norm_ffn1, ffn1_w1, ffn1_w2, norm_mix, w_in, mla_q_norm, mla_w_qb, mla_kv_norm,
           mla_w_kvb, swa_sink, gdn_conv_w, gdn_a_log, gdn_dt_bias, gdn_norm, w_out, norm_ffn2, ffn2_w1,
           ffn2_w2, final_norm):
    cond = jnp.concatenate([c_ctx[None, :], c, jnp.zeros((COND_ROWS - N_GROUPS, D_MODEL), F32)], axis=0)
    mods = _adaln(cond, w_ada, b_ada)[:, :N_GROUPS].reshape(DEPTH, N_GROUPS, 1, N_MOD * D_MODEL)
    tab = _rope_table(TM_PROJ)
    win, wqb, wk, wv = _mixer_weights(w_in, mla_w_qb, mla_w_kvb)
    w11, w12, w21, w22 = ffn1_w1, ffn1_w2, ffn2_w1, ffn2_w2
    wo = w_out.astype(BF)
    alog, dtb = _misc_rows(gdn_a_log), _misc_rows(gdn_dt_bias)
    sink_rows = jnp.broadcast_to(swa_sink[:, :, None], (DEPTH, SWA_HEADS, LANES))
    cache_misc = jnp.pad(cache_mla_krope, ((0, 0), (0, 0), (0, 0), (0, LANES - MLA_ROPE)))
    cache_k = cache_swa_k.reshape(DEC_BATCH, DEPTH, PAST_LEN, LANES)
    cache_v = cache_swa_v.reshape(DEC_BATCH, DEPTH, PAST_LEN, LANES)
    s0 = state_gdn.reshape(DEC_BATCH, DEPTH, GDN_NH, GDN_DK, GDN_DV)

    xs = [x_prompt.reshape(N_CTX, D_MODEL), x_sample.reshape(N_LAT, D_MODEL)]
    caches = new_st = None
    for l in range(DEPTH):
        x = _ffn(xs, mods, l, 0, norm_ffn1, w11, w12)
        (q_mla, misc, k_mla, v_mla, sq, sk, sv, g3, gz, *caches) = _proj(
            x, mods, l, norm_mix, win, mla_q_norm, wqb, mla_kv_norm, wk, wv, tab, alog, dtb, caches)

        k_c, v_c = _kv_cache(cache_mla_ckv, cache_misc, l, wk, wv)
        o_mla = _mla_attend(q_mla, [k_mla], [v_mla], BATCH, SEQ, 0, [(0, SEQ)], None)
        o_mla = _mla_attend(q_mla, [k_mla, k_c], [v_mla, v_c], DEC_BATCH, DEC_SEQ, N_CTX,
                            [(N_CTX // DEC_SEQ, DEC_SEQ), (0, PAST_LEN)], o_mla)

        o_swa = _swa_ctx(sq, sk, sv, sink_rows, l)
        o_swa = _swa_lat(sq, sk, sv, cache_k, cache_v, sink_rows, l, o_swa)

        qkv = _gdn_conv(g3, gdn_conv_w, l, 0, BATCH, SEQ, None)
        qkv = _gdn_conv(g3, gdn_conv_w, l, N_CTX, DEC_BATCH, DEC_SEQ, qkv)
        o_gf, o_gb, new_st = _gdn(qkv, misc, l, SEQ, BATCH, 0, None, [None, None, new_st], n_par=GDN_PAR_CTX)
        o_gf, o_gb = _gdn(qkv, misc, l, DEC_SEQ, DEC_BATCH, N_CTX // DEC_SEQ, s0, [o_gf, o_gb])

        x = _out_proj(x, mods, l, o_mla, o_swa, o_gf.reshape(N_TOK, GDN_V_DIM), o_gb.reshape(N_TOK, GDN_V_DIM),
                      gz, gdn_norm, wo)
        if l + 1 < DEPTH:
            xs = [_ffn([x], mods, l, 2, norm_ffn2, w21, w22)]
        else:
            y_prompt, y_sample = _ffn([x], mods, l, 2, norm_ffn2, w21, w22, final_gain=final_norm)

    new_ckv, new_krope, new_sk, new_sv = caches
    kv_shape = (BATCH, DEPTH, SEQ, SWA_KV_HEADS, SWA_HD)
    return (y_prompt.reshape(BATCH, SEQ, D_MODEL), y_sample.reshape(DEC_BATCH, DEC_SEQ, D_MODEL), new_ckv,
            new_krope, new_sk.reshape(kv_shape), new_sv.reshape(kv_shape),
            new_st.reshape(BATCH, DEPTH, 2, GDN_HEADS, GDN_DK, GDN_DV))
```

```python
import functools

import numpy as np
import jax
import jax.numpy as jnp
from jax import lax
from jax.experimental import pallas as pl
from jax.experimental.pallas import tpu as pltpu

D_MODEL = 1024
BATCH = 16
SEQ = 256
DEPTH = 2
DEC_BATCH = 2
DEC_SEQ = 2048
PAST_LEN = 512
GRID_W = 64
ROPE_BASE = 10000.0
NORM_EPS = 1e-6
N_MOD = 9
D_FF = 2816
MLA_HEADS = 8
MLA_Q_LORA = 384
MLA_KV_LORA = 256
MLA_NOPE = 64
MLA_ROPE = 32
MLA_V = 64
SWA_HEADS = 8
SWA_KV_HEADS = 2
SWA_GROUP = SWA_HEADS // SWA_KV_HEADS
SWA_HD = 64
SWA_WINDOW = 128
GDN_HEADS = 4
GDN_DK = 128
GDN_DV = 128
GDN_CONV = 5
GDN_CHUNK = 64
GDN_QK_DIM = GDN_HEADS * GDN_DK
GDN_V_DIM = GDN_HEADS * GDN_DV
GDN_CONV_CH = 2 * GDN_QK_DIM + GDN_V_DIM
GDN_NH = 2 * GDN_HEADS

N_CTX = BATCH * SEQ
N_LAT = DEC_BATCH * DEC_SEQ
N_TOK = N_CTX + N_LAT
N_GROUPS = 1 + DEC_BATCH
COND_ROWS = 8

LANES = 128
SUBLANES = 8
VMEM_LIMIT_BYTES = 56 * 1024 * 1024

TM_FFN = 512
FFN_LOAD_CHUNKS = 8
TM_PROJ = 2 * SEQ
TM_OUT = 512
TQ_MLA = 512
GDN_PAR = 2
GDN_PAR_CTX = 4

BF = jnp.bfloat16
F32 = jnp.float32
LOG2_E = 1.4426950408889634
MLA_Q_SCALE = (MLA_NOPE + MLA_ROPE) ** -0.5 * LOG2_E

_C_CQ = 0
_C_CKV = _C_CQ + MLA_Q_LORA
_C_SQ = _C_CKV + MLA_KV_LORA
_C_SK = _C_SQ + SWA_HEADS * SWA_HD
_C_SV = _C_SK + SWA_KV_HEADS * SWA_HD
_C_G3 = _C_SV + SWA_KV_HEADS * SWA_HD
_C_GZ = _C_G3 + GDN_CONV_CH
_C_MA = _C_GZ + GDN_V_DIM
_C_END = _C_MA + LANES
_M_G = MLA_ROPE
_M_B = MLA_ROPE + GDN_NH
_M_END = MLA_ROPE + 2 * GDN_NH
_CACHE_WIDTHS = (MLA_KV_LORA, MLA_ROPE, SWA_KV_HEADS * SWA_HD, SWA_KV_HEADS * SWA_HD)
_T_CS, _T_SS, _T_CA, _T_SA, _T_CQ, _T_SQ = (i * LANES for i in range(6))
_T_END = 6 * LANES


def _params(*sem):
    return pltpu.CompilerParams(dimension_semantics=sem, vmem_limit_bytes=VMEM_LIMIT_BYTES)


def _bdot(a, b):
    return jnp.dot(a.astype(BF), b.astype(BF), preferred_element_type=F32)


def _nt(a, b):
    return lax.dot_general(a, b, (((1,), (1,)), ((), ())), preferred_element_type=F32)


def _bdot_nt(a, b):
    return _nt(a.astype(BF), b.astype(BF))


def _bdot_tn(a, b):
    return lax.dot_general(a.astype(BF), b.astype(BF), (((0,), (0,)), ((), ())),
                           preferred_element_type=F32)


def _split3(a):
    b1 = a.astype(BF)
    r = a - b1.astype(F32)
    b2 = r.astype(BF)
    b3 = (r - b2.astype(F32)).astype(BF)
    return b1, b2, b3


def _silu(x):
    return x / (1.0 + jnp.exp(-x))


def _rms(x, gain, eps=NORM_EPS):
    return x * lax.rsqrt(jnp.mean(x * x, axis=-1, keepdims=True) + eps) * gain


def _group_of_row(r):
    return jnp.where(r < N_CTX, 0, 1 + (r - N_CTX) // DEC_SEQ)


def _any_spec():
    return pl.BlockSpec(memory_space=pl.ANY)


def _adaln_kernel(c_ref, w_ref, b_ref, o_ref):
    o_ref[0] = _bdot(_silu(c_ref[...]), w_ref[0]) + b_ref[0]


def _adaln(cond, w_ada, b_ada, tn=1536):
    n = N_MOD * D_MODEL
    return pl.pallas_call(
        _adaln_kernel,
        grid=(DEPTH, n // tn),
        in_specs=[pl.BlockSpec((COND_ROWS, D_MODEL), lambda l, j: (0, 0)),
                  pl.BlockSpec((1, D_MODEL, tn), lambda l, j: (l, 0, j)),
                  pl.BlockSpec((1, 1, tn), lambda l, j: (l, 0, j))],
        out_specs=pl.BlockSpec((1, COND_ROWS, tn), lambda l, j: (l, 0, j)),
        out_shape=jax.ShapeDtypeStruct((DEPTH, COND_ROWS, n), F32),
        compiler_params=_params("parallel", "parallel"),
        name="adaln",
    )(cond, w_ada, b_ada.reshape(DEPTH, 1, n))


def _mod_spec(l, which, tm):
    return pl.BlockSpec((1, 1, 1, 3 * D_MODEL), lambda i: (l, _group_of_row(i * tm), 0, which))


def _load_cast_rows(w_hbm, layer, dst_ref, stage_ref, sem_ref):
    rows = stage_ref.shape[1]
    n_chunks = dst_ref.shape[0] // rows

    def chunk_copy(c):
        slot = c % 2
        return pltpu.make_async_copy(w_hbm.at[layer, pl.ds(c * rows, rows), :], stage_ref.at[slot],
                                     sem_ref.at[slot])

    chunk_copy(0).start()
    for c in range(n_chunks):
        if c + 1 < n_chunks:
            chunk_copy(c + 1).start()
        chunk_copy(c).wait()
        dst_ref[c * rows:(c + 1) * rows, :] = stage_ref[c % 2].astype(BF)


def _ffn_kernel(*refs, n_in, n_a_tiles, final, layer):
    x_refs = refs[:n_in]
    mod_ref, gain_ref, w1_hbm, w2_hbm = refs[n_in:n_in + 4]
    w1_ref, w2_ref, stage1_ref, stage2_ref, sem1_ref, sem2_ref = refs[-6:]
    rest = refs[n_in + 4:-6]
    i = pl.program_id(0)

    @pl.when(i == 0)
    def _():
        _load_cast_rows(w1_hbm, layer, w1_ref, stage1_ref, sem1_ref)
        _load_cast_rows(w2_hbm, layer, w2_ref, stage2_ref, sem2_ref)

    if n_in == 2:
        x = jnp.where(i < n_a_tiles, x_refs[0][...], x_refs[1][...])
    else:
        x = x_refs[0][...]
    mod = mod_ref[0, 0]
    shift, scale, gate = mod[:, :D_MODEL], mod[:, D_MODEL:2 * D_MODEL], mod[:, 2 * D_MODEL:]
    h = _rms(x, gain_ref[0]) * (1.0 + scale) + shift
    gu = jnp.dot(h.astype(BF), w1_ref[...], preferred_element_type=F32)
    a = _silu(gu[:, :D_FF]) * gu[:, D_FF:]
    y = x + gate * (0.5 * jnp.dot(a.astype(BF), w2_ref[...], preferred_element_type=F32))
    if final:
        fg_ref, oa_ref, ob_ref = rest
        yn = _rms(y, fg_ref[...])

        @pl.when(i < n_a_tiles)
        def _():
            oa_ref[...] = yn

        @pl.when(i >= n_a_tiles)
        def _():
            ob_ref[...] = yn
    else:
        rest[0][...] = y


def _ffn(xs, mods, l, which, gain, w1, w2, final_gain=None, tm=TM_FFN):
    n_a = N_CTX // tm
    row = lambda i: (i, 0)
    first = lambda i: (jnp.minimum(i, n_a - 1), 0)
    second = lambda i: (jnp.maximum(i - n_a, 0), 0)
    lay3 = lambda i: (l, 0, 0)
    x_specs = ([pl.BlockSpec((tm, D_MODEL), row)] if len(xs) == 1
               else [pl.BlockSpec((tm, D_MODEL), first), pl.BlockSpec((tm, D_MODEL), second)])
    in_specs = x_specs + [_mod_spec(l, which, tm),
                          pl.BlockSpec((1, 1, D_MODEL), lay3),
                          _any_spec(), _any_spec()]
    args = list(xs) + [mods, gain.reshape(DEPTH, 1, D_MODEL), w1, w2]
    if final_gain is None:
        out_specs = pl.BlockSpec((tm, D_MODEL), row)
        out_shape = jax.ShapeDtypeStruct((N_TOK, D_MODEL), F32)
    else:
        in_specs.append(pl.BlockSpec((1, D_MODEL), lambda i: (0, 0)))
        args.append(final_gain.reshape(1, D_MODEL))
        out_specs = [pl.BlockSpec((tm, D_MODEL), first), pl.BlockSpec((tm, D_MODEL), second)]
        out_shape = [jax.ShapeDtypeStruct((N_CTX, D_MODEL), F32), jax.ShapeDtypeStruct((N_LAT, D_MODEL), F32)]
    return pl.pallas_call(
        functools.partial(_ffn_kernel, n_in=len(xs), n_a_tiles=n_a, final=final_gain is not None, layer=l),
        grid=(N_TOK // tm,),
        in_specs=in_specs, out_specs=out_specs, out_shape=out_shape,
        scratch_shapes=[pltpu.VMEM((D_MODEL, 2 * D_FF), BF), pltpu.VMEM((D_FF, D_MODEL), BF),
                        pltpu.VMEM((2, D_MODEL // FFN_LOAD_CHUNKS, 2 * D_FF), F32),
                        pltpu.VMEM((2, D_FF // FFN_LOAD_CHUNKS, D_MODEL), F32),
                        pltpu.SemaphoreType.DMA((2,)), pltpu.SemaphoreType.DMA((2,))],
        compiler_params=_params("arbitrary"),
        name="ffn",
    )(*args)


def _kv_expand(ckv_n, misc, wk, wv):
    kin = jnp.concatenate([ckv_n, misc], axis=1).astype(BF)
    kk = jnp.dot(kin, wk, preferred_element_type=F32)
    vv = jnp.dot(ckv_n.astype(BF), wv, preferred_element_type=F32)
    return kk.astype(BF), vv.astype(BF)


def _proj_kernel(*refs, n_ctx_tiles):
    (x_ref, mod_ref, gain_ref, win_ref, qg_ref, wqb_ref, kvg_ref, wk_ref, wv_ref, tab_ref,
     alog_ref, dtb_ref) = refs[:12]
    (q_ref, misc_ref, kmla_ref, vmla_ref, sq_ref, sk_ref, sv_ref, g3_ref, gz_ref,
     c_ckv_ref, c_krope_ref, c_sk_ref, c_sv_ref) = refs[-13:]
    i = pl.program_id(0)
    mod = mod_ref[0, 0]
    shift, scale = mod[:, :D_MODEL], mod[:, D_MODEL:2 * D_MODEL]
    hb = (_rms(x_ref[...], gain_ref[0]) * (1.0 + scale) + shift).astype(BF)
    tab = tab_ref[...]

    def project(c0, c1):
        return _nt(hb, win_ref[0, c0:c1, :])

    lane = lax.broadcasted_iota(jnp.int32, (x_ref.shape[0], LANES), 1)

    def rotate_half(t, quarter):
        even = (lane // quarter) % 2 == 0
        tiles = [t[:, c:c + LANES] for c in range(0, t.shape[1], LANES)]
        out = [jnp.where(even, -pltpu.roll(v, LANES - quarter, 1), pltpu.roll(v, quarter, 1)) for v in tiles]
        return out[0] if len(out) == 1 else jnp.concatenate(out, axis=1)

    u_lora = project(_C_CQ, _C_SQ)
    u_misc = project(_C_MA, _C_END)
    u_sq = project(_C_SQ, _C_SK)
    u_skv = project(_C_SK, _C_G3)

    qn = _rms(u_lora[:, :MLA_Q_LORA], qg_ref[0])
    ckv_n = _rms(u_lora[:, MLA_Q_LORA:], kvg_ref[0])

    m = (u_misc * tab[:, _T_CA:_T_CA + LANES]
         + rotate_half(u_misc, MLA_ROPE // 4) * tab[:, _T_SA:_T_SA + LANES])
    z = m + dtb_ref[0]
    softplus = jnp.maximum(z, 0.0) + jnp.log(1.0 + jnp.exp(-jnp.abs(z)))
    decay = -jnp.exp(alog_ref[0]) * softplus
    strength = 1.0 / (1.0 + jnp.exp(-m))
    misc = jnp.where((lane >= _M_G) & (lane < _M_B), decay,
                     jnp.where((lane >= _M_B) & (lane < _M_END), strength, m))
    misc_ref[...] = misc

    n_sq = SWA_HEADS * SWA_HD
    cos_s = tab[:, _T_CS:_T_CS + LANES]
    sin_s = tab[:, _T_SS:_T_SS + LANES]
    cos4 = jnp.concatenate([cos_s] * (n_sq // LANES), axis=1)
    sin4 = jnp.concatenate([sin_s] * (n_sq // LANES), axis=1)
    sq_ref[...] = ((u_sq * cos4 + rotate_half(u_sq, SWA_HD // 4) * sin4) * (SWA_HD ** -0.5 * LOG2_E)).astype(BF)
    sk = u_skv[:, :LANES] * cos_s + rotate_half(u_skv[:, :LANES], SWA_HD // 4) * sin_s
    sv = u_skv[:, LANES:]
    sk_ref[...] = sk
    sv_ref[...] = sv

    @pl.when(i < n_ctx_tiles)
    def _():
        for j in range(c_ckv_ref.shape[0]):
            rows = slice(j * SEQ, (j + 1) * SEQ)
            c_ckv_ref[j, 0] = ckv_n[rows]
            c_krope_ref[j, 0] = misc[rows, :MLA_ROPE]
            c_sk_ref[j, 0] = sk[rows]
            c_sv_ref[j, 0] = sv[rows]

    g3_ref[...] = project(_C_G3, _C_GZ)
    gz_ref[...] = project(_C_GZ, _C_MA)

    q2 = jnp.dot(qn.astype(BF), wqb_ref[0], preferred_element_type=F32)
    cosq = jnp.concatenate([tab[:, _T_CQ:_T_CQ + LANES]] * MLA_HEADS, axis=1)
    sinq = jnp.concatenate([tab[:, _T_SQ:_T_SQ + LANES]] * MLA_HEADS, axis=1)
    q_ref[...] = ((q2 * cosq + rotate_half(q2, MLA_ROPE // 4) * sinq) * MLA_Q_SCALE).astype(BF)
    kk, vv = _kv_expand(ckv_n, misc, wk_ref[0], wv_ref[0])
    kmla_ref[...] = kk
    vmla_ref[...] = vv


def _proj(x, mods, l, gain, win, qg, wqb, kvg, wk, wv, tab, alog, dtb, cache_prev, tm=TM_PROJ):
    assert tm % SEQ == 0
    n_ctx_tiles = N_CTX // tm
    lat_tiles = DEC_SEQ // tm
    lay3 = lambda i: (l, 0, 0)
    row = lambda i: (i, 0)

    def tab_map(i):
        return (jnp.where(i < n_ctx_tiles, 0, 1 + (i - n_ctx_tiles) % lat_tiles), 0)

    def lay_spec(a):
        return pl.BlockSpec((1,) + a.shape[1:], lay3)

    widths = [(MLA_HEADS * LANES, BF), (LANES, F32), (MLA_HEADS * LANES, BF),
              (MLA_HEADS * MLA_V, BF), (SWA_HEADS * SWA_HD, BF), (SWA_KV_HEADS * SWA_HD, F32),
              (SWA_KV_HEADS * SWA_HD, F32), (GDN_CONV_CH, F32), (GDN_V_DIM, F32)]
    out_specs = [pl.BlockSpec((tm, w), row) for w, _ in widths]
    out_shape = [jax.ShapeDtypeStruct((N_TOK, w), dt) for w, dt in widths]
    for w in _CACHE_WIDTHS:
        out_specs.append(pl.BlockSpec((tm // SEQ, 1, SEQ, w),
                                      lambda i: (jnp.minimum(i, n_ctx_tiles - 1), l, 0, 0)))
        out_shape.append(jax.ShapeDtypeStruct((BATCH, DEPTH, SEQ, w), F32))
    qg, kvg = qg.reshape(DEPTH, 1, -1), kvg.reshape(DEPTH, 1, -1)
    args = [x, mods, gain.reshape(DEPTH, 1, D_MODEL), win, qg, wqb, kvg, wk, wv, tab, alog, dtb]
    in_specs = [pl.BlockSpec((tm, D_MODEL), row), _mod_spec(l, 1, tm),
                pl.BlockSpec((1, 1, D_MODEL), lay3), lay_spec(win), lay_spec(qg), lay_spec(wqb),
                lay_spec(kvg), lay_spec(wk), lay_spec(wv), pl.BlockSpec((tm, _T_END), tab_map),
                lay_spec(alog), lay_spec(dtb)]
    aliases = {}
    if cache_prev is not None:
        for j, prev in enumerate(cache_prev):
            aliases[len(args)] = len(widths) + j
            args.append(prev)
            in_specs.append(_any_spec())
    return pl.pallas_call(
        functools.partial(_proj_kernel, n_ctx_tiles=n_ctx_tiles),
        grid=(N_TOK // tm,),
        in_specs=in_specs, out_specs=out_specs, out_shape=out_shape,
        input_output_aliases=aliases,
        compiler_params=_params("arbitrary"),
        name="proj",
    )(*args)


def _kv_cache_kernel(ckv_ref, misc_ref, wk_ref, wv_ref, k_ref, v_ref):
    kk, vv = _kv_expand(ckv_ref[0, 0], misc_ref[0, 0], wk_ref[0], wv_ref[0])
    k_ref[...] = kk
    v_ref[...] = vv


def _kv_cache(cache_ckv, cache_misc, l, wk, wv):
    lay3 = lambda b: (l, 0, 0)
    n = DEC_BATCH * PAST_LEN
    return pl.pallas_call(
        _kv_cache_kernel,
        grid=(DEC_BATCH,),
        in_specs=[pl.BlockSpec((1, 1, PAST_LEN, MLA_KV_LORA), lambda b: (b, l, 0, 0)),
                  pl.BlockSpec((1, 1, PAST_LEN, LANES), lambda b: (b, l, 0, 0)),
                  pl.BlockSpec((1,) + wk.shape[1:], lay3),
                  pl.BlockSpec((1,) + wv.shape[1:], lay3)],
        out_specs=[pl.BlockSpec((PAST_LEN, MLA_HEADS * LANES), lambda b: (b, 0)),
                   pl.BlockSpec((PAST_LEN, MLA_HEADS * MLA_V), lambda b: (b, 0))],
        out_shape=[jax.ShapeDtypeStruct((n, MLA_HEADS * LANES), BF),
                   jax.ShapeDtypeStruct((n, MLA_HEADS * MLA_V), BF)],
        compiler_params=_params("parallel"),
        name="kv_cache",
    )(cache_ckv, cache_misc, wk, wv)


def _softmax_numerator(s_ref, p_ref, sink=None, row_block=128):
    rows, cols = s_ref.shape
    mx = jnp.max(s_ref[...], axis=-1, keepdims=True)
    if sink is not None:
        mx = jnp.maximum(mx, sink)
    sums = []
    for r0 in range(0, rows, row_block):
        mb = jnp.broadcast_to(mx[r0:r0 + row_block], (row_block, LANES))
        part = jnp.zeros((row_block, LANES), F32)
        for c in range(0, cols, LANES):
            p = jnp.exp2(s_ref[r0:r0 + row_block, c:c + LANES] - mb)
            part = part + p
            p_ref[r0:r0 + row_block, c:c + LANES] = p.astype(BF)
        sums.append(jnp.sum(part, axis=-1, keepdims=True))
    den = jnp.concatenate(sums, axis=0)
    return den if sink is None else den + jnp.exp2(sink - mx)


def _mla_kernel(*refs, n_seg):
    q_ref = refs[0]
    k_refs = refs[1:1 + n_seg]
    v_refs = refs[1 + n_seg:1 + 2 * n_seg]
    o_ref = refs[-5]
    s_refs, p_refs = refs[-4:-2], refs[-2:]
    tq = q_ref.shape[0]
    lane = lax.broadcasted_iota(jnp.int32, (tq, LANES), 1)

    def scores(h):
        off = 0
        for k in k_refs:
            s_refs[h % 2][:, off:off + k.shape[0]] = _nt(q_ref[:, h * LANES:(h + 1) * LANES],
                                                         k[:, h * LANES:(h + 1) * LANES])
            off += k.shape[0]

    scores(0)
    outs = []
    for h in range(MLA_HEADS):
        if h + 1 < MLA_HEADS:
            scores(h + 1)
        den = _softmax_numerator(s_refs[h % 2], p_refs[h % 2])
        pair = h // 2
        acc, off = None, 0
        for v in v_refs:
            pv = jnp.dot(p_refs[h % 2][:, off:off + v.shape[0]], v[:, pair * LANES:(pair + 1) * LANES],
                         preferred_element_type=F32)
            acc = pv if acc is None else acc + pv
            off += v.shape[0]
        outs.append(acc / den)
        if h % 2 == 1:
            o_ref[:, pair * LANES:(pair + 1) * LANES] = jnp.where(lane < MLA_V, outs[-2], outs[-1]).astype(BF)


def _mla_attend(q, ks, vs, n_batch, t, row0, k_blocks, prev):
    n_seg = len(ks)
    tq = min(TQ_MLA, t)
    n_keys = sum(s for _, s in k_blocks)
    qb0 = row0 // tq
    tiles = t // tq
    q_map = lambda b, i: (qb0 + b * tiles + i, 0)
    in_specs = [pl.BlockSpec((tq, MLA_HEADS * LANES), q_map)]
    for (b0, s), width in ([(kb, MLA_HEADS * LANES) for kb in k_blocks]
                           + [(kb, MLA_HEADS * MLA_V) for kb in k_blocks]):
        in_specs.append(pl.BlockSpec((s, width), functools.partial(lambda b, i, b0: (b0 + b, 0), b0=b0)))
    args = [q, *ks, *vs]
    aliases = {}
    if prev is not None:
        args.append(prev)
        in_specs.append(_any_spec())
        aliases = {len(args) - 1: 0}
    return pl.pallas_call(
        functools.partial(_mla_kernel, n_seg=n_seg),
        grid=(n_batch, tiles),
        in_specs=in_specs,
        out_specs=pl.BlockSpec((tq, MLA_HEADS * MLA_V), q_map),
        out_shape=jax.ShapeDtypeStruct((N_TOK, MLA_HEADS * MLA_V), BF),
        input_output_aliases=aliases,
        scratch_shapes=[pltpu.VMEM((tq, n_keys), F32)] * 2 + [pltpu.VMEM((tq, n_keys), BF)] * 2,
        compiler_params=_params("parallel", "parallel"),
        name="mla_attend",
    )(*args)


def _gqa_heads(q_ref, k_segs, v_segs, masks, sink_ref, o_ref, s_refs, p_refs):
    tq = q_ref.shape[0]
    lane_q = lax.broadcasted_iota(jnp.int32, (tq, LANES), 1)
    lo_q = lane_q < SWA_HD
    k_roll = [pltpu.roll(k, SWA_HD, 1) for k in k_segs]
    v_roll = [pltpu.roll(v, SWA_HD, 1) for v in v_segs]
    heads = range(SWA_KV_HEADS)
    sinks = []
    for kvh in heads:
        tiles = [q_ref[:, (kvh * 2 + j) * LANES:(kvh * 2 + j + 1) * LANES] for j in range(2)]
        zero = jnp.zeros_like(tiles[0])
        qs = jnp.concatenate([jnp.where(lo_q, tiles[0], zero), jnp.where(lo_q, zero, tiles[0]),
                              jnp.where(lo_q, tiles[1], zero), jnp.where(lo_q, zero, tiles[1])], axis=0)
        sinks.append(jnp.concatenate(
            [sink_ref[0, kvh * SWA_GROUP + g:kvh * SWA_GROUP + g + 1, :]
             for g in range(SWA_GROUP) for _ in range(tq // LANES)], axis=1) * LOG2_E)
        off = 0
        for k, kr, msk in zip(k_segs, k_roll, masks):
            lane_k = lax.broadcasted_iota(jnp.int32, k.shape, 1)
            first = (lane_k < SWA_HD) == (kvh == 0)
            kd = jnp.where(first, k, kr)
            s = _bdot_nt(kd, qs)
            if msk is not None:
                s = jnp.where(msk, s, -1e30)
            s_refs[kvh][off:off + k.shape[0], :] = s
            off += k.shape[0]
    dens = [_softmax_numerator_t(s_refs[kvh], p_refs[kvh], sinks[kvh]) for kvh in heads]
    for kvh in heads:
        acc, off = None, 0
        for v, vr in zip(v_segs, v_roll):
            lane_v = lax.broadcasted_iota(jnp.int32, v.shape, 1)
            first = (lane_v < SWA_HD) == (kvh == 0)
            vd = jnp.where(first, v, vr)
            pv = lax.dot_general(vd.astype(BF), p_refs[kvh][off:off + v.shape[0], :],
                                 (((0,), (0,)), ((), ())), preferred_element_type=F32)
            acc = pv if acc is None else acc + pv
            off += v.shape[0]
        o = (acc / dens[kvh]).T
        for j in range(2):
            o_ref[:, (kvh * 2 + j) * LANES:(kvh * 2 + j + 1) * LANES] = jnp.where(
                lo_q, o[(2 * j) * tq:(2 * j + 1) * tq], o[(2 * j + 1) * tq:(2 * j + 2) * tq]).astype(BF)


def _softmax_numerator_t(s_ref, p_ref, sink, row_block=128):
    keys, cols = s_ref.shape
    dens = []
    for c in range(0, cols, LANES):
        mx = s_ref[0:row_block, c:c + LANES]
        for r0 in range(row_block, keys, row_block):
            mx = jnp.maximum(mx, s_ref[r0:r0 + row_block, c:c + LANES])
        snk = sink[:, c:c + LANES]
        m = jnp.maximum(jnp.max(mx, axis=0, keepdims=True), snk)
        mb = jnp.broadcast_to(m, (row_block, LANES))
        part = jnp.zeros((row_block, LANES), F32)
        for r0 in range(0, keys, row_block):
            p = jnp.exp2(s_ref[r0:r0 + row_block, c:c + LANES] - mb)
            part = part + p
            p_ref[r0:r0 + row_block, c:c + LANES] = p.astype(BF)
        dens.append(jnp.sum(part, axis=0, keepdims=True) + jnp.exp2(snk - m))
    return jnp.concatenate(dens, axis=1)


def _swa_scratch(keys, rows):
    return ([pltpu.VMEM((keys, rows), F32)] * SWA_KV_HEADS + [pltpu.VMEM((keys, rows), BF)] * SWA_KV_HEADS)


def _swa_ctx_kernel(q_ref, k_ref, v_ref, sink_ref, o_ref, *scratch):
    _gqa_heads(q_ref, [k_ref[...]], [v_ref[...]], [None], sink_ref, o_ref,
               scratch[:SWA_KV_HEADS], scratch[SWA_KV_HEADS:])


def _swa_ctx(sq, sk, sv, sink_rows, l):
    return pl.pallas_call(
        _swa_ctx_kernel,
        grid=(BATCH,),
        in_specs=[pl.BlockSpec((SEQ, SWA_HEADS * SWA_HD), lambda b: (b, 0)),
                  pl.BlockSpec((SEQ, LANES), lambda b: (b, 0)),
                  pl.BlockSpec((SEQ, LANES), lambda b: (b, 0)),
                  pl.BlockSpec((1, SWA_HEADS, LANES), lambda b: (l, 0, 0))],
        out_specs=pl.BlockSpec((SEQ, SWA_HEADS * SWA_HD), lambda b: (b, 0)),
        out_shape=jax.ShapeDtypeStruct((N_TOK, SWA_HEADS * SWA_HD), BF),
        scratch_shapes=_swa_scratch(SEQ, SWA_GROUP * SEQ),
        compiler_params=_params("parallel"),
        name="swa_ctx",
    )(sq, sk, sv, sink_rows)


def _swa_lat_kernel(q_ref, kp_ref, kc_ref, kn_ref, vp_ref, vc_ref, vn_ref, kx_ref, vx_ref, sink_ref, prev_ref,
                    o_ref, *scratch):
    w = SWA_WINDOW
    n = pl.program_id(1)
    nb = pl.num_programs(1)
    k_band = jnp.concatenate([kp_ref[...], kc_ref[...], kn_ref[...]], axis=0)
    v_band = jnp.concatenate([vp_ref[...], vc_ref[...], vn_ref[...]], axis=0)
    rows = SWA_GROUP * w
    r = lax.broadcasted_iota(jnp.int32, (3 * w, rows), 1) & (w - 1)
    c = lax.broadcasted_iota(jnp.int32, (3 * w, rows), 0)
    valid = (c >= r) & (c <= r + 2 * w) & ((c >= w) | (n > 0)) & ((c < 2 * w) | (n < nb - 1))
    _gqa_heads(q_ref, [k_band, kx_ref[0, 0]], [v_band, vx_ref[0, 0]], [valid, None], sink_ref, o_ref,
               scratch[:SWA_KV_HEADS], scratch[SWA_KV_HEADS:])


def _swa_lat(sq, sk, sv, k_cache, v_cache, sink_rows, l, prev):
    w = SWA_WINDOW
    nb = DEC_SEQ // w
    q0 = N_CTX // w

    def blk(d):
        return lambda b, n: (q0 + b * nb + jnp.clip(n + d, 0, nb - 1), 0)

    kv_specs = [pl.BlockSpec((w, LANES), blk(d)) for d in (-1, 0, 1)]
    cache_spec = pl.BlockSpec((1, 1, PAST_LEN, LANES), lambda b, n: (b, l, 0, 0))
    return pl.pallas_call(
        _swa_lat_kernel,
        grid=(DEC_BATCH, nb),
        in_specs=[pl.BlockSpec((w, SWA_HEADS * SWA_HD), blk(0))] + kv_specs + kv_specs
        + [cache_spec, cache_spec, pl.BlockSpec((1, SWA_HEADS, LANES), lambda b, n: (l, 0, 0)), _any_spec()],
        out_specs=pl.BlockSpec((w, SWA_HEADS * SWA_HD), blk(0)),
        out_shape=jax.ShapeDtypeStruct((N_TOK, SWA_HEADS * SWA_HD), BF),
        input_output_aliases={10: 0},
        scratch_shapes=_swa_scratch(3 * w + PAST_LEN, SWA_GROUP * w),
        compiler_params=_params("parallel", "parallel"),
        name="swa_lat",
    )(sq, sk, sk, sk, sv, sv, sv, k_cache, v_cache, sink_rows, prev)


def _gdn_conv_kernel(*refs):
    x_ref, w_ref, o_ref = refs[0], refs[1], refs[-1]
    x = x_ref[...]
    t = x.shape[0]
    w = w_ref[0]
    r8 = lax.broadcasted_iota(jnp.int32, (SUBLANES, x.shape[1]), 0)
    half = GDN_CONV // 2
    acc = x * w[half:half + 1, :]
    for k in range(GDN_CONV):
        d = k - half
        if d == 0:
            continue
        xs = pltpu.roll(x, (-d) % t, 0)
        top, bot = xs[:SUBLANES], xs[t - SUBLANES:]
        if d < 0:
            top = jnp.where(r8 + d >= 0, top, 0.0)
        else:
            bot = jnp.where(r8 + d < SUBLANES, bot, 0.0)
        xs = jnp.concatenate([top, xs[SUBLANES:t - SUBLANES], bot], axis=0)
        acc = acc + xs * w[k:k + 1, :]
    y = _silu(acc)
    is_qk = pl.program_id(1) < 2
    cols = []
    for h in range(y.shape[1] // LANES):
        yh = y[:, h * LANES:(h + 1) * LANES]
        nrm = lax.rsqrt(jnp.sum(yh * yh, axis=-1, keepdims=True) + 1e-6)
        cols.append(yh * jnp.where(is_qk, nrm, 1.0))
    o_ref[...] = jnp.concatenate(cols, axis=1)


def _gdn_conv(g3, conv_w, l, row0, n_seq, t, prev):
    blk0 = row0 // t
    args = [g3, conv_w]
    in_specs = [pl.BlockSpec((t, GDN_QK_DIM), lambda s, j: (blk0 + s, j)),
                pl.BlockSpec((1, GDN_CONV, GDN_QK_DIM), lambda s, j: (l, 0, j))]
    aliases = {}
    if prev is not None:
        args.append(prev)
        in_specs.append(_any_spec())
        aliases = {2: 0}
    return pl.pallas_call(
        _gdn_conv_kernel,
        grid=(n_seq, 3),
        in_specs=in_specs,
        out_specs=pl.BlockSpec((t, GDN_QK_DIM), lambda s, j: (blk0 + s, j)),
        out_shape=jax.ShapeDtypeStruct((N_TOK, GDN_CONV_CH), F32),
        input_output_aliases=aliases,
        compiler_params=_params("parallel", "parallel"),
        name="gdn_conv",
    )(*args)


def _gate_rows(misc):
    shape = (2 * GDN_NH, LANES)
    sel = (lax.broadcasted_iota(jnp.int32, shape, 1)
           == lax.broadcasted_iota(jnp.int32, shape, 0) + _M_G).astype(BF)
    b1, b2, b3 = _split3(misc)
    return _nt(sel, b1) + (_nt(sel, b2) + _nt(sel, b3))


def _gdn_chunk_pairs(pairs):
    c = GDN_CHUNK
    shape = (c, 2 * c)
    ri = lax.broadcasted_iota(jnp.int32, shape, 0)
    lane = lax.broadcasted_iota(jnp.int32, shape, 1)
    cj = lane & (c - 1)
    fwd_half = lane < c
    lower, upper = ri >= cj, ri <= cj
    incl = (fwd_half & lower) | (~fwd_half & upper)
    incl_t = (fwd_half & upper) | (~fwd_half & lower)
    strict = incl & (ri != cj)
    incl_f, incl_b = incl & fwd_half, incl & ~fwd_half
    eye = (ri == cj).astype(F32)
    n = len(pairs)
    rng = range(n)
    fw = [p[0] for p in pairs]
    bw = [p[1] for p in pairs]

    def halves(a, b):
        return jnp.where(fwd_half, a, b)

    def split_rows(m):
        return jnp.concatenate([jnp.where(fwd_half, m, 0.0), jnp.where(fwd_half, 0.0, m)], axis=0)

    def pair_dot3(x, p):
        xh = x.astype(BF)
        xl = (x - xh.astype(F32)).astype(BF)
        phf = p.astype(BF).astype(F32)
        bd_hi = split_rows(phf).astype(BF)
        bd_lo = split_rows(p - phf).astype(BF)
        return jnp.dot(jnp.concatenate([xh, xl, xh], axis=1), jnp.concatenate([bd_hi, bd_hi, bd_lo], axis=0),
                       preferred_element_type=F32)

    g_col = [halves(fw[i]["g_col"], bw[i]["g_col"]) for i in rng]
    gc_row = [jnp.sum(jnp.where(incl_t, g_col[i], 0.0), axis=0, keepdims=True) for i in rng]
    gcf = [jnp.sum(jnp.where(incl_f, pairs[i][2], 0.0), axis=1, keepdims=True) for i in rng]
    gcb = [jnp.sum(jnp.where(incl_b, pairs[i][2], 0.0), axis=1, keepdims=True) for i in rng]
    decay = [jnp.where(incl, jnp.exp(halves(gcf[i], gcb[i]) - gc_row[i]), 0.0) for i in rng]
    qf = [p["q"] * (GDN_DK ** -0.5) for p in fw]
    qb = [p["q"] * (GDN_DK ** -0.5) for p in bw]
    kbf = [p["k"] * p["beta"] for p in fw]
    kbb = [p["k"] * p["beta"] for p in bw]
    z = jnp.zeros((c, GDN_DK), F32)
    kq = [_bdot_nt(jnp.concatenate([jnp.concatenate([kbf[i], kbb[i]], axis=1),
                                    jnp.concatenate([qf[i], qb[i]], axis=1)], axis=0),
                   jnp.concatenate([jnp.concatenate([fw[i]["k"], z], axis=1),
                                    jnp.concatenate([z, bw[i]["k"]], axis=1)], axis=0)) for i in rng]
    pw = [jnp.where(strict, -(kq[i][:c] * decay[i]), 0.0) for i in rng]
    inv = [eye + m for m in pw]
    levels = int(np.log2(c)) - 1
    pw = [pair_dot3(m, m) for m in pw]
    for level in range(levels):
        if level + 1 < levels:
            both = [pair_dot3(jnp.concatenate([inv[i], pw[i]], axis=0), pw[i]) for i in rng]
            inv = [inv[i] + both[i][:c] for i in rng]
            pw = [both[i][c:] for i in rng]
        else:
            inv = [inv[i] + pair_dot3(inv[i], pw[i]) for i in rng]
    ef = [jnp.exp(g) for g in gcf]
    eb = [jnp.exp(g) for g in gcb]
    uw = [_bdot(split_rows(inv[i]),
                jnp.concatenate([jnp.concatenate([fw[i]["v"] * fw[i]["beta"], kbf[i] * ef[i]], axis=1),
                                 jnp.concatenate([bw[i]["v"] * bw[i]["beta"], kbb[i] * eb[i]], axis=1)], axis=0))
          for i in rng]
    a = [jnp.where(incl, kq[i][c:] * decay[i], 0.0) for i in rng]
    glf = [g[c - 1:c, :] for g in gcf]
    glb = [g[0:1, :] for g in gcb]
    wqf = [_bdot(jnp.concatenate([uw[i][:c, GDN_DV:], qf[i] * ef[i]], axis=0), fw[i]["state"]) for i in rng]
    wqb = [_bdot(jnp.concatenate([uw[i][c:, GDN_DV:], qb[i] * eb[i]], axis=0), bw[i]["state"]) for i in rng]
    vnf = [uw[i][:c, :GDN_DV] - wqf[i][:c] for i in rng]
    vnb = [uw[i][c:, :GDN_DV] - wqb[i][:c] for i in rng]
    av = [_bdot(split_rows(a[i]), jnp.concatenate([vnf[i], vnb[i]], axis=0)) for i in rng]
    sf = [fw[i]["state"] * jnp.exp(glf[i]) + _bdot_tn(fw[i]["k"] * jnp.exp(glf[i] - gcf[i]), vnf[i]) for i in rng]
    sb = [bw[i]["state"] * jnp.exp(glb[i]) + _bdot_tn(bw[i]["k"] * jnp.exp(glb[i] - gcb[i]), vnb[i]) for i in rng]
    return [(wqf[i][c:] + av[i][:c], sf[i], wqb[i][c:] + av[i][c:], sb[i]) for i in rng]


def _gdn_kernel(*refs, context, n_par):
    qf_ref, qb_ref, mf_ref, mb_ref = refs[:4]
    st_ref = refs[-1]
    if context:
        of_ref, ob_ref, sfin_ref = refs[-4:-1]
    else:
        s0_ref = refs[4]
        of_ref, ob_ref = refs[-3:-1]
    n = pl.program_id(1)

    @pl.when(n == 0)
    def _():
        if context:
            st_ref[...] = jnp.zeros_like(st_ref)
        else:
            st_ref[...] = s0_ref[:, 0]

    def problem(qkv, misc, s, h, d):
        i = d * GDN_HEADS + h
        return dict(q=qkv[:, h * GDN_DK:(h + 1) * GDN_DK],
                    k=qkv[:, GDN_QK_DIM + h * GDN_DK:GDN_QK_DIM + (h + 1) * GDN_DK],
                    v=qkv[:, 2 * GDN_QK_DIM + h * GDN_DV:2 * GDN_QK_DIM + (h + 1) * GDN_DV],
                    g_col=misc[:, _M_G + i:_M_G + i + 1], beta=misc[:, _M_B + i:_M_B + i + 1],
                    state=st_ref[s, i])

    lane = lax.broadcasted_iota(jnp.int32, (1, 2 * GDN_CHUNK), 1)
    pairs = []
    for s in range(n_par):
        qkv_f, qkv_b, misc_f, misc_b = qf_ref[s], qb_ref[s], mf_ref[s], mb_ref[s]
        rows = _gate_rows(jnp.concatenate([misc_f, misc_b], axis=0))
        for h in range(GDN_HEADS):
            g_row = jnp.where(lane < GDN_CHUNK, rows[h:h + 1, :], rows[GDN_HEADS + h:GDN_HEADS + h + 1, :])
            pairs.append((problem(qkv_f, misc_f, s, h, 0), problem(qkv_b, misc_b, s, h, 1), g_row))
    results = _gdn_chunk_pairs(pairs)
    for s in range(n_par):
        for h in range(GDN_HEADS):
            o_f, s_f, o_b, s_b = results[s * GDN_HEADS + h]
            of_ref[s, :, h * GDN_DV:(h + 1) * GDN_DV] = o_f
            ob_ref[s, :, h * GDN_DV:(h + 1) * GDN_DV] = o_b
            st_ref[s, h] = s_f
            st_ref[s, GDN_HEADS + h] = s_b

    if context:
        @pl.when(n == pl.num_programs(1) - 1)
        def _():
            sfin_ref[:, 0] = st_ref[...]


def _gdn(qkv, misc, l, t, n_seq, seq0, s0, prevs, n_par=GDN_PAR):
    c = GDN_CHUNK
    nc = t // c
    context = s0 is None
    sb0 = seq0 // n_par
    n_all = N_TOK // t
    fwd = lambda s, n: (sb0 + s, n, 0)
    bwd = lambda s, n: (sb0 + s, nc - 1 - n, 0)
    st_spec = pl.BlockSpec((n_par, 1, GDN_NH, GDN_DK, GDN_DV), lambda s, n: (s, l, 0, 0, 0))
    qkv3 = qkv.reshape(n_all, t, GDN_CONV_CH)
    misc3 = misc.reshape(n_all, t, LANES)
    in_specs = [pl.BlockSpec((n_par, c, GDN_CONV_CH), fwd), pl.BlockSpec((n_par, c, GDN_CONV_CH), bwd),
                pl.BlockSpec((n_par, c, LANES), fwd), pl.BlockSpec((n_par, c, LANES), bwd)]
    args = [qkv3, qkv3, misc3, misc3]
    if not context:
        in_specs.append(st_spec)
        args.append(s0)
    out_specs = [pl.BlockSpec((n_par, c, GDN_V_DIM), fwd), pl.BlockSpec((n_par, c, GDN_V_DIM), bwd)]
    out_shape = [jax.ShapeDtypeStruct((n_all, t, GDN_V_DIM), F32)] * 2
    if context:
        out_specs.append(st_spec)
        out_shape.append(jax.ShapeDtypeStruct((n_seq, DEPTH, GDN_NH, GDN_DK, GDN_DV), F32))
    aliases = {}
    for j, p in enumerate(prevs):
        if p is not None:
            aliases[len(args)] = j
            args.append(p.reshape(out_shape[j].shape))
            in_specs.append(_any_spec())
    return pl.pallas_call(
        functools.partial(_gdn_kernel, context=context, n_par=n_par),
        grid=(n_seq // n_par, nc),
        in_specs=in_specs, out_specs=out_specs, out_shape=out_shape,
        input_output_aliases=aliases,
        scratch_shapes=[pltpu.VMEM((n_par, GDN_NH, GDN_DK, GDN_DV), F32)],
        compiler_params=_params("parallel", "arbitrary"),
        name="gdn",
    )(*args)


def _out_kernel(x_ref, mod_ref, om_ref, os_ref, gf_ref, gb_ref, gz_ref, gn_ref, w_ref, o_ref):
    gate = mod_ref[0, 0][:, 2 * D_MODEL:]
    s = gf_ref[...] + gb_ref[...]
    gz = gz_ref[...]
    cols = []
    for h in range(GDN_HEADS):
        sh = s[:, h * GDN_DV:(h + 1) * GDN_DV]
        cols.append(_rms(sh, gn_ref[0]) * _silu(gz[:, h * GDN_DV:(h + 1) * GDN_DV]))
    og = jnp.concatenate(cols, axis=1)
    n_m = MLA_HEADS * MLA_V
    n_s = SWA_HEADS * SWA_HD
    y = (jnp.dot(om_ref[...], w_ref[0, :n_m, :], preferred_element_type=F32)
         + jnp.dot(os_ref[...], w_ref[0, n_m:n_m + n_s, :], preferred_element_type=F32)
         + jnp.dot(og.astype(BF), w_ref[0, n_m + n_s:, :], preferred_element_type=F32))
    o_ref[...] = x_ref[...] + gate * y


def _out_proj(x, mods, l, o_mla, o_swa, o_gf, o_gb, gz, gdn_norm, w_out, tm=TM_OUT):
    row = lambda i: (i, 0)
    lay3 = lambda i: (l, 0, 0)
    return pl.pallas_call(
        _out_kernel,
        grid=(N_TOK // tm,),
        in_specs=[pl.BlockSpec((tm, D_MODEL), row),
                  _mod_spec(l, 1, tm),
                  pl.BlockSpec((tm, MLA_HEADS * MLA_V), row),
                  pl.BlockSpec((tm, SWA_HEADS * SWA_HD), row),
                  pl.BlockSpec((tm, GDN_V_DIM), row),
                  pl.BlockSpec((tm, GDN_V_DIM), row),
                  pl.BlockSpec((tm, GDN_V_DIM), row),
                  pl.BlockSpec((1, 1, GDN_DV), lay3),
                  pl.BlockSpec((1,) + w_out.shape[1:], lay3)],
        out_specs=pl.BlockSpec((tm, D_MODEL), row),
        out_shape=jax.ShapeDtypeStruct((N_TOK, D_MODEL), F32),
        compiler_params=_params("parallel"),
        name="out_proj",
    )(x, mods, o_mla, o_swa, o_gf, o_gb, gz, gdn_norm.reshape(DEPTH, 1, GDN_DV), w_out)


def _axial_rope(n_tokens, dim):
    f32 = np.float32
    rows = n_tokens // GRID_W
    row = np.repeat(np.arange(rows, dtype=f32), GRID_W)
    col = np.tile(np.arange(GRID_W, dtype=f32), rows)
    axis_dim = dim // 2
    inv_freq = (f32(1.0) / (f32(ROPE_BASE) ** (np.arange(0, axis_dim, 2, dtype=f32) / f32(axis_dim)))).astype(f32)
    ang_r = row[:, None] * inv_freq[None, :]
    ang_c = col[:, None] * inv_freq[None, :]
    ang = np.concatenate([ang_r, ang_r, ang_c, ang_c], axis=-1).astype(f32)
    return np.cos(ang).astype(f32), np.sin(ang).astype(f32)


def _rope_table(tm):
    cos_m, sin_m = _axial_rope(DEC_SEQ, MLA_ROPE)
    cos_s, sin_s = _axial_rope(DEC_SEQ, SWA_HD)
    t = DEC_SEQ
    one = lambda w: np.ones((t, w), np.float32)
    zero = lambda w: np.zeros((t, w), np.float32)
    lat = np.concatenate([
        cos_s, cos_s, sin_s, sin_s,
        cos_m, one(LANES - MLA_ROPE), sin_m, zero(LANES - MLA_ROPE),
        one(MLA_NOPE), cos_m, one(LANES - MLA_NOPE - MLA_ROPE),
        zero(MLA_NOPE), sin_m, zero(LANES - MLA_NOPE - MLA_ROPE)], axis=1)
    ident_row = np.concatenate([np.ones(LANES), np.zeros(LANES)] * 3).astype(np.float32)
    ident = np.broadcast_to(ident_row[None, :], (tm, _T_END))
    return jnp.asarray(np.concatenate([ident, lat], axis=0))


def _mixer_weights(w_in, mla_w_qb, mla_w_kvb):
    nl = DEPTH
    offs = np.cumsum([0, MLA_Q_LORA, MLA_KV_LORA, MLA_ROPE, SWA_HEADS * SWA_HD, SWA_KV_HEADS * SWA_HD,
                      SWA_KV_HEADS * SWA_HD, GDN_CONV_CH, GDN_V_DIM, 2 * GDN_NH])
    w_t = jnp.swapaxes(w_in, 1, 2)
    cq_ckv, krope, body, gates = (w_t[:, :offs[2]], w_t[:, offs[2]:offs[3]], w_t[:, offs[3]:offs[8]],
                                  w_t[:, offs[8]:])
    misc = jnp.concatenate([krope, gates, jnp.zeros((nl, LANES - _M_END, D_MODEL), F32)], axis=1)
    win = jnp.concatenate([cq_ckv, body, misc], axis=1).astype(BF)

    r = MLA_Q_LORA
    wq = mla_w_qb.reshape(nl, r, MLA_HEADS, MLA_NOPE + MLA_ROPE)
    pad = jnp.zeros((nl, r, MLA_HEADS, LANES - MLA_NOPE - MLA_ROPE), F32)
    wqb = jnp.concatenate([wq, pad], axis=-1).reshape(nl, r, MLA_HEADS * LANES).astype(BF)

    kvb = mla_w_kvb.reshape(nl, MLA_KV_LORA, MLA_HEADS, MLA_NOPE + MLA_V)
    k_nope = jnp.concatenate([kvb[..., :MLA_NOPE],
                              jnp.zeros((nl, MLA_KV_LORA, MLA_HEADS, LANES - MLA_NOPE), F32)],
                             axis=-1).reshape(nl, MLA_KV_LORA, MLA_HEADS * LANES)
    place = np.zeros((LANES, MLA_HEADS, LANES), np.float32)
    for i in range(MLA_ROPE):
        place[i, :, MLA_NOPE + i] = 1.0
    place = jnp.broadcast_to(jnp.asarray(place.reshape(1, LANES, MLA_HEADS * LANES)),
                             (nl, LANES, MLA_HEADS * LANES))
    wk = jnp.concatenate([k_nope, place], axis=1).astype(BF)
    wv = kvb[..., MLA_NOPE:].reshape(nl, MLA_KV_LORA, MLA_HEADS * MLA_V).astype(BF)
    return win, wqb, wk, wv


def _misc_rows(vals):
    rows = jnp.zeros((DEPTH, 1, LANES), F32)
    return rows.at[:, 0, _M_G:_M_B].set(vals.reshape(DEPTH, GDN_NH).astype(F32))


def kernel(x_prompt, x_sample, cache_mla_ckv, cache_mla_krope, cache_swa_k, cache_swa_v, state_gdn, c, c_ctx,
           w_ada, b_ada, norm_ffn1, ffn1_w1, ffn1_w2, norm_mix, w_in, mla_q_norm, mla_w_qb, mla_kv_norm,
           mla_w_kvb, swa_sink, gdn_conv_w, gdn_a_log, gdn_dt_bias, gdn_norm, w_out, norm_ffn2, ffn2_w1,
           ffn2_w2, final_norm):
    cond = jnp.concatenate([c_ctx[None, :], c, jnp.zeros((COND_ROWS - N_GROUPS, D_MODEL), F32)], axis=0)
    mods = _adaln(cond, w_ada, b_ada)[:, :N_GROUPS].reshape(DEPTH, N_GROUPS, 1, N_MOD * D_MODEL)
    tab = _rope_table(TM_PROJ)
    win, wqb, wk, wv = _mixer_weights(w_in, mla_w_qb, mla_w_kvb)
    w11, w12, w21, w22 = ffn1_w1, ffn1_w2, ffn2_w1, ffn2_w2
    wo = w_out.astype(BF)
    alog, dtb = _misc_rows(gdn_a_log), _misc_rows(gdn_dt_bias)
    sink_rows = jnp.broadcast_to(swa_sink[:, :, None], (DEPTH, SWA_HEADS, LANES))
    cache_misc = jnp.pad(cache_mla_krope, ((0, 0), (0, 0), (0, 0), (0, LANES - MLA_ROPE)))
    cache_k = cache_swa_k.reshape(DEC_BATCH, DEPTH, PAST_LEN, LANES)
    cache_v = cache_swa_v.reshape(DEC_BATCH, DEPTH, PAST_LEN, LANES)
    s0 = state_gdn.reshape(DEC_BATCH, DEPTH, GDN_NH, GDN_DK, GDN_DV)

    xs = [x_prompt.reshape(N_CTX, D_MODEL), x_sample.reshape(N_LAT, D_MODEL)]
    caches = new_st = None
    for l in range(DEPTH):
        x = _ffn(xs, mods, l, 0, norm_ffn1, w11, w12)
        (q_mla, misc, k_mla, v_mla, sq, sk, sv, g3, gz, *caches) = _proj(
            x, mods, l, norm_mix, win, mla_q_norm, wqb, mla_kv_norm, wk, wv, tab, alog, dtb, caches)

        k_c, v_c = _kv_cache(cache_mla_ckv, cache_misc, l, wk, wv)
        o_mla = _mla_attend(q_mla, [k_mla], [v_mla], BATCH, SEQ, 0, [(0, SEQ)], None)
        o_mla = _mla_attend(q_mla, [k_mla, k_c], [v_mla, v_c], DEC_BATCH, DEC_SEQ, N_CTX,
                            [(N_CTX // DEC_SEQ, DEC_SEQ), (0, PAST_LEN)], o_mla)

        o_swa = _swa_ctx(sq, sk, sv, sink_rows, l)
        o_swa = _swa_lat(sq, sk, sv, cache_k, cache_v, sink_rows, l, o_swa)

        qkv = _gdn_conv(g3, gdn_conv_w, l, 0, BATCH, SEQ, None)
        qkv = _gdn_conv(g3, gdn_conv_w, l, N_CTX, DEC_BATCH, DEC_SEQ, qkv)
        o_gf, o_gb, new_st = _gdn(qkv, misc, l, SEQ, BATCH, 0, None, [None, None, new_st], n_par=GDN_PAR_CTX)
        o_gf, o_gb = _gdn(qkv, misc, l, DEC_SEQ, DEC_BATCH, N_CTX // DEC_SEQ, s0, [o_gf, o_gb])

        x = _out_proj(x, mods, l, o_mla, o_swa, o_gf.reshape(N_TOK, GDN_V_DIM), o_gb.reshape(N_TOK, GDN_V_DIM),
                      gz, gdn_norm, wo)
        if l + 1 < DEPTH:
            xs = [_ffn([x], mods, l, 2, norm_ffn2, w21, w22)]
        else:
            y_prompt, y_sample = _ffn([x], mods, l, 2, norm_ffn2, w21, w22, final_gain=final_norm)

    new_ckv, new_krope, new_sk, new_sv = caches
    kv_shape = (BATCH, DEPTH, SEQ, SWA_KV_HEADS, SWA_HD)
    return (y_prompt.reshape(BATCH, SEQ, D_MODEL), y_sample.reshape(DEC_BATCH, DEC_SEQ, D_MODEL), new_ckv,
            new_krope, new_sk.reshape(kv_shape), new_sv.reshape(kv_shape),
            new_st.reshape(BATCH, DEPTH, 2, GDN_HEADS, GDN_DK, GDN_DV))
```

```python
import functools

import numpy as np
import jax
import jax.numpy as jnp
from jax import lax
from jax.experimental import pallas as pl
from jax.experimental.pallas import tpu as pltpu

D_MODEL = 1024
BATCH = 16
SEQ = 256
DEPTH = 2
DEC_BATCH = 2
DEC_SEQ = 2048
PAST_LEN = 512
GRID_W = 64
ROPE_BASE = 10000.0
NORM_EPS = 1e-6
N_MOD = 9
D_FF = 2816
MLA_HEADS = 8
MLA_Q_LORA = 384
MLA_KV_LORA = 256
MLA_NOPE = 64
MLA_ROPE = 32
MLA_V = 64
SWA_HEADS = 8
SWA_KV_HEADS = 2
SWA_GROUP = SWA_HEADS // SWA_KV_HEADS
SWA_HD = 64
SWA_WINDOW = 128
GDN_HEADS = 4
GDN_DK = 128
GDN_DV = 128
GDN_CONV = 5
GDN_CHUNK = 64
GDN_QK_DIM = GDN_HEADS * GDN_DK
GDN_V_DIM = GDN_HEADS * GDN_DV
GDN_CONV_CH = 2 * GDN_QK_DIM + GDN_V_DIM
GDN_NH = 2 * GDN_HEADS

N_CTX = BATCH * SEQ
N_LAT = DEC_BATCH * DEC_SEQ
N_TOK = N_CTX + N_LAT
N_GROUPS = 1 + DEC_BATCH
COND_ROWS = 8

LANES = 128
SUBLANES = 8
VMEM_LIMIT_BYTES = 56 * 1024 * 1024

TM_FFN = 512
FFN_LOAD_COLS = 256
TM_PROJ = 2 * SEQ
TM_OUT = 512
TQ_MLA = 512
GDN_PAR = 2
GDN_PAR_CTX = 4

BF = jnp.bfloat16
F32 = jnp.float32
LOG2_E = 1.4426950408889634
MLA_Q_SCALE = (MLA_NOPE + MLA_ROPE) ** -0.5 * LOG2_E

_C_CQ = 0
_C_CKV = _C_CQ + MLA_Q_LORA
_C_SQ = _C_CKV + MLA_KV_LORA
_C_SK = _C_SQ + SWA_HEADS * SWA_HD
_C_SV = _C_SK + SWA_KV_HEADS * SWA_HD
_C_G3 = _C_SV + SWA_KV_HEADS * SWA_HD
_C_GZ = _C_G3 + GDN_CONV_CH
_C_MA = _C_GZ + GDN_V_DIM
_C_END = _C_MA + LANES
_M_G = MLA_ROPE
_M_B = MLA_ROPE + GDN_NH
_M_END = MLA_ROPE + 2 * GDN_NH
_CACHE_WIDTHS = (MLA_KV_LORA, MLA_ROPE, SWA_KV_HEADS * SWA_HD, SWA_KV_HEADS * SWA_HD)
_T_CS, _T_SS, _T_CA, _T_SA, _T_CQ, _T_SQ = (i * LANES for i in range(6))
_T_END = 6 * LANES


def _params(*sem):
    return pltpu.CompilerParams(dimension_semantics=sem, vmem_limit_bytes=VMEM_LIMIT_BYTES)


def _bdot(a, b):
    return jnp.dot(a.astype(BF), b.astype(BF), preferred_element_type=F32)


def _nt(a, b):
    return lax.dot_general(a, b, (((1,), (1,)), ((), ())), preferred_element_type=F32)


def _bdot_nt(a, b):
    return _nt(a.astype(BF), b.astype(BF))


def _bdot_tn(a, b):
    return lax.dot_general(a.astype(BF), b.astype(BF), (((0,), (0,)), ((), ())),
                           preferred_element_type=F32)


def _split3(a):
    b1 = a.astype(BF)
    r = a - b1.astype(F32)
    b2 = r.astype(BF)
    b3 = (r - b2.astype(F32)).astype(BF)
    return b1, b2, b3


def _silu(x):
    return x / (1.0 + jnp.exp(-x))


def _rms(x, gain, eps=NORM_EPS):
    return x * lax.rsqrt(jnp.mean(x * x, axis=-1, keepdims=True) + eps) * gain


def _group_of_row(r):
    return jnp.where(r < N_CTX, 0, 1 + (r - N_CTX) // DEC_SEQ)


def _any_spec():
    return pl.BlockSpec(memory_space=pl.ANY)


def _adaln_kernel(c_ref, w_ref, b_ref, o_ref):
    o_ref[0] = _bdot(_silu(c_ref[...]), w_ref[0]) + b_ref[0]


def _adaln(cond, w_ada, b_ada, tn=1536):
    n = N_MOD * D_MODEL
    return pl.pallas_call(
        _adaln_kernel,
        grid=(DEPTH, n // tn),
        in_specs=[pl.BlockSpec((COND_ROWS, D_MODEL), lambda l, j: (0, 0)),
                  pl.BlockSpec((1, D_MODEL, tn), lambda l, j: (l, 0, j)),
                  pl.BlockSpec((1, 1, tn), lambda l, j: (l, 0, j))],
        out_specs=pl.BlockSpec((1, COND_ROWS, tn), lambda l, j: (l, 0, j)),
        out_shape=jax.ShapeDtypeStruct((DEPTH, COND_ROWS, n), F32),
        compiler_params=_params("parallel", "parallel"),
        name="adaln",
    )(cond, w_ada, b_ada.reshape(DEPTH, 1, n))


def _mod_spec(l, which, tm):
    return pl.BlockSpec((1, 1, 1, 3 * D_MODEL), lambda i: (l, _group_of_row(i * tm), 0, which))


def _ffn_kernel(*refs, n_in, n_a_tiles, final, layer):
    x_refs = refs[:n_in]
    mod_ref, gain_ref, w1_hbm, w2_hbm = refs[n_in:n_in + 4]
    w1_ref, w2_ref, stage_g, stage_u, stage_d, sem_ref = refs[-6:]
    rest = refs[n_in + 4:-6]
    i = pl.program_id(0)

    if n_in == 2:
        x = jnp.where(i < n_a_tiles, x_refs[0][...], x_refs[1][...])
    else:
        x = x_refs[0][...]
    mod = mod_ref[0, 0]
    shift, scale, gate = mod[:, :D_MODEL], mod[:, D_MODEL:2 * D_MODEL], mod[:, 2 * D_MODEL:]
    hb = (_rms(x, gain_ref[0]) * (1.0 + scale) + shift).astype(BF)

    def finish(ffn):
        y = x + gate * (0.5 * ffn)
        if final:
            fg_ref, oa_ref, ob_ref = rest
            yn = _rms(y, fg_ref[...])

            @pl.when(i < n_a_tiles)
            def _():
                oa_ref[...] = yn

            @pl.when(i >= n_a_tiles)
            def _():
                ob_ref[...] = yn
        else:
            rest[0][...] = y

    fc = stage_d.shape[1]
    n_chunks = D_FF // fc

    def chunk_copies(c):
        slot = c % 2
        cols = pl.ds(c * fc, fc)
        return (pltpu.make_async_copy(w1_hbm.at[layer, :, cols], stage_g.at[slot], sem_ref.at[0, slot]),
                pltpu.make_async_copy(w1_hbm.at[layer, :, pl.ds(D_FF + c * fc, fc)], stage_u.at[slot],
                                      sem_ref.at[1, slot]),
                pltpu.make_async_copy(w2_hbm.at[layer, cols, :], stage_d.at[slot], sem_ref.at[2, slot]))

    @pl.when(i == 0)
    def _():
        for cp in chunk_copies(0):
            cp.start()
        acc = None
        for c in range(n_chunks):
            if c + 1 < n_chunks:
                for cp in chunk_copies(c + 1):
                    cp.start()
            for cp in chunk_copies(c):
                cp.wait()
            slot = c % 2
            wg, wu, wd = stage_g[slot].astype(BF), stage_u[slot].astype(BF), stage_d[slot].astype(BF)
            w1_ref[:, c * fc:(c + 1) * fc] = wg
            w1_ref[:, D_FF + c * fc:D_FF + (c + 1) * fc] = wu
            w2_ref[c * fc:(c + 1) * fc, :] = wd
            g = jnp.dot(hb, wg, preferred_element_type=F32)
            u = jnp.dot(hb, wu, preferred_element_type=F32)
            part = jnp.dot((_silu(g) * u).astype(BF), wd, preferred_element_type=F32)
            acc = part if acc is None else acc + part
        finish(acc)

    @pl.when(i > 0)
    def _():
        gu = jnp.dot(hb, w1_ref[...], preferred_element_type=F32)
        a = _silu(gu[:, :D_FF]) * gu[:, D_FF:]
        finish(jnp.dot(a.astype(BF), w2_ref[...], preferred_element_type=F32))


def _ffn(xs, mods, l, which, gain, w1, w2, final_gain=None, tm=TM_FFN):
    n_a = N_CTX // tm
    row = lambda i: (i, 0)
    first = lambda i: (jnp.minimum(i, n_a - 1), 0)
    second = lambda i: (jnp.maximum(i - n_a, 0), 0)
    lay3 = lambda i: (l, 0, 0)
    x_specs = ([pl.BlockSpec((tm, D_MODEL), row)] if len(xs) == 1
               else [pl.BlockSpec((tm, D_MODEL), first), pl.BlockSpec((tm, D_MODEL), second)])
    in_specs = x_specs + [_mod_spec(l, which, tm),
                          pl.BlockSpec((1, 1, D_MODEL), lay3),
                          _any_spec(), _any_spec()]
    args = list(xs) + [mods, gain.reshape(DEPTH, 1, D_MODEL), w1, w2]
    if final_gain is None:
        out_specs = pl.BlockSpec((tm, D_MODEL), row)
        out_shape = jax.ShapeDtypeStruct((N_TOK, D_MODEL), F32)
    else:
        in_specs.append(pl.BlockSpec((1, D_MODEL), lambda i: (0, 0)))
        args.append(final_gain.reshape(1, D_MODEL))
        out_specs = [pl.BlockSpec((tm, D_MODEL), first), pl.BlockSpec((tm, D_MODEL), second)]
        out_shape = [jax.ShapeDtypeStruct((N_CTX, D_MODEL), F32), jax.ShapeDtypeStruct((N_LAT, D_MODEL), F32)]
    return pl.pallas_call(
        functools.partial(_ffn_kernel, n_in=len(xs), n_a_tiles=n_a, final=final_gain is not None, layer=l),
        grid=(N_TOK // tm,),
        in_specs=in_specs, out_specs=out_specs, out_shape=out_shape,
        scratch_shapes=[pltpu.VMEM((D_MODEL, 2 * D_FF), BF), pltpu.VMEM((D_FF, D_MODEL), BF),
                        pltpu.VMEM((2, D_MODEL, FFN_LOAD_COLS), F32), pltpu.VMEM((2, D_MODEL, FFN_LOAD_COLS), F32),
                        pltpu.VMEM((2, FFN_LOAD_COLS, D_MODEL), F32), pltpu.SemaphoreType.DMA((3, 2))],
        compiler_params=_params("arbitrary"),
        name="ffn",
    )(*args)


def _kv_expand(ckv_n, misc, wk, wv):
    kin = jnp.concatenate([ckv_n, misc], axis=1).astype(BF)
    kk = jnp.dot(kin, wk, preferred_element_type=F32)
    vv = jnp.dot(ckv_n.astype(BF), wv, preferred_element_type=F32)
    return kk.astype(BF), vv.astype(BF)


def _proj_kernel(*refs, n_ctx_tiles):
    (x_ref, mod_ref, gain_ref, win_ref, qg_ref, wqb_ref, kvg_ref, wk_ref, wv_ref, tab_ref,
     alog_ref, dtb_ref) = refs[:12]
    (q_ref, misc_ref, kmla_ref, vmla_ref, sq_ref, sk_ref, sv_ref, g3_ref, gz_ref,
     c_ckv_ref, c_krope_ref, c_sk_ref, c_sv_ref) = refs[-13:]
    i = pl.program_id(0)
    mod = mod_ref[0, 0]
    shift, scale = mod[:, :D_MODEL], mod[:, D_MODEL:2 * D_MODEL]
    hb = (_rms(x_ref[...], gain_ref[0]) * (1.0 + scale) + shift).astype(BF)
    tab = tab_ref[...]

    def project(c0, c1):
        return _nt(hb, win_ref[0, c0:c1, :])

    lane = lax.broadcasted_iota(jnp.int32, (x_ref.shape[0], LANES), 1)

    def rotate_half(t, quarter):
        even = (lane // quarter) % 2 == 0
        tiles = [t[:, c:c + LANES] for c in range(0, t.shape[1], LANES)]
        out = [jnp.where(even, -pltpu.roll(v, LANES - quarter, 1), pltpu.roll(v, quarter, 1)) for v in tiles]
        return out[0] if len(out) == 1 else jnp.concatenate(out, axis=1)

    u_lora = project(_C_CQ, _C_SQ)
    u_misc = project(_C_MA, _C_END)
    u_sq = project(_C_SQ, _C_SK)
    u_skv = project(_C_SK, _C_G3)

    qn = _rms(u_lora[:, :MLA_Q_LORA], qg_ref[0])
    ckv_n = _rms(u_lora[:, MLA_Q_LORA:], kvg_ref[0])

    m = (u_misc * tab[:, _T_CA:_T_CA + LANES]
         + rotate_half(u_misc, MLA_ROPE // 4) * tab[:, _T_SA:_T_SA + LANES])
    z = m + dtb_ref[0]
    softplus = jnp.maximum(z, 0.0) + jnp.log(1.0 + jnp.exp(-jnp.abs(z)))
    decay = -jnp.exp(alog_ref[0]) * softplus
    strength = 1.0 / (1.0 + jnp.exp(-m))
    misc = jnp.where((lane >= _M_G) & (lane < _M_B), decay,
                     jnp.where((lane >= _M_B) & (lane < _M_END), strength, m))
    misc_ref[...] = misc

    n_sq = SWA_HEADS * SWA_HD
    cos_s = tab[:, _T_CS:_T_CS + LANES]
    sin_s = tab[:, _T_SS:_T_SS + LANES]
    cos4 = jnp.concatenate([cos_s] * (n_sq // LANES), axis=1)
    sin4 = jnp.concatenate([sin_s] * (n_sq // LANES), axis=1)
    sq_ref[...] = ((u_sq * cos4 + rotate_half(u_sq, SWA_HD // 4) * sin4) * (SWA_HD ** -0.5 * LOG2_E)).astype(BF)
    sk = u_skv[:, :LANES] * cos_s + rotate_half(u_skv[:, :LANES], SWA_HD // 4) * sin_s
    sv = u_skv[:, LANES:]
    sk_ref[...] = sk
    sv_ref[...] = sv

    @pl.when(i < n_ctx_tiles)
    def _():
        for j in range(c_ckv_ref.shape[0]):
            rows = slice(j * SEQ, (j + 1) * SEQ)
            c_ckv_ref[j, 0] = ckv_n[rows]
            c_krope_ref[j, 0] = misc[rows, :MLA_ROPE]
            c_sk_ref[j, 0] = sk[rows]
            c_sv_ref[j, 0] = sv[rows]

    g3_ref[...] = project(_C_G3, _C_GZ)
    gz_ref[...] = project(_C_GZ, _C_MA)

    q2 = jnp.dot(qn.astype(BF), wqb_ref[0], preferred_element_type=F32)
    cosq = jnp.concatenate([tab[:, _T_CQ:_T_CQ + LANES]] * MLA_HEADS, axis=1)
    sinq = jnp.concatenate([tab[:, _T_SQ:_T_SQ + LANES]] * MLA_HEADS, axis=1)
    q_ref[...] = ((q2 * cosq + rotate_half(q2, MLA_ROPE // 4) * sinq) * MLA_Q_SCALE).astype(BF)
    kk, vv = _kv_expand(ckv_n, misc, wk_ref[0], wv_ref[0])
    kmla_ref[...] = kk
    vmla_ref[...] = vv


def _proj(x, mods, l, gain, win, qg, wqb, kvg, wk, wv, tab, alog, dtb, cache_prev, tm=TM_PROJ):
    assert tm % SEQ == 0
    n_ctx_tiles = N_CTX // tm
    lat_tiles = DEC_SEQ // tm
    lay3 = lambda i: (l, 0, 0)
    row = lambda i: (i, 0)

    def tab_map(i):
        return (jnp.where(i < n_ctx_tiles, 0, 1 + (i - n_ctx_tiles) % lat_tiles), 0)

    def lay_spec(a):
        return pl.BlockSpec((1,) + a.shape[1:], lay3)

    widths = [(MLA_HEADS * LANES, BF), (LANES, F32), (MLA_HEADS * LANES, BF),
              (MLA_HEADS * MLA_V, BF), (SWA_HEADS * SWA_HD, BF), (SWA_KV_HEADS * SWA_HD, F32),
              (SWA_KV_HEADS * SWA_HD, F32), (GDN_CONV_CH, F32), (GDN_V_DIM, F32)]
    out_specs = [pl.BlockSpec((tm, w), row) for w, _ in widths]
    out_shape = [jax.ShapeDtypeStruct((N_TOK, w), dt) for w, dt in widths]
    for w in _CACHE_WIDTHS:
        out_specs.append(pl.BlockSpec((tm // SEQ, 1, SEQ, w),
                                      lambda i: (jnp.minimum(i, n_ctx_tiles - 1), l, 0, 0)))
        out_shape.append(jax.ShapeDtypeStruct((BATCH, DEPTH, SEQ, w), F32))
    qg, kvg = qg.reshape(DEPTH, 1, -1), kvg.reshape(DEPTH, 1, -1)
    args = [x, mods, gain.reshape(DEPTH, 1, D_MODEL), win, qg, wqb, kvg, wk, wv, tab, alog, dtb]
    in_specs = [pl.BlockSpec((tm, D_MODEL), row), _mod_spec(l, 1, tm),
                pl.BlockSpec((1, 1, D_MODEL), lay3), lay_spec(win), lay_spec(qg), lay_spec(wqb),
                lay_spec(kvg), lay_spec(wk), lay_spec(wv), pl.BlockSpec((tm, _T_END), tab_map),
                lay_spec(alog), lay_spec(dtb)]
    aliases = {}
    if cache_prev is not None:
        for j, prev in enumerate(cache_prev):
            aliases[len(args)] = len(widths) + j
            args.append(prev)
            in_specs.append(_any_spec())
    return pl.pallas_call(
        functools.partial(_proj_kernel, n_ctx_tiles=n_ctx_tiles),
        grid=(N_TOK // tm,),
        in_specs=in_specs, out_specs=out_specs, out_shape=out_shape,
        input_output_aliases=aliases,
        compiler_params=_params("arbitrary"),
        name="proj",
    )(*args)


def _kv_cache_kernel(ckv_ref, misc_ref, wk_ref, wv_ref, k_ref, v_ref):
    kk, vv = _kv_expand(ckv_ref[0, 0], misc_ref[0, 0], wk_ref[0], wv_ref[0])
    k_ref[...] = kk
    v_ref[...] = vv


def _kv_cache(cache_ckv, cache_misc, l, wk, wv):
    lay3 = lambda b: (l, 0, 0)
    n = DEC_BATCH * PAST_LEN
    return pl.pallas_call(
        _kv_cache_kernel,
        grid=(DEC_BATCH,),
        in_specs=[pl.BlockSpec((1, 1, PAST_LEN, MLA_KV_LORA), lambda b: (b, l, 0, 0)),
                  pl.BlockSpec((1, 1, PAST_LEN, LANES), lambda b: (b, l, 0, 0)),
                  pl.BlockSpec((1,) + wk.shape[1:], lay3),
                  pl.BlockSpec((1,) + wv.shape[1:], lay3)],
        out_specs=[pl.BlockSpec((PAST_LEN, MLA_HEADS * LANES), lambda b: (b, 0)),
                   pl.BlockSpec((PAST_LEN, MLA_HEADS * MLA_V), lambda b: (b, 0))],
        out_shape=[jax.ShapeDtypeStruct((n, MLA_HEADS * LANES), BF),
                   jax.ShapeDtypeStruct((n, MLA_HEADS * MLA_V), BF)],
        compiler_params=_params("parallel"),
        name="kv_cache",
    )(cache_ckv, cache_misc, wk, wv)


def _softmax_numerator(s_ref, p_ref, sink=None, row_block=128):
    rows, cols = s_ref.shape
    mx = jnp.max(s_ref[...], axis=-1, keepdims=True)
    if sink is not None:
        mx = jnp.maximum(mx, sink)
    sums = []
    for r0 in range(0, rows, row_block):
        mb = jnp.broadcast_to(mx[r0:r0 + row_block], (row_block, LANES))
        part = jnp.zeros((row_block, LANES), F32)
        for c in range(0, cols, LANES):
            p = jnp.exp2(s_ref[r0:r0 + row_block, c:c + LANES] - mb)
            part = part + p
            p_ref[r0:r0 + row_block, c:c + LANES] = p.astype(BF)
        sums.append(jnp.sum(part, axis=-1, keepdims=True))
    den = jnp.concatenate(sums, axis=0)
    return den if sink is None else den + jnp.exp2(sink - mx)


def _mla_kernel(*refs, n_seg):
    q_ref = refs[0]
    k_refs = refs[1:1 + n_seg]
    v_refs = refs[1 + n_seg:1 + 2 * n_seg]
    o_ref = refs[-5]
    s_refs, p_refs = refs[-4:-2], refs[-2:]
    tq = q_ref.shape[0]
    lane = lax.broadcasted_iota(jnp.int32, (tq, LANES), 1)

    def scores(h):
        off = 0
        for k in k_refs:
            s_refs[h % 2][:, off:off + k.shape[0]] = _nt(q_ref[:, h * LANES:(h + 1) * LANES],
                                                         k[:, h * LANES:(h + 1) * LANES])
            off += k.shape[0]

    scores(0)
    outs = []
    for h in range(MLA_HEADS):
        if h + 1 < MLA_HEADS:
            scores(h + 1)
        den = _softmax_numerator(s_refs[h % 2], p_refs[h % 2])
        pair = h // 2
        acc, off = None, 0
        for v in v_refs:
            pv = jnp.dot(p_refs[h % 2][:, off:off + v.shape[0]], v[:, pair * LANES:(pair + 1) * LANES],
                         preferred_element_type=F32)
            acc = pv if acc is None else acc + pv
            off += v.shape[0]
        outs.append(acc / den)
        if h % 2 == 1:
            o_ref[:, pair * LANES:(pair + 1) * LANES] = jnp.where(lane < MLA_V, outs[-2], outs[-1]).astype(BF)


def _mla_attend(q, ks, vs, n_batch, t, row0, k_blocks, prev):
    n_seg = len(ks)
    tq = min(TQ_MLA, t)
    n_keys = sum(s for _, s in k_blocks)
    qb0 = row0 // tq
    tiles = t // tq
    q_map = lambda b, i: (qb0 + b * tiles + i, 0)
    in_specs = [pl.BlockSpec((tq, MLA_HEADS * LANES), q_map)]
    for (b0, s), width in ([(kb, MLA_HEADS * LANES) for kb in k_blocks]
                           + [(kb, MLA_HEADS * MLA_V) for kb in k_blocks]):
        in_specs.append(pl.BlockSpec((s, width), functools.partial(lambda b, i, b0: (b0 + b, 0), b0=b0)))
    args = [q, *ks, *vs]
    aliases = {}
    if prev is not None:
        args.append(prev)
        in_specs.append(_any_spec())
        aliases = {len(args) - 1: 0}
    return pl.pallas_call(
        functools.partial(_mla_kernel, n_seg=n_seg),
        grid=(n_batch, tiles),
        in_specs=in_specs,
        out_specs=pl.BlockSpec((tq, MLA_HEADS * MLA_V), q_map),
        out_shape=jax.ShapeDtypeStruct((N_TOK, MLA_HEADS * MLA_V), BF),
        input_output_aliases=aliases,
        scratch_shapes=[pltpu.VMEM((tq, n_keys), F32)] * 2 + [pltpu.VMEM((tq, n_keys), BF)] * 2,
        compiler_params=_params("parallel", "parallel"),
        name="mla_attend",
    )(*args)


def _gqa_heads(q_ref, k_segs, v_segs, masks, sink_ref, o_ref, s_refs, p_refs):
    tq = q_ref.shape[0]
    lane_q = lax.broadcasted_iota(jnp.int32, (tq, LANES), 1)
    lo_q = lane_q < SWA_HD
    k_roll = [pltpu.roll(k, SWA_HD, 1) for k in k_segs]
    v_roll = [pltpu.roll(v, SWA_HD, 1) for v in v_segs]
    heads = range(SWA_KV_HEADS)
    sinks = []
    for kvh in heads:
        tiles = [q_ref[:, (kvh * 2 + j) * LANES:(kvh * 2 + j + 1) * LANES] for j in range(2)]
        zero = jnp.zeros_like(tiles[0])
        qs = jnp.concatenate([jnp.where(lo_q, tiles[0], zero), jnp.where(lo_q, zero, tiles[0]),
                              jnp.where(lo_q, tiles[1], zero), jnp.where(lo_q, zero, tiles[1])], axis=0)
        sinks.append(jnp.concatenate(
            [sink_ref[0, kvh * SWA_GROUP + g:kvh * SWA_GROUP + g + 1, :]
             for g in range(SWA_GROUP) for _ in range(tq // LANES)], axis=1) * LOG2_E)
        off = 0
        for k, kr, msk in zip(k_segs, k_roll, masks):
            lane_k = lax.broadcasted_iota(jnp.int32, k.shape, 1)
            first = (lane_k < SWA_HD) == (kvh == 0)
            kd = jnp.where(first, k, kr)
            s = _bdot_nt(kd, qs)
            if msk is not None:
                s = jnp.where(msk, s, -1e30)
            s_refs[kvh][off:off + k.shape[0], :] = s
            off += k.shape[0]
    dens = [_softmax_numerator_t(s_refs[kvh], p_refs[kvh], sinks[kvh]) for kvh in heads]
    for kvh in heads:
        acc, off = None, 0
        for v, vr in zip(v_segs, v_roll):
            lane_v = lax.broadcasted_iota(jnp.int32, v.shape, 1)
            first = (lane_v < SWA_HD) == (kvh == 0)
            vd = jnp.where(first, v, vr)
            pv = lax.dot_general(vd.astype(BF), p_refs[kvh][off:off + v.shape[0], :],
                                 (((0,), (0,)), ((), ())), preferred_element_type=F32)
            acc = pv if acc is None else acc + pv
            off += v.shape[0]
        o = (acc / dens[kvh]).T
        for j in range(2):
            o_ref[:, (kvh * 2 + j) * LANES:(kvh * 2 + j + 1) * LANES] = jnp.where(
                lo_q, o[(2 * j) * tq:(2 * j + 1) * tq], o[(2 * j + 1) * tq:(2 * j + 2) * tq]).astype(BF)


def _softmax_numerator_t(s_ref, p_ref, sink, row_block=128):
    keys, cols = s_ref.shape
    dens = []
    for c in range(0, cols, LANES):
        mx = s_ref[0:row_block, c:c + LANES]
        for r0 in range(row_block, keys, row_block):
            mx = jnp.maximum(mx, s_ref[r0:r0 + row_block, c:c + LANES])
        snk = sink[:, c:c + LANES]
        m = jnp.maximum(jnp.max(mx, axis=0, keepdims=True), snk)
        mb = jnp.broadcast_to(m, (row_block, LANES))
        part = jnp.zeros((row_block, LANES), F32)
        for r0 in range(0, keys, row_block):
            p = jnp.exp2(s_ref[r0:r0 + row_block, c:c + LANES] - mb)
            part = part + p
            p_ref[r0:r0 + row_block, c:c + LANES] = p.astype(BF)
        dens.append(jnp.sum(part, axis=0, keepdims=True) + jnp.exp2(snk - m))
    return jnp.concatenate(dens, axis=1)


def _swa_scratch(keys, rows):
    return ([pltpu.VMEM((keys, rows), F32)] * SWA_KV_HEADS + [pltpu.VMEM((keys, rows), BF)] * SWA_KV_HEADS)


def _swa_ctx_kernel(q_ref, k_ref, v_ref, sink_ref, o_ref, *scratch):
    _gqa_heads(q_ref, [k_ref[...]], [v_ref[...]], [None], sink_ref, o_ref,
               scratch[:SWA_KV_HEADS], scratch[SWA_KV_HEADS:])


def _swa_ctx(sq, sk, sv, sink_rows, l):
    return pl.pallas_call(
        _swa_ctx_kernel,
        grid=(BATCH,),
        in_specs=[pl.BlockSpec((SEQ, SWA_HEADS * SWA_HD), lambda b: (b, 0)),
                  pl.BlockSpec((SEQ, LANES), lambda b: (b, 0)),
                  pl.BlockSpec((SEQ, LANES), lambda b: (b, 0)),
                  pl.BlockSpec((1, SWA_HEADS, LANES), lambda b: (l, 0, 0))],
        out_specs=pl.BlockSpec((SEQ, SWA_HEADS * SWA_HD), lambda b: (b, 0)),
        out_shape=jax.ShapeDtypeStruct((N_TOK, SWA_HEADS * SWA_HD), BF),
        scratch_shapes=_swa_scratch(SEQ, SWA_GROUP * SEQ),
        compiler_params=_params("parallel"),
        name="swa_ctx",
    )(sq, sk, sv, sink_rows)


def _swa_lat_kernel(q_ref, kp_ref, kc_ref, kn_ref, vp_ref, vc_ref, vn_ref, kx_ref, vx_ref, sink_ref, prev_ref,
                    o_ref, *scratch):
    w = SWA_WINDOW
    n = pl.program_id(1)
    nb = pl.num_programs(1)
    k_band = jnp.concatenate([kp_ref[...], kc_ref[...], kn_ref[...]], axis=0)
    v_band = jnp.concatenate([vp_ref[...], vc_ref[...], vn_ref[...]], axis=0)
    rows = SWA_GROUP * w
    r = lax.broadcasted_iota(jnp.int32, (3 * w, rows), 1) & (w - 1)
    c = lax.broadcasted_iota(jnp.int32, (3 * w, rows), 0)
    valid = (c >= r) & (c <= r + 2 * w) & ((c >= w) | (n > 0)) & ((c < 2 * w) | (n < nb - 1))
    _gqa_heads(q_ref, [k_band, kx_ref[0, 0]], [v_band, vx_ref[0, 0]], [valid, None], sink_ref, o_ref,
               scratch[:SWA_KV_HEADS], scratch[SWA_KV_HEADS:])


def _swa_lat(sq, sk, sv, k_cache, v_cache, sink_rows, l, prev):
    w = SWA_WINDOW
    nb = DEC_SEQ // w
    q0 = N_CTX // w

    def blk(d):
        return lambda b, n: (q0 + b * nb + jnp.clip(n + d, 0, nb - 1), 0)

    kv_specs = [pl.BlockSpec((w, LANES), blk(d)) for d in (-1, 0, 1)]
    cache_spec = pl.BlockSpec((1, 1, PAST_LEN, LANES), lambda b, n: (b, l, 0, 0))
    return pl.pallas_call(
        _swa_lat_kernel,
        grid=(DEC_BATCH, nb),
        in_specs=[pl.BlockSpec((w, SWA_HEADS * SWA_HD), blk(0))] + kv_specs + kv_specs
        + [cache_spec, cache_spec, pl.BlockSpec((1, SWA_HEADS, LANES), lambda b, n: (l, 0, 0)), _any_spec()],
        out_specs=pl.BlockSpec((w, SWA_HEADS * SWA_HD), blk(0)),
        out_shape=jax.ShapeDtypeStruct((N_TOK, SWA_HEADS * SWA_HD), BF),
        input_output_aliases={10: 0},
        scratch_shapes=_swa_scratch(3 * w + PAST_LEN, SWA_GROUP * w),
        compiler_params=_params("parallel", "parallel"),
        name="swa_lat",
    )(sq, sk, sk, sk, sv, sv, sv, k_cache, v_cache, sink_rows, prev)


def _gdn_conv_kernel(*refs):
    x_ref, w_ref, o_ref = refs[0], refs[1], refs[-1]
    x = x_ref[...]
    t = x.shape[0]
    w = w_ref[0]
    r8 = lax.broadcasted_iota(jnp.int32, (SUBLANES, x.shape[1]), 0)
    half = GDN_CONV // 2
    acc = x * w[half:half + 1, :]
    for k in range(GDN_CONV):
        d = k - half
        if d == 0:
            continue
        xs = pltpu.roll(x, (-d) % t, 0)
        top, bot = xs[:SUBLANES], xs[t - SUBLANES:]
        if d < 0:
            top = jnp.where(r8 + d >= 0, top, 0.0)
        else:
            bot = jnp.where(r8 + d < SUBLANES, bot, 0.0)
        xs = jnp.concatenate([top, xs[SUBLANES:t - SUBLANES], bot], axis=0)
        acc = acc + xs * w[k:k + 1, :]
    y = _silu(acc)
    is_qk = pl.program_id(1) < 2
    cols = []
    for h in range(y.shape[1] // LANES):
        yh = y[:, h * LANES:(h + 1) * LANES]
        nrm = lax.rsqrt(jnp.sum(yh * yh, axis=-1, keepdims=True) + 1e-6)
        cols.append(yh * jnp.where(is_qk, nrm, 1.0))
    o_ref[...] = jnp.concatenate(cols, axis=1)


def _gdn_conv(g3, conv_w, l, row0, n_seq, t, prev):
    blk0 = row0 // t
    args = [g3, conv_w]
    in_specs = [pl.BlockSpec((t, GDN_QK_DIM), lambda s, j: (blk0 + s, j)),
                pl.BlockSpec((1, GDN_CONV, GDN_QK_DIM), lambda s, j: (l, 0, j))]
    aliases = {}
    if prev is not None:
        args.append(prev)
        in_specs.append(_any_spec())
        aliases = {2: 0}
    return pl.pallas_call(
        _gdn_conv_kernel,
        grid=(n_seq, 3),
        in_specs=in_specs,
        out_specs=pl.BlockSpec((t, GDN_QK_DIM), lambda s, j: (blk0 + s, j)),
        out_shape=jax.ShapeDtypeStruct((N_TOK, GDN_CONV_CH), F32),
        input_output_aliases=aliases,
        compiler_params=_params("parallel", "parallel"),
        name="gdn_conv",
    )(*args)


def _gate_rows(misc):
    shape = (2 * GDN_NH, LANES)
    sel = (lax.broadcasted_iota(jnp.int32, shape, 1)
           == lax.broadcasted_iota(jnp.int32, shape, 0) + _M_G).astype(BF)
    b1, b2, b3 = _split3(misc)
    return _nt(sel, b1) + (_nt(sel, b2) + _nt(sel, b3))


def _gdn_chunk_pairs(pairs):
    c = GDN_CHUNK
    shape = (c, 2 * c)
    ri = lax.broadcasted_iota(jnp.int32, shape, 0)
    lane = lax.broadcasted_iota(jnp.int32, shape, 1)
    cj = lane & (c - 1)
    fwd_half = lane < c
    lower, upper = ri >= cj, ri <= cj
    incl = (fwd_half & lower) | (~fwd_half & upper)
    incl_t = (fwd_half & upper) | (~fwd_half & lower)
    strict = incl & (ri != cj)
    incl_f, incl_b = incl & fwd_half, incl & ~fwd_half
    eye = (ri == cj).astype(F32)
    n = len(pairs)
    rng = range(n)
    fw = [p[0] for p in pairs]
    bw = [p[1] for p in pairs]

    def halves(a, b):
        return jnp.where(fwd_half, a, b)

    def split_rows(m):
        return jnp.concatenate([jnp.where(fwd_half, m, 0.0), jnp.where(fwd_half, 0.0, m)], axis=0)

    def pair_dot3(x, p):
        xh = x.astype(BF)
        xl = (x - xh.astype(F32)).astype(BF)
        phf = p.astype(BF).astype(F32)
        bd_hi = split_rows(phf).astype(BF)
        bd_lo = split_rows(p - phf).astype(BF)
        return jnp.dot(jnp.concatenate([xh, xl, xh], axis=1), jnp.concatenate([bd_hi, bd_hi, bd_lo], axis=0),
                       preferred_element_type=F32)

    g_col = [halves(fw[i]["g_col"], bw[i]["g_col"]) for i in rng]
    gc_row = [jnp.sum(jnp.where(incl_t, g_col[i], 0.0), axis=0, keepdims=True) for i in rng]
    gcf = [jnp.sum(jnp.where(incl_f, pairs[i][2], 0.0), axis=1, keepdims=True) for i in rng]
    gcb = [jnp.sum(jnp.where(incl_b, pairs[i][2], 0.0), axis=1, keepdims=True) for i in rng]
    decay = [jnp.where(incl, jnp.exp(halves(gcf[i], gcb[i]) - gc_row[i]), 0.0) for i in rng]
    qf = [p["q"] * (GDN_DK ** -0.5) for p in fw]
    qb = [p["q"] * (GDN_DK ** -0.5) for p in bw]
    kbf = [p["k"] * p["beta"] for p in fw]
    kbb = [p["k"] * p["beta"] for p in bw]
    z = jnp.zeros((c, GDN_DK), F32)
    kq = [_bdot_nt(jnp.concatenate([jnp.concatenate([kbf[i], kbb[i]], axis=1),
                                    jnp.concatenate([qf[i], qb[i]], axis=1)], axis=0),
                   jnp.concatenate([jnp.concatenate([fw[i]["k"], z], axis=1),
                                    jnp.concatenate([z, bw[i]["k"]], axis=1)], axis=0)) for i in rng]
    pw = [jnp.where(strict, -(kq[i][:c] * decay[i]), 0.0) for i in rng]
    inv = [eye + m for m in pw]
    levels = int(np.log2(c)) - 1
    pw = [pair_dot3(m, m) for m in pw]
    for level in range(levels):
        if level + 1 < levels:
            both = [pair_dot3(jnp.concatenate([inv[i], pw[i]], axis=0), pw[i]) for i in rng]
            inv = [inv[i] + both[i][:c] for i in rng]
            pw = [both[i][c:] for i in rng]
        else:
            inv = [inv[i] + pair_dot3(inv[i], pw[i]) for i in rng]
    ef = [jnp.exp(g) for g in gcf]
    eb = [jnp.exp(g) for g in gcb]
    uw = [_bdot(split_rows(inv[i]),
                jnp.concatenate([jnp.concatenate([fw[i]["v"] * fw[i]["beta"], kbf[i] * ef[i]], axis=1),
                                 jnp.concatenate([bw[i]["v"] * bw[i]["beta"], kbb[i] * eb[i]], axis=1)], axis=0))
          for i in rng]
    a = [jnp.where(incl, kq[i][c:] * decay[i], 0.0) for i in rng]
    glf = [g[c - 1:c, :] for g in gcf]
    glb = [g[0:1, :] for g in gcb]
    wqf = [_bdot(jnp.concatenate([uw[i][:c, GDN_DV:], qf[i] * ef[i]], axis=0), fw[i]["state"]) for i in rng]
    wqb = [_bdot(jnp.concatenate([uw[i][c:, GDN_DV:], qb[i] * eb[i]], axis=0), bw[i]["state"]) for i in rng]
    vnf = [uw[i][:c, :GDN_DV] - wqf[i][:c] for i in rng]
    vnb = [uw[i][c:, :GDN_DV] - wqb[i][:c] for i in rng]
    av = [_bdot(split_rows(a[i]), jnp.concatenate([vnf[i], vnb[i]], axis=0)) for i in rng]
    sf = [fw[i]["state"] * jnp.exp(glf[i]) + _bdot_tn(fw[i]["k"] * jnp.exp(glf[i] - gcf[i]), vnf[i]) for i in rng]
    sb = [bw[i]["state"] * jnp.exp(glb[i]) + _bdot_tn(bw[i]["k"] * jnp.exp(glb[i] - gcb[i]), vnb[i]) for i in rng]
    return [(wqf[i][c:] + av[i][:c], sf[i], wqb[i][c:] + av[i][c:], sb[i]) for i in rng]


def _gdn_kernel(*refs, context, n_par):
    qf_ref, qb_ref, mf_ref, mb_ref = refs[:4]
    st_ref = refs[-1]
    if context:
        of_ref, ob_ref, sfin_ref = refs[-4:-1]
    else:
        s0_ref = refs[4]
        of_ref, ob_ref = refs[-3:-1]
    n = pl.program_id(1)

    @pl.when(n == 0)
    def _():
        if context:
            st_ref[...] = jnp.zeros_like(st_ref)
        else:
            st_ref[...] = s0_ref[:, 0]

    def problem(qkv, misc, s, h, d):
        i = d * GDN_HEADS + h
        return dict(q=qkv[:, h * GDN_DK:(h + 1) * GDN_DK],
                    k=qkv[:, GDN_QK_DIM + h * GDN_DK:GDN_QK_DIM + (h + 1) * GDN_DK],
                    v=qkv[:, 2 * GDN_QK_DIM + h * GDN_DV:2 * GDN_QK_DIM + (h + 1) * GDN_DV],
                    g_col=misc[:, _M_G + i:_M_G + i + 1], beta=misc[:, _M_B + i:_M_B + i + 1],
                    state=st_ref[s, i])

    lane = lax.broadcasted_iota(jnp.int32, (1, 2 * GDN_CHUNK), 1)
    pairs = []
    for s in range(n_par):
        qkv_f, qkv_b, misc_f, misc_b = qf_ref[s], qb_ref[s], mf_ref[s], mb_ref[s]
        rows = _gate_rows(jnp.concatenate([misc_f, misc_b], axis=0))
        for h in range(GDN_HEADS):
            g_row = jnp.where(lane < GDN_CHUNK, rows[h:h + 1, :], rows[GDN_HEADS + h:GDN_HEADS + h + 1, :])
            pairs.append((problem(qkv_f, misc_f, s, h, 0), problem(qkv_b, misc_b, s, h, 1), g_row))
    results = _gdn_chunk_pairs(pairs)
    for s in range(n_par):
        for h in range(GDN_HEADS):
            o_f, s_f, o_b, s_b = results[s * GDN_HEADS + h]
            of_ref[s, :, h * GDN_DV:(h + 1) * GDN_DV] = o_f
            ob_ref[s, :, h * GDN_DV:(h + 1) * GDN_DV] = o_b
            st_ref[s, h] = s_f
            st_ref[s, GDN_HEADS + h] = s_b

    if context:
        @pl.when(n == pl.num_programs(1) - 1)
        def _():
            sfin_ref[:, 0] = st_ref[...]


def _gdn(qkv, misc, l, t, n_seq, seq0, s0, prevs, n_par=GDN_PAR):
    c = GDN_CHUNK
    nc = t // c
    context = s0 is None
    sb0 = seq0 // n_par
    n_all = N_TOK // t
    fwd = lambda s, n: (sb0 + s, n, 0)
    bwd = lambda s, n: (sb0 + s, nc - 1 - n, 0)
    st_spec = pl.BlockSpec((n_par, 1, GDN_NH, GDN_DK, GDN_DV), lambda s, n: (s, l, 0, 0, 0))
    qkv3 = qkv.reshape(n_all, t, GDN_CONV_CH)
    misc3 = misc.reshape(n_all, t, LANES)
    in_specs = [pl.BlockSpec((n_par, c, GDN_CONV_CH), fwd), pl.BlockSpec((n_par, c, GDN_CONV_CH), bwd),
                pl.BlockSpec((n_par, c, LANES), fwd), pl.BlockSpec((n_par, c, LANES), bwd)]
    args = [qkv3, qkv3, misc3, misc3]
    if not context:
        in_specs.append(st_spec)
        args.append(s0)
    out_specs = [pl.BlockSpec((n_par, c, GDN_V_DIM), fwd), pl.BlockSpec((n_par, c, GDN_V_DIM), bwd)]
    out_shape = [jax.ShapeDtypeStruct((n_all, t, GDN_V_DIM), F32)] * 2
    if context:
        out_specs.append(st_spec)
        out_shape.append(jax.ShapeDtypeStruct((n_seq, DEPTH, GDN_NH, GDN_DK, GDN_DV), F32))
    aliases = {}
    for j, p in enumerate(prevs):
        if p is not None:
            aliases[len(args)] = j
            args.append(p.reshape(out_shape[j].shape))
            in_specs.append(_any_spec())
    return pl.pallas_call(
        functools.partial(_gdn_kernel, context=context, n_par=n_par),
        grid=(n_seq // n_par, nc),
        in_specs=in_specs, out_specs=out_specs, out_shape=out_shape,
        input_output_aliases=aliases,
        scratch_shapes=[pltpu.VMEM((n_par, GDN_NH, GDN_DK, GDN_DV), F32)],
        compiler_params=_params("parallel", "arbitrary"),
        name="gdn",
    )(*args)


def _out_kernel(x_ref, mod_ref, om_ref, os_ref, gf_ref, gb_ref, gz_ref, gn_ref, w_ref, o_ref):
    gate = mod_ref[0, 0][:, 2 * D_MODEL:]
    s = gf_ref[...] + gb_ref[...]
    gz = gz_ref[...]
    cols = []
    for h in range(GDN_HEADS):
        sh = s[:, h * GDN_DV:(h + 1) * GDN_DV]
        cols.append(_rms(sh, gn_ref[0]) * _silu(gz[:, h * GDN_DV:(h + 1) * GDN_DV]))
    og = jnp.concatenate(cols, axis=1)
    n_m = MLA_HEADS * MLA_V
    n_s = SWA_HEADS * SWA_HD
    y = (jnp.dot(om_ref[...], w_ref[0, :n_m, :], preferred_element_type=F32)
         + jnp.dot(os_ref[...], w_ref[0, n_m:n_m + n_s, :], preferred_element_type=F32)
         + jnp.dot(og.astype(BF), w_ref[0, n_m + n_s:, :], preferred_element_type=F32))
    o_ref[...] = x_ref[...] + gate * y


def _out_proj(x, mods, l, o_mla, o_swa, o_gf, o_gb, gz, gdn_norm, w_out, tm=TM_OUT):
    row = lambda i: (i, 0)
    lay3 = lambda i: (l, 0, 0)
    return pl.pallas_call(
        _out_kernel,
        grid=(N_TOK // tm,),
        in_specs=[pl.BlockSpec((tm, D_MODEL), row),
                  _mod_spec(l, 1, tm),
                  pl.BlockSpec((tm, MLA_HEADS * MLA_V), row),
                  pl.BlockSpec((tm, SWA_HEADS * SWA_HD), row),
                  pl.BlockSpec((tm, GDN_V_DIM), row),
                  pl.BlockSpec((tm, GDN_V_DIM), row),
                  pl.BlockSpec((tm, GDN_V_DIM), row),
                  pl.BlockSpec((1, 1, GDN_DV), lay3),
                  pl.BlockSpec((1,) + w_out.shape[1:], lay3)],
        out_specs=pl.BlockSpec((tm, D_MODEL), row),
        out_shape=jax.ShapeDtypeStruct((N_TOK, D_MODEL), F32),
        compiler_params=_params("parallel"),
        name="out_proj",
    )(x, mods, o_mla, o_swa, o_gf, o_gb, gz, gdn_norm.reshape(DEPTH, 1, GDN_DV), w_out)


def _axial_rope(n_tokens, dim):
    f32 = np.float32
    rows = n_tokens // GRID_W
    row = np.repeat(np.arange(rows, dtype=f32), GRID_W)
    col = np.tile(np.arange(GRID_W, dtype=f32), rows)
    axis_dim = dim // 2
    inv_freq = (f32(1.0) / (f32(ROPE_BASE) ** (np.arange(0, axis_dim, 2, dtype=f32) / f32(axis_dim)))).astype(f32)
    ang_r = row[:, None] * inv_freq[None, :]
    ang_c = col[:, None] * inv_freq[None, :]
    ang = np.concatenate([ang_r, ang_r, ang_c, ang_c], axis=-1).astype(f32)
    return np.cos(ang).astype(f32), np.sin(ang).astype(f32)


def _rope_table(tm):
    cos_m, sin_m = _axial_rope(DEC_SEQ, MLA_ROPE)
    cos_s, sin_s = _axial_rope(DEC_SEQ, SWA_HD)
    t = DEC_SEQ
    one = lambda w: np.ones((t, w), np.float32)
    zero = lambda w: np.zeros((t, w), np.float32)
    lat = np.concatenate([
        cos_s, cos_s, sin_s, sin_s,
        cos_m, one(LANES - MLA_ROPE), sin_m, zero(LANES - MLA_ROPE),
        one(MLA_NOPE), cos_m, one(LANES - MLA_NOPE - MLA_ROPE),
        zero(MLA_NOPE), sin_m, zero(LANES - MLA_NOPE - MLA_ROPE)], axis=1)
    ident_row = np.concatenate([np.ones(LANES), np.zeros(LANES)] * 3).astype(np.float32)
    ident = np.broadcast_to(ident_row[None, :], (tm, _T_END))
    return jnp.asarray(np.concatenate([ident, lat], axis=0))


def _mixer_weights(w_in, mla_w_qb, mla_w_kvb):
    nl = DEPTH
    offs = np.cumsum([0, MLA_Q_LORA, MLA_KV_LORA, MLA_ROPE, SWA_HEADS * SWA_HD, SWA_KV_HEADS * SWA_HD,
                      SWA_KV_HEADS * SWA_HD, GDN_CONV_CH, GDN_V_DIM, 2 * GDN_NH])
    w_t = jnp.swapaxes(w_in, 1, 2)
    cq_ckv, krope, body, gates = (w_t[:, :offs[2]], w_t[:, offs[2]:offs[3]], w_t[:, offs[3]:offs[8]],
                                  w_t[:, offs[8]:])
    misc = jnp.concatenate([krope, gates, jnp.zeros((nl, LANES - _M_END, D_MODEL), F32)], axis=1)
    win = jnp.concatenate([cq_ckv, body, misc], axis=1).astype(BF)

    r = MLA_Q_LORA
    wq = mla_w_qb.reshape(nl, r, MLA_HEADS, MLA_NOPE + MLA_ROPE)
    pad = jnp.zeros((nl, r, MLA_HEADS, LANES - MLA_NOPE - MLA_ROPE), F32)
    wqb = jnp.concatenate([wq, pad], axis=-1).reshape(nl, r, MLA_HEADS * LANES).astype(BF)

    kvb = mla_w_kvb.reshape(nl, MLA_KV_LORA, MLA_HEADS, MLA_NOPE + MLA_V)
    k_nope = jnp.concatenate([kvb[..., :MLA_NOPE],
                              jnp.zeros((nl, MLA_KV_LORA, MLA_HEADS, LANES - MLA_NOPE), F32)],
                             axis=-1).reshape(nl, MLA_KV_LORA, MLA_HEADS * LANES)
    place = np.zeros((LANES, MLA_HEADS, LANES), np.float32)
    for i in range(MLA_ROPE):
        place[i, :, MLA_NOPE + i] = 1.0
    place = jnp.broadcast_to(jnp.asarray(place.reshape(1, LANES, MLA_HEADS * LANES)),
                             (nl, LANES, MLA_HEADS * LANES))
    wk = jnp.concatenate([k_nope, place], axis=1).astype(BF)
    wv = kvb[..., MLA_NOPE:].reshape(nl, MLA_KV_LORA, MLA_HEADS * MLA_V).astype(BF)
    return win, wqb, wk, wv


def _misc_rows(vals):
    rows = jnp.zeros((DEPTH, 1, LANES), F32)
    return rows.at[:, 0, _M_G:_M_B].set(vals.reshape(DEPTH, GDN_NH).astype(F32))


def kernel(x_prompt, x_sample, cache_mla_ckv, cache_mla_krope, cache_swa_k, cache_swa_v, state_gdn, c, c_ctx,
           w_ada, b_ada, norm_ffn1, ffn1_w1, ffn1_w2, norm_mix, w_in, mla_q_norm, mla_w_qb, mla_kv_norm,
           mla_w_kvb, swa_sink, gdn_conv_w, gdn_a_log, gdn_dt_bias, gdn_norm, w_out, norm_ffn2, ffn2_w1,
           ffn2_w2, final_norm):
    cond = jnp.concatenate([c_ctx[None, :], c, jnp.zeros((COND_ROWS - N_GROUPS, D_MODEL), F32)], axis=0)
    mods = _adaln(cond, w_ada, b_ada)[:, :N_GROUPS].reshape(DEPTH, N_GROUPS, 1, N_MOD * D_MODEL)
    tab = _rope_table(TM_PROJ)
    win, wqb, wk, wv = _mixer_weights(w_in, mla_w_qb, mla_w_kvb)
    w11, w12, w21, w22 = ffn1_w1, ffn1_w2, ffn2_w1, ffn2_w2
    wo = w_out.astype(BF)
    alog, dtb = _misc_rows(gdn_a_log), _misc_rows(gdn_dt_bias)
    sink_rows = jnp.broadcast_to(swa_sink[:, :, None], (DEPTH, SWA_HEADS, LANES))
    cache_misc = jnp.pad(cache_mla_krope, ((0, 0), (0, 0), (0, 0), (0, LANES - MLA_ROPE)))
    cache_k = cache_swa_k.reshape(DEC_BATCH, DEPTH, PAST_LEN, LANES)
    cache_v = cache_swa_v.reshape(DEC_BATCH, DEPTH, PAST_LEN, LANES)
    s0 = state_gdn.reshape(DEC_BATCH, DEPTH, GDN_NH, GDN_DK, GDN_DV)

    xs = [x_prompt.reshape(N_CTX, D_MODEL), x_sample.reshape(N_LAT, D_MODEL)]
    caches = new_st = None
    for l in range(DEPTH):
        x = _ffn(xs, mods, l, 0, norm_ffn1, w11, w12)
        (q_mla, misc, k_mla, v_mla, sq, sk, sv, g3, gz, *caches) = _proj(
            x, mods, l, norm_mix, win, mla_q_norm, wqb, mla_kv_norm, wk, wv, tab, alog, dtb, caches)

        k_c, v_c = _kv_cache(cache_mla_ckv, cache_misc, l, wk, wv)
        o_mla = _mla_attend(q_mla, [k_mla], [v_mla], BATCH, SEQ, 0, [(0, SEQ)], None)
        o_mla = _mla_attend(q_mla, [k_mla, k_c], [v_mla, v_c], DEC_BATCH, DEC_SEQ, N_CTX,
                            [(N_CTX // DEC_SEQ, DEC_SEQ), (0, PAST_LEN)], o_mla)

        o_swa = _swa_ctx(sq, sk, sv, sink_rows, l)
        o_swa = _swa_lat(sq, sk, sv, cache_k, cache_v, sink_rows, l, o_swa)

        qkv = _gdn_conv(g3, gdn_conv_w, l, 0, BATCH, SEQ, None)
        qkv = _gdn_conv(g3, gdn_conv_w, l, N_CTX, DEC_BATCH, DEC_SEQ, qkv)
        o_gf, o_gb, new_st = _gdn(qkv, misc, l, SEQ, BATCH, 0, None, [None, None, new_st], n_par=GDN_PAR_CTX)
        o_gf, o_gb = _gdn(qkv, misc, l, DEC_SEQ, DEC_BATCH, N_CTX // DEC_SEQ, s0, [o_gf, o_gb])

        x = _out_proj(x, mods, l, o_mla, o_swa, o_gf.reshape(N_TOK, GDN_V_DIM), o_gb.reshape(N_TOK, GDN_V_DIM),
                      gz, gdn_norm, wo)
        if l + 1 < DEPTH:
            xs = [_ffn([x], mods, l, 2, norm_ffn2, w21, w22)]
        else:
            y_prompt, y_sample = _ffn([x], mods, l, 2, norm_ffn2, w21, w22, final_gain=final_norm)

    new_ckv, new_krope, new_sk, new_sv = caches
    kv_shape = (BATCH, DEPTH, SEQ, SWA_KV_HEADS, SWA_HD)
    return (y_prompt.reshape(BATCH, SEQ, D_MODEL), y_sample.reshape(DEC_BATCH, DEC_SEQ, D_MODEL), new_ckv,
            new_krope, new_sk.reshape(kv_shape), new_sv.reshape(kv_shape),
            new_st.reshape(BATCH, DEPTH, 2, GDN_HEADS, GDN_DK, GDN_DV))
```

```python
import functools

import numpy as np
import jax
import jax.numpy as jnp
from jax import lax
from jax.experimental import pallas as pl
from jax.experimental.pallas import tpu as pltpu

D_MODEL = 1024
BATCH = 16
SEQ = 256
DEPTH = 2
DEC_BATCH = 2
DEC_SEQ = 2048
PAST_LEN = 512
GRID_W = 64
ROPE_BASE = 10000.0
NORM_EPS = 1e-6
N_MOD = 9
D_FF = 2816
MLA_HEADS = 8
MLA_Q_LORA = 384
MLA_KV_LORA = 256
MLA_NOPE = 64
MLA_ROPE = 32
MLA_V = 64
SWA_HEADS = 8
SWA_KV_HEADS = 2
SWA_GROUP = SWA_HEADS // SWA_KV_HEADS
SWA_HD = 64
SWA_WINDOW = 128
GDN_HEADS = 4
GDN_DK = 128
GDN_DV = 128
GDN_CONV = 5
GDN_CHUNK = 64
GDN_QK_DIM = GDN_HEADS * GDN_DK
GDN_V_DIM = GDN_HEADS * GDN_DV
GDN_CONV_CH = 2 * GDN_QK_DIM + GDN_V_DIM
GDN_NH = 2 * GDN_HEADS

N_CTX = BATCH * SEQ
N_LAT = DEC_BATCH * DEC_SEQ
N_TOK = N_CTX + N_LAT
N_GROUPS = 1 + DEC_BATCH
COND_ROWS = 8

LANES = 128
SUBLANES = 8
VMEM_LIMIT_BYTES = 56 * 1024 * 1024

TM_FFN = 512
FFN_LOAD_COLS = 256
TM_PROJ = 2 * SEQ
TM_OUT = 512
TQ_MLA = 512
GDN_PAR = 2
GDN_PAR_CTX = 4

BF = jnp.bfloat16
F32 = jnp.float32
LOG2_E = 1.4426950408889634
MLA_Q_SCALE = (MLA_NOPE + MLA_ROPE) ** -0.5 * LOG2_E

_C_CQ = 0
_C_CKV = _C_CQ + MLA_Q_LORA
_C_SQ = _C_CKV + MLA_KV_LORA
_C_SK = _C_SQ + SWA_HEADS * SWA_HD
_C_SV = _C_SK + SWA_KV_HEADS * SWA_HD
_C_G3 = _C_SV + SWA_KV_HEADS * SWA_HD
_C_GZ = _C_G3 + GDN_CONV_CH
_C_MA = _C_GZ + GDN_V_DIM
_C_END = _C_MA + LANES
_M_G = MLA_ROPE
_M_B = MLA_ROPE + GDN_NH
_M_END = MLA_ROPE + 2 * GDN_NH
_CACHE_WIDTHS = (MLA_KV_LORA, MLA_ROPE, SWA_KV_HEADS * SWA_HD, SWA_KV_HEADS * SWA_HD)
_T_CS, _T_SS, _T_CA, _T_SA, _T_CQ, _T_SQ = (i * LANES for i in range(6))
_T_END = 6 * LANES


def _params(*sem):
    return pltpu.CompilerParams(dimension_semantics=sem, vmem_limit_bytes=VMEM_LIMIT_BYTES)


def _bdot(a, b):
    return jnp.dot(a.astype(BF), b.astype(BF), preferred_element_type=F32)


def _nt(a, b):
    return lax.dot_general(a, b, (((1,), (1,)), ((), ())), preferred_element_type=F32)


def _bdot_nt(a, b):
    return _nt(a.astype(BF), b.astype(BF))


def _bdot_tn(a, b):
    return lax.dot_general(a.astype(BF), b.astype(BF), (((0,), (0,)), ((), ())),
                           preferred_element_type=F32)


def _split3(a):
    b1 = a.astype(BF)
    r = a - b1.astype(F32)
    b2 = r.astype(BF)
    b3 = (r - b2.astype(F32)).astype(BF)
    return b1, b2, b3


def _silu(x):
    return x / (1.0 + jnp.exp(-x))


def _rms(x, gain, eps=NORM_EPS):
    return x * lax.rsqrt(jnp.mean(x * x, axis=-1, keepdims=True) + eps) * gain


def _group_of_row(r):
    return jnp.where(r < N_CTX, 0, 1 + (r - N_CTX) // DEC_SEQ)


def _any_spec():
    return pl.BlockSpec(memory_space=pl.ANY)


def _adaln_kernel(c_ref, w_ref, b_ref, o_ref):
    o_ref[0] = _bdot(_silu(c_ref[...]), w_ref[0]) + b_ref[0]


def _adaln(cond, w_ada, b_ada, tn=1536):
    n = N_MOD * D_MODEL
    return pl.pallas_call(
        _adaln_kernel,
        grid=(DEPTH, n // tn),
        in_specs=[pl.BlockSpec((COND_ROWS, D_MODEL), lambda l, j: (0, 0)),
                  pl.BlockSpec((1, D_MODEL, tn), lambda l, j: (l, 0, j)),
                  pl.BlockSpec((1, 1, tn), lambda l, j: (l, 0, j))],
        out_specs=pl.BlockSpec((1, COND_ROWS, tn), lambda l, j: (l, 0, j)),
        out_shape=jax.ShapeDtypeStruct((DEPTH, COND_ROWS, n), F32),
        compiler_params=_params("parallel", "parallel"),
        name="adaln",
    )(cond, w_ada, b_ada.reshape(DEPTH, 1, n))


def _mod_spec(l, which, tm):
    return pl.BlockSpec((1, 1, 1, 3 * D_MODEL), lambda i: (l, _group_of_row(i * tm), 0, which))


def _ffn_kernel(*refs, n_in, n_a_tiles, final, layer):
    x_refs = refs[:n_in]
    mod_ref, gain_ref, w1_hbm, w2_hbm = refs[n_in:n_in + 4]
    w1_ref, w2_ref, stage_g, stage_u, stage_d, sem_ref = refs[-6:]
    rest = refs[n_in + 4:-6]
    i = pl.program_id(0)

    if n_in == 2:
        x = jnp.where(i < n_a_tiles, x_refs[0][...], x_refs[1][...])
    else:
        x = x_refs[0][...]
    mod = mod_ref[0, 0]
    shift, scale, gate = mod[:, :D_MODEL], mod[:, D_MODEL:2 * D_MODEL], mod[:, 2 * D_MODEL:]
    hb = (_rms(x, gain_ref[0]) * (1.0 + scale) + shift).astype(BF)

    def finish(ffn):
        y = x + gate * (0.5 * ffn)
        if final:
            fg_ref, oa_ref, ob_ref = rest
            yn = _rms(y, fg_ref[...])

            @pl.when(i < n_a_tiles)
            def _():
                oa_ref[...] = yn

            @pl.when(i >= n_a_tiles)
            def _():
                ob_ref[...] = yn
        else:
            rest[0][...] = y

    fc = stage_d.shape[1]
    n_chunks = D_FF // fc

    def chunk_copies(c):
        slot = c % 2
        cols = pl.ds(c * fc, fc)
        return (pltpu.make_async_copy(w1_hbm.at[layer, :, cols], stage_g.at[slot], sem_ref.at[0, slot]),
                pltpu.make_async_copy(w1_hbm.at[layer, :, pl.ds(D_FF + c * fc, fc)], stage_u.at[slot],
                                      sem_ref.at[1, slot]),
                pltpu.make_async_copy(w2_hbm.at[layer, cols, :], stage_d.at[slot], sem_ref.at[2, slot]))

    @pl.when(i == 0)
    def _():
        for cp in chunk_copies(0):
            cp.start()
        acc = None
        for c in range(n_chunks):
            if c + 1 < n_chunks:
                for cp in chunk_copies(c + 1):
                    cp.start()
            for cp in chunk_copies(c):
                cp.wait()
            slot = c % 2
            wg, wu, wd = stage_g[slot].astype(BF), stage_u[slot].astype(BF), stage_d[slot].astype(BF)
            w1_ref[:, c * fc:(c + 1) * fc] = wg
            w1_ref[:, D_FF + c * fc:D_FF + (c + 1) * fc] = wu
            w2_ref[c * fc:(c + 1) * fc, :] = wd
            g = jnp.dot(hb, wg, preferred_element_type=F32)
            u = jnp.dot(hb, wu, preferred_element_type=F32)
            part = jnp.dot((_silu(g) * u).astype(BF), wd, preferred_element_type=F32)
            acc = part if acc is None else acc + part
        finish(acc)

    @pl.when(i > 0)
    def _():
        gu = jnp.dot(hb, w1_ref[...], preferred_element_type=F32)
        a = _silu(gu[:, :D_FF]) * gu[:, D_FF:]
        finish(jnp.dot(a.astype(BF), w2_ref[...], preferred_element_type=F32))


def _ffn(xs, mods, l, which, gain, w1, w2, final_gain=None, tm=TM_FFN):
    n_a = N_CTX // tm
    row = lambda i: (i, 0)
    first = lambda i: (jnp.minimum(i, n_a - 1), 0)
    second = lambda i: (jnp.maximum(i - n_a, 0), 0)
    lay3 = lambda i: (l, 0, 0)
    x_specs = ([pl.BlockSpec((tm, D_MODEL), row)] if len(xs) == 1
               else [pl.BlockSpec((tm, D_MODEL), first), pl.BlockSpec((tm, D_MODEL), second)])
    in_specs = x_specs + [_mod_spec(l, which, tm),
                          pl.BlockSpec((1, 1, D_MODEL), lay3),
                          _any_spec(), _any_spec()]
    args = list(xs) + [mods, gain.reshape(DEPTH, 1, D_MODEL), w1, w2]
    if final_gain is None:
        out_specs = pl.BlockSpec((tm, D_MODEL), row)
        out_shape = jax.ShapeDtypeStruct((N_TOK, D_MODEL), F32)
    else:
        in_specs.append(pl.BlockSpec((1, D_MODEL), lambda i: (0, 0)))
        args.append(final_gain.reshape(1, D_MODEL))
        out_specs = [pl.BlockSpec((tm, D_MODEL), first), pl.BlockSpec((tm, D_MODEL), second)]
        out_shape = [jax.ShapeDtypeStruct((N_CTX, D_MODEL), F32), jax.ShapeDtypeStruct((N_LAT, D_MODEL), F32)]
    return pl.pallas_call(
        functools.partial(_ffn_kernel, n_in=len(xs), n_a_tiles=n_a, final=final_gain is not None, layer=l),
        grid=(N_TOK // tm,),
        in_specs=in_specs, out_specs=out_specs, out_shape=out_shape,
        scratch_shapes=[pltpu.VMEM((D_MODEL, 2 * D_FF), BF), pltpu.VMEM((D_FF, D_MODEL), BF),
                        pltpu.VMEM((2, D_MODEL, FFN_LOAD_COLS), F32), pltpu.VMEM((2, D_MODEL, FFN_LOAD_COLS), F32),
                        pltpu.VMEM((2, FFN_LOAD_COLS, D_MODEL), F32), pltpu.SemaphoreType.DMA((3, 2))],
        compiler_params=_params("arbitrary"),
        name="ffn",
    )(*args)


def _kv_expand(ckv_n, misc, wk, wv_t):
    kin = jnp.concatenate([ckv_n, misc], axis=1).astype(BF)
    kk = jnp.dot(kin, wk, preferred_element_type=F32)
    vv_t = _nt(wv_t, ckv_n.astype(BF))
    return kk.astype(BF), vv_t.astype(BF)


def _proj_kernel(*refs, n_ctx_tiles):
    (x_ref, mod_ref, gain_ref, win_ref, qg_ref, wqb_ref, kvg_ref, wk_ref, wv_ref, tab_ref,
     alog_ref, dtb_ref) = refs[:12]
    (q_ref, misc_ref, kmla_ref, vmla_ref, sq_ref, sk_ref, sv_ref, g3_ref, gz_ref,
     c_ckv_ref, c_krope_ref, c_sk_ref, c_sv_ref) = refs[-13:]
    i = pl.program_id(0)
    mod = mod_ref[0, 0]
    shift, scale = mod[:, :D_MODEL], mod[:, D_MODEL:2 * D_MODEL]
    hb = (_rms(x_ref[...], gain_ref[0]) * (1.0 + scale) + shift).astype(BF)
    tab = tab_ref[...]

    def project(c0, c1):
        return _nt(hb, win_ref[0, c0:c1, :])

    lane = lax.broadcasted_iota(jnp.int32, (x_ref.shape[0], LANES), 1)

    def rotate_half(t, quarter):
        even = (lane // quarter) % 2 == 0
        tiles = [t[:, c:c + LANES] for c in range(0, t.shape[1], LANES)]
        out = [jnp.where(even, -pltpu.roll(v, LANES - quarter, 1), pltpu.roll(v, quarter, 1)) for v in tiles]
        return out[0] if len(out) == 1 else jnp.concatenate(out, axis=1)

    u_lora = project(_C_CQ, _C_SQ)
    u_misc = project(_C_MA, _C_END)
    u_sq = project(_C_SQ, _C_SK)
    u_skv = project(_C_SK, _C_G3)

    qn = _rms(u_lora[:, :MLA_Q_LORA], qg_ref[0])
    ckv_n = _rms(u_lora[:, MLA_Q_LORA:], kvg_ref[0])

    m = (u_misc * tab[:, _T_CA:_T_CA + LANES]
         + rotate_half(u_misc, MLA_ROPE // 4) * tab[:, _T_SA:_T_SA + LANES])
    z = m + dtb_ref[0]
    softplus = jnp.maximum(z, 0.0) + jnp.log(1.0 + jnp.exp(-jnp.abs(z)))
    decay = -jnp.exp(alog_ref[0]) * softplus
    strength = 1.0 / (1.0 + jnp.exp(-m))
    misc = jnp.where((lane >= _M_G) & (lane < _M_B), decay,
                     jnp.where((lane >= _M_B) & (lane < _M_END), strength, m))
    misc_ref[...] = misc

    n_sq = SWA_HEADS * SWA_HD
    cos_s = tab[:, _T_CS:_T_CS + LANES]
    sin_s = tab[:, _T_SS:_T_SS + LANES]
    cos4 = jnp.concatenate([cos_s] * (n_sq // LANES), axis=1)
    sin4 = jnp.concatenate([sin_s] * (n_sq // LANES), axis=1)
    sq_ref[...] = ((u_sq * cos4 + rotate_half(u_sq, SWA_HD // 4) * sin4) * (SWA_HD ** -0.5 * LOG2_E)).astype(BF)
    sk = u_skv[:, :LANES] * cos_s + rotate_half(u_skv[:, :LANES], SWA_HD // 4) * sin_s
    sv = u_skv[:, LANES:]
    sk_ref[...] = sk
    sv_ref[...] = sv

    @pl.when(i < n_ctx_tiles)
    def _():
        for j in range(c_ckv_ref.shape[0]):
            rows = slice(j * SEQ, (j + 1) * SEQ)
            c_ckv_ref[j, 0] = ckv_n[rows]
            c_krope_ref[j, 0] = misc[rows, :MLA_ROPE]
            c_sk_ref[j, 0] = sk[rows]
            c_sv_ref[j, 0] = sv[rows]

    g3_ref[...] = project(_C_G3, _C_GZ)
    gz_ref[...] = project(_C_GZ, _C_MA)

    q2 = jnp.dot(qn.astype(BF), wqb_ref[0], preferred_element_type=F32)
    cosq = jnp.concatenate([tab[:, _T_CQ:_T_CQ + LANES]] * MLA_HEADS, axis=1)
    sinq = jnp.concatenate([tab[:, _T_SQ:_T_SQ + LANES]] * MLA_HEADS, axis=1)
    q_ref[...] = ((q2 * cosq + rotate_half(q2, MLA_ROPE // 4) * sinq) * MLA_Q_SCALE).astype(BF)
    kk, vv = _kv_expand(ckv_n, misc, wk_ref[0], wv_ref[0])
    kmla_ref[...] = kk
    vmla_ref[...] = vv


def _proj(x, mods, l, gain, win, qg, wqb, kvg, wk, wv, tab, alog, dtb, cache_prev, tm=TM_PROJ):
    assert tm % SEQ == 0
    n_ctx_tiles = N_CTX // tm
    lat_tiles = DEC_SEQ // tm
    lay3 = lambda i: (l, 0, 0)
    row = lambda i: (i, 0)

    def tab_map(i):
        return (jnp.where(i < n_ctx_tiles, 0, 1 + (i - n_ctx_tiles) % lat_tiles), 0)

    def lay_spec(a):
        return pl.BlockSpec((1,) + a.shape[1:], lay3)

    widths = [(MLA_HEADS * LANES, BF), (LANES, F32), (MLA_HEADS * LANES, BF),
              (MLA_HEADS * MLA_V, BF), (SWA_HEADS * SWA_HD, BF), (SWA_KV_HEADS * SWA_HD, F32),
              (SWA_KV_HEADS * SWA_HD, F32), (GDN_CONV_CH, F32), (GDN_V_DIM, F32)]
    out_specs = [pl.BlockSpec((tm, w), row) for w, _ in widths]
    out_shape = [jax.ShapeDtypeStruct((N_TOK, w), dt) for w, dt in widths]
    out_specs[3] = pl.BlockSpec((MLA_HEADS * MLA_V, tm), lambda i: (0, i))
    out_shape[3] = jax.ShapeDtypeStruct((MLA_HEADS * MLA_V, N_TOK), BF)
    for w in _CACHE_WIDTHS:
        out_specs.append(pl.BlockSpec((tm // SEQ, 1, SEQ, w),
                                      lambda i: (jnp.minimum(i, n_ctx_tiles - 1), l, 0, 0)))
        out_shape.append(jax.ShapeDtypeStruct((BATCH, DEPTH, SEQ, w), F32))
    qg, kvg = qg.reshape(DEPTH, 1, -1), kvg.reshape(DEPTH, 1, -1)
    args = [x, mods, gain.reshape(DEPTH, 1, D_MODEL), win, qg, wqb, kvg, wk, wv, tab, alog, dtb]
    in_specs = [pl.BlockSpec((tm, D_MODEL), row), _mod_spec(l, 1, tm),
                pl.BlockSpec((1, 1, D_MODEL), lay3), lay_spec(win), lay_spec(qg), lay_spec(wqb),
                lay_spec(kvg), lay_spec(wk), lay_spec(wv), pl.BlockSpec((tm, _T_END), tab_map),
                lay_spec(alog), lay_spec(dtb)]
    aliases = {}
    if cache_prev is not None:
        for j, prev in enumerate(cache_prev):
            aliases[len(args)] = len(widths) + j
            args.append(prev)
            in_specs.append(_any_spec())
    return pl.pallas_call(
        functools.partial(_proj_kernel, n_ctx_tiles=n_ctx_tiles),
        grid=(N_TOK // tm,),
        in_specs=in_specs, out_specs=out_specs, out_shape=out_shape,
        input_output_aliases=aliases,
        compiler_params=_params("arbitrary"),
        name="proj",
    )(*args)


def _kv_cache_kernel(ckv_ref, misc_ref, wk_ref, wv_ref, k_ref, v_ref):
    kk, vv = _kv_expand(ckv_ref[0, 0], misc_ref[0, 0], wk_ref[0], wv_ref[0])
    k_ref[...] = kk
    v_ref[...] = vv


def _kv_cache(cache_ckv, cache_misc, l, wk, wv):
    lay3 = lambda b: (l, 0, 0)
    n = DEC_BATCH * PAST_LEN
    return pl.pallas_call(
        _kv_cache_kernel,
        grid=(DEC_BATCH,),
        in_specs=[pl.BlockSpec((1, 1, PAST_LEN, MLA_KV_LORA), lambda b: (b, l, 0, 0)),
                  pl.BlockSpec((1, 1, PAST_LEN, LANES), lambda b: (b, l, 0, 0)),
                  pl.BlockSpec((1,) + wk.shape[1:], lay3),
                  pl.BlockSpec((1,) + wv.shape[1:], lay3)],
        out_specs=[pl.BlockSpec((PAST_LEN, MLA_HEADS * LANES), lambda b: (b, 0)),
                   pl.BlockSpec((MLA_HEADS * MLA_V, PAST_LEN), lambda b: (0, b))],
        out_shape=[jax.ShapeDtypeStruct((n, MLA_HEADS * LANES), BF),
                   jax.ShapeDtypeStruct((MLA_HEADS * MLA_V, n), BF)],
        compiler_params=_params("parallel"),
        name="kv_cache",
    )(cache_ckv, cache_misc, wk, wv)


def _mla_kernel(*refs, n_seg):
    q_ref = refs[0]
    k_refs = refs[1:1 + n_seg]
    vt_refs = refs[1 + n_seg:1 + 2 * n_seg]
    o_ref = refs[-5]
    s_refs, p_refs = refs[-4:-2], refs[-2:]

    def scores(h):
        off = 0
        for k in k_refs:
            s_refs[h % 2][off:off + k.shape[0], :] = _nt(k[:, h * LANES:(h + 1) * LANES],
                                                         q_ref[:, h * LANES:(h + 1) * LANES])
            off += k.shape[0]

    scores(0)
    outs = []
    for h in range(MLA_HEADS):
        if h + 1 < MLA_HEADS:
            scores(h + 1)
        den = _softmax_numerator_t(s_refs[h % 2], p_refs[h % 2])
        acc, off = None, 0
        for vt in vt_refs:
            n_k = vt.shape[1]
            pv = jnp.dot(vt[h * MLA_V:(h + 1) * MLA_V, :], p_refs[h % 2][off:off + n_k, :],
                         preferred_element_type=F32)
            acc = pv if acc is None else acc + pv
            off += n_k
        outs.append(acc / den)
        if h % 2 == 1:
            pair = h // 2
            o_ref[:, pair * LANES:(pair + 1) * LANES] = jnp.concatenate(outs[-2:], axis=0).T.astype(BF)


def _mla_attend(q, ks, vs, n_batch, t, row0, k_blocks, prev):
    n_seg = len(ks)
    tq = min(TQ_MLA, t)
    n_keys = sum(s for _, s in k_blocks)
    qb0 = row0 // tq
    tiles = t // tq
    q_map = lambda b, i: (qb0 + b * tiles + i, 0)
    in_specs = [pl.BlockSpec((tq, MLA_HEADS * LANES), q_map)]
    for b0, s in k_blocks:
        in_specs.append(pl.BlockSpec((s, MLA_HEADS * LANES), functools.partial(lambda b, i, b0: (b0 + b, 0), b0=b0)))
    for b0, s in k_blocks:
        in_specs.append(pl.BlockSpec((MLA_HEADS * MLA_V, s), functools.partial(lambda b, i, b0: (0, b0 + b), b0=b0)))
    args = [q, *ks, *vs]
    aliases = {}
    if prev is not None:
        args.append(prev)
        in_specs.append(_any_spec())
        aliases = {len(args) - 1: 0}
    return pl.pallas_call(
        functools.partial(_mla_kernel, n_seg=n_seg),
        grid=(n_batch, tiles),
        in_specs=in_specs,
        out_specs=pl.BlockSpec((tq, MLA_HEADS * MLA_V), q_map),
        out_shape=jax.ShapeDtypeStruct((N_TOK, MLA_HEADS * MLA_V), BF),
        input_output_aliases=aliases,
        scratch_shapes=[pltpu.VMEM((n_keys, tq), F32)] * 2 + [pltpu.VMEM((n_keys, tq), BF)] * 2,
        compiler_params=_params("parallel", "parallel"),
        name="mla_attend",
    )(*args)


def _gqa_heads(q_ref, k_segs, v_segs, masks, sink_ref, o_ref, s_refs, p_refs):
    tq = q_ref.shape[0]
    lane_q = lax.broadcasted_iota(jnp.int32, (tq, LANES), 1)
    lo_q = lane_q < SWA_HD
    k_roll = [pltpu.roll(k, SWA_HD, 1) for k in k_segs]
    v_roll = [pltpu.roll(v, SWA_HD, 1) for v in v_segs]
    heads = range(SWA_KV_HEADS)
    sinks = []
    for kvh in heads:
        tiles = [q_ref[:, (kvh * 2 + j) * LANES:(kvh * 2 + j + 1) * LANES] for j in range(2)]
        zero = jnp.zeros_like(tiles[0])
        qs = jnp.concatenate([jnp.where(lo_q, tiles[0], zero), jnp.where(lo_q, zero, tiles[0]),
                              jnp.where(lo_q, tiles[1], zero), jnp.where(lo_q, zero, tiles[1])], axis=0)
        sinks.append(jnp.concatenate(
            [sink_ref[0, kvh * SWA_GROUP + g:kvh * SWA_GROUP + g + 1, :]
             for g in range(SWA_GROUP) for _ in range(tq // LANES)], axis=1) * LOG2_E)
        off = 0
        for k, kr, msk in zip(k_segs, k_roll, masks):
            lane_k = lax.broadcasted_iota(jnp.int32, k.shape, 1)
            first = (lane_k < SWA_HD) == (kvh == 0)
            kd = jnp.where(first, k, kr)
            s = _bdot_nt(kd, qs)
            if msk is not None:
                s = jnp.where(msk, s, -1e30)
            s_refs[kvh][off:off + k.shape[0], :] = s
            off += k.shape[0]
    dens = [_softmax_numerator_t(s_refs[kvh], p_refs[kvh], sinks[kvh]) for kvh in heads]
    for kvh in heads:
        acc, off = None, 0
        for v, vr in zip(v_segs, v_roll):
            lane_v = lax.broadcasted_iota(jnp.int32, v.shape, 1)
            first = (lane_v < SWA_HD) == (kvh == 0)
            vd = jnp.where(first, v, vr)
            pv = lax.dot_general(vd.astype(BF), p_refs[kvh][off:off + v.shape[0], :],
                                 (((0,), (0,)), ((), ())), preferred_element_type=F32)
            acc = pv if acc is None else acc + pv
            off += v.shape[0]
        o = (acc / dens[kvh]).T
        for j in range(2):
            o_ref[:, (kvh * 2 + j) * LANES:(kvh * 2 + j + 1) * LANES] = jnp.where(
                lo_q, o[(2 * j) * tq:(2 * j + 1) * tq], o[(2 * j + 1) * tq:(2 * j + 2) * tq]).astype(BF)


def _softmax_numerator_t(s_ref, p_ref, sink=None, row_block=128):
    keys, cols = s_ref.shape
    dens = []
    for c in range(0, cols, LANES):
        mx = s_ref[0:row_block, c:c + LANES]
        for r0 in range(row_block, keys, row_block):
            mx = jnp.maximum(mx, s_ref[r0:r0 + row_block, c:c + LANES])
        m = jnp.max(mx, axis=0, keepdims=True)
        if sink is not None:
            snk = sink[:, c:c + LANES]
            m = jnp.maximum(m, snk)
        mb = jnp.broadcast_to(m, (row_block, LANES))
        part = jnp.zeros((row_block, LANES), F32)
        for r0 in range(0, keys, row_block):
            p = jnp.exp2(s_ref[r0:r0 + row_block, c:c + LANES] - mb)
            part = part + p
            p_ref[r0:r0 + row_block, c:c + LANES] = p.astype(BF)
        den = jnp.sum(part, axis=0, keepdims=True)
        dens.append(den if sink is None else den + jnp.exp2(snk - m))
    return jnp.concatenate(dens, axis=1)


def _swa_scratch(keys, rows):
    return ([pltpu.VMEM((keys, rows), F32)] * SWA_KV_HEADS + [pltpu.VMEM((keys, rows), BF)] * SWA_KV_HEADS)


def _swa_ctx_kernel(q_ref, k_ref, v_ref, sink_ref, o_ref, *scratch):
    _gqa_heads(q_ref, [k_ref[...]], [v_ref[...]], [None], sink_ref, o_ref,
               scratch[:SWA_KV_HEADS], scratch[SWA_KV_HEADS:])


def _swa_ctx(sq, sk, sv, sink_rows, l):
    return pl.pallas_call(
        _swa_ctx_kernel,
        grid=(BATCH,),
        in_specs=[pl.BlockSpec((SEQ, SWA_HEADS * SWA_HD), lambda b: (b, 0)),
                  pl.BlockSpec((SEQ, LANES), lambda b: (b, 0)),
                  pl.BlockSpec((SEQ, LANES), lambda b: (b, 0)),
                  pl.BlockSpec((1, SWA_HEADS, LANES), lambda b: (l, 0, 0))],
        out_specs=pl.BlockSpec((SEQ, SWA_HEADS * SWA_HD), lambda b: (b, 0)),
        out_shape=jax.ShapeDtypeStruct((N_TOK, SWA_HEADS * SWA_HD), BF),
        scratch_shapes=_swa_scratch(SEQ, SWA_GROUP * SEQ),
        compiler_params=_params("parallel"),
        name="swa_ctx",
    )(sq, sk, sv, sink_rows)


def _swa_lat_kernel(q_ref, kp_ref, kc_ref, kn_ref, vp_ref, vc_ref, vn_ref, kx_ref, vx_ref, sink_ref, prev_ref,
                    o_ref, *scratch):
    w = SWA_WINDOW
    n = pl.program_id(1)
    nb = pl.num_programs(1)
    k_band = jnp.concatenate([kp_ref[...], kc_ref[...], kn_ref[...]], axis=0)
    v_band = jnp.concatenate([vp_ref[...], vc_ref[...], vn_ref[...]], axis=0)
    rows = SWA_GROUP * w
    r = lax.broadcasted_iota(jnp.int32, (3 * w, rows), 1) & (w - 1)
    c = lax.broadcasted_iota(jnp.int32, (3 * w, rows), 0)
    valid = (c >= r) & (c <= r + 2 * w) & ((c >= w) | (n > 0)) & ((c < 2 * w) | (n < nb - 1))
    _gqa_heads(q_ref, [k_band, kx_ref[0, 0]], [v_band, vx_ref[0, 0]], [valid, None], sink_ref, o_ref,
               scratch[:SWA_KV_HEADS], scratch[SWA_KV_HEADS:])


def _swa_lat(sq, sk, sv, k_cache, v_cache, sink_rows, l, prev):
    w = SWA_WINDOW
    nb = DEC_SEQ // w
    q0 = N_CTX // w

    def blk(d):
        return lambda b, n: (q0 + b * nb + jnp.clip(n + d, 0, nb - 1), 0)

    kv_specs = [pl.BlockSpec((w, LANES), blk(d)) for d in (-1, 0, 1)]
    cache_spec = pl.BlockSpec((1, 1, PAST_LEN, LANES), lambda b, n: (b, l, 0, 0))
    return pl.pallas_call(
        _swa_lat_kernel,
        grid=(DEC_BATCH, nb),
        in_specs=[pl.BlockSpec((w, SWA_HEADS * SWA_HD), blk(0))] + kv_specs + kv_specs
        + [cache_spec, cache_spec, pl.BlockSpec((1, SWA_HEADS, LANES), lambda b, n: (l, 0, 0)), _any_spec()],
        out_specs=pl.BlockSpec((w, SWA_HEADS * SWA_HD), blk(0)),
        out_shape=jax.ShapeDtypeStruct((N_TOK, SWA_HEADS * SWA_HD), BF),
        input_output_aliases={10: 0},
        scratch_shapes=_swa_scratch(3 * w + PAST_LEN, SWA_GROUP * w),
        compiler_params=_params("parallel", "parallel"),
        name="swa_lat",
    )(sq, sk, sk, sk, sv, sv, sv, k_cache, v_cache, sink_rows, prev)


def _gdn_conv_kernel(*refs):
    x_ref, w_ref, o_ref = refs[0], refs[1], refs[-1]
    x = x_ref[...]
    t = x.shape[0]
    w = w_ref[0]
    r8 = lax.broadcasted_iota(jnp.int32, (SUBLANES, x.shape[1]), 0)
    half = GDN_CONV // 2
    acc = x * w[half:half + 1, :]
    for k in range(GDN_CONV):
        d = k - half
        if d == 0:
            continue
        xs = pltpu.roll(x, (-d) % t, 0)
        top, bot = xs[:SUBLANES], xs[t - SUBLANES:]
        if d < 0:
            top = jnp.where(r8 + d >= 0, top, 0.0)
        else:
            bot = jnp.where(r8 + d < SUBLANES, bot, 0.0)
        xs = jnp.concatenate([top, xs[SUBLANES:t - SUBLANES], bot], axis=0)
        acc = acc + xs * w[k:k + 1, :]
    y = _silu(acc)
    is_qk = pl.program_id(1) < 2
    cols = []
    for h in range(y.shape[1] // LANES):
        yh = y[:, h * LANES:(h + 1) * LANES]
        nrm = lax.rsqrt(jnp.sum(yh * yh, axis=-1, keepdims=True) + 1e-6)
        cols.append(yh * jnp.where(is_qk, nrm, 1.0))
    o_ref[...] = jnp.concatenate(cols, axis=1)


def _gdn_conv(g3, conv_w, l, row0, n_seq, t, prev):
    blk0 = row0 // t
    args = [g3, conv_w]
    in_specs = [pl.BlockSpec((t, GDN_QK_DIM), lambda s, j: (blk0 + s, j)),
                pl.BlockSpec((1, GDN_CONV, GDN_QK_DIM), lambda s, j: (l, 0, j))]
    aliases = {}
    if prev is not None:
        args.append(prev)
        in_specs.append(_any_spec())
        aliases = {2: 0}
    return pl.pallas_call(
        _gdn_conv_kernel,
        grid=(n_seq, 3),
        in_specs=in_specs,
        out_specs=pl.BlockSpec((t, GDN_QK_DIM), lambda s, j: (blk0 + s, j)),
        out_shape=jax.ShapeDtypeStruct((N_TOK, GDN_CONV_CH), F32),
        input_output_aliases=aliases,
        compiler_params=_params("parallel", "parallel"),
        name="gdn_conv",
    )(*args)


def _gate_rows(misc):
    shape = (2 * GDN_NH, LANES)
    sel = (lax.broadcasted_iota(jnp.int32, shape, 1)
           == lax.broadcasted_iota(jnp.int32, shape, 0) + _M_G).astype(BF)
    b1, b2, b3 = _split3(misc)
    return _nt(sel, b1) + (_nt(sel, b2) + _nt(sel, b3))


def _gdn_chunk_pairs(pairs):
    c = GDN_CHUNK
    shape = (c, 2 * c)
    ri = lax.broadcasted_iota(jnp.int32, shape, 0)
    lane = lax.broadcasted_iota(jnp.int32, shape, 1)
    cj = lane & (c - 1)
    fwd_half = lane < c
    lower, upper = ri >= cj, ri <= cj
    incl = (fwd_half & lower) | (~fwd_half & upper)
    incl_t = (fwd_half & upper) | (~fwd_half & lower)
    strict = incl & (ri != cj)
    incl_f, incl_b = incl & fwd_half, incl & ~fwd_half
    eye = (ri == cj).astype(F32)
    n = len(pairs)
    rng = range(n)
    fw = [p[0] for p in pairs]
    bw = [p[1] for p in pairs]

    def halves(a, b):
        return jnp.where(fwd_half, a, b)

    def split_rows(m):
        return jnp.concatenate([jnp.where(fwd_half, m, 0.0), jnp.where(fwd_half, 0.0, m)], axis=0)

    def pair_dot3(x, p):
        xh = x.astype(BF)
        xl = (x - xh.astype(F32)).astype(BF)
        phf = p.astype(BF).astype(F32)
        bd_hi = split_rows(phf).astype(BF)
        bd_lo = split_rows(p - phf).astype(BF)
        return jnp.dot(jnp.concatenate([xh, xl, xh], axis=1), jnp.concatenate([bd_hi, bd_hi, bd_lo], axis=0),
                       preferred_element_type=F32)

    g_col = [halves(fw[i]["g_col"], bw[i]["g_col"]) for i in rng]
    gc_row = [jnp.sum(jnp.where(incl_t, g_col[i], 0.0), axis=0, keepdims=True) for i in rng]
    gcf = [jnp.sum(jnp.where(incl_f, pairs[i][2], 0.0), axis=1, keepdims=True) for i in rng]
    gcb = [jnp.sum(jnp.where(incl_b, pairs[i][2], 0.0), axis=1, keepdims=True) for i in rng]
    decay = [jnp.where(incl, jnp.exp(halves(gcf[i], gcb[i]) - gc_row[i]), 0.0) for i in rng]
    qf = [p["q"] * (GDN_DK ** -0.5) for p in fw]
    qb = [p["q"] * (GDN_DK ** -0.5) for p in bw]
    kbf = [p["k"] * p["beta"] for p in fw]
    kbb = [p["k"] * p["beta"] for p in bw]
    z = jnp.zeros((c, GDN_DK), F32)
    kq = [_bdot_nt(jnp.concatenate([jnp.concatenate([kbf[i], kbb[i]], axis=1),
                                    jnp.concatenate([qf[i], qb[i]], axis=1)], axis=0),
                   jnp.concatenate([jnp.concatenate([fw[i]["k"], z], axis=1),
                                    jnp.concatenate([z, bw[i]["k"]], axis=1)], axis=0)) for i in rng]
    pw = [jnp.where(strict, -(kq[i][:c] * decay[i]), 0.0) for i in rng]
    inv = [eye + m for m in pw]
    levels = int(np.log2(c)) - 1
    pw = [pair_dot3(m, m) for m in pw]
    for level in range(levels):
        if level + 1 < levels:
            both = [pair_dot3(jnp.concatenate([inv[i], pw[i]], axis=0), pw[i]) for i in rng]
            inv = [inv[i] + both[i][:c] for i in rng]
            pw = [both[i][c:] for i in rng]
        else:
            inv = [inv[i] + pair_dot3(inv[i], pw[i]) for i in rng]
    ef = [jnp.exp(g) for g in gcf]
    eb = [jnp.exp(g) for g in gcb]
    uw = [_bdot(split_rows(inv[i]),
                jnp.concatenate([jnp.concatenate([fw[i]["v"] * fw[i]["beta"], kbf[i] * ef[i]], axis=1),
                                 jnp.concatenate([bw[i]["v"] * bw[i]["beta"], kbb[i] * eb[i]], axis=1)], axis=0))
          for i in rng]
    a = [jnp.where(incl, kq[i][c:] * decay[i], 0.0) for i in rng]
    glf = [g[c - 1:c, :] for g in gcf]
    glb = [g[0:1, :] for g in gcb]
    wqf = [_bdot(jnp.concatenate([uw[i][:c, GDN_DV:], qf[i] * ef[i]], axis=0), fw[i]["state"]) for i in rng]
    wqb = [_bdot(jnp.concatenate([uw[i][c:, GDN_DV:], qb[i] * eb[i]], axis=0), bw[i]["state"]) for i in rng]
    vnf = [uw[i][:c, :GDN_DV] - wqf[i][:c] for i in rng]
    vnb = [uw[i][c:, :GDN_DV] - wqb[i][:c] for i in rng]
    av = [_bdot(split_rows(a[i]), jnp.concatenate([vnf[i], vnb[i]], axis=0)) for i in rng]
    sf = [fw[i]["state"] * jnp.exp(glf[i]) + _bdot_tn(fw[i]["k"] * jnp.exp(glf[i] - gcf[i]), vnf[i]) for i in rng]
    sb = [bw[i]["state"] * jnp.exp(glb[i]) + _bdot_tn(bw[i]["k"] * jnp.exp(glb[i] - gcb[i]), vnb[i]) for i in rng]
    return [(wqf[i][c:] + av[i][:c], sf[i], wqb[i][c:] + av[i][c:], sb[i]) for i in rng]


def _gdn_kernel(*refs, context, n_par):
    qf_ref, qb_ref, mf_ref, mb_ref = refs[:4]
    st_ref = refs[-1]
    if context:
        of_ref, ob_ref, sfin_ref = refs[-4:-1]
    else:
        s0_ref = refs[4]
        of_ref, ob_ref = refs[-3:-1]
    n = pl.program_id(1)

    @pl.when(n == 0)
    def _():
        if context:
            st_ref[...] = jnp.zeros_like(st_ref)
        else:
            st_ref[...] = s0_ref[:, 0]

    def problem(qkv, misc, s, h, d):
        i = d * GDN_HEADS + h
        return dict(q=qkv[:, h * GDN_DK:(h + 1) * GDN_DK],
                    k=qkv[:, GDN_QK_DIM + h * GDN_DK:GDN_QK_DIM + (h + 1) * GDN_DK],
                    v=qkv[:, 2 * GDN_QK_DIM + h * GDN_DV:2 * GDN_QK_DIM + (h + 1) * GDN_DV],
                    g_col=misc[:, _M_G + i:_M_G + i + 1], beta=misc[:, _M_B + i:_M_B + i + 1],
                    state=st_ref[s, i])

    lane = lax.broadcasted_iota(jnp.int32, (1, 2 * GDN_CHUNK), 1)
    pairs = []
    for s in range(n_par):
        qkv_f, qkv_b, misc_f, misc_b = qf_ref[s], qb_ref[s], mf_ref[s], mb_ref[s]
        rows = _gate_rows(jnp.concatenate([misc_f, misc_b], axis=0))
        for h in range(GDN_HEADS):
            g_row = jnp.where(lane < GDN_CHUNK, rows[h:h + 1, :], rows[GDN_HEADS + h:GDN_HEADS + h + 1, :])
            pairs.append((problem(qkv_f, misc_f, s, h, 0), problem(qkv_b, misc_b, s, h, 1), g_row))
    results = _gdn_chunk_pairs(pairs)
    for s in range(n_par):
        for h in range(GDN_HEADS):
            o_f, s_f, o_b, s_b = results[s * GDN_HEADS + h]
            of_ref[s, :, h * GDN_DV:(h + 1) * GDN_DV] = o_f
            ob_ref[s, :, h * GDN_DV:(h + 1) * GDN_DV] = o_b
            st_ref[s, h] = s_f
            st_ref[s, GDN_HEADS + h] = s_b

    if context:
        @pl.when(n == pl.num_programs(1) - 1)
        def _():
            sfin_ref[:, 0] = st_ref[...]


def _gdn(qkv, misc, l, t, n_seq, seq0, s0, prevs, n_par=GDN_PAR):
    c = GDN_CHUNK
    nc = t // c
    context = s0 is None
    sb0 = seq0 // n_par
    n_all = N_TOK // t
    fwd = lambda s, n: (sb0 + s, n, 0)
    bwd = lambda s, n: (sb0 + s, nc - 1 - n, 0)
    st_spec = pl.BlockSpec((n_par, 1, GDN_NH, GDN_DK, GDN_DV), lambda s, n: (s, l, 0, 0, 0))
    qkv3 = qkv.reshape(n_all, t, GDN_CONV_CH)
    misc3 = misc.reshape(n_all, t, LANES)
    in_specs = [pl.BlockSpec((n_par, c, GDN_CONV_CH), fwd), pl.BlockSpec((n_par, c, GDN_CONV_CH), bwd),
                pl.BlockSpec((n_par, c, LANES), fwd), pl.BlockSpec((n_par, c, LANES), bwd)]
    args = [qkv3, qkv3, misc3, misc3]
    if not context:
        in_specs.append(st_spec)
        args.append(s0)
    out_specs = [pl.BlockSpec((n_par, c, GDN_V_DIM), fwd), pl.BlockSpec((n_par, c, GDN_V_DIM), bwd)]
    out_shape = [jax.ShapeDtypeStruct((n_all, t, GDN_V_DIM), F32)] * 2
    if context:
        out_specs.append(st_spec)
        out_shape.append(jax.ShapeDtypeStruct((n_seq, DEPTH, GDN_NH, GDN_DK, GDN_DV), F32))
    aliases = {}
    for j, p in enumerate(prevs):
        if p is not None:
            aliases[len(args)] = j
            args.append(p.reshape(out_shape[j].shape))
            in_specs.append(_any_spec())
    return pl.pallas_call(
        functools.partial(_gdn_kernel, context=context, n_par=n_par),
        grid=(n_seq // n_par, nc),
        in_specs=in_specs, out_specs=out_specs, out_shape=out_shape,
        input_output_aliases=aliases,
        scratch_shapes=[pltpu.VMEM((n_par, GDN_NH, GDN_DK, GDN_DV), F32)],
        compiler_params=_params("parallel", "arbitrary"),
        name="gdn",
    )(*args)


def _out_kernel(x_ref, mod_ref, om_ref, os_ref, gf_ref, gb_ref, gz_ref, gn_ref, w_ref, o_ref):
    gate = mod_ref[0, 0][:, 2 * D_MODEL:]
    s = gf_ref[...] + gb_ref[...]
    gz = gz_ref[...]
    cols = []
    for h in range(GDN_HEADS):
        sh = s[:, h * GDN_DV:(h + 1) * GDN_DV]
        cols.append(_rms(sh, gn_ref[0]) * _silu(gz[:, h * GDN_DV:(h + 1) * GDN_DV]))
    og = jnp.concatenate(cols, axis=1)
    n_m = MLA_HEADS * MLA_V
    n_s = SWA_HEADS * SWA_HD
    y = (jnp.dot(om_ref[...], w_ref[0, :n_m, :], preferred_element_type=F32)
         + jnp.dot(os_ref[...], w_ref[0, n_m:n_m + n_s, :], preferred_element_type=F32)
         + jnp.dot(og.astype(BF), w_ref[0, n_m + n_s:, :], preferred_element_type=F32))
    o_ref[...] = x_ref[...] + gate * y


def _out_proj(x, mods, l, o_mla, o_swa, o_gf, o_gb, gz, gdn_norm, w_out, tm=TM_OUT):
    row = lambda i: (i, 0)
    lay3 = lambda i: (l, 0, 0)
    return pl.pallas_call(
        _out_kernel,
        grid=(N_TOK // tm,),
        in_specs=[pl.BlockSpec((tm, D_MODEL), row),
                  _mod_spec(l, 1, tm),
                  pl.BlockSpec((tm, MLA_HEADS * MLA_V), row),
                  pl.BlockSpec((tm, SWA_HEADS * SWA_HD), row),
                  pl.BlockSpec((tm, GDN_V_DIM), row),
                  pl.BlockSpec((tm, GDN_V_DIM), row),
                  pl.BlockSpec((tm, GDN_V_DIM), row),
                  pl.BlockSpec((1, 1, GDN_DV), lay3),
                  pl.BlockSpec((1,) + w_out.shape[1:], lay3)],
        out_specs=pl.BlockSpec((tm, D_MODEL), row),
        out_shape=jax.ShapeDtypeStruct((N_TOK, D_MODEL), F32),
        compiler_params=_params("parallel"),
        name="out_proj",
    )(x, mods, o_mla, o_swa, o_gf, o_gb, gz, gdn_norm.reshape(DEPTH, 1, GDN_DV), w_out)


def _axial_rope(n_tokens, dim):
    f32 = np.float32
    rows = n_tokens // GRID_W
    row = np.repeat(np.arange(rows, dtype=f32), GRID_W)
    col = np.tile(np.arange(GRID_W, dtype=f32), rows)
    axis_dim = dim // 2
    inv_freq = (f32(1.0) / (f32(ROPE_BASE) ** (np.arange(0, axis_dim, 2, dtype=f32) / f32(axis_dim)))).astype(f32)
    ang_r = row[:, None] * inv_freq[None, :]
    ang_c = col[:, None] * inv_freq[None, :]
    ang = np.concatenate([ang_r, ang_r, ang_c, ang_c], axis=-1).astype(f32)
    return np.cos(ang).astype(f32), np.sin(ang).astype(f32)


def _rope_table(tm):
    cos_m, sin_m = _axial_rope(DEC_SEQ, MLA_ROPE)
    cos_s, sin_s = _axial_rope(DEC_SEQ, SWA_HD)
    t = DEC_SEQ
    one = lambda w: np.ones((t, w), np.float32)
    zero = lambda w: np.zeros((t, w), np.float32)
    lat = np.concatenate([
        cos_s, cos_s, sin_s, sin_s,
        cos_m, one(LANES - MLA_ROPE), sin_m, zero(LANES - MLA_ROPE),
        one(MLA_NOPE), cos_m, one(LANES - MLA_NOPE - MLA_ROPE),
        zero(MLA_NOPE), sin_m, zero(LANES - MLA_NOPE - MLA_ROPE)], axis=1)
    ident_row = np.concatenate([np.ones(LANES), np.zeros(LANES)] * 3).astype(np.float32)
    ident = np.broadcast_to(ident_row[None, :], (tm, _T_END))
    return jnp.asarray(np.concatenate([ident, lat], axis=0))


def _mixer_weights(w_in, mla_w_qb, mla_w_kvb):
    nl = DEPTH
    offs = np.cumsum([0, MLA_Q_LORA, MLA_KV_LORA, MLA_ROPE, SWA_HEADS * SWA_HD, SWA_KV_HEADS * SWA_HD,
                      SWA_KV_HEADS * SWA_HD, GDN_CONV_CH, GDN_V_DIM, 2 * GDN_NH])
    w_t = jnp.swapaxes(w_in, 1, 2)
    cq_ckv, krope, body, gates = (w_t[:, :offs[2]], w_t[:, offs[2]:offs[3]], w_t[:, offs[3]:offs[8]],
                                  w_t[:, offs[8]:])
    misc = jnp.concatenate([krope, gates, jnp.zeros((nl, LANES - _M_END, D_MODEL), F32)], axis=1)
    win = jnp.concatenate([cq_ckv, body, misc], axis=1).astype(BF)

    r = MLA_Q_LORA
    wq = mla_w_qb.reshape(nl, r, MLA_HEADS, MLA_NOPE + MLA_ROPE)
    pad = jnp.zeros((nl, r, MLA_HEADS, LANES - MLA_NOPE - MLA_ROPE), F32)
    wqb = jnp.concatenate([wq, pad], axis=-1).reshape(nl, r, MLA_HEADS * LANES).astype(BF)

    kvb = mla_w_kvb.reshape(nl, MLA_KV_LORA, MLA_HEADS, MLA_NOPE + MLA_V)
    k_nope = jnp.concatenate([kvb[..., :MLA_NOPE],
                              jnp.zeros((nl, MLA_KV_LORA, MLA_HEADS, LANES - MLA_NOPE), F32)],
                             axis=-1).reshape(nl, MLA_KV_LORA, MLA_HEADS * LANES)
    place = np.zeros((LANES, MLA_HEADS, LANES), np.float32)
    for i in range(MLA_ROPE):
        place[i, :, MLA_NOPE + i] = 1.0
    place = jnp.broadcast_to(jnp.asarray(place.reshape(1, LANES, MLA_HEADS * LANES)),
                             (nl, LANES, MLA_HEADS * LANES))
    wk = jnp.concatenate([k_nope, place], axis=1).astype(BF)
    wv = jnp.swapaxes(kvb[..., MLA_NOPE:].reshape(nl, MLA_KV_LORA, MLA_HEADS * MLA_V), 1, 2).astype(BF)
    return win, wqb, wk, wv


def _misc_rows(vals):
    rows = jnp.zeros((DEPTH, 1, LANES), F32)
    return rows.at[:, 0, _M_G:_M_B].set(vals.reshape(DEPTH, GDN_NH).astype(F32))


def kernel(x_prompt, x_sample, cache_mla_ckv, cache_mla_krope, cache_swa_k, cache_swa_v, state_gdn, c, c_ctx,
           w_ada, b_ada, norm_ffn1, ffn1_w1, ffn1_w2, norm_mix, w_in, mla_q_norm, mla_w_qb, mla_kv_norm,
           mla_w_kvb, swa_sink, gdn_conv_w, gdn_a_log, gdn_dt_bias, gdn_norm, w_out, norm_ffn2, ffn2_w1,
           ffn2_w2, final_norm):
    cond = jnp.concatenate([c_ctx[None, :], c, jnp.zeros((COND_ROWS - N_GROUPS, D_MODEL), F32)], axis=0)
    mods = _adaln(cond, w_ada, b_ada)[:, :N_GROUPS].reshape(DEPTH, N_GROUPS, 1, N_MOD * D_MODEL)
    tab = _rope_table(TM_PROJ)
    win, wqb, wk, wv = _mixer_weights(w_in, mla_w_qb, mla_w_kvb)
    w11, w12, w21, w22 = ffn1_w1, ffn1_w2, ffn2_w1, ffn2_w2
    wo = w_out.astype(BF)
    alog, dtb = _misc_rows(gdn_a_log), _misc_rows(gdn_dt_bias)
    sink_rows = jnp.broadcast_to(swa_sink[:, :, None], (DEPTH, SWA_HEADS, LANES))
    cache_misc = jnp.pad(cache_mla_krope, ((0, 0), (0, 0), (0, 0), (0, LANES - MLA_ROPE)))
    cache_k = cache_swa_k.reshape(DEC_BATCH, DEPTH, PAST_LEN, LANES)
    cache_v = cache_swa_v.reshape(DEC_BATCH, DEPTH, PAST_LEN, LANES)
    s0 = state_gdn.reshape(DEC_BATCH, DEPTH, GDN_NH, GDN_DK, GDN_DV)

    xs = [x_prompt.reshape(N_CTX, D_MODEL), x_sample.reshape(N_LAT, D_MODEL)]
    caches = new_st = None
    for l in range(DEPTH):
        x = _ffn(xs, mods, l, 0, norm_ffn1, w11, w12)
        (q_mla, misc, k_mla, v_mla, sq, sk, sv, g3, gz, *caches) = _proj(
            x, mods, l, norm_mix, win, mla_q_norm, wqb, mla_kv_norm, wk, wv, tab, alog, dtb, caches)

        k_c, v_c = _kv_cache(cache_mla_ckv, cache_misc, l, wk, wv)
        o_mla = _mla_attend(q_mla, [k_mla], [v_mla], BATCH, SEQ, 0, [(0, SEQ)], None)
        o_mla = _mla_attend(q_mla, [k_mla, k_c], [v_mla, v_c], DEC_BATCH, DEC_SEQ, N_CTX,
                            [(N_CTX // DEC_SEQ, DEC_SEQ), (0, PAST_LEN)], o_mla)

        o_swa = _swa_ctx(sq, sk, sv, sink_rows, l)
        o_swa = _swa_lat(sq, sk, sv, cache_k, cache_v, sink_rows, l, o_swa)

        qkv = _gdn_conv(g3, gdn_conv_w, l, 0, BATCH, SEQ, None)
        qkv = _gdn_conv(g3, gdn_conv_w, l, N_CTX, DEC_BATCH, DEC_SEQ, qkv)
        o_gf, o_gb, new_st = _gdn(qkv, misc, l, SEQ, BATCH, 0, None, [None, None, new_st], n_par=GDN_PAR_CTX)
        o_gf, o_gb = _gdn(qkv, misc, l, DEC_SEQ, DEC_BATCH, N_CTX // DEC_SEQ, s0, [o_gf, o_gb])

        x = _out_proj(x, mods, l, o_mla, o_swa, o_gf.reshape(N_TOK, GDN_V_DIM), o_gb.reshape(N_TOK, GDN_V_DIM),
                      gz, gdn_norm, wo)
        if l + 1 < DEPTH:
            xs = [_ffn([x], mods, l, 2, norm_ffn2, w21, w22)]
        else:
            y_prompt, y_sample = _ffn([x], mods, l, 2, norm_ffn2, w21, w22, final_gain=final_norm)

    new_ckv, new_krope, new_sk, new_sv = caches
    kv_shape = (BATCH, DEPTH, SEQ, SWA_KV_HEADS, SWA_HD)
    return (y_prompt.reshape(BATCH, SEQ, D_MODEL), y_sample.reshape(DEC_BATCH, DEC_SEQ, D_MODEL), new_ckv,
            new_krope, new_sk.reshape(kv_shape), new_sv.reshape(kv_shape),
            new_st.reshape(BATCH, DEPTH, 2, GDN_HEADS, GDN_DK, GDN_DV))
```

```python
import functools

import numpy as np
import jax
import jax.numpy as jnp
from jax import lax
from jax.experimental import pallas as pl
from jax.experimental.pallas import tpu as pltpu

D_MODEL = 1024
BATCH = 16
SEQ = 256
DEPTH = 2
DEC_BATCH = 2
DEC_SEQ = 2048
PAST_LEN = 512
GRID_W = 64
ROPE_BASE = 10000.0
NORM_EPS = 1e-6
N_MOD = 9
D_FF = 2816
MLA_HEADS = 8
MLA_Q_LORA = 384
MLA_KV_LORA = 256
MLA_NOPE = 64
MLA_ROPE = 32
MLA_V = 64
SWA_HEADS = 8
SWA_KV_HEADS = 2
SWA_GROUP = SWA_HEADS // SWA_KV_HEADS
SWA_HD = 64
SWA_WINDOW = 128
GDN_HEADS = 4
GDN_DK = 128
GDN_DV = 128
GDN_CONV = 5
GDN_CHUNK = 64
GDN_QK_DIM = GDN_HEADS * GDN_DK
GDN_V_DIM = GDN_HEADS * GDN_DV
GDN_CONV_CH = 2 * GDN_QK_DIM + GDN_V_DIM
GDN_NH = 2 * GDN_HEADS

N_CTX = BATCH * SEQ
N_LAT = DEC_BATCH * DEC_SEQ
N_TOK = N_CTX + N_LAT
N_GROUPS = 1 + DEC_BATCH
COND_ROWS = 8

LANES = 128
SUBLANES = 8
VMEM_LIMIT_BYTES = 56 * 1024 * 1024

TM_FFN = 512
FFN_LOAD_COLS = 256
TM_PROJ = 2 * SEQ
TM_OUT = 1024
TQ_MLA = 512
GDN_PAR = 2
GDN_PAR_CTX = 8

BF = jnp.bfloat16
F32 = jnp.float32
LOG2_E = 1.4426950408889634
MLA_Q_SCALE = (MLA_NOPE + MLA_ROPE) ** -0.5 * LOG2_E

_C_CQ = 0
_C_CKV = _C_CQ + MLA_Q_LORA
_C_SQ = _C_CKV + MLA_KV_LORA
_C_SK = _C_SQ + SWA_HEADS * SWA_HD
_C_SV = _C_SK + SWA_KV_HEADS * SWA_HD
_C_G3 = _C_SV + SWA_KV_HEADS * SWA_HD
_C_GZ = _C_G3 + GDN_CONV_CH
_C_MA = _C_GZ + GDN_V_DIM
_C_END = _C_MA + LANES
_M_G = MLA_ROPE
_M_B = MLA_ROPE + GDN_NH
_M_END = MLA_ROPE + 2 * GDN_NH
_CACHE_WIDTHS = (MLA_KV_LORA, MLA_ROPE, SWA_KV_HEADS * SWA_HD, SWA_KV_HEADS * SWA_HD)
_T_CS, _T_SS, _T_CA, _T_SA, _T_CQ, _T_SQ = (i * LANES for i in range(6))
_T_END = 6 * LANES


def _params(*sem):
    return pltpu.CompilerParams(dimension_semantics=sem, vmem_limit_bytes=VMEM_LIMIT_BYTES)


def _bdot(a, b):
    return jnp.dot(a.astype(BF), b.astype(BF), preferred_element_type=F32)


def _nt(a, b):
    return lax.dot_general(a, b, (((1,), (1,)), ((), ())), preferred_element_type=F32)


def _bdot_nt(a, b):
    return _nt(a.astype(BF), b.astype(BF))


def _bdot_tn(a, b):
    return lax.dot_general(a.astype(BF), b.astype(BF), (((0,), (0,)), ((), ())),
                           preferred_element_type=F32)


def _split3(a):
    b1 = a.astype(BF)
    r = a - b1.astype(F32)
    b2 = r.astype(BF)
    b3 = (r - b2.astype(F32)).astype(BF)
    return b1, b2, b3


def _silu(x):
    return x / (1.0 + jnp.exp(-x))


def _rms(x, gain, eps=NORM_EPS):
    return x * lax.rsqrt(jnp.mean(x * x, axis=-1, keepdims=True) + eps) * gain


def _group_of_row(r):
    return jnp.where(r < N_CTX, 0, 1 + (r - N_CTX) // DEC_SEQ)


def _any_spec():
    return pl.BlockSpec(memory_space=pl.ANY)


def _adaln_kernel(c_ref, w_ref, b_ref, o_ref):
    o_ref[0] = _bdot(_silu(c_ref[...]), w_ref[0]) + b_ref[0]


def _adaln(cond, w_ada, b_ada, tn=1536):
    n = N_MOD * D_MODEL
    return pl.pallas_call(
        _adaln_kernel,
        grid=(DEPTH, n // tn),
        in_specs=[pl.BlockSpec((COND_ROWS, D_MODEL), lambda l, j: (0, 0)),
                  pl.BlockSpec((1, D_MODEL, tn), lambda l, j: (l, 0, j)),
                  pl.BlockSpec((1, 1, tn), lambda l, j: (l, 0, j))],
        out_specs=pl.BlockSpec((1, COND_ROWS, tn), lambda l, j: (l, 0, j)),
        out_shape=jax.ShapeDtypeStruct((DEPTH, COND_ROWS, n), F32),
        compiler_params=_params("parallel", "parallel"),
        name="adaln",
    )(cond, w_ada, b_ada.reshape(DEPTH, 1, n))


def _mod_spec(l, which, tm):
    return pl.BlockSpec((1, 1, 1, 3 * D_MODEL), lambda i: (l, _group_of_row(i * tm), 0, which))


def _ffn_kernel(*refs, n_in, n_a_tiles, final, layer):
    x_refs = refs[:n_in]
    mod_ref, gain_ref, w1_hbm, w2_hbm = refs[n_in:n_in + 4]
    w1_ref, w2_ref, stage_g, stage_u, stage_d, sem_ref = refs[-6:]
    rest = refs[n_in + 4:-6]
    i = pl.program_id(0)

    if n_in == 2:
        x = jnp.where(i < n_a_tiles, x_refs[0][...], x_refs[1][...])
    else:
        x = x_refs[0][...]
    mod = mod_ref[0, 0]
    shift, scale, gate = mod[:, :D_MODEL], mod[:, D_MODEL:2 * D_MODEL], mod[:, 2 * D_MODEL:]
    hb = (_rms(x, gain_ref[0]) * (1.0 + scale) + shift).astype(BF)

    def finish(ffn):
        y = x + gate * (0.5 * ffn)
        if final:
            fg_ref, oa_ref, ob_ref = rest
            yn = _rms(y, fg_ref[...])

            @pl.when(i < n_a_tiles)
            def _():
                oa_ref[...] = yn

            @pl.when(i >= n_a_tiles)
            def _():
                ob_ref[...] = yn
        else:
            rest[0][...] = y

    fc = stage_d.shape[1]
    n_chunks = D_FF // fc

    def chunk_copies(c):
        slot = c % 2
        cols = pl.ds(c * fc, fc)
        return (pltpu.make_async_copy(w1_hbm.at[layer, :, cols], stage_g.at[slot], sem_ref.at[0, slot]),
                pltpu.make_async_copy(w1_hbm.at[layer, :, pl.ds(D_FF + c * fc, fc)], stage_u.at[slot],
                                      sem_ref.at[1, slot]),
                pltpu.make_async_copy(w2_hbm.at[layer, cols, :], stage_d.at[slot], sem_ref.at[2, slot]))

    @pl.when(i == 0)
    def _():
        for cp in chunk_copies(0):
            cp.start()
        acc = None
        for c in range(n_chunks):
            if c + 1 < n_chunks:
                for cp in chunk_copies(c + 1):
                    cp.start()
            for cp in chunk_copies(c):
                cp.wait()
            slot = c % 2
            wg, wu, wd = stage_g[slot].astype(BF), stage_u[slot].astype(BF), stage_d[slot].astype(BF)
            w1_ref[:, c * fc:(c + 1) * fc] = wg
            w1_ref[:, D_FF + c * fc:D_FF + (c + 1) * fc] = wu
            w2_ref[c * fc:(c + 1) * fc, :] = wd
            g = jnp.dot(hb, wg, preferred_element_type=F32)
            u = jnp.dot(hb, wu, preferred_element_type=F32)
            part = jnp.dot((_silu(g) * u).astype(BF), wd, preferred_element_type=F32)
            acc = part if acc is None else acc + part
        finish(acc)

    @pl.when(i > 0)
    def _():
        gu = jnp.dot(hb, w1_ref[...], preferred_element_type=F32)
        a = _silu(gu[:, :D_FF]) * gu[:, D_FF:]
        finish(jnp.dot(a.astype(BF), w2_ref[...], preferred_element_type=F32))


def _ffn(xs, mods, l, which, gain, w1, w2, final_gain=None, tm=TM_FFN):
    n_a = N_CTX // tm
    row = lambda i: (i, 0)
    first = lambda i: (jnp.minimum(i, n_a - 1), 0)
    second = lambda i: (jnp.maximum(i - n_a, 0), 0)
    lay3 = lambda i: (l, 0, 0)
    x_specs = ([pl.BlockSpec((tm, D_MODEL), row)] if len(xs) == 1
               else [pl.BlockSpec((tm, D_MODEL), first), pl.BlockSpec((tm, D_MODEL), second)])
    in_specs = x_specs + [_mod_spec(l, which, tm),
                          pl.BlockSpec((1, 1, D_MODEL), lay3),
                          _any_spec(), _any_spec()]
    args = list(xs) + [mods, gain.reshape(DEPTH, 1, D_MODEL), w1, w2]
    if final_gain is None:
        out_specs = pl.BlockSpec((tm, D_MODEL), row)
        out_shape = jax.ShapeDtypeStruct((N_TOK, D_MODEL), F32)
    else:
        in_specs.append(pl.BlockSpec((1, D_MODEL), lambda i: (0, 0)))
        args.append(final_gain.reshape(1, D_MODEL))
        out_specs = [pl.BlockSpec((tm, D_MODEL), first), pl.BlockSpec((tm, D_MODEL), second)]
        out_shape = [jax.ShapeDtypeStruct((N_CTX, D_MODEL), F32), jax.ShapeDtypeStruct((N_LAT, D_MODEL), F32)]
    return pl.pallas_call(
        functools.partial(_ffn_kernel, n_in=len(xs), n_a_tiles=n_a, final=final_gain is not None, layer=l),
        grid=(N_TOK // tm,),
        in_specs=in_specs, out_specs=out_specs, out_shape=out_shape,
        scratch_shapes=[pltpu.VMEM((D_MODEL, 2 * D_FF), BF), pltpu.VMEM((D_FF, D_MODEL), BF),
                        pltpu.VMEM((2, D_MODEL, FFN_LOAD_COLS), F32), pltpu.VMEM((2, D_MODEL, FFN_LOAD_COLS), F32),
                        pltpu.VMEM((2, FFN_LOAD_COLS, D_MODEL), F32), pltpu.SemaphoreType.DMA((3, 2))],
        compiler_params=_params("arbitrary"),
        name="ffn",
    )(*args)


def _kv_expand(ckv_n, misc, wk, wv):
    kin = jnp.concatenate([ckv_n, misc], axis=1).astype(BF)
    kk = jnp.dot(kin, wk, preferred_element_type=F32)
    vv = jnp.dot(ckv_n.astype(BF), wv, preferred_element_type=F32)
    return kk.astype(BF), vv.astype(BF)


def _proj_kernel(*refs, n_ctx_tiles):
    (x_ref, mod_ref, gain_ref, win_ref, qg_ref, wqb_ref, kvg_ref, wk_ref, wv_ref, tab_ref,
     alog_ref, dtb_ref) = refs[:12]
    (q_ref, misc_ref, kmla_ref, vmla_ref, sq_ref, sk_ref, sv_ref, g3_ref, gz_ref,
     c_ckv_ref, c_krope_ref, c_sk_ref, c_sv_ref) = refs[-13:]
    i = pl.program_id(0)
    mod = mod_ref[0, 0]
    shift, scale = mod[:, :D_MODEL], mod[:, D_MODEL:2 * D_MODEL]
    hb = (_rms(x_ref[...], gain_ref[0]) * (1.0 + scale) + shift).astype(BF)
    tab = tab_ref[...]

    def project(c0, c1):
        return _nt(hb, win_ref[0, c0:c1, :])

    lane = lax.broadcasted_iota(jnp.int32, (x_ref.shape[0], LANES), 1)

    def rotate_half(t, quarter):
        even = (lane // quarter) % 2 == 0
        tiles = [t[:, c:c + LANES] for c in range(0, t.shape[1], LANES)]
        out = [jnp.where(even, -pltpu.roll(v, LANES - quarter, 1), pltpu.roll(v, quarter, 1)) for v in tiles]
        return out[0] if len(out) == 1 else jnp.concatenate(out, axis=1)

    u_lora = project(_C_CQ, _C_SQ)
    u_misc = project(_C_MA, _C_END)
    u_sq = project(_C_SQ, _C_SK)
    u_skv = project(_C_SK, _C_G3)

    qn = _rms(u_lora[:, :MLA_Q_LORA], qg_ref[0])
    ckv_n = _rms(u_lora[:, MLA_Q_LORA:], kvg_ref[0])

    m = (u_misc * tab[:, _T_CA:_T_CA + LANES]
         + rotate_half(u_misc, MLA_ROPE // 4) * tab[:, _T_SA:_T_SA + LANES])
    z = m + dtb_ref[0]
    softplus = jnp.maximum(z, 0.0) + jnp.log(1.0 + jnp.exp(-jnp.abs(z)))
    decay = -jnp.exp(alog_ref[0]) * softplus
    strength = 1.0 / (1.0 + jnp.exp(-m))
    misc = jnp.where((lane >= _M_G) & (lane < _M_B), decay,
                     jnp.where((lane >= _M_B) & (lane < _M_END), strength, m))
    misc_ref[...] = misc

    n_sq = SWA_HEADS * SWA_HD
    cos_s = tab[:, _T_CS:_T_CS + LANES]
    sin_s = tab[:, _T_SS:_T_SS + LANES]
    cos4 = jnp.concatenate([cos_s] * (n_sq // LANES), axis=1)
    sin4 = jnp.concatenate([sin_s] * (n_sq // LANES), axis=1)
    sq_ref[...] = ((u_sq * cos4 + rotate_half(u_sq, SWA_HD // 4) * sin4) * (SWA_HD ** -0.5 * LOG2_E)).astype(BF)
    sk = u_skv[:, :LANES] * cos_s + rotate_half(u_skv[:, :LANES], SWA_HD // 4) * sin_s
    sv = u_skv[:, LANES:]
    sk_ref[...] = sk
    sv_ref[...] = sv

    @pl.when(i < n_ctx_tiles)
    def _():
        for j in range(c_ckv_ref.shape[0]):
            rows = slice(j * SEQ, (j + 1) * SEQ)
            c_ckv_ref[j, 0] = ckv_n[rows]
            c_krope_ref[j, 0] = misc[rows, :MLA_ROPE]
            c_sk_ref[j, 0] = sk[rows]
            c_sv_ref[j, 0] = sv[rows]

    g3_ref[...] = project(_C_G3, _C_GZ)
    gz_ref[...] = project(_C_GZ, _C_MA)

    q2 = jnp.dot(qn.astype(BF), wqb_ref[0], preferred_element_type=F32)
    cosq = jnp.concatenate([tab[:, _T_CQ:_T_CQ + LANES]] * MLA_HEADS, axis=1)
    sinq = jnp.concatenate([tab[:, _T_SQ:_T_SQ + LANES]] * MLA_HEADS, axis=1)
    q_ref[...] = ((q2 * cosq + rotate_half(q2, MLA_ROPE // 4) * sinq) * MLA_Q_SCALE).astype(BF)
    kk, vv = _kv_expand(ckv_n, misc, wk_ref[0], wv_ref[0])
    kmla_ref[...] = kk
    vmla_ref[...] = vv


def _proj(x, mods, l, gain, win, qg, wqb, kvg, wk, wv, tab, alog, dtb, cache_prev, tm=TM_PROJ):
    assert tm % SEQ == 0
    n_ctx_tiles = N_CTX // tm
    lat_tiles = DEC_SEQ // tm
    lay3 = lambda i: (l, 0, 0)
    row = lambda i: (i, 0)

    def tab_map(i):
        return (jnp.where(i < n_ctx_tiles, 0, 1 + (i - n_ctx_tiles) % lat_tiles), 0)

    def lay_spec(a):
        return pl.BlockSpec((1,) + a.shape[1:], lay3)

    widths = [(MLA_HEADS * LANES, BF), (LANES, F32), (MLA_HEADS * LANES, BF),
              (MLA_HEADS * MLA_V, BF), (SWA_HEADS * SWA_HD, BF), (SWA_KV_HEADS * SWA_HD, F32),
              (SWA_KV_HEADS * SWA_HD, F32), (GDN_CONV_CH, F32), (GDN_V_DIM, F32)]
    out_specs = [pl.BlockSpec((tm, w), row) for w, _ in widths]
    out_shape = [jax.ShapeDtypeStruct((N_TOK, w), dt) for w, dt in widths]
    for w in _CACHE_WIDTHS:
        out_specs.append(pl.BlockSpec((tm // SEQ, 1, SEQ, w),
                                      lambda i: (jnp.minimum(i, n_ctx_tiles - 1), l, 0, 0)))
        out_shape.append(jax.ShapeDtypeStruct((BATCH, DEPTH, SEQ, w), F32))
    qg, kvg = qg.reshape(DEPTH, 1, -1), kvg.reshape(DEPTH, 1, -1)
    args = [x, mods, gain.reshape(DEPTH, 1, D_MODEL), win, qg, wqb, kvg, wk, wv, tab, alog, dtb]
    in_specs = [pl.BlockSpec((tm, D_MODEL), row), _mod_spec(l, 1, tm),
                pl.BlockSpec((1, 1, D_MODEL), lay3), lay_spec(win), lay_spec(qg), lay_spec(wqb),
                lay_spec(kvg), lay_spec(wk), lay_spec(wv), pl.BlockSpec((tm, _T_END), tab_map),
                lay_spec(alog), lay_spec(dtb)]
    aliases = {}
    if cache_prev is not None:
        for j, prev in enumerate(cache_prev):
            aliases[len(args)] = len(widths) + j
            args.append(prev)
            in_specs.append(_any_spec())
    return pl.pallas_call(
        functools.partial(_proj_kernel, n_ctx_tiles=n_ctx_tiles),
        grid=(N_TOK // tm,),
        in_specs=in_specs, out_specs=out_specs, out_shape=out_shape,
        input_output_aliases=aliases,
        compiler_params=_params("arbitrary"),
        name="proj",
    )(*args)


def _kv_cache_kernel(ckv_ref, misc_ref, wk_ref, wv_ref, k_ref, v_ref):
    kk, vv = _kv_expand(ckv_ref[0, 0], misc_ref[0, 0], wk_ref[0], wv_ref[0])
    k_ref[...] = kk
    v_ref[...] = vv


def _kv_cache(cache_ckv, cache_misc, l, wk, wv):
    lay3 = lambda b: (l, 0, 0)
    n = DEC_BATCH * PAST_LEN
    return pl.pallas_call(
        _kv_cache_kernel,
        grid=(DEC_BATCH,),
        in_specs=[pl.BlockSpec((1, 1, PAST_LEN, MLA_KV_LORA), lambda b: (b, l, 0, 0)),
                  pl.BlockSpec((1, 1, PAST_LEN, LANES), lambda b: (b, l, 0, 0)),
                  pl.BlockSpec((1,) + wk.shape[1:], lay3),
                  pl.BlockSpec((1,) + wv.shape[1:], lay3)],
        out_specs=[pl.BlockSpec((PAST_LEN, MLA_HEADS * LANES), lambda b: (b, 0)),
                   pl.BlockSpec((PAST_LEN, MLA_HEADS * MLA_V), lambda b: (b, 0))],
        out_shape=[jax.ShapeDtypeStruct((n, MLA_HEADS * LANES), BF),
                   jax.ShapeDtypeStruct((n, MLA_HEADS * MLA_V), BF)],
        compiler_params=_params("parallel"),
        name="kv_cache",
    )(cache_ckv, cache_misc, wk, wv)


def _softmax_numerator(s_ref, p_ref, row_block=128):
    rows, cols = s_ref.shape
    mx = jnp.max(s_ref[...], axis=-1, keepdims=True)
    sums = []
    for r0 in range(0, rows, row_block):
        mb = jnp.broadcast_to(mx[r0:r0 + row_block], (row_block, LANES))
        part = jnp.zeros((row_block, LANES), F32)
        for c in range(0, cols, LANES):
            p = jnp.exp2(s_ref[r0:r0 + row_block, c:c + LANES] - mb)
            part = part + p
            p_ref[r0:r0 + row_block, c:c + LANES] = p.astype(BF)
        sums.append(jnp.sum(part, axis=-1, keepdims=True))
    return jnp.concatenate(sums, axis=0)


def _mla_kernel(*refs, n_seg):
    q_ref = refs[0]
    k_refs = refs[1:1 + n_seg]
    v_refs = refs[1 + n_seg:1 + 2 * n_seg]
    o_ref = refs[-5]
    s_refs, p_refs = refs[-4:-2], refs[-2:]
    tq = q_ref.shape[0]
    lane = lax.broadcasted_iota(jnp.int32, (tq, LANES), 1)

    def scores(h):
        off = 0
        for k in k_refs:
            s_refs[h % 2][:, off:off + k.shape[0]] = _nt(q_ref[:, h * LANES:(h + 1) * LANES],
                                                         k[:, h * LANES:(h + 1) * LANES])
            off += k.shape[0]

    scores(0)
    outs = []
    for h in range(MLA_HEADS):
        if h + 1 < MLA_HEADS:
            scores(h + 1)
        den = _softmax_numerator(s_refs[h % 2], p_refs[h % 2])
        pair = h // 2
        acc, off = None, 0
        for v in v_refs:
            pv = jnp.dot(p_refs[h % 2][:, off:off + v.shape[0]], v[:, pair * LANES:(pair + 1) * LANES],
                         preferred_element_type=F32)
            acc = pv if acc is None else acc + pv
            off += v.shape[0]
        outs.append(acc / den)
        if h % 2 == 1:
            o_ref[:, pair * LANES:(pair + 1) * LANES] = jnp.where(lane < MLA_V, outs[-2], outs[-1]).astype(BF)


def _mla_attend(q, ks, vs, n_batch, t, row0, k_blocks, prev):
    n_seg = len(ks)
    tq = min(TQ_MLA, t)
    n_keys = sum(s for _, s in k_blocks)
    qb0 = row0 // tq
    tiles = t // tq
    q_map = lambda b, i: (qb0 + b * tiles + i, 0)
    in_specs = [pl.BlockSpec((tq, MLA_HEADS * LANES), q_map)]
    for (b0, s), width in ([(kb, MLA_HEADS * LANES) for kb in k_blocks]
                           + [(kb, MLA_HEADS * MLA_V) for kb in k_blocks]):
        in_specs.append(pl.BlockSpec((s, width), functools.partial(lambda b, i, b0: (b0 + b, 0), b0=b0)))
    args = [q, *ks, *vs]
    aliases = {}
    if prev is not None:
        args.append(prev)
        in_specs.append(_any_spec())
        aliases = {len(args) - 1: 0}
    return pl.pallas_call(
        functools.partial(_mla_kernel, n_seg=n_seg),
        grid=(n_batch, tiles),
        in_specs=in_specs,
        out_specs=pl.BlockSpec((tq, MLA_HEADS * MLA_V), q_map),
        out_shape=jax.ShapeDtypeStruct((N_TOK, MLA_HEADS * MLA_V), BF),
        input_output_aliases=aliases,
        scratch_shapes=[pltpu.VMEM((tq, n_keys), F32)] * 2 + [pltpu.VMEM((tq, n_keys), BF)] * 2,
        compiler_params=_params("parallel", "parallel"),
        name="mla_attend",
    )(*args)


def _gqa_heads(q_ref, k_segs, v_segs, masks, sink_ref, o_ref, s_refs, p_refs):
    tq = q_ref.shape[0]
    lane_q = lax.broadcasted_iota(jnp.int32, (tq, LANES), 1)
    lo_q = lane_q < SWA_HD
    k_roll = [pltpu.roll(k, SWA_HD, 1) for k in k_segs]
    v_roll = [pltpu.roll(v, SWA_HD, 1) for v in v_segs]
    heads = range(SWA_KV_HEADS)
    sinks = []
    for kvh in heads:
        tiles = [q_ref[:, (kvh * 2 + j) * LANES:(kvh * 2 + j + 1) * LANES] for j in range(2)]
        zero = jnp.zeros_like(tiles[0])
        qs = jnp.concatenate([jnp.where(lo_q, tiles[0], zero), jnp.where(lo_q, zero, tiles[0]),
                              jnp.where(lo_q, tiles[1], zero), jnp.where(lo_q, zero, tiles[1])], axis=0)
        sinks.append(jnp.concatenate(
            [sink_ref[0, kvh * SWA_GROUP + g:kvh * SWA_GROUP + g + 1, :]
             for g in range(SWA_GROUP) for _ in range(tq // LANES)], axis=1) * LOG2_E)
        off = 0
        for k, kr, msk in zip(k_segs, k_roll, masks):
            lane_k = lax.broadcasted_iota(jnp.int32, k.shape, 1)
            first = (lane_k < SWA_HD) == (kvh == 0)
            kd = jnp.where(first, k, kr)
            s = _bdot_nt(kd, qs)
            if msk is not None:
                s = jnp.where(msk, s, -1e30)
            s_refs[kvh][off:off + k.shape[0], :] = s
            off += k.shape[0]
    dens = [_softmax_numerator_t(s_refs[kvh], p_refs[kvh], sinks[kvh]) for kvh in heads]
    for kvh in heads:
        acc, off = None, 0
        for v, vr in zip(v_segs, v_roll):
            lane_v = lax.broadcasted_iota(jnp.int32, v.shape, 1)
            first = (lane_v < SWA_HD) == (kvh == 0)
            vd = jnp.where(first, v, vr)
            pv = lax.dot_general(vd.astype(BF), p_refs[kvh][off:off + v.shape[0], :],
                                 (((0,), (0,)), ((), ())), preferred_element_type=F32)
            acc = pv if acc is None else acc + pv
            off += v.shape[0]
        o = (acc / dens[kvh]).T
        for j in range(2):
            o_ref[:, (kvh * 2 + j) * LANES:(kvh * 2 + j + 1) * LANES] = jnp.where(
                lo_q, o[(2 * j) * tq:(2 * j + 1) * tq], o[(2 * j + 1) * tq:(2 * j + 2) * tq]).astype(BF)


def _softmax_numerator_t(s_ref, p_ref, sink=None, row_block=128):
    keys, cols = s_ref.shape
    dens = []
    for c in range(0, cols, LANES):
        mx = s_ref[0:row_block, c:c + LANES]
        for r0 in range(row_block, keys, row_block):
            mx = jnp.maximum(mx, s_ref[r0:r0 + row_block, c:c + LANES])
        m = jnp.max(mx, axis=0, keepdims=True)
        if sink is not None:
            snk = sink[:, c:c + LANES]
            m = jnp.maximum(m, snk)
        mb = jnp.broadcast_to(m, (row_block, LANES))
        part = jnp.zeros((row_block, LANES), F32)
        for r0 in range(0, keys, row_block):
            p = jnp.exp2(s_ref[r0:r0 + row_block, c:c + LANES] - mb)
            part = part + p
            p_ref[r0:r0 + row_block, c:c + LANES] = p.astype(BF)
        den = jnp.sum(part, axis=0, keepdims=True)
        dens.append(den if sink is None else den + jnp.exp2(snk - m))
    return jnp.concatenate(dens, axis=1)


def _swa_scratch(keys, rows):
    return ([pltpu.VMEM((keys, rows), F32)] * SWA_KV_HEADS + [pltpu.VMEM((keys, rows), BF)] * SWA_KV_HEADS)


def _swa_ctx_kernel(q_ref, k_ref, v_ref, sink_ref, o_ref, *scratch):
    _gqa_heads(q_ref, [k_ref[...]], [v_ref[...]], [None], sink_ref, o_ref,
               scratch[:SWA_KV_HEADS], scratch[SWA_KV_HEADS:])


def _swa_ctx(sq, sk, sv, sink_rows, l):
    return pl.pallas_call(
        _swa_ctx_kernel,
        grid=(BATCH,),
        in_specs=[pl.BlockSpec((SEQ, SWA_HEADS * SWA_HD), lambda b: (b, 0)),
                  pl.BlockSpec((SEQ, LANES), lambda b: (b, 0)),
                  pl.BlockSpec((SEQ, LANES), lambda b: (b, 0)),
                  pl.BlockSpec((1, SWA_HEADS, LANES), lambda b: (l, 0, 0))],
        out_specs=pl.BlockSpec((SEQ, SWA_HEADS * SWA_HD), lambda b: (b, 0)),
        out_shape=jax.ShapeDtypeStruct((N_TOK, SWA_HEADS * SWA_HD), BF),
        scratch_shapes=_swa_scratch(SEQ, SWA_GROUP * SEQ),
        compiler_params=_params("parallel"),
        name="swa_ctx",
    )(sq, sk, sv, sink_rows)


def _swa_lat_kernel(q_ref, kp_ref, kc_ref, kn_ref, vp_ref, vc_ref, vn_ref, kx_ref, vx_ref, sink_ref, prev_ref,
                    o_ref, *scratch):
    w = SWA_WINDOW
    n = pl.program_id(1)
    nb = pl.num_programs(1)
    k_band = jnp.concatenate([kp_ref[...], kc_ref[...], kn_ref[...]], axis=0)
    v_band = jnp.concatenate([vp_ref[...], vc_ref[...], vn_ref[...]], axis=0)
    rows = SWA_GROUP * w
    r = lax.broadcasted_iota(jnp.int32, (3 * w, rows), 1) & (w - 1)
    c = lax.broadcasted_iota(jnp.int32, (3 * w, rows), 0)
    valid = (c >= r) & (c <= r + 2 * w) & ((c >= w) | (n > 0)) & ((c < 2 * w) | (n < nb - 1))
    _gqa_heads(q_ref, [k_band, kx_ref[0, 0]], [v_band, vx_ref[0, 0]], [valid, None], sink_ref, o_ref,
               scratch[:SWA_KV_HEADS], scratch[SWA_KV_HEADS:])


def _swa_lat(sq, sk, sv, k_cache, v_cache, sink_rows, l, prev):
    w = SWA_WINDOW
    nb = DEC_SEQ // w
    q0 = N_CTX // w

    def blk(d):
        return lambda b, n: (q0 + b * nb + jnp.clip(n + d, 0, nb - 1), 0)

    kv_specs = [pl.BlockSpec((w, LANES), blk(d)) for d in (-1, 0, 1)]
    cache_spec = pl.BlockSpec((1, 1, PAST_LEN, LANES), lambda b, n: (b, l, 0, 0))
    return pl.pallas_call(
        _swa_lat_kernel,
        grid=(DEC_BATCH, nb),
        in_specs=[pl.BlockSpec((w, SWA_HEADS * SWA_HD), blk(0))] + kv_specs + kv_specs
        + [cache_spec, cache_spec, pl.BlockSpec((1, SWA_HEADS, LANES), lambda b, n: (l, 0, 0)), _any_spec()],
        out_specs=pl.BlockSpec((w, SWA_HEADS * SWA_HD), blk(0)),
        out_shape=jax.ShapeDtypeStruct((N_TOK, SWA_HEADS * SWA_HD), BF),
        input_output_aliases={10: 0},
        scratch_shapes=_swa_scratch(3 * w + PAST_LEN, SWA_GROUP * w),
        compiler_params=_params("parallel", "parallel"),
        name="swa_lat",
    )(sq, sk, sk, sk, sv, sv, sv, k_cache, v_cache, sink_rows, prev)


def _gdn_conv_kernel(*refs):
    x_ref, w_ref, o_ref = refs[0], refs[1], refs[-1]
    x = x_ref[...]
    t = x.shape[0]
    w = w_ref[0]
    r8 = lax.broadcasted_iota(jnp.int32, (SUBLANES, x.shape[1]), 0)
    half = GDN_CONV // 2
    acc = x * w[half:half + 1, :]
    for k in range(GDN_CONV):
        d = k - half
        if d == 0:
            continue
        xs = pltpu.roll(x, (-d) % t, 0)
        top, bot = xs[:SUBLANES], xs[t - SUBLANES:]
        if d < 0:
            top = jnp.where(r8 + d >= 0, top, 0.0)
        else:
            bot = jnp.where(r8 + d < SUBLANES, bot, 0.0)
        xs = jnp.concatenate([top, xs[SUBLANES:t - SUBLANES], bot], axis=0)
        acc = acc + xs * w[k:k + 1, :]
    y = _silu(acc)
    is_qk = pl.program_id(1) < 2
    cols = []
    for h in range(y.shape[1] // LANES):
        yh = y[:, h * LANES:(h + 1) * LANES]
        nrm = lax.rsqrt(jnp.sum(yh * yh, axis=-1, keepdims=True) + 1e-6)
        cols.append(yh * jnp.where(is_qk, nrm, 1.0))
    o_ref[...] = jnp.concatenate(cols, axis=1)


def _gdn_conv(g3, conv_w, l, row0, n_seq, t, prev):
    blk0 = row0 // t
    args = [g3, conv_w]
    in_specs = [pl.BlockSpec((t, GDN_QK_DIM), lambda s, j: (blk0 + s, j)),
                pl.BlockSpec((1, GDN_CONV, GDN_QK_DIM), lambda s, j: (l, 0, j))]
    aliases = {}
    if prev is not None:
        args.append(prev)
        in_specs.append(_any_spec())
        aliases = {2: 0}
    return pl.pallas_call(
        _gdn_conv_kernel,
        grid=(n_seq, 3),
        in_specs=in_specs,
        out_specs=pl.BlockSpec((t, GDN_QK_DIM), lambda s, j: (blk0 + s, j)),
        out_shape=jax.ShapeDtypeStruct((N_TOK, GDN_CONV_CH), F32),
        input_output_aliases=aliases,
        compiler_params=_params("parallel", "parallel"),
        name="gdn_conv",
    )(*args)


def _gate_rows(misc):
    shape = (2 * GDN_NH, LANES)
    sel = (lax.broadcasted_iota(jnp.int32, shape, 1)
           == lax.broadcasted_iota(jnp.int32, shape, 0) + _M_G).astype(BF)
    b1, b2, b3 = _split3(misc)
    return _nt(sel, b1) + (_nt(sel, b2) + _nt(sel, b3))


def _gdn_chunk_pairs(pairs):
    c = GDN_CHUNK
    shape = (c, 2 * c)
    ri = lax.broadcasted_iota(jnp.int32, shape, 0)
    lane = lax.broadcasted_iota(jnp.int32, shape, 1)
    cj = lane & (c - 1)
    fwd_half = lane < c
    lower, upper = ri >= cj, ri <= cj
    incl = (fwd_half & lower) | (~fwd_half & upper)
    incl_t = (fwd_half & upper) | (~fwd_half & lower)
    strict = incl & (ri != cj)
    incl_f, incl_b = incl & fwd_half, incl & ~fwd_half
    eye = (ri == cj).astype(F32)
    n = len(pairs)
    rng = range(n)
    fw = [p[0] for p in pairs]
    bw = [p[1] for p in pairs]

    def halves(a, b):
        return jnp.where(fwd_half, a, b)

    def split_rows(m):
        return jnp.concatenate([jnp.where(fwd_half, m, 0.0), jnp.where(fwd_half, 0.0, m)], axis=0)

    def pair_dot3(x, p):
        xh = x.astype(BF)
        xl = (x - xh.astype(F32)).astype(BF)
        phf = p.astype(BF).astype(F32)
        bd_hi = split_rows(phf).astype(BF)
        bd_lo = split_rows(p - phf).astype(BF)
        return jnp.dot(jnp.concatenate([xh, xl, xh], axis=1), jnp.concatenate([bd_hi, bd_hi, bd_lo], axis=0),
                       preferred_element_type=F32)

    g_col = [halves(fw[i]["g_col"], bw[i]["g_col"]) for i in rng]
    gc_row = [jnp.sum(jnp.where(incl_t, g_col[i], 0.0), axis=0, keepdims=True) for i in rng]
    gcf = [jnp.sum(jnp.where(incl_f, pairs[i][2], 0.0), axis=1, keepdims=True) for i in rng]
    gcb = [jnp.sum(jnp.where(incl_b, pairs[i][2], 0.0), axis=1, keepdims=True) for i in rng]
    decay = [jnp.where(incl, jnp.exp(halves(gcf[i], gcb[i]) - gc_row[i]), 0.0) for i in rng]
    qf = [p["q"] * (GDN_DK ** -0.5) for p in fw]
    qb = [p["q"] * (GDN_DK ** -0.5) for p in bw]
    kbf = [p["k"] * p["beta"] for p in fw]
    kbb = [p["k"] * p["beta"] for p in bw]
    z = jnp.zeros((c, GDN_DK), F32)
    kq = [_bdot_nt(jnp.concatenate([jnp.concatenate([kbf[i], kbb[i]], axis=1),
                                    jnp.concatenate([qf[i], qb[i]], axis=1)], axis=0),
                   jnp.concatenate([jnp.concatenate([fw[i]["k"], z], axis=1),
                                    jnp.concatenate([z, bw[i]["k"]], axis=1)], axis=0)) for i in rng]
    pw = [jnp.where(strict, -(kq[i][:c] * decay[i]), 0.0) for i in rng]
    inv = [eye + m for m in pw]
    levels = int(np.log2(c)) - 1
    pw = [pair_dot3(m, m) for m in pw]
    for level in range(levels):
        if level + 1 < levels:
            both = [pair_dot3(jnp.concatenate([inv[i], pw[i]], axis=0), pw[i]) for i in rng]
            inv = [inv[i] + both[i][:c] for i in rng]
            pw = [both[i][c:] for i in rng]
        else:
            inv = [inv[i] + pair_dot3(inv[i], pw[i]) for i in rng]
    ef = [jnp.exp(g) for g in gcf]
    eb = [jnp.exp(g) for g in gcb]
    uw = [_bdot(split_rows(inv[i]),
                jnp.concatenate([jnp.concatenate([fw[i]["v"] * fw[i]["beta"], kbf[i] * ef[i]], axis=1),
                                 jnp.concatenate([bw[i]["v"] * bw[i]["beta"], kbb[i] * eb[i]], axis=1)], axis=0))
          for i in rng]
    a = [jnp.where(incl, kq[i][c:] * decay[i], 0.0) for i in rng]
    glf = [g[c - 1:c, :] for g in gcf]
    glb = [g[0:1, :] for g in gcb]
    wqf = [_bdot(jnp.concatenate([uw[i][:c, GDN_DV:], qf[i] * ef[i]], axis=0), fw[i]["state"]) for i in rng]
    wqb = [_bdot(jnp.concatenate([uw[i][c:, GDN_DV:], qb[i] * eb[i]], axis=0), bw[i]["state"]) for i in rng]
    vnf = [uw[i][:c, :GDN_DV] - wqf[i][:c] for i in rng]
    vnb = [uw[i][c:, :GDN_DV] - wqb[i][:c] for i in rng]
    av = [_bdot(split_rows(a[i]), jnp.concatenate([vnf[i], vnb[i]], axis=0)) for i in rng]
    sf = [fw[i]["state"] * jnp.exp(glf[i]) + _bdot_tn(fw[i]["k"] * jnp.exp(glf[i] - gcf[i]), vnf[i]) for i in rng]
    sb = [bw[i]["state"] * jnp.exp(glb[i]) + _bdot_tn(bw[i]["k"] * jnp.exp(glb[i] - gcb[i]), vnb[i]) for i in rng]
    return [(wqf[i][c:] + av[i][:c], sf[i], wqb[i][c:] + av[i][c:], sb[i]) for i in rng]


def _gdn_kernel(*refs, context, n_par):
    qf_ref, qb_ref, mf_ref, mb_ref = refs[:4]
    st_ref = refs[-1]
    if context:
        of_ref, ob_ref, sfin_ref = refs[-4:-1]
    else:
        s0_ref = refs[4]
        of_ref, ob_ref = refs[-3:-1]
    n = pl.program_id(1)

    @pl.when(n == 0)
    def _():
        if context:
            st_ref[...] = jnp.zeros_like(st_ref)
        else:
            st_ref[...] = s0_ref[:, 0]

    def problem(qkv, misc, s, h, d):
        i = d * GDN_HEADS + h
        return dict(q=qkv[:, h * GDN_DK:(h + 1) * GDN_DK],
                    k=qkv[:, GDN_QK_DIM + h * GDN_DK:GDN_QK_DIM + (h + 1) * GDN_DK],
                    v=qkv[:, 2 * GDN_QK_DIM + h * GDN_DV:2 * GDN_QK_DIM + (h + 1) * GDN_DV],
                    g_col=misc[:, _M_G + i:_M_G + i + 1], beta=misc[:, _M_B + i:_M_B + i + 1],
                    state=st_ref[s, i])

    lane = lax.broadcasted_iota(jnp.int32, (1, 2 * GDN_CHUNK), 1)
    pairs = []
    for s in range(n_par):
        qkv_f, qkv_b, misc_f, misc_b = qf_ref[s], qb_ref[s], mf_ref[s], mb_ref[s]
        rows = _gate_rows(jnp.concatenate([misc_f, misc_b], axis=0))
        for h in range(GDN_HEADS):
            g_row = jnp.where(lane < GDN_CHUNK, rows[h:h + 1, :], rows[GDN_HEADS + h:GDN_HEADS + h + 1, :])
            pairs.append((problem(qkv_f, misc_f, s, h, 0), problem(qkv_b, misc_b, s, h, 1), g_row))
    results = _gdn_chunk_pairs(pairs)
    for s in range(n_par):
        for h in range(GDN_HEADS):
            o_f, s_f, o_b, s_b = results[s * GDN_HEADS + h]
            of_ref[s, :, h * GDN_DV:(h + 1) * GDN_DV] = o_f
            ob_ref[s, :, h * GDN_DV:(h + 1) * GDN_DV] = o_b
            st_ref[s, h] = s_f
            st_ref[s, GDN_HEADS + h] = s_b

    if context:
        @pl.when(n == pl.num_programs(1) - 1)
        def _():
            sfin_ref[:, 0] = st_ref[...]


def _gdn(qkv, misc, l, t, n_seq, seq0, s0, prevs, n_par=GDN_PAR):
    c = GDN_CHUNK
    nc = t // c
    context = s0 is None
    sb0 = seq0 // n_par
    n_all = N_TOK // t
    fwd = lambda s, n: (sb0 + s, n, 0)
    bwd = lambda s, n: (sb0 + s, nc - 1 - n, 0)
    st_spec = pl.BlockSpec((n_par, 1, GDN_NH, GDN_DK, GDN_DV), lambda s, n: (s, l, 0, 0, 0))
    qkv3 = qkv.reshape(n_all, t, GDN_CONV_CH)
    misc3 = misc.reshape(n_all, t, LANES)
    in_specs = [pl.BlockSpec((n_par, c, GDN_CONV_CH), fwd), pl.BlockSpec((n_par, c, GDN_CONV_CH), bwd),
                pl.BlockSpec((n_par, c, LANES), fwd), pl.BlockSpec((n_par, c, LANES), bwd)]
    args = [qkv3, qkv3, misc3, misc3]
    if not context:
        in_specs.append(st_spec)
        args.append(s0)
    out_specs = [pl.BlockSpec((n_par, c, GDN_V_DIM), fwd), pl.BlockSpec((n_par, c, GDN_V_DIM), bwd)]
    out_shape = [jax.ShapeDtypeStruct((n_all, t, GDN_V_DIM), F32)] * 2
    if context:
        out_specs.append(st_spec)
        out_shape.append(jax.ShapeDtypeStruct((n_seq, DEPTH, GDN_NH, GDN_DK, GDN_DV), F32))
    aliases = {}
    for j, p in enumerate(prevs):
        if p is not None:
            aliases[len(args)] = j
            args.append(p.reshape(out_shape[j].shape))
            in_specs.append(_any_spec())
    return pl.pallas_call(
        functools.partial(_gdn_kernel, context=context, n_par=n_par),
        grid=(n_seq // n_par, nc),
        in_specs=in_specs, out_specs=out_specs, out_shape=out_shape,
        input_output_aliases=aliases,
        scratch_shapes=[pltpu.VMEM((n_par, GDN_NH, GDN_DK, GDN_DV), F32)],
        compiler_params=_params("parallel", "arbitrary"),
        name="gdn",
    )(*args)


def _out_kernel(x_ref, mod_ref, om_ref, os_ref, gf_ref, gb_ref, gz_ref, gn_ref, w_ref, o_ref):
    gate = mod_ref[0, 0][:, 2 * D_MODEL:]
    s = gf_ref[...] + gb_ref[...]
    gz = gz_ref[...]
    cols = []
    for h in range(GDN_HEADS):
        sh = s[:, h * GDN_DV:(h + 1) * GDN_DV]
        cols.append(_rms(sh, gn_ref[0]) * _silu(gz[:, h * GDN_DV:(h + 1) * GDN_DV]))
    og = jnp.concatenate(cols, axis=1)
    n_m = MLA_HEADS * MLA_V
    n_s = SWA_HEADS * SWA_HD
    y = (jnp.dot(om_ref[...], w_ref[0, :n_m, :], preferred_element_type=F32)
         + jnp.dot(os_ref[...], w_ref[0, n_m:n_m + n_s, :], preferred_element_type=F32)
         + jnp.dot(og.astype(BF), w_ref[0, n_m + n_s:, :], preferred_element_type=F32))
    o_ref[...] = x_ref[...] + gate * y


def _out_proj(x, mods, l, o_mla, o_swa, o_gf, o_gb, gz, gdn_norm, w_out, tm=TM_OUT):
    row = lambda i: (i, 0)
    lay3 = lambda i: (l, 0, 0)
    return pl.pallas_call(
        _out_kernel,
        grid=(N_TOK // tm,),
        in_specs=[pl.BlockSpec((tm, D_MODEL), row),
                  _mod_spec(l, 1, tm),
                  pl.BlockSpec((tm, MLA_HEADS * MLA_V), row),
                  pl.BlockSpec((tm, SWA_HEADS * SWA_HD), row),
                  pl.BlockSpec((tm, GDN_V_DIM), row),
                  pl.BlockSpec((tm, GDN_V_DIM), row),
                  pl.BlockSpec((tm, GDN_V_DIM), row),
                  pl.BlockSpec((1, 1, GDN_DV), lay3),
                  pl.BlockSpec((1,) + w_out.shape[1:], lay3)],
        out_specs=pl.BlockSpec((tm, D_MODEL), row),
        out_shape=jax.ShapeDtypeStruct((N_TOK, D_MODEL), F32),
        compiler_params=_params("parallel"),
        name="out_proj",
    )(x, mods, o_mla, o_swa, o_gf, o_gb, gz, gdn_norm.reshape(DEPTH, 1, GDN_DV), w_out)


def _axial_rope(n_tokens, dim):
    f32 = np.float32
    rows = n_tokens // GRID_W
    row = np.repeat(np.arange(rows, dtype=f32), GRID_W)
    col = np.tile(np.arange(GRID_W, dtype=f32), rows)
    axis_dim = dim // 2
    inv_freq = (f32(1.0) / (f32(ROPE_BASE) ** (np.arange(0, axis_dim, 2, dtype=f32) / f32(axis_dim)))).astype(f32)
    ang_r = row[:, None] * inv_freq[None, :]
    ang_c = col[:, None] * inv_freq[None, :]
    ang = np.concatenate([ang_r, ang_r, ang_c, ang_c], axis=-1).astype(f32)
    return np.cos(ang).astype(f32), np.sin(ang).astype(f32)


def _rope_table(tm):
    cos_m, sin_m = _axial_rope(DEC_SEQ, MLA_ROPE)
    cos_s, sin_s = _axial_rope(DEC_SEQ, SWA_HD)
    t = DEC_SEQ
    one = lambda w: np.ones((t, w), np.float32)
    zero = lambda w: np.zeros((t, w), np.float32)
    lat = np.concatenate([
        cos_s, cos_s, sin_s, sin_s,
        cos_m, one(LANES - MLA_ROPE), sin_m, zero(LANES - MLA_ROPE),
        one(MLA_NOPE), cos_m, one(LANES - MLA_NOPE - MLA_ROPE),
        zero(MLA_NOPE), sin_m, zero(LANES - MLA_NOPE - MLA_ROPE)], axis=1)
    ident_row = np.concatenate([np.ones(LANES), np.zeros(LANES)] * 3).astype(np.float32)
    ident = np.broadcast_to(ident_row[None, :], (tm, _T_END))
    return jnp.asarray(np.concatenate([ident, lat], axis=0))


def _mixer_weights(w_in, mla_w_qb, mla_w_kvb):
    nl = DEPTH
    offs = np.cumsum([0, MLA_Q_LORA, MLA_KV_LORA, MLA_ROPE, SWA_HEADS * SWA_HD, SWA_KV_HEADS * SWA_HD,
                      SWA_KV_HEADS * SWA_HD, GDN_CONV_CH, GDN_V_DIM, 2 * GDN_NH])
    w_t = jnp.swapaxes(w_in, 1, 2)
    cq_ckv, krope, body, gates = (w_t[:, :offs[2]], w_t[:, offs[2]:offs[3]], w_t[:, offs[3]:offs[8]],
                                  w_t[:, offs[8]:])
    misc = jnp.concatenate([krope, gates, jnp.zeros((nl, LANES - _M_END, D_MODEL), F32)], axis=1)
    win = jnp.concatenate([cq_ckv, body, misc], axis=1).astype(BF)

    r = MLA_Q_LORA
    wq = mla_w_qb.reshape(nl, r, MLA_HEADS, MLA_NOPE + MLA_ROPE)
    pad = jnp.zeros((nl, r, MLA_HEADS, LANES - MLA_NOPE - MLA_ROPE), F32)
    wqb = jnp.concatenate([wq, pad], axis=-1).reshape(nl, r, MLA_HEADS * LANES).astype(BF)

    kvb = mla_w_kvb.reshape(nl, MLA_KV_LORA, MLA_HEADS, MLA_NOPE + MLA_V)
    k_nope = jnp.concatenate([kvb[..., :MLA_NOPE],
                              jnp.zeros((nl, MLA_KV_LORA, MLA_HEADS, LANES - MLA_NOPE), F32)],
                             axis=-1).reshape(nl, MLA_KV_LORA, MLA_HEADS * LANES)
    place = np.zeros((LANES, MLA_HEADS, LANES), np.float32)
    for i in range(MLA_ROPE):
        place[i, :, MLA_NOPE + i] = 1.0
    place = jnp.broadcast_to(jnp.asarray(place.reshape(1, LANES, MLA_HEADS * LANES)),
                             (nl, LANES, MLA_HEADS * LANES))
    wk = jnp.concatenate([k_nope, place], axis=1).astype(BF)
    wv = kvb[..., MLA_NOPE:].reshape(nl, MLA_KV_LORA, MLA_HEADS * MLA_V).astype(BF)
    return win, wqb, wk, wv


def _misc_rows(vals):
    rows = jnp.zeros((DEPTH, 1, LANES), F32)
    return rows.at[:, 0, _M_G:_M_B].set(vals.reshape(DEPTH, GDN_NH).astype(F32))


def kernel(x_prompt, x_sample, cache_mla_ckv, cache_mla_krope, cache_swa_k, cache_swa_v, state_gdn, c, c_ctx,
           w_ada, b_ada, norm_ffn1, ffn1_w1, ffn1_w2, norm_mix, w_in, mla_q_norm, mla_w_qb, mla_kv_norm,
           mla_w_kvb, swa_sink, gdn_conv_w, gdn_a_log, gdn_dt_bias, gdn_norm, w_out, norm_ffn2, ffn2_w1,
           ffn2_w2, final_norm):
    cond = jnp.concatenate([c_ctx[None, :], c, jnp.zeros((COND_ROWS - N_GROUPS, D_MODEL), F32)], axis=0)
    mods = _adaln(cond, w_ada, b_ada)[:, :N_GROUPS].reshape(DEPTH, N_GROUPS, 1, N_MOD * D_MODEL)
    tab = _rope_table(TM_PROJ)
    win, wqb, wk, wv = _mixer_weights(w_in, mla_w_qb, mla_w_kvb)
    w11, w12, w21, w22 = ffn1_w1, ffn1_w2, ffn2_w1, ffn2_w2
    wo = w_out.astype(BF)
    alog, dtb = _misc_rows(gdn_a_log), _misc_rows(gdn_dt_bias)
    sink_rows = jnp.broadcast_to(swa_sink[:, :, None], (DEPTH, SWA_HEADS, LANES))
    cache_misc = jnp.pad(cache_mla_krope, ((0, 0), (0, 0), (0, 0), (0, LANES - MLA_ROPE)))
    cache_k = cache_swa_k.reshape(DEC_BATCH, DEPTH, PAST_LEN, LANES)
    cache_v = cache_swa_v.reshape(DEC_BATCH, DEPTH, PAST_LEN, LANES)
    s0 = state_gdn.reshape(DEC_BATCH, DEPTH, GDN_NH, GDN_DK, GDN_DV)

    xs = [x_prompt.reshape(N_CTX, D_MODEL), x_sample.reshape(N_LAT, D_MODEL)]
    caches = new_st = None
    for l in range(DEPTH):
        x = _ffn(xs, mods, l, 0, norm_ffn1, w11, w12)
        (q_mla, misc, k_mla, v_mla, sq, sk, sv, g3, gz, *caches) = _proj(
            x, mods, l, norm_mix, win, mla_q_norm, wqb, mla_kv_norm, wk, wv, tab, alog, dtb, caches)

        k_c, v_c = _kv_cache(cache_mla_ckv, cache_misc, l, wk, wv)
        o_mla = _mla_attend(q_mla, [k_mla], [v_mla], BATCH, SEQ, 0, [(0, SEQ)], None)
        o_mla = _mla_attend(q_mla, [k_mla, k_c], [v_mla, v_c], DEC_BATCH, DEC_SEQ, N_CTX,
                            [(N_CTX // DEC_SEQ, DEC_SEQ), (0, PAST_LEN)], o_mla)

        o_swa = _swa_ctx(sq, sk, sv, sink_rows, l)
        o_swa = _swa_lat(sq, sk, sv, cache_k, cache_v, sink_rows, l, o_swa)

        qkv = _gdn_conv(g3, gdn_conv_w, l, 0, BATCH, SEQ, None)
        qkv = _gdn_conv(g3, gdn_conv_w, l, N_CTX, DEC_BATCH, DEC_SEQ, qkv)
        o_gf, o_gb, new_st = _gdn(qkv, misc, l, SEQ, BATCH, 0, None, [None, None, new_st], n_par=GDN_PAR_CTX)
        o_gf, o_gb = _gdn(qkv, misc, l, DEC_SEQ, DEC_BATCH, N_CTX // DEC_SEQ, s0, [o_gf, o_gb])

        x = _out_proj(x, mods, l, o_mla, o_swa, o_gf.reshape(N_TOK, GDN_V_DIM), o_gb.reshape(N_TOK, GDN_V_DIM),
                      gz, gdn_norm, wo)
        if l + 1 < DEPTH:
            xs = [_ffn([x], mods, l, 2, norm_ffn2, w21, w22)]
        else:
            y_prompt, y_sample = _ffn([x], mods, l, 2, norm_ffn2, w21, w22, final_gain=final_norm)

    new_ckv, new_krope, new_sk, new_sv = caches
    kv_shape = (BATCH, DEPTH, SEQ, SWA_KV_HEADS, SWA_HD)
    return (y_prompt.reshape(BATCH, SEQ, D_MODEL), y_sample.reshape(DEC_BATCH, DEC_SEQ, D_MODEL), new_ckv,
            new_krope, new_sk.reshape(kv_shape), new_sv.reshape(kv_shape),
            new_st.reshape(BATCH, DEPTH, 2, GDN_HEADS, GDN_DK, GDN_DV))
```

```python
import functools

import numpy as np
import jax
import jax.numpy as jnp
from jax import lax
from jax.experimental import pallas as pl
from jax.experimental.pallas import tpu as pltpu

D_MODEL = 1024
BATCH = 16
SEQ = 256
DEPTH = 2
DEC_BATCH = 2
DEC_SEQ = 2048
PAST_LEN = 512
GRID_W = 64
ROPE_BASE = 10000.0
NORM_EPS = 1e-6
N_MOD = 9
D_FF = 2816
MLA_HEADS = 8
MLA_Q_LORA = 384
MLA_KV_LORA = 256
MLA_NOPE = 64
MLA_ROPE = 32
MLA_V = 64
SWA_HEADS = 8
SWA_KV_HEADS = 2
SWA_GROUP = SWA_HEADS // SWA_KV_HEADS
SWA_HD = 64
SWA_WINDOW = 128
GDN_HEADS = 4
GDN_DK = 128
GDN_DV = 128
GDN_CONV = 5
GDN_CHUNK = 64
GDN_QK_DIM = GDN_HEADS * GDN_DK
GDN_V_DIM = GDN_HEADS * GDN_DV
GDN_CONV_CH = 2 * GDN_QK_DIM + GDN_V_DIM
GDN_NH = 2 * GDN_HEADS

N_CTX = BATCH * SEQ
N_LAT = DEC_BATCH * DEC_SEQ
N_TOK = N_CTX + N_LAT
N_GROUPS = 1 + DEC_BATCH
COND_ROWS = 8

LANES = 128
SUBLANES = 8
VMEM_LIMIT_BYTES = 56 * 1024 * 1024

TM_FFN = 512
FFN_LOAD_COLS = 256
TM_PROJ = 2 * SEQ
TM_OUT = 1024
TQ_MLA = 512
GDN_PAR = 2
GDN_PAR_CTX = 8
GDN_SUB_LAT = 4

BF = jnp.bfloat16
F32 = jnp.float32
LOG2_E = 1.4426950408889634
MLA_Q_SCALE = (MLA_NOPE + MLA_ROPE) ** -0.5 * LOG2_E

_C_CQ = 0
_C_CKV = _C_CQ + MLA_Q_LORA
_C_SQ = _C_CKV + MLA_KV_LORA
_C_SK = _C_SQ + SWA_HEADS * SWA_HD
_C_SV = _C_SK + SWA_KV_HEADS * SWA_HD
_C_G3 = _C_SV + SWA_KV_HEADS * SWA_HD
_C_GZ = _C_G3 + GDN_CONV_CH
_C_MA = _C_GZ + GDN_V_DIM
_C_END = _C_MA + LANES
_M_G = MLA_ROPE
_M_B = MLA_ROPE + GDN_NH
_M_END = MLA_ROPE + 2 * GDN_NH
_CACHE_WIDTHS = (MLA_KV_LORA, MLA_ROPE, SWA_KV_HEADS * SWA_HD, SWA_KV_HEADS * SWA_HD)
_T_CS, _T_SS, _T_CA, _T_SA, _T_CQ, _T_SQ = (i * LANES for i in range(6))
_T_END = 6 * LANES


def _params(*sem):
    return pltpu.CompilerParams(dimension_semantics=sem, vmem_limit_bytes=VMEM_LIMIT_BYTES)


def _bdot(a, b):
    return jnp.dot(a.astype(BF), b.astype(BF), preferred_element_type=F32)


def _nt(a, b):
    return lax.dot_general(a, b, (((1,), (1,)), ((), ())), preferred_element_type=F32)


def _bdot_nt(a, b):
    return _nt(a.astype(BF), b.astype(BF))


def _bdot_tn(a, b):
    return lax.dot_general(a.astype(BF), b.astype(BF), (((0,), (0,)), ((), ())),
                           preferred_element_type=F32)


def _split3(a):
    b1 = a.astype(BF)
    r = a - b1.astype(F32)
    b2 = r.astype(BF)
    b3 = (r - b2.astype(F32)).astype(BF)
    return b1, b2, b3


def _silu(x):
    return x / (1.0 + jnp.exp(-x))


def _rms(x, gain, eps=NORM_EPS):
    return x * lax.rsqrt(jnp.mean(x * x, axis=-1, keepdims=True) + eps) * gain


def _group_of_row(r):
    return jnp.where(r < N_CTX, 0, 1 + (r - N_CTX) // DEC_SEQ)


def _any_spec():
    return pl.BlockSpec(memory_space=pl.ANY)


def _adaln_kernel(c_ref, w_ref, b_ref, o_ref):
    o_ref[0] = _bdot(_silu(c_ref[...]), w_ref[0]) + b_ref[0]


def _adaln(cond, w_ada, b_ada, tn=1536):
    n = N_MOD * D_MODEL
    return pl.pallas_call(
        _adaln_kernel,
        grid=(DEPTH, n // tn),
        in_specs=[pl.BlockSpec((COND_ROWS, D_MODEL), lambda l, j: (0, 0)),
                  pl.BlockSpec((1, D_MODEL, tn), lambda l, j: (l, 0, j)),
                  pl.BlockSpec((1, 1, tn), lambda l, j: (l, 0, j))],
        out_specs=pl.BlockSpec((1, COND_ROWS, tn), lambda l, j: (l, 0, j)),
        out_shape=jax.ShapeDtypeStruct((DEPTH, COND_ROWS, n), F32),
        compiler_params=_params("parallel", "parallel"),
        name="adaln",
    )(cond, w_ada, b_ada.reshape(DEPTH, 1, n))


def _mod_spec(l, which, tm):
    return pl.BlockSpec((1, 1, 1, 3 * D_MODEL), lambda i: (l, _group_of_row(i * tm), 0, which))


def _ffn_kernel(*refs, n_in, n_a_tiles, final, layer):
    x_refs = refs[:n_in]
    mod_ref, gain_ref, w1_hbm, w2_hbm = refs[n_in:n_in + 4]
    w1_ref, w2_ref, stage_g, stage_u, stage_d, sem_ref = refs[-6:]
    rest = refs[n_in + 4:-6]
    i = pl.program_id(0)

    if n_in == 2:
        x = jnp.where(i < n_a_tiles, x_refs[0][...], x_refs[1][...])
    else:
        x = x_refs[0][...]
    mod = mod_ref[0, 0]
    shift, scale, gate = mod[:, :D_MODEL], mod[:, D_MODEL:2 * D_MODEL], mod[:, 2 * D_MODEL:]
    hb = (_rms(x, gain_ref[0]) * (1.0 + scale) + shift).astype(BF)

    def finish(ffn):
        y = x + gate * (0.5 * ffn)
        if final:
            fg_ref, oa_ref, ob_ref = rest
            yn = _rms(y, fg_ref[...])

            @pl.when(i < n_a_tiles)
            def _():
                oa_ref[...] = yn

            @pl.when(i >= n_a_tiles)
            def _():
                ob_ref[...] = yn
        else:
            rest[0][...] = y

    fc = stage_d.shape[1]
    n_chunks = D_FF // fc

    def chunk_copies(c):
        slot = c % 2
        cols = pl.ds(c * fc, fc)
        return (pltpu.make_async_copy(w1_hbm.at[layer, :, cols], stage_g.at[slot], sem_ref.at[0, slot]),
                pltpu.make_async_copy(w1_hbm.at[layer, :, pl.ds(D_FF + c * fc, fc)], stage_u.at[slot],
                                      sem_ref.at[1, slot]),
                pltpu.make_async_copy(w2_hbm.at[layer, cols, :], stage_d.at[slot], sem_ref.at[2, slot]))

    @pl.when(i == 0)
    def _():
        for cp in chunk_copies(0):
            cp.start()
        acc = None
        for c in range(n_chunks):
            if c + 1 < n_chunks:
                for cp in chunk_copies(c + 1):
                    cp.start()
            for cp in chunk_copies(c):
                cp.wait()
            slot = c % 2
            wg, wu, wd = stage_g[slot].astype(BF), stage_u[slot].astype(BF), stage_d[slot].astype(BF)
            w1_ref[:, c * fc:(c + 1) * fc] = wg
            w1_ref[:, D_FF + c * fc:D_FF + (c + 1) * fc] = wu
            w2_ref[c * fc:(c + 1) * fc, :] = wd
            g = jnp.dot(hb, wg, preferred_element_type=F32)
            u = jnp.dot(hb, wu, preferred_element_type=F32)
            part = jnp.dot((_silu(g) * u).astype(BF), wd, preferred_element_type=F32)
            acc = part if acc is None else acc + part
        finish(acc)

    @pl.when(i > 0)
    def _():
        gu = jnp.dot(hb, w1_ref[...], preferred_element_type=F32)
        a = _silu(gu[:, :D_FF]) * gu[:, D_FF:]
        finish(jnp.dot(a.astype(BF), w2_ref[...], preferred_element_type=F32))


def _ffn(xs, mods, l, which, gain, w1, w2, final_gain=None, tm=TM_FFN):
    n_a = N_CTX // tm
    row = lambda i: (i, 0)
    first = lambda i: (jnp.minimum(i, n_a - 1), 0)
    second = lambda i: (jnp.maximum(i - n_a, 0), 0)
    lay3 = lambda i: (l, 0, 0)
    x_specs = ([pl.BlockSpec((tm, D_MODEL), row)] if len(xs) == 1
               else [pl.BlockSpec((tm, D_MODEL), first), pl.BlockSpec((tm, D_MODEL), second)])
    in_specs = x_specs + [_mod_spec(l, which, tm),
                          pl.BlockSpec((1, 1, D_MODEL), lay3),
                          _any_spec(), _any_spec()]
    args = list(xs) + [mods, gain.reshape(DEPTH, 1, D_MODEL), w1, w2]
    if final_gain is None:
        out_specs = pl.BlockSpec((tm, D_MODEL), row)
        out_shape = jax.ShapeDtypeStruct((N_TOK, D_MODEL), F32)
    else:
        in_specs.append(pl.BlockSpec((1, D_MODEL), lambda i: (0, 0)))
        args.append(final_gain.reshape(1, D_MODEL))
        out_specs = [pl.BlockSpec((tm, D_MODEL), first), pl.BlockSpec((tm, D_MODEL), second)]
        out_shape = [jax.ShapeDtypeStruct((N_CTX, D_MODEL), F32), jax.ShapeDtypeStruct((N_LAT, D_MODEL), F32)]
    return pl.pallas_call(
        functools.partial(_ffn_kernel, n_in=len(xs), n_a_tiles=n_a, final=final_gain is not None, layer=l),
        grid=(N_TOK // tm,),
        in_specs=in_specs, out_specs=out_specs, out_shape=out_shape,
        scratch_shapes=[pltpu.VMEM((D_MODEL, 2 * D_FF), BF), pltpu.VMEM((D_FF, D_MODEL), BF),
                        pltpu.VMEM((2, D_MODEL, FFN_LOAD_COLS), F32), pltpu.VMEM((2, D_MODEL, FFN_LOAD_COLS), F32),
                        pltpu.VMEM((2, FFN_LOAD_COLS, D_MODEL), F32), pltpu.SemaphoreType.DMA((3, 2))],
        compiler_params=_params("arbitrary"),
        name="ffn",
    )(*args)


def _kv_expand(ckv_n, misc, wk, wv):
    kin = jnp.concatenate([ckv_n, misc], axis=1).astype(BF)
    kk = jnp.dot(kin, wk, preferred_element_type=F32)
    vv = jnp.dot(ckv_n.astype(BF), wv, preferred_element_type=F32)
    return kk.astype(BF), vv.astype(BF)


def _proj_kernel(*refs, n_ctx_tiles):
    (x_ref, mod_ref, gain_ref, win_ref, qg_ref, wqb_ref, kvg_ref, wk_ref, wv_ref, tab_ref,
     alog_ref, dtb_ref) = refs[:12]
    (q_ref, misc_ref, kmla_ref, vmla_ref, sq_ref, sk_ref, sv_ref, g3_ref, gz_ref,
     c_ckv_ref, c_krope_ref, c_sk_ref, c_sv_ref) = refs[-13:]
    i = pl.program_id(0)
    mod = mod_ref[0, 0]
    shift, scale = mod[:, :D_MODEL], mod[:, D_MODEL:2 * D_MODEL]
    hb = (_rms(x_ref[...], gain_ref[0]) * (1.0 + scale) + shift).astype(BF)
    tab = tab_ref[...]

    def project(c0, c1):
        return _nt(hb, win_ref[0, c0:c1, :])

    lane = lax.broadcasted_iota(jnp.int32, (x_ref.shape[0], LANES), 1)

    def rotate_half(t, quarter):
        even = (lane // quarter) % 2 == 0
        tiles = [t[:, c:c + LANES] for c in range(0, t.shape[1], LANES)]
        out = [jnp.where(even, -pltpu.roll(v, LANES - quarter, 1), pltpu.roll(v, quarter, 1)) for v in tiles]
        return out[0] if len(out) == 1 else jnp.concatenate(out, axis=1)

    u_lora = project(_C_CQ, _C_SQ)
    u_misc = project(_C_MA, _C_END)
    u_sq = project(_C_SQ, _C_SK)
    u_skv = project(_C_SK, _C_G3)

    qn = _rms(u_lora[:, :MLA_Q_LORA], qg_ref[0])
    ckv_n = _rms(u_lora[:, MLA_Q_LORA:], kvg_ref[0])

    m = (u_misc * tab[:, _T_CA:_T_CA + LANES]
         + rotate_half(u_misc, MLA_ROPE // 4) * tab[:, _T_SA:_T_SA + LANES])
    z = m + dtb_ref[0]
    softplus = jnp.maximum(z, 0.0) + jnp.log(1.0 + jnp.exp(-jnp.abs(z)))
    decay = -jnp.exp(alog_ref[0]) * softplus
    strength = 1.0 / (1.0 + jnp.exp(-m))
    misc = jnp.where((lane >= _M_G) & (lane < _M_B), decay,
                     jnp.where((lane >= _M_B) & (lane < _M_END), strength, m))
    misc_ref[...] = misc

    n_sq = SWA_HEADS * SWA_HD
    cos_s = tab[:, _T_CS:_T_CS + LANES]
    sin_s = tab[:, _T_SS:_T_SS + LANES]
    cos4 = jnp.concatenate([cos_s] * (n_sq // LANES), axis=1)
    sin4 = jnp.concatenate([sin_s] * (n_sq // LANES), axis=1)
    sq_ref[...] = ((u_sq * cos4 + rotate_half(u_sq, SWA_HD // 4) * sin4) * (SWA_HD ** -0.5 * LOG2_E)).astype(BF)
    sk = u_skv[:, :LANES] * cos_s + rotate_half(u_skv[:, :LANES], SWA_HD // 4) * sin_s
    sv = u_skv[:, LANES:]
    sk_ref[...] = sk
    sv_ref[...] = sv

    @pl.when(i < n_ctx_tiles)
    def _():
        for j in range(c_ckv_ref.shape[0]):
            rows = slice(j * SEQ, (j + 1) * SEQ)
            c_ckv_ref[j, 0] = ckv_n[rows]
            c_krope_ref[j, 0] = misc[rows, :MLA_ROPE]
            c_sk_ref[j, 0] = sk[rows]
            c_sv_ref[j, 0] = sv[rows]

    g3_ref[...] = project(_C_G3, _C_GZ)
    gz_ref[...] = project(_C_GZ, _C_MA)

    q2 = jnp.dot(qn.astype(BF), wqb_ref[0], preferred_element_type=F32)
    cosq = jnp.concatenate([tab[:, _T_CQ:_T_CQ + LANES]] * MLA_HEADS, axis=1)
    sinq = jnp.concatenate([tab[:, _T_SQ:_T_SQ + LANES]] * MLA_HEADS, axis=1)
    q_ref[...] = ((q2 * cosq + rotate_half(q2, MLA_ROPE // 4) * sinq) * MLA_Q_SCALE).astype(BF)
    kk, vv = _kv_expand(ckv_n, misc, wk_ref[0], wv_ref[0])
    kmla_ref[...] = kk
    vmla_ref[...] = vv


def _proj(x, mods, l, gain, win, qg, wqb, kvg, wk, wv, tab, alog, dtb, cache_prev, tm=TM_PROJ):
    assert tm % SEQ == 0
    n_ctx_tiles = N_CTX // tm
    lat_tiles = DEC_SEQ // tm
    lay3 = lambda i: (l, 0, 0)
    row = lambda i: (i, 0)

    def tab_map(i):
        return (jnp.where(i < n_ctx_tiles, 0, 1 + (i - n_ctx_tiles) % lat_tiles), 0)

    def lay_spec(a):
        return pl.BlockSpec((1,) + a.shape[1:], lay3)

    widths = [(MLA_HEADS * LANES, BF), (LANES, F32), (MLA_HEADS * LANES, BF),
              (MLA_HEADS * MLA_V, BF), (SWA_HEADS * SWA_HD, BF), (SWA_KV_HEADS * SWA_HD, F32),
              (SWA_KV_HEADS * SWA_HD, F32), (GDN_CONV_CH, F32), (GDN_V_DIM, F32)]
    out_specs = [pl.BlockSpec((tm, w), row) for w, _ in widths]
    out_shape = [jax.ShapeDtypeStruct((N_TOK, w), dt) for w, dt in widths]
    for w in _CACHE_WIDTHS:
        out_specs.append(pl.BlockSpec((tm // SEQ, 1, SEQ, w),
                                      lambda i: (jnp.minimum(i, n_ctx_tiles - 1), l, 0, 0)))
        out_shape.append(jax.ShapeDtypeStruct((BATCH, DEPTH, SEQ, w), F32))
    qg, kvg = qg.reshape(DEPTH, 1, -1), kvg.reshape(DEPTH, 1, -1)
    args = [x, mods, gain.reshape(DEPTH, 1, D_MODEL), win, qg, wqb, kvg, wk, wv, tab, alog, dtb]
    in_specs = [pl.BlockSpec((tm, D_MODEL), row), _mod_spec(l, 1, tm),
                pl.BlockSpec((1, 1, D_MODEL), lay3), lay_spec(win), lay_spec(qg), lay_spec(wqb),
                lay_spec(kvg), lay_spec(wk), lay_spec(wv), pl.BlockSpec((tm, _T_END), tab_map),
                lay_spec(alog), lay_spec(dtb)]
    aliases = {}
    if cache_prev is not None:
        for j, prev in enumerate(cache_prev):
            aliases[len(args)] = len(widths) + j
            args.append(prev)
            in_specs.append(_any_spec())
    return pl.pallas_call(
        functools.partial(_proj_kernel, n_ctx_tiles=n_ctx_tiles),
        grid=(N_TOK // tm,),
        in_specs=in_specs, out_specs=out_specs, out_shape=out_shape,
        input_output_aliases=aliases,
        compiler_params=_params("arbitrary"),
        name="proj",
    )(*args)


def _kv_cache_kernel(ckv_ref, misc_ref, wk_ref, wv_ref, k_ref, v_ref):
    kk, vv = _kv_expand(ckv_ref[0, 0], misc_ref[0, 0], wk_ref[0], wv_ref[0])
    k_ref[...] = kk
    v_ref[...] = vv


def _kv_cache(cache_ckv, cache_misc, l, wk, wv):
    lay3 = lambda b: (l, 0, 0)
    n = DEC_BATCH * PAST_LEN
    return pl.pallas_call(
        _kv_cache_kernel,
        grid=(DEC_BATCH,),
        in_specs=[pl.BlockSpec((1, 1, PAST_LEN, MLA_KV_LORA), lambda b: (b, l, 0, 0)),
                  pl.BlockSpec((1, 1, PAST_LEN, LANES), lambda b: (b, l, 0, 0)),
                  pl.BlockSpec((1,) + wk.shape[1:], lay3),
                  pl.BlockSpec((1,) + wv.shape[1:], lay3)],
        out_specs=[pl.BlockSpec((PAST_LEN, MLA_HEADS * LANES), lambda b: (b, 0)),
                   pl.BlockSpec((PAST_LEN, MLA_HEADS * MLA_V), lambda b: (b, 0))],
        out_shape=[jax.ShapeDtypeStruct((n, MLA_HEADS * LANES), BF),
                   jax.ShapeDtypeStruct((n, MLA_HEADS * MLA_V), BF)],
        compiler_params=_params("parallel"),
        name="kv_cache",
    )(cache_ckv, cache_misc, wk, wv)


def _softmax_numerator(s_ref, p_ref, row_block=128):
    rows, cols = s_ref.shape
    mx = jnp.max(s_ref[...], axis=-1, keepdims=True)
    sums = []
    for r0 in range(0, rows, row_block):
        mb = jnp.broadcast_to(mx[r0:r0 + row_block], (row_block, LANES))
        part = jnp.zeros((row_block, LANES), F32)
        for c in range(0, cols, LANES):
            p = jnp.exp2(s_ref[r0:r0 + row_block, c:c + LANES] - mb)
            part = part + p
            p_ref[r0:r0 + row_block, c:c + LANES] = p.astype(BF)
        sums.append(jnp.sum(part, axis=-1, keepdims=True))
    return jnp.concatenate(sums, axis=0)


def _mla_kernel(*refs, n_seg):
    q_ref = refs[0]
    k_refs = refs[1:1 + n_seg]
    v_refs = refs[1 + n_seg:1 + 2 * n_seg]
    o_ref = refs[-5]
    s_refs, p_refs = refs[-4:-2], refs[-2:]
    tq = q_ref.shape[0]
    lane = lax.broadcasted_iota(jnp.int32, (tq, LANES), 1)

    def scores(h):
        off = 0
        for k in k_refs:
            s_refs[h % 2][:, off:off + k.shape[0]] = _nt(q_ref[:, h * LANES:(h + 1) * LANES],
                                                         k[:, h * LANES:(h + 1) * LANES])
            off += k.shape[0]

    scores(0)
    outs = []
    for h in range(MLA_HEADS):
        if h + 1 < MLA_HEADS:
            scores(h + 1)
        den = _softmax_numerator(s_refs[h % 2], p_refs[h % 2])
        pair = h // 2
        acc, off = None, 0
        for v in v_refs:
            pv = jnp.dot(p_refs[h % 2][:, off:off + v.shape[0]], v[:, pair * LANES:(pair + 1) * LANES],
                         preferred_element_type=F32)
            acc = pv if acc is None else acc + pv
            off += v.shape[0]
        outs.append(acc / den)
        if h % 2 == 1:
            o_ref[:, pair * LANES:(pair + 1) * LANES] = jnp.where(lane < MLA_V, outs[-2], outs[-1]).astype(BF)


def _mla_attend(q, ks, vs, n_batch, t, row0, k_blocks, prev):
    n_seg = len(ks)
    tq = min(TQ_MLA, t)
    n_keys = sum(s for _, s in k_blocks)
    qb0 = row0 // tq
    tiles = t // tq
    q_map = lambda b, i: (qb0 + b * tiles + i, 0)
    in_specs = [pl.BlockSpec((tq, MLA_HEADS * LANES), q_map)]
    for (b0, s), width in ([(kb, MLA_HEADS * LANES) for kb in k_blocks]
                           + [(kb, MLA_HEADS * MLA_V) for kb in k_blocks]):
        in_specs.append(pl.BlockSpec((s, width), functools.partial(lambda b, i, b0: (b0 + b, 0), b0=b0)))
    args = [q, *ks, *vs]
    aliases = {}
    if prev is not None:
        args.append(prev)
        in_specs.append(_any_spec())
        aliases = {len(args) - 1: 0}
    return pl.pallas_call(
        functools.partial(_mla_kernel, n_seg=n_seg),
        grid=(n_batch, tiles),
        in_specs=in_specs,
        out_specs=pl.BlockSpec((tq, MLA_HEADS * MLA_V), q_map),
        out_shape=jax.ShapeDtypeStruct((N_TOK, MLA_HEADS * MLA_V), BF),
        input_output_aliases=aliases,
        scratch_shapes=[pltpu.VMEM((tq, n_keys), F32)] * 2 + [pltpu.VMEM((tq, n_keys), BF)] * 2,
        compiler_params=_params("parallel", "parallel"),
        name="mla_attend",
    )(*args)


def _gqa_heads(q_ref, k_segs, v_segs, masks, sink_ref, o_ref, s_refs, p_refs):
    tq = q_ref.shape[0]
    lane_q = lax.broadcasted_iota(jnp.int32, (tq, LANES), 1)
    lo_q = lane_q < SWA_HD
    k_roll = [pltpu.roll(k, SWA_HD, 1) for k in k_segs]
    v_roll = [pltpu.roll(v, SWA_HD, 1) for v in v_segs]
    heads = range(SWA_KV_HEADS)
    sinks = []
    for kvh in heads:
        tiles = [q_ref[:, (kvh * 2 + j) * LANES:(kvh * 2 + j + 1) * LANES] for j in range(2)]
        zero = jnp.zeros_like(tiles[0])
        qs = jnp.concatenate([jnp.where(lo_q, tiles[0], zero), jnp.where(lo_q, zero, tiles[0]),
                              jnp.where(lo_q, tiles[1], zero), jnp.where(lo_q, zero, tiles[1])], axis=0)
        sinks.append(jnp.concatenate(
            [sink_ref[0, kvh * SWA_GROUP + g:kvh * SWA_GROUP + g + 1, :]
             for g in range(SWA_GROUP) for _ in range(tq // LANES)], axis=1) * LOG2_E)
        off = 0
        for k, kr, msk in zip(k_segs, k_roll, masks):
            lane_k = lax.broadcasted_iota(jnp.int32, k.shape, 1)
            first = (lane_k < SWA_HD) == (kvh == 0)
            kd = jnp.where(first, k, kr)
            s = _bdot_nt(kd, qs)
            if msk is not None:
                s = jnp.where(msk, s, -1e30)
            s_refs[kvh][off:off + k.shape[0], :] = s
            off += k.shape[0]
    dens = [_softmax_numerator_t(s_refs[kvh], p_refs[kvh], sinks[kvh]) for kvh in heads]
    for kvh in heads:
        acc, off = None, 0
        for v, vr in zip(v_segs, v_roll):
            lane_v = lax.broadcasted_iota(jnp.int32, v.shape, 1)
            first = (lane_v < SWA_HD) == (kvh == 0)
            vd = jnp.where(first, v, vr)
            pv = lax.dot_general(vd.astype(BF), p_refs[kvh][off:off + v.shape[0], :],
                                 (((0,), (0,)), ((), ())), preferred_element_type=F32)
            acc = pv if acc is None else acc + pv
            off += v.shape[0]
        o = (acc / dens[kvh]).T
        for j in range(2):
            o_ref[:, (kvh * 2 + j) * LANES:(kvh * 2 + j + 1) * LANES] = jnp.where(
                lo_q, o[(2 * j) * tq:(2 * j + 1) * tq], o[(2 * j + 1) * tq:(2 * j + 2) * tq]).astype(BF)


def _softmax_numerator_t(s_ref, p_ref, sink=None, row_block=128):
    keys, cols = s_ref.shape
    dens = []
    for c in range(0, cols, LANES):
        mx = s_ref[0:row_block, c:c + LANES]
        for r0 in range(row_block, keys, row_block):
            mx = jnp.maximum(mx, s_ref[r0:r0 + row_block, c:c + LANES])
        m = jnp.max(mx, axis=0, keepdims=True)
        if sink is not None:
            snk = sink[:, c:c + LANES]
            m = jnp.maximum(m, snk)
        mb = jnp.broadcast_to(m, (row_block, LANES))
        part = jnp.zeros((row_block, LANES), F32)
        for r0 in range(0, keys, row_block):
            p = jnp.exp2(s_ref[r0:r0 + row_block, c:c + LANES] - mb)
            part = part + p
            p_ref[r0:r0 + row_block, c:c + LANES] = p.astype(BF)
        den = jnp.sum(part, axis=0, keepdims=True)
        dens.append(den if sink is None else den + jnp.exp2(snk - m))
    return jnp.concatenate(dens, axis=1)


def _swa_scratch(keys, rows):
    return ([pltpu.VMEM((keys, rows), F32)] * SWA_KV_HEADS + [pltpu.VMEM((keys, rows), BF)] * SWA_KV_HEADS)


def _swa_ctx_kernel(q_ref, k_ref, v_ref, sink_ref, o_ref, *scratch):
    _gqa_heads(q_ref, [k_ref[...]], [v_ref[...]], [None], sink_ref, o_ref,
               scratch[:SWA_KV_HEADS], scratch[SWA_KV_HEADS:])


def _swa_ctx(sq, sk, sv, sink_rows, l):
    return pl.pallas_call(
        _swa_ctx_kernel,
        grid=(BATCH,),
        in_specs=[pl.BlockSpec((SEQ, SWA_HEADS * SWA_HD), lambda b: (b, 0)),
                  pl.BlockSpec((SEQ, LANES), lambda b: (b, 0)),
                  pl.BlockSpec((SEQ, LANES), lambda b: (b, 0)),
                  pl.BlockSpec((1, SWA_HEADS, LANES), lambda b: (l, 0, 0))],
        out_specs=pl.BlockSpec((SEQ, SWA_HEADS * SWA_HD), lambda b: (b, 0)),
        out_shape=jax.ShapeDtypeStruct((N_TOK, SWA_HEADS * SWA_HD), BF),
        scratch_shapes=_swa_scratch(SEQ, SWA_GROUP * SEQ),
        compiler_params=_params("parallel"),
        name="swa_ctx",
    )(sq, sk, sv, sink_rows)


def _swa_lat_kernel(q_ref, kp_ref, kc_ref, kn_ref, vp_ref, vc_ref, vn_ref, kx_ref, vx_ref, sink_ref, prev_ref,
                    o_ref, *scratch):
    w = SWA_WINDOW
    n = pl.program_id(1)
    nb = pl.num_programs(1)
    k_band = jnp.concatenate([kp_ref[...], kc_ref[...], kn_ref[...]], axis=0)
    v_band = jnp.concatenate([vp_ref[...], vc_ref[...], vn_ref[...]], axis=0)
    rows = SWA_GROUP * w
    r = lax.broadcasted_iota(jnp.int32, (3 * w, rows), 1) & (w - 1)
    c = lax.broadcasted_iota(jnp.int32, (3 * w, rows), 0)
    valid = (c >= r) & (c <= r + 2 * w) & ((c >= w) | (n > 0)) & ((c < 2 * w) | (n < nb - 1))
    _gqa_heads(q_ref, [k_band, kx_ref[0, 0]], [v_band, vx_ref[0, 0]], [valid, None], sink_ref, o_ref,
               scratch[:SWA_KV_HEADS], scratch[SWA_KV_HEADS:])


def _swa_lat(sq, sk, sv, k_cache, v_cache, sink_rows, l, prev):
    w = SWA_WINDOW
    nb = DEC_SEQ // w
    q0 = N_CTX // w

    def blk(d):
        return lambda b, n: (q0 + b * nb + jnp.clip(n + d, 0, nb - 1), 0)

    kv_specs = [pl.BlockSpec((w, LANES), blk(d)) for d in (-1, 0, 1)]
    cache_spec = pl.BlockSpec((1, 1, PAST_LEN, LANES), lambda b, n: (b, l, 0, 0))
    return pl.pallas_call(
        _swa_lat_kernel,
        grid=(DEC_BATCH, nb),
        in_specs=[pl.BlockSpec((w, SWA_HEADS * SWA_HD), blk(0))] + kv_specs + kv_specs
        + [cache_spec, cache_spec, pl.BlockSpec((1, SWA_HEADS, LANES), lambda b, n: (l, 0, 0)), _any_spec()],
        out_specs=pl.BlockSpec((w, SWA_HEADS * SWA_HD), blk(0)),
        out_shape=jax.ShapeDtypeStruct((N_TOK, SWA_HEADS * SWA_HD), BF),
        input_output_aliases={10: 0},
        scratch_shapes=_swa_scratch(3 * w + PAST_LEN, SWA_GROUP * w),
        compiler_params=_params("parallel", "parallel"),
        name="swa_lat",
    )(sq, sk, sk, sk, sv, sv, sv, k_cache, v_cache, sink_rows, prev)


def _gdn_conv_kernel(*refs):
    x_ref, w_ref, o_ref = refs[0], refs[1], refs[-1]
    x = x_ref[...]
    t = x.shape[0]
    w = w_ref[0]
    r8 = lax.broadcasted_iota(jnp.int32, (SUBLANES, x.shape[1]), 0)
    half = GDN_CONV // 2
    acc = x * w[half:half + 1, :]
    for k in range(GDN_CONV):
        d = k - half
        if d == 0:
            continue
        xs = pltpu.roll(x, (-d) % t, 0)
        top, bot = xs[:SUBLANES], xs[t - SUBLANES:]
        if d < 0:
            top = jnp.where(r8 + d >= 0, top, 0.0)
        else:
            bot = jnp.where(r8 + d < SUBLANES, bot, 0.0)
        xs = jnp.concatenate([top, xs[SUBLANES:t - SUBLANES], bot], axis=0)
        acc = acc + xs * w[k:k + 1, :]
    y = _silu(acc)
    is_qk = pl.program_id(1) < 2
    cols = []
    for h in range(y.shape[1] // LANES):
        yh = y[:, h * LANES:(h + 1) * LANES]
        nrm = lax.rsqrt(jnp.sum(yh * yh, axis=-1, keepdims=True) + 1e-6)
        cols.append(yh * jnp.where(is_qk, nrm, 1.0))
    o_ref[...] = jnp.concatenate(cols, axis=1)


def _gdn_conv(g3, conv_w, l, row0, n_seq, t, prev):
    blk0 = row0 // t
    args = [g3, conv_w]
    in_specs = [pl.BlockSpec((t, GDN_QK_DIM), lambda s, j: (blk0 + s, j)),
                pl.BlockSpec((1, GDN_CONV, GDN_QK_DIM), lambda s, j: (l, 0, j))]
    aliases = {}
    if prev is not None:
        args.append(prev)
        in_specs.append(_any_spec())
        aliases = {2: 0}
    return pl.pallas_call(
        _gdn_conv_kernel,
        grid=(n_seq, 3),
        in_specs=in_specs,
        out_specs=pl.BlockSpec((t, GDN_QK_DIM), lambda s, j: (blk0 + s, j)),
        out_shape=jax.ShapeDtypeStruct((N_TOK, GDN_CONV_CH), F32),
        input_output_aliases=aliases,
        compiler_params=_params("parallel", "parallel"),
        name="gdn_conv",
    )(*args)


def _gate_rows(misc):
    shape = (2 * GDN_NH, LANES)
    sel = (lax.broadcasted_iota(jnp.int32, shape, 1)
           == lax.broadcasted_iota(jnp.int32, shape, 0) + _M_G).astype(BF)
    b1, b2, b3 = _split3(misc)
    return _nt(sel, b1) + (_nt(sel, b2) + _nt(sel, b3))


def _gdn_chunk_pairs(pairs, states_f, states_b):
    c = GDN_CHUNK
    shape = (c, 2 * c)
    ri = lax.broadcasted_iota(jnp.int32, shape, 0)
    lane = lax.broadcasted_iota(jnp.int32, shape, 1)
    cj = lane & (c - 1)
    fwd_half = lane < c
    lower, upper = ri >= cj, ri <= cj
    incl = (fwd_half & lower) | (~fwd_half & upper)
    incl_t = (fwd_half & upper) | (~fwd_half & lower)
    strict = incl & (ri != cj)
    incl_f, incl_b = incl & fwd_half, incl & ~fwd_half
    eye = (ri == cj).astype(F32)
    n = len(pairs)
    rng = range(n)
    fw = [p[0] for p in pairs]
    bw = [p[1] for p in pairs]

    def halves(a, b):
        return jnp.where(fwd_half, a, b)

    def split_rows(m):
        return jnp.concatenate([jnp.where(fwd_half, m, 0.0), jnp.where(fwd_half, 0.0, m)], axis=0)

    def pair_dot3(x, p):
        xh = x.astype(BF)
        xl = (x - xh.astype(F32)).astype(BF)
        phf = p.astype(BF).astype(F32)
        bd_hi = split_rows(phf).astype(BF)
        bd_lo = split_rows(p - phf).astype(BF)
        return jnp.dot(jnp.concatenate([xh, xl, xh], axis=1), jnp.concatenate([bd_hi, bd_hi, bd_lo], axis=0),
                       preferred_element_type=F32)

    g_col = [halves(fw[i]["g_col"], bw[i]["g_col"]) for i in rng]
    gc_row = [jnp.sum(jnp.where(incl_t, g_col[i], 0.0), axis=0, keepdims=True) for i in rng]
    gcf = [jnp.sum(jnp.where(incl_f, pairs[i][2], 0.0), axis=1, keepdims=True) for i in rng]
    gcb = [jnp.sum(jnp.where(incl_b, pairs[i][2], 0.0), axis=1, keepdims=True) for i in rng]
    decay = [jnp.where(incl, jnp.exp(halves(gcf[i], gcb[i]) - gc_row[i]), 0.0) for i in rng]
    qf = [p["q"] * (GDN_DK ** -0.5) for p in fw]
    qb = [p["q"] * (GDN_DK ** -0.5) for p in bw]
    kbf = [p["k"] * p["beta"] for p in fw]
    kbb = [p["k"] * p["beta"] for p in bw]
    z = jnp.zeros((c, GDN_DK), F32)
    kq = [_bdot_nt(jnp.concatenate([jnp.concatenate([kbf[i], kbb[i]], axis=1),
                                    jnp.concatenate([qf[i], qb[i]], axis=1)], axis=0),
                   jnp.concatenate([jnp.concatenate([fw[i]["k"], z], axis=1),
                                    jnp.concatenate([z, bw[i]["k"]], axis=1)], axis=0)) for i in rng]
    pw = [jnp.where(strict, -(kq[i][:c] * decay[i]), 0.0) for i in rng]
    inv = [eye + m for m in pw]
    levels = int(np.log2(c)) - 1
    pw = [pair_dot3(m, m) for m in pw]
    for level in range(levels):
        if level + 1 < levels:
            both = [pair_dot3(jnp.concatenate([inv[i], pw[i]], axis=0), pw[i]) for i in rng]
            inv = [inv[i] + both[i][:c] for i in rng]
            pw = [both[i][c:] for i in rng]
        else:
            inv = [inv[i] + pair_dot3(inv[i], pw[i]) for i in rng]
    ef = [jnp.exp(g) for g in gcf]
    eb = [jnp.exp(g) for g in gcb]
    uw = [_bdot(split_rows(inv[i]),
                jnp.concatenate([jnp.concatenate([fw[i]["v"] * fw[i]["beta"], kbf[i] * ef[i]], axis=1),
                                 jnp.concatenate([bw[i]["v"] * bw[i]["beta"], kbb[i] * eb[i]], axis=1)], axis=0))
          for i in rng]
    a = [jnp.where(incl, kq[i][c:] * decay[i], 0.0) for i in rng]
    glf = [g[c - 1:c, :] for g in gcf]
    glb = [g[0:1, :] for g in gcb]
    n_groups = len(pairs) // len(states_f)
    per = len(states_f)
    sf, sb = list(states_f), list(states_b)
    outs = []
    for grp in range(n_groups):
        ids = [grp * per + p for p in range(per)]
        wqf = [_bdot(jnp.concatenate([uw[i][:c, GDN_DV:], qf[i] * ef[i]], axis=0), sf[p])
               for p, i in enumerate(ids)]
        wqb = [_bdot(jnp.concatenate([uw[i][c:, GDN_DV:], qb[i] * eb[i]], axis=0), sb[p])
               for p, i in enumerate(ids)]
        vnf = [uw[i][:c, :GDN_DV] - wqf[p][:c] for p, i in enumerate(ids)]
        vnb = [uw[i][c:, :GDN_DV] - wqb[p][:c] for p, i in enumerate(ids)]
        av = [_bdot(split_rows(a[i]), jnp.concatenate([vnf[p], vnb[p]], axis=0)) for p, i in enumerate(ids)]
        sf = [sf[p] * jnp.exp(glf[i]) + _bdot_tn(fw[i]["k"] * jnp.exp(glf[i] - gcf[i]), vnf[p])
              for p, i in enumerate(ids)]
        sb = [sb[p] * jnp.exp(glb[i]) + _bdot_tn(bw[i]["k"] * jnp.exp(glb[i] - gcb[i]), vnb[p])
              for p, i in enumerate(ids)]
        outs.append([(wqf[p][c:] + av[p][:c], wqb[p][c:] + av[p][c:]) for p in range(per)])
    return outs, sf, sb


def _gdn_kernel(*refs, context, n_par):
    qf_ref, qb_ref, mf_ref, mb_ref = refs[:4]
    st_ref = refs[-1]
    if context:
        of_ref, ob_ref, sfin_ref = refs[-4:-1]
    else:
        s0_ref = refs[4]
        of_ref, ob_ref = refs[-3:-1]
    n = pl.program_id(1)

    @pl.when(n == 0)
    def _():
        if context:
            st_ref[...] = jnp.zeros_like(st_ref)
        else:
            st_ref[...] = s0_ref[:, 0]

    def problem(qkv, misc, h, d):
        i = d * GDN_HEADS + h
        return dict(q=qkv[:, h * GDN_DK:(h + 1) * GDN_DK],
                    k=qkv[:, GDN_QK_DIM + h * GDN_DK:GDN_QK_DIM + (h + 1) * GDN_DK],
                    v=qkv[:, 2 * GDN_QK_DIM + h * GDN_DV:2 * GDN_QK_DIM + (h + 1) * GDN_DV],
                    g_col=misc[:, _M_G + i:_M_G + i + 1], beta=misc[:, _M_B + i:_M_B + i + 1])

    c = GDN_CHUNK
    n_sub = qf_ref.shape[1] // c
    lane = lax.broadcasted_iota(jnp.int32, (1, 2 * c), 1)
    pairs = []
    for j in range(n_sub):
        rows_f = slice(j * c, (j + 1) * c)
        rows_b = slice((n_sub - 1 - j) * c, (n_sub - j) * c)
        for s in range(n_par):
            qkv_f, qkv_b = qf_ref[s, rows_f, :], qb_ref[s, rows_b, :]
            misc_f, misc_b = mf_ref[s, rows_f, :], mb_ref[s, rows_b, :]
            rows = _gate_rows(jnp.concatenate([misc_f, misc_b], axis=0))
            for h in range(GDN_HEADS):
                g_row = jnp.where(lane < c, rows[h:h + 1, :], rows[GDN_HEADS + h:GDN_HEADS + h + 1, :])
                pairs.append((problem(qkv_f, misc_f, h, 0), problem(qkv_b, misc_b, h, 1), g_row))
    per = [(s, h) for s in range(n_par) for h in range(GDN_HEADS)]
    outs, states_f, states_b = _gdn_chunk_pairs(pairs, [st_ref[s, h] for s, h in per],
                                                [st_ref[s, GDN_HEADS + h] for s, h in per])
    for j in range(n_sub):
        for p, (s, h) in enumerate(per):
            o_f, o_b = outs[j][p]
            of_ref[s, j * c:(j + 1) * c, h * GDN_DV:(h + 1) * GDN_DV] = o_f
            ob_ref[s, (n_sub - 1 - j) * c:(n_sub - j) * c, h * GDN_DV:(h + 1) * GDN_DV] = o_b
    for p, (s, h) in enumerate(per):
        st_ref[s, h] = states_f[p]
        st_ref[s, GDN_HEADS + h] = states_b[p]

    if context:
        @pl.when(n == pl.num_programs(1) - 1)
        def _():
            sfin_ref[:, 0] = st_ref[...]


def _gdn(qkv, misc, l, t, n_seq, seq0, s0, prevs, n_par=GDN_PAR, n_sub=1):
    c = n_sub * GDN_CHUNK
    nc = t // c
    context = s0 is None
    sb0 = seq0 // n_par
    n_all = N_TOK // t
    fwd = lambda s, n: (sb0 + s, n, 0)
    bwd = lambda s, n: (sb0 + s, nc - 1 - n, 0)
    st_spec = pl.BlockSpec((n_par, 1, GDN_NH, GDN_DK, GDN_DV), lambda s, n: (s, l, 0, 0, 0))
    qkv3 = qkv.reshape(n_all, t, GDN_CONV_CH)
    misc3 = misc.reshape(n_all, t, LANES)
    in_specs = [pl.BlockSpec((n_par, c, GDN_CONV_CH), fwd), pl.BlockSpec((n_par, c, GDN_CONV_CH), bwd),
                pl.BlockSpec((n_par, c, LANES), fwd), pl.BlockSpec((n_par, c, LANES), bwd)]
    args = [qkv3, qkv3, misc3, misc3]
    if not context:
        in_specs.append(st_spec)
        args.append(s0)
    out_specs = [pl.BlockSpec((n_par, c, GDN_V_DIM), fwd), pl.BlockSpec((n_par, c, GDN_V_DIM), bwd)]
    out_shape = [jax.ShapeDtypeStruct((n_all, t, GDN_V_DIM), F32)] * 2
    if context:
        out_specs.append(st_spec)
        out_shape.append(jax.ShapeDtypeStruct((n_seq, DEPTH, GDN_NH, GDN_DK, GDN_DV), F32))
    aliases = {}
    for j, p in enumerate(prevs):
        if p is not None:
            aliases[len(args)] = j
            args.append(p.reshape(out_shape[j].shape))
            in_specs.append(_any_spec())
    return pl.pallas_call(
        functools.partial(_gdn_kernel, context=context, n_par=n_par),
        grid=(n_seq // n_par, nc),
        in_specs=in_specs, out_specs=out_specs, out_shape=out_shape,
        input_output_aliases=aliases,
        scratch_shapes=[pltpu.VMEM((n_par, GDN_NH, GDN_DK, GDN_DV), F32)],
        compiler_params=_params("parallel", "arbitrary"),
        name="gdn",
    )(*args)


def _out_kernel(x_ref, mod_ref, om_ref, os_ref, gf_ref, gb_ref, gz_ref, gn_ref, w_ref, o_ref):
    gate = mod_ref[0, 0][:, 2 * D_MODEL:]
    s = gf_ref[...] + gb_ref[...]
    gz = gz_ref[...]
    cols = []
    for h in range(GDN_HEADS):
        sh = s[:, h * GDN_DV:(h + 1) * GDN_DV]
        cols.append(_rms(sh, gn_ref[0]) * _silu(gz[:, h * GDN_DV:(h + 1) * GDN_DV]))
    og = jnp.concatenate(cols, axis=1)
    n_m = MLA_HEADS * MLA_V
    n_s = SWA_HEADS * SWA_HD
    y = (jnp.dot(om_ref[...], w_ref[0, :n_m, :], preferred_element_type=F32)
         + jnp.dot(os_ref[...], w_ref[0, n_m:n_m + n_s, :], preferred_element_type=F32)
         + jnp.dot(og.astype(BF), w_ref[0, n_m + n_s:, :], preferred_element_type=F32))
    o_ref[...] = x_ref[...] + gate * y


def _out_proj(x, mods, l, o_mla, o_swa, o_gf, o_gb, gz, gdn_norm, w_out, tm=TM_OUT):
    row = lambda i: (i, 0)
    lay3 = lambda i: (l, 0, 0)
    return pl.pallas_call(
        _out_kernel,
        grid=(N_TOK // tm,),
        in_specs=[pl.BlockSpec((tm, D_MODEL), row),
                  _mod_spec(l, 1, tm),
                  pl.BlockSpec((tm, MLA_HEADS * MLA_V), row),
                  pl.BlockSpec((tm, SWA_HEADS * SWA_HD), row),
                  pl.BlockSpec((tm, GDN_V_DIM), row),
                  pl.BlockSpec((tm, GDN_V_DIM), row),
                  pl.BlockSpec((tm, GDN_V_DIM), row),
                  pl.BlockSpec((1, 1, GDN_DV), lay3),
                  pl.BlockSpec((1,) + w_out.shape[1:], lay3)],
        out_specs=pl.BlockSpec((tm, D_MODEL), row),
        out_shape=jax.ShapeDtypeStruct((N_TOK, D_MODEL), F32),
        compiler_params=_params("parallel"),
        name="out_proj",
    )(x, mods, o_mla, o_swa, o_gf, o_gb, gz, gdn_norm.reshape(DEPTH, 1, GDN_DV), w_out)


def _axial_rope(n_tokens, dim):
    f32 = np.float32
    rows = n_tokens // GRID_W
    row = np.repeat(np.arange(rows, dtype=f32), GRID_W)
    col = np.tile(np.arange(GRID_W, dtype=f32), rows)
    axis_dim = dim // 2
    inv_freq = (f32(1.0) / (f32(ROPE_BASE) ** (np.arange(0, axis_dim, 2, dtype=f32) / f32(axis_dim)))).astype(f32)
    ang_r = row[:, None] * inv_freq[None, :]
    ang_c = col[:, None] * inv_freq[None, :]
    ang = np.concatenate([ang_r, ang_r, ang_c, ang_c], axis=-1).astype(f32)
    return np.cos(ang).astype(f32), np.sin(ang).astype(f32)


def _rope_table(tm):
    cos_m, sin_m = _axial_rope(DEC_SEQ, MLA_ROPE)
    cos_s, sin_s = _axial_rope(DEC_SEQ, SWA_HD)
    t = DEC_SEQ
    one = lambda w: np.ones((t, w), np.float32)
    zero = lambda w: np.zeros((t, w), np.float32)
    lat = np.concatenate([
        cos_s, cos_s, sin_s, sin_s,
        cos_m, one(LANES - MLA_ROPE), sin_m, zero(LANES - MLA_ROPE),
        one(MLA_NOPE), cos_m, one(LANES - MLA_NOPE - MLA_ROPE),
        zero(MLA_NOPE), sin_m, zero(LANES - MLA_NOPE - MLA_ROPE)], axis=1)
    ident_row = np.concatenate([np.ones(LANES), np.zeros(LANES)] * 3).astype(np.float32)
    ident = np.broadcast_to(ident_row[None, :], (tm, _T_END))
    return jnp.asarray(np.concatenate([ident, lat], axis=0))


def _mixer_weights(w_in, mla_w_qb, mla_w_kvb):
    nl = DEPTH
    offs = np.cumsum([0, MLA_Q_LORA, MLA_KV_LORA, MLA_ROPE, SWA_HEADS * SWA_HD, SWA_KV_HEADS * SWA_HD,
                      SWA_KV_HEADS * SWA_HD, GDN_CONV_CH, GDN_V_DIM, 2 * GDN_NH])
    w_t = jnp.swapaxes(w_in, 1, 2)
    cq_ckv, krope, body, gates = (w_t[:, :offs[2]], w_t[:, offs[2]:offs[3]], w_t[:, offs[3]:offs[8]],
                                  w_t[:, offs[8]:])
    misc = jnp.concatenate([krope, gates, jnp.zeros((nl, LANES - _M_END, D_MODEL), F32)], axis=1)
    win = jnp.concatenate([cq_ckv, body, misc], axis=1).astype(BF)

    r = MLA_Q_LORA
    wq = mla_w_qb.reshape(nl, r, MLA_HEADS, MLA_NOPE + MLA_ROPE)
    pad = jnp.zeros((nl, r, MLA_HEADS, LANES - MLA_NOPE - MLA_ROPE), F32)
    wqb = jnp.concatenate([wq, pad], axis=-1).reshape(nl, r, MLA_HEADS * LANES).astype(BF)

    kvb = mla_w_kvb.reshape(nl, MLA_KV_LORA, MLA_HEADS, MLA_NOPE + MLA_V)
    k_nope = jnp.concatenate([kvb[..., :MLA_NOPE],
                              jnp.zeros((nl, MLA_KV_LORA, MLA_HEADS, LANES - MLA_NOPE), F32)],
                             axis=-1).reshape(nl, MLA_KV_LORA, MLA_HEADS * LANES)
    place = np.zeros((LANES, MLA_HEADS, LANES), np.float32)
    for i in range(MLA_ROPE):
        place[i, :, MLA_NOPE + i] = 1.0
    place = jnp.broadcast_to(jnp.asarray(place.reshape(1, LANES, MLA_HEADS * LANES)),
                             (nl, LANES, MLA_HEADS * LANES))
    wk = jnp.concatenate([k_nope, place], axis=1).astype(BF)
    wv = kvb[..., MLA_NOPE:].reshape(nl, MLA_KV_LORA, MLA_HEADS * MLA_V).astype(BF)
    return win, wqb, wk, wv


def _misc_rows(vals):
    rows = jnp.zeros((DEPTH, 1, LANES), F32)
    return rows.at[:, 0, _M_G:_M_B].set(vals.reshape(DEPTH, GDN_NH).astype(F32))


def kernel(x_prompt, x_sample, cache_mla_ckv, cache_mla_krope, cache_swa_k, cache_swa_v, state_gdn, c, c_ctx,
           w_ada, b_ada, norm_ffn1, ffn1_w1, ffn1_w2, norm_mix, w_in, mla_q_norm, mla_w_qb, mla_kv_norm,
           mla_w_kvb, swa_sink, gdn_conv_w, gdn_a_log, gdn_dt_bias, gdn_norm, w_out, norm_ffn2, ffn2_w1,
           ffn2_w2, final_norm):
    cond = jnp.concatenate([c_ctx[None, :], c, jnp.zeros((COND_ROWS - N_GROUPS, D_MODEL), F32)], axis=0)
    mods = _adaln(cond, w_ada, b_ada)[:, :N_GROUPS].reshape(DEPTH, N_GROUPS, 1, N_MOD * D_MODEL)
    tab = _rope_table(TM_PROJ)
    win, wqb, wk, wv = _mixer_weights(w_in, mla_w_qb, mla_w_kvb)
    w11, w12, w21, w22 = ffn1_w1, ffn1_w2, ffn2_w1, ffn2_w2
    wo = w_out.astype(BF)
    alog, dtb = _misc_rows(gdn_a_log), _misc_rows(gdn_dt_bias)
    sink_rows = jnp.broadcast_to(swa_sink[:, :, None], (DEPTH, SWA_HEADS, LANES))
    cache_misc = jnp.pad(cache_mla_krope, ((0, 0), (0, 0), (0, 0), (0, LANES - MLA_ROPE)))
    cache_k = cache_swa_k.reshape(DEC_BATCH, DEPTH, PAST_LEN, LANES)
    cache_v = cache_swa_v.reshape(DEC_BATCH, DEPTH, PAST_LEN, LANES)
    s0 = state_gdn.reshape(DEC_BATCH, DEPTH, GDN_NH, GDN_DK, GDN_DV)

    xs = [x_prompt.reshape(N_CTX, D_MODEL), x_sample.reshape(N_LAT, D_MODEL)]
    caches = new_st = None
    for l in range(DEPTH):
        x = _ffn(xs, mods, l, 0, norm_ffn1, w11, w12)
        (q_mla, misc, k_mla, v_mla, sq, sk, sv, g3, gz, *caches) = _proj(
            x, mods, l, norm_mix, win, mla_q_norm, wqb, mla_kv_norm, wk, wv, tab, alog, dtb, caches)

        k_c, v_c = _kv_cache(cache_mla_ckv, cache_misc, l, wk, wv)
        o_mla = _mla_attend(q_mla, [k_mla], [v_mla], BATCH, SEQ, 0, [(0, SEQ)], None)
        o_mla = _mla_attend(q_mla, [k_mla, k_c], [v_mla, v_c], DEC_BATCH, DEC_SEQ, N_CTX,
                            [(N_CTX // DEC_SEQ, DEC_SEQ), (0, PAST_LEN)], o_mla)

        o_swa = _swa_ctx(sq, sk, sv, sink_rows, l)
        o_swa = _swa_lat(sq, sk, sv, cache_k, cache_v, sink_rows, l, o_swa)

        qkv = _gdn_conv(g3, gdn_conv_w, l, 0, BATCH, SEQ, None)
        qkv = _gdn_conv(g3, gdn_conv_w, l, N_CTX, DEC_BATCH, DEC_SEQ, qkv)
        o_gf, o_gb, new_st = _gdn(qkv, misc, l, SEQ, BATCH, 0, None, [None, None, new_st], n_par=GDN_PAR_CTX)
        o_gf, o_gb = _gdn(qkv, misc, l, DEC_SEQ, DEC_BATCH, N_CTX // DEC_SEQ, s0, [o_gf, o_gb],
                          n_sub=GDN_SUB_LAT)

        x = _out_proj(x, mods, l, o_mla, o_swa, o_gf.reshape(N_TOK, GDN_V_DIM), o_gb.reshape(N_TOK, GDN_V_DIM),
                      gz, gdn_norm, wo)
        if l + 1 < DEPTH:
            xs = [_ffn([x], mods, l, 2, norm_ffn2, w21, w22)]
        else:
            y_prompt, y_sample = _ffn([x], mods, l, 2, norm_ffn2, w21, w22, final_gain=final_norm)

    new_ckv, new_krope, new_sk, new_sv = caches
    kv_shape = (BATCH, DEPTH, SEQ, SWA_KV_HEADS, SWA_HD)
    return (y_prompt.reshape(BATCH, SEQ, D_MODEL), y_sample.reshape(DEC_BATCH, DEC_SEQ, D_MODEL), new_ckv,
            new_krope, new_sk.reshape(kv_shape), new_sv.reshape(kv_shape),
            new_st.reshape(BATCH, DEPTH, 2, GDN_HEADS, GDN_DK, GDN_DV))
```

```python
import functools

import numpy as np
import jax
import jax.numpy as jnp
from jax import lax
from jax.experimental import pallas as pl
from jax.experimental.pallas import tpu as pltpu

D_MODEL = 1024
BATCH = 16
SEQ = 256
DEPTH = 2
DEC_BATCH = 2
DEC_SEQ = 2048
PAST_LEN = 512
GRID_W = 64
ROPE_BASE = 10000.0
NORM_EPS = 1e-6
N_MOD = 9
D_FF = 2816
MLA_HEADS = 8
MLA_Q_LORA = 384
MLA_KV_LORA = 256
MLA_NOPE = 64
MLA_ROPE = 32
MLA_V = 64
SWA_HEADS = 8
SWA_KV_HEADS = 2
SWA_GROUP = SWA_HEADS // SWA_KV_HEADS
SWA_HD = 64
SWA_WINDOW = 128
GDN_HEADS = 4
GDN_DK = 128
GDN_DV = 128
GDN_CONV = 5
GDN_CHUNK = 64
GDN_QK_DIM = GDN_HEADS * GDN_DK
GDN_V_DIM = GDN_HEADS * GDN_DV
GDN_CONV_CH = 2 * GDN_QK_DIM + GDN_V_DIM
GDN_NH = 2 * GDN_HEADS

N_CTX = BATCH * SEQ
N_LAT = DEC_BATCH * DEC_SEQ
N_TOK = N_CTX + N_LAT
N_GROUPS = 1 + DEC_BATCH
COND_ROWS = 8

LANES = 128
SUBLANES = 8
VMEM_LIMIT_BYTES = 56 * 1024 * 1024

TM_FFN = 512
FFN_LOAD_COLS = 256
TM_PROJ = 2 * SEQ
TM_OUT = 1024
TQ_MLA = 512
GDN_PAR = 2
GDN_PAR_CTX = 8
GDN_SUB_LAT = 4

BF = jnp.bfloat16
F32 = jnp.float32
LOG2_E = 1.4426950408889634
MLA_Q_SCALE = (MLA_NOPE + MLA_ROPE) ** -0.5 * LOG2_E

_C_CQ = 0
_C_CKV = _C_CQ + MLA_Q_LORA
_C_SQ = _C_CKV + MLA_KV_LORA
_C_SK = _C_SQ + SWA_HEADS * SWA_HD
_C_SV = _C_SK + SWA_KV_HEADS * SWA_HD
_C_G3 = _C_SV + SWA_KV_HEADS * SWA_HD
_C_GZ = _C_G3 + GDN_CONV_CH
_C_MA = _C_GZ + GDN_V_DIM
_C_END = _C_MA + LANES
_M_G = MLA_ROPE
_M_B = MLA_ROPE + GDN_NH
_M_END = MLA_ROPE + 2 * GDN_NH
_CACHE_WIDTHS = (MLA_KV_LORA, MLA_ROPE, SWA_KV_HEADS * SWA_HD, SWA_KV_HEADS * SWA_HD)
_T_CS, _T_SS, _T_CA, _T_SA, _T_CQ, _T_SQ = (i * LANES for i in range(6))
_T_END = 6 * LANES


def _params(*sem):
    return pltpu.CompilerParams(dimension_semantics=sem, vmem_limit_bytes=VMEM_LIMIT_BYTES)


def _bdot(a, b):
    return jnp.dot(a.astype(BF), b.astype(BF), preferred_element_type=F32)


def _nt(a, b):
    return lax.dot_general(a, b, (((1,), (1,)), ((), ())), preferred_element_type=F32)


def _bdot_nt(a, b):
    return _nt(a.astype(BF), b.astype(BF))


def _bdot_tn(a, b):
    return lax.dot_general(a.astype(BF), b.astype(BF), (((0,), (0,)), ((), ())),
                           preferred_element_type=F32)


def _split3(a):
    b1 = a.astype(BF)
    r = a - b1.astype(F32)
    b2 = r.astype(BF)
    b3 = (r - b2.astype(F32)).astype(BF)
    return b1, b2, b3


def _silu(x):
    return x / (1.0 + jnp.exp(-x))


def _rms(x, gain, eps=NORM_EPS):
    return x * lax.rsqrt(jnp.mean(x * x, axis=-1, keepdims=True) + eps) * gain


def _group_of_row(r):
    return jnp.where(r < N_CTX, 0, 1 + (r - N_CTX) // DEC_SEQ)


def _any_spec():
    return pl.BlockSpec(memory_space=pl.ANY)


def _adaln_kernel(c_ref, w_ref, b_ref, o_ref):
    o_ref[0] = _bdot(_silu(c_ref[...]), w_ref[0]) + b_ref[0]


def _adaln(cond, w_ada, b_ada, tn=1536):
    n = N_MOD * D_MODEL
    return pl.pallas_call(
        _adaln_kernel,
        grid=(DEPTH, n // tn),
        in_specs=[pl.BlockSpec((COND_ROWS, D_MODEL), lambda l, j: (0, 0)),
                  pl.BlockSpec((1, D_MODEL, tn), lambda l, j: (l, 0, j)),
                  pl.BlockSpec((1, 1, tn), lambda l, j: (l, 0, j))],
        out_specs=pl.BlockSpec((1, COND_ROWS, tn), lambda l, j: (l, 0, j)),
        out_shape=jax.ShapeDtypeStruct((DEPTH, COND_ROWS, n), F32),
        compiler_params=_params("parallel", "parallel"),
        name="adaln",
    )(cond, w_ada, b_ada.reshape(DEPTH, 1, n))


def _mod_spec(l, which, tm):
    return pl.BlockSpec((1, 1, 1, 3 * D_MODEL), lambda i: (l, _group_of_row(i * tm), 0, which))


def _ffn_kernel(*refs, n_in, n_a_tiles, final, layer):
    x_refs = refs[:n_in]
    mod_ref, gain_ref, w1_hbm, w2_hbm = refs[n_in:n_in + 4]
    w1_ref, w2_ref, stage_g, stage_u, stage_d, sem_ref = refs[-6:]
    rest = refs[n_in + 4:-6]
    i = pl.program_id(0)

    if n_in == 2:
        x = jnp.where(i < n_a_tiles, x_refs[0][...], x_refs[1][...])
    else:
        x = x_refs[0][...]
    mod = mod_ref[0, 0]
    shift, scale, gate = mod[:, :D_MODEL], mod[:, D_MODEL:2 * D_MODEL], mod[:, 2 * D_MODEL:]
    hb = (_rms(x, gain_ref[0]) * (1.0 + scale) + shift).astype(BF)

    def finish(ffn):
        y = x + gate * (0.5 * ffn)
        if final:
            fg_ref, oa_ref, ob_ref = rest
            yn = _rms(y, fg_ref[...])

            @pl.when(i < n_a_tiles)
            def _():
                oa_ref[...] = yn

            @pl.when(i >= n_a_tiles)
            def _():
                ob_ref[...] = yn
        else:
            rest[0][...] = y

    fc = stage_d.shape[1]
    n_chunks = D_FF // fc

    def chunk_copies(c):
        slot = c % 2
        cols = pl.ds(c * fc, fc)
        return (pltpu.make_async_copy(w1_hbm.at[layer, :, cols], stage_g.at[slot], sem_ref.at[0, slot]),
                pltpu.make_async_copy(w1_hbm.at[layer, :, pl.ds(D_FF + c * fc, fc)], stage_u.at[slot],
                                      sem_ref.at[1, slot]),
                pltpu.make_async_copy(w2_hbm.at[layer, cols, :], stage_d.at[slot], sem_ref.at[2, slot]))

    @pl.when(i == 0)
    def _():
        for cp in chunk_copies(0):
            cp.start()
        acc = None
        for c in range(n_chunks):
            if c + 1 < n_chunks:
                for cp in chunk_copies(c + 1):
                    cp.start()
            for cp in chunk_copies(c):
                cp.wait()
            slot = c % 2
            wg, wu, wd = stage_g[slot].astype(BF), stage_u[slot].astype(BF), stage_d[slot].astype(BF)
            w1_ref[:, c * fc:(c + 1) * fc] = wg
            w1_ref[:, D_FF + c * fc:D_FF + (c + 1) * fc] = wu
            w2_ref[c * fc:(c + 1) * fc, :] = wd
            g = jnp.dot(hb, wg, preferred_element_type=F32)
            u = jnp.dot(hb, wu, preferred_element_type=F32)
            part = jnp.dot((_silu(g) * u).astype(BF), wd, preferred_element_type=F32)
            acc = part if acc is None else acc + part
        finish(acc)

    @pl.when(i > 0)
    def _():
        gu = jnp.dot(hb, w1_ref[...], preferred_element_type=F32)
        a = _silu(gu[:, :D_FF]) * gu[:, D_FF:]
        finish(jnp.dot(a.astype(BF), w2_ref[...], preferred_element_type=F32))


def _ffn(xs, mods, l, which, gain, w1, w2, final_gain=None, tm=TM_FFN):
    n_a = N_CTX // tm
    row = lambda i: (i, 0)
    first = lambda i: (jnp.minimum(i, n_a - 1), 0)
    second = lambda i: (jnp.maximum(i - n_a, 0), 0)
    lay3 = lambda i: (l, 0, 0)
    x_specs = ([pl.BlockSpec((tm, D_MODEL), row)] if len(xs) == 1
               else [pl.BlockSpec((tm, D_MODEL), first), pl.BlockSpec((tm, D_MODEL), second)])
    in_specs = x_specs + [_mod_spec(l, which, tm),
                          pl.BlockSpec((1, 1, D_MODEL), lay3),
                          _any_spec(), _any_spec()]
    args = list(xs) + [mods, gain.reshape(DEPTH, 1, D_MODEL), w1, w2]
    if final_gain is None:
        out_specs = pl.BlockSpec((tm, D_MODEL), row)
        out_shape = jax.ShapeDtypeStruct((N_TOK, D_MODEL), F32)
    else:
        in_specs.append(pl.BlockSpec((1, D_MODEL), lambda i: (0, 0)))
        args.append(final_gain.reshape(1, D_MODEL))
        out_specs = [pl.BlockSpec((tm, D_MODEL), first), pl.BlockSpec((tm, D_MODEL), second)]
        out_shape = [jax.ShapeDtypeStruct((N_CTX, D_MODEL), F32), jax.ShapeDtypeStruct((N_LAT, D_MODEL), F32)]
    return pl.pallas_call(
        functools.partial(_ffn_kernel, n_in=len(xs), n_a_tiles=n_a, final=final_gain is not None, layer=l),
        grid=(N_TOK // tm,),
        in_specs=in_specs, out_specs=out_specs, out_shape=out_shape,
        scratch_shapes=[pltpu.VMEM((D_MODEL, 2 * D_FF), BF), pltpu.VMEM((D_FF, D_MODEL), BF),
                        pltpu.VMEM((2, D_MODEL, FFN_LOAD_COLS), F32), pltpu.VMEM((2, D_MODEL, FFN_LOAD_COLS), F32),
                        pltpu.VMEM((2, FFN_LOAD_COLS, D_MODEL), F32), pltpu.SemaphoreType.DMA((3, 2))],
        compiler_params=_params("arbitrary"),
        name="ffn",
    )(*args)


def _kv_expand(ckv_n, misc, wk, wv):
    kin = jnp.concatenate([ckv_n, misc], axis=1).astype(BF)
    kk = jnp.dot(kin, wk, preferred_element_type=F32)
    vv = jnp.dot(ckv_n.astype(BF), wv, preferred_element_type=F32)
    return kk.astype(BF), vv.astype(BF)


def _proj_kernel(*refs, n_ctx_tiles):
    (x_ref, mod_ref, gain_ref, win_ref, qg_ref, wqb_ref, kvg_ref, wk_ref, wv_ref, tab_ref,
     alog_ref, dtb_ref) = refs[:12]
    (q_ref, misc_ref, kmla_ref, vmla_ref, sq_ref, sk_ref, sv_ref, g3_ref, gz_ref,
     c_ckv_ref, c_krope_ref, c_sk_ref, c_sv_ref) = refs[-13:]
    i = pl.program_id(0)
    mod = mod_ref[0, 0]
    shift, scale = mod[:, :D_MODEL], mod[:, D_MODEL:2 * D_MODEL]
    hb = (_rms(x_ref[...], gain_ref[0]) * (1.0 + scale) + shift).astype(BF)
    tab = tab_ref[...]

    def project(c0, c1):
        return _nt(hb, win_ref[0, c0:c1, :])

    lane = lax.broadcasted_iota(jnp.int32, (x_ref.shape[0], LANES), 1)

    def rotate_half(t, quarter):
        even = (lane // quarter) % 2 == 0
        tiles = [t[:, c:c + LANES] for c in range(0, t.shape[1], LANES)]
        out = [jnp.where(even, -pltpu.roll(v, LANES - quarter, 1), pltpu.roll(v, quarter, 1)) for v in tiles]
        return out[0] if len(out) == 1 else jnp.concatenate(out, axis=1)

    u_lora = project(_C_CQ, _C_SQ)
    u_misc = project(_C_MA, _C_END)
    u_sq = project(_C_SQ, _C_SK)
    u_skv = project(_C_SK, _C_G3)

    qn = _rms(u_lora[:, :MLA_Q_LORA], qg_ref[0])
    ckv_n = _rms(u_lora[:, MLA_Q_LORA:], kvg_ref[0])

    m = (u_misc * tab[:, _T_CA:_T_CA + LANES]
         + rotate_half(u_misc, MLA_ROPE // 4) * tab[:, _T_SA:_T_SA + LANES])
    z = m + dtb_ref[0]
    softplus = jnp.maximum(z, 0.0) + jnp.log(1.0 + jnp.exp(-jnp.abs(z)))
    decay = -jnp.exp(alog_ref[0]) * softplus
    strength = 1.0 / (1.0 + jnp.exp(-m))
    misc = jnp.where((lane >= _M_G) & (lane < _M_B), decay,
                     jnp.where((lane >= _M_B) & (lane < _M_END), strength, m))
    misc_ref[...] = misc

    n_sq = SWA_HEADS * SWA_HD
    cos_s = tab[:, _T_CS:_T_CS + LANES]
    sin_s = tab[:, _T_SS:_T_SS + LANES]
    cos4 = jnp.concatenate([cos_s] * (n_sq // LANES), axis=1)
    sin4 = jnp.concatenate([sin_s] * (n_sq // LANES), axis=1)
    sq_ref[...] = ((u_sq * cos4 + rotate_half(u_sq, SWA_HD // 4) * sin4) * (SWA_HD ** -0.5 * LOG2_E)).astype(BF)
    sk = u_skv[:, :LANES] * cos_s + rotate_half(u_skv[:, :LANES], SWA_HD // 4) * sin_s
    sv = u_skv[:, LANES:]
    sk_ref[...] = sk
    sv_ref[...] = sv

    @pl.when(i < n_ctx_tiles)
    def _():
        for j in range(c_ckv_ref.shape[0]):
            rows = slice(j * SEQ, (j + 1) * SEQ)
            c_ckv_ref[j, 0] = ckv_n[rows]
            c_krope_ref[j, 0] = misc[rows, :MLA_ROPE]
            c_sk_ref[j, 0] = sk[rows]
            c_sv_ref[j, 0] = sv[rows]

    g3_ref[...] = project(_C_G3, _C_GZ)
    gz_ref[...] = project(_C_GZ, _C_MA)

    q2 = jnp.dot(qn.astype(BF), wqb_ref[0], preferred_element_type=F32)
    cosq = jnp.concatenate([tab[:, _T_CQ:_T_CQ + LANES]] * MLA_HEADS, axis=1)
    sinq = jnp.concatenate([tab[:, _T_SQ:_T_SQ + LANES]] * MLA_HEADS, axis=1)
    q_ref[...] = ((q2 * cosq + rotate_half(q2, MLA_ROPE // 4) * sinq) * MLA_Q_SCALE).astype(BF)
    kk, vv = _kv_expand(ckv_n, misc, wk_ref[0], wv_ref[0])
    kmla_ref[...] = kk
    vmla_ref[...] = vv


def _proj(x, mods, l, gain, win, qg, wqb, kvg, wk, wv, tab, alog, dtb, cache_prev, tm=TM_PROJ):
    assert tm % SEQ == 0
    n_ctx_tiles = N_CTX // tm
    lat_tiles = DEC_SEQ // tm
    lay3 = lambda i: (l, 0, 0)
    row = lambda i: (i, 0)

    def tab_map(i):
        return (jnp.where(i < n_ctx_tiles, 0, 1 + (i - n_ctx_tiles) % lat_tiles), 0)

    def lay_spec(a):
        return pl.BlockSpec((1,) + a.shape[1:], lay3)

    widths = [(MLA_HEADS * LANES, BF), (LANES, F32), (MLA_HEADS * LANES, BF),
              (MLA_HEADS * MLA_V, BF), (SWA_HEADS * SWA_HD, BF), (SWA_KV_HEADS * SWA_HD, F32),
              (SWA_KV_HEADS * SWA_HD, F32), (GDN_CONV_CH, F32), (GDN_V_DIM, F32)]
    out_specs = [pl.BlockSpec((tm, w), row) for w, _ in widths]
    out_shape = [jax.ShapeDtypeStruct((N_TOK, w), dt) for w, dt in widths]
    for w in _CACHE_WIDTHS:
        out_specs.append(pl.BlockSpec((tm // SEQ, 1, SEQ, w),
                                      lambda i: (jnp.minimum(i, n_ctx_tiles - 1), l, 0, 0)))
        out_shape.append(jax.ShapeDtypeStruct((BATCH, DEPTH, SEQ, w), F32))
    qg, kvg = qg.reshape(DEPTH, 1, -1), kvg.reshape(DEPTH, 1, -1)
    args = [x, mods, gain.reshape(DEPTH, 1, D_MODEL), win, qg, wqb, kvg, wk, wv, tab, alog, dtb]
    in_specs = [pl.BlockSpec((tm, D_MODEL), row), _mod_spec(l, 1, tm),
                pl.BlockSpec((1, 1, D_MODEL), lay3), lay_spec(win), lay_spec(qg), lay_spec(wqb),
                lay_spec(kvg), lay_spec(wk), lay_spec(wv), pl.BlockSpec((tm, _T_END), tab_map),
                lay_spec(alog), lay_spec(dtb)]
    aliases = {}
    if cache_prev is not None:
        for j, prev in enumerate(cache_prev):
            aliases[len(args)] = len(widths) + j
            args.append(prev)
            in_specs.append(_any_spec())
    return pl.pallas_call(
        functools.partial(_proj_kernel, n_ctx_tiles=n_ctx_tiles),
        grid=(N_TOK // tm,),
        in_specs=in_specs, out_specs=out_specs, out_shape=out_shape,
        input_output_aliases=aliases,
        compiler_params=_params("arbitrary"),
        name="proj",
    )(*args)


def _kv_cache_kernel(ckv_ref, misc_ref, wk_ref, wv_ref, k_ref, v_ref):
    kk, vv = _kv_expand(ckv_ref[0, 0], misc_ref[0, 0], wk_ref[0], wv_ref[0])
    k_ref[...] = kk
    v_ref[...] = vv


def _kv_cache(cache_ckv, cache_misc, l, wk, wv):
    lay3 = lambda b: (l, 0, 0)
    n = DEC_BATCH * PAST_LEN
    return pl.pallas_call(
        _kv_cache_kernel,
        grid=(DEC_BATCH,),
        in_specs=[pl.BlockSpec((1, 1, PAST_LEN, MLA_KV_LORA), lambda b: (b, l, 0, 0)),
                  pl.BlockSpec((1, 1, PAST_LEN, LANES), lambda b: (b, l, 0, 0)),
                  pl.BlockSpec((1,) + wk.shape[1:], lay3),
                  pl.BlockSpec((1,) + wv.shape[1:], lay3)],
        out_specs=[pl.BlockSpec((PAST_LEN, MLA_HEADS * LANES), lambda b: (b, 0)),
                   pl.BlockSpec((PAST_LEN, MLA_HEADS * MLA_V), lambda b: (b, 0))],
        out_shape=[jax.ShapeDtypeStruct((n, MLA_HEADS * LANES), BF),
                   jax.ShapeDtypeStruct((n, MLA_HEADS * MLA_V), BF)],
        compiler_params=_params("parallel"),
        name="kv_cache",
    )(cache_ckv, cache_misc, wk, wv)


def _softmax_numerator(s_ref, p_ref, row_block=128):
    rows, cols = s_ref.shape
    mx = jnp.max(s_ref[...], axis=-1, keepdims=True)
    sums = []
    for r0 in range(0, rows, row_block):
        mb = jnp.broadcast_to(mx[r0:r0 + row_block], (row_block, LANES))
        part = jnp.zeros((row_block, LANES), F32)
        for c in range(0, cols, LANES):
            p = jnp.exp2(s_ref[r0:r0 + row_block, c:c + LANES] - mb)
            part = part + p
            p_ref[r0:r0 + row_block, c:c + LANES] = p.astype(BF)
        sums.append(jnp.sum(part, axis=-1, keepdims=True))
    return jnp.concatenate(sums, axis=0)


def _mla_kernel(*refs, n_seg):
    q_ref = refs[0]
    k_refs = refs[1:1 + n_seg]
    v_refs = refs[1 + n_seg:1 + 2 * n_seg]
    o_ref = refs[-5]
    s_refs, p_refs = refs[-4:-2], refs[-2:]
    tq = q_ref.shape[0]
    lane = lax.broadcasted_iota(jnp.int32, (tq, LANES), 1)

    def scores(h):
        off = 0
        for k in k_refs:
            s_refs[h % 2][:, off:off + k.shape[0]] = _nt(q_ref[:, h * LANES:(h + 1) * LANES],
                                                         k[:, h * LANES:(h + 1) * LANES])
            off += k.shape[0]

    scores(0)
    outs = []
    for h in range(MLA_HEADS):
        if h + 1 < MLA_HEADS:
            scores(h + 1)
        den = _softmax_numerator(s_refs[h % 2], p_refs[h % 2])
        pair = h // 2
        acc, off = None, 0
        for v in v_refs:
            pv = jnp.dot(p_refs[h % 2][:, off:off + v.shape[0]], v[:, pair * LANES:(pair + 1) * LANES],
                         preferred_element_type=F32)
            acc = pv if acc is None else acc + pv
            off += v.shape[0]
        outs.append(acc / den)
        if h % 2 == 1:
            o_ref[:, pair * LANES:(pair + 1) * LANES] = jnp.where(lane < MLA_V, outs[-2], outs[-1]).astype(BF)


def _mla_attend(q, ks, vs, n_batch, t, row0, k_blocks, prev):
    n_seg = len(ks)
    tq = min(TQ_MLA, t)
    n_keys = sum(s for _, s in k_blocks)
    qb0 = row0 // tq
    tiles = t // tq
    q_map = lambda b, i: (qb0 + b * tiles + i, 0)
    in_specs = [pl.BlockSpec((tq, MLA_HEADS * LANES), q_map)]
    for (b0, s), width in ([(kb, MLA_HEADS * LANES) for kb in k_blocks]
                           + [(kb, MLA_HEADS * MLA_V) for kb in k_blocks]):
        in_specs.append(pl.BlockSpec((s, width), functools.partial(lambda b, i, b0: (b0 + b, 0), b0=b0)))
    args = [q, *ks, *vs]
    aliases = {}
    if prev is not None:
        args.append(prev)
        in_specs.append(_any_spec())
        aliases = {len(args) - 1: 0}
    return pl.pallas_call(
        functools.partial(_mla_kernel, n_seg=n_seg),
        grid=(n_batch, tiles),
        in_specs=in_specs,
        out_specs=pl.BlockSpec((tq, MLA_HEADS * MLA_V), q_map),
        out_shape=jax.ShapeDtypeStruct((N_TOK, MLA_HEADS * MLA_V), BF),
        input_output_aliases=aliases,
        scratch_shapes=[pltpu.VMEM((tq, n_keys), F32)] * 2 + [pltpu.VMEM((tq, n_keys), BF)] * 2,
        compiler_params=_params("parallel", "parallel"),
        name="mla_attend",
    )(*args)


def _gqa_heads(q_ref, k_segs, v_segs, masks, sink_ref, o_ref, s_refs, p_refs):
    tq = q_ref.shape[0]
    lane_q = lax.broadcasted_iota(jnp.int32, (tq, LANES), 1)
    lo_q = lane_q < SWA_HD
    k_roll = [pltpu.roll(k, SWA_HD, 1) for k in k_segs]
    v_roll = [pltpu.roll(v, SWA_HD, 1) for v in v_segs]
    heads = range(SWA_KV_HEADS)
    sinks = []
    for kvh in heads:
        tiles = [q_ref[:, (kvh * 2 + j) * LANES:(kvh * 2 + j + 1) * LANES] for j in range(2)]
        zero = jnp.zeros_like(tiles[0])
        qs = jnp.concatenate([jnp.where(lo_q, tiles[0], zero), jnp.where(lo_q, zero, tiles[0]),
                              jnp.where(lo_q, tiles[1], zero), jnp.where(lo_q, zero, tiles[1])], axis=0)
        sinks.append(jnp.concatenate(
            [sink_ref[0, kvh * SWA_GROUP + g:kvh * SWA_GROUP + g + 1, :]
             for g in range(SWA_GROUP) for _ in range(tq // LANES)], axis=1) * LOG2_E)
        off = 0
        for k, kr, msk in zip(k_segs, k_roll, masks):
            lane_k = lax.broadcasted_iota(jnp.int32, k.shape, 1)
            first = (lane_k < SWA_HD) == (kvh == 0)
            kd = jnp.where(first, k, kr)
            s = _bdot_nt(kd, qs)
            if msk is not None:
                s = jnp.where(msk, s, -1e30)
            s_refs[kvh][off:off + k.shape[0], :] = s
            off += k.shape[0]
    dens = [_softmax_numerator_t(s_refs[kvh], p_refs[kvh], sinks[kvh]) for kvh in heads]
    for kvh in heads:
        acc, off = None, 0
        for v, vr in zip(v_segs, v_roll):
            lane_v = lax.broadcasted_iota(jnp.int32, v.shape, 1)
            first = (lane_v < SWA_HD) == (kvh == 0)
            vd = jnp.where(first, v, vr)
            pv = lax.dot_general(vd.astype(BF), p_refs[kvh][off:off + v.shape[0], :],
                                 (((0,), (0,)), ((), ())), preferred_element_type=F32)
            acc = pv if acc is None else acc + pv
            off += v.shape[0]
        o = (acc / dens[kvh]).T
        for j in range(2):
            o_ref[:, (kvh * 2 + j) * LANES:(kvh * 2 + j + 1) * LANES] = jnp.where(
                lo_q, o[(2 * j) * tq:(2 * j + 1) * tq], o[(2 * j + 1) * tq:(2 * j + 2) * tq]).astype(BF)


def _softmax_numerator_t(s_ref, p_ref, sink=None, row_block=128):
    keys, cols = s_ref.shape
    dens = []
    for c in range(0, cols, LANES):
        mx = s_ref[0:row_block, c:c + LANES]
        for r0 in range(row_block, keys, row_block):
            mx = jnp.maximum(mx, s_ref[r0:r0 + row_block, c:c + LANES])
        m = jnp.max(mx, axis=0, keepdims=True)
        if sink is not None:
            snk = sink[:, c:c + LANES]
            m = jnp.maximum(m, snk)
        mb = jnp.broadcast_to(m, (row_block, LANES))
        part = jnp.zeros((row_block, LANES), F32)
        for r0 in range(0, keys, row_block):
            p = jnp.exp2(s_ref[r0:r0 + row_block, c:c + LANES] - mb)
            part = part + p
            p_ref[r0:r0 + row_block, c:c + LANES] = p.astype(BF)
        den = jnp.sum(part, axis=0, keepdims=True)
        dens.append(den if sink is None else den + jnp.exp2(snk - m))
    return jnp.concatenate(dens, axis=1)


def _swa_scratch(keys, rows):
    return ([pltpu.VMEM((keys, rows), F32)] * SWA_KV_HEADS + [pltpu.VMEM((keys, rows), BF)] * SWA_KV_HEADS)


def _swa_ctx_kernel(q_ref, k_ref, v_ref, sink_ref, o_ref, *scratch):
    _gqa_heads(q_ref, [k_ref[...]], [v_ref[...]], [None], sink_ref, o_ref,
               scratch[:SWA_KV_HEADS], scratch[SWA_KV_HEADS:])


def _swa_ctx(sq, sk, sv, sink_rows, l):
    return pl.pallas_call(
        _swa_ctx_kernel,
        grid=(BATCH,),
        in_specs=[pl.BlockSpec((SEQ, SWA_HEADS * SWA_HD), lambda b: (b, 0)),
                  pl.BlockSpec((SEQ, LANES), lambda b: (b, 0)),
                  pl.BlockSpec((SEQ, LANES), lambda b: (b, 0)),
                  pl.BlockSpec((1, SWA_HEADS, LANES), lambda b: (l, 0, 0))],
        out_specs=pl.BlockSpec((SEQ, SWA_HEADS * SWA_HD), lambda b: (b, 0)),
        out_shape=jax.ShapeDtypeStruct((N_TOK, SWA_HEADS * SWA_HD), BF),
        scratch_shapes=_swa_scratch(SEQ, SWA_GROUP * SEQ),
        compiler_params=_params("parallel"),
        name="swa_ctx",
    )(sq, sk, sv, sink_rows)


def _swa_lat_kernel(q_ref, kp_ref, kc_ref, kn_ref, vp_ref, vc_ref, vn_ref, kx_ref, vx_ref, sink_ref, prev_ref,
                    o_ref, *scratch):
    w = SWA_WINDOW
    n = pl.program_id(1)
    nb = pl.num_programs(1)
    k_band = jnp.concatenate([kp_ref[...], kc_ref[...], kn_ref[...]], axis=0)
    v_band = jnp.concatenate([vp_ref[...], vc_ref[...], vn_ref[...]], axis=0)
    rows = SWA_GROUP * w
    r = lax.broadcasted_iota(jnp.int32, (3 * w, rows), 1) & (w - 1)
    c = lax.broadcasted_iota(jnp.int32, (3 * w, rows), 0)
    valid = (c >= r) & (c <= r + 2 * w) & ((c >= w) | (n > 0)) & ((c < 2 * w) | (n < nb - 1))
    _gqa_heads(q_ref, [k_band, kx_ref[0, 0]], [v_band, vx_ref[0, 0]], [valid, None], sink_ref, o_ref,
               scratch[:SWA_KV_HEADS], scratch[SWA_KV_HEADS:])


def _swa_lat(sq, sk, sv, k_cache, v_cache, sink_rows, l, prev):
    w = SWA_WINDOW
    nb = DEC_SEQ // w
    q0 = N_CTX // w

    def blk(d):
        return lambda b, n: (q0 + b * nb + jnp.clip(n + d, 0, nb - 1), 0)

    kv_specs = [pl.BlockSpec((w, LANES), blk(d)) for d in (-1, 0, 1)]
    cache_spec = pl.BlockSpec((1, 1, PAST_LEN, LANES), lambda b, n: (b, l, 0, 0))
    return pl.pallas_call(
        _swa_lat_kernel,
        grid=(DEC_BATCH, nb),
        in_specs=[pl.BlockSpec((w, SWA_HEADS * SWA_HD), blk(0))] + kv_specs + kv_specs
        + [cache_spec, cache_spec, pl.BlockSpec((1, SWA_HEADS, LANES), lambda b, n: (l, 0, 0)), _any_spec()],
        out_specs=pl.BlockSpec((w, SWA_HEADS * SWA_HD), blk(0)),
        out_shape=jax.ShapeDtypeStruct((N_TOK, SWA_HEADS * SWA_HD), BF),
        input_output_aliases={10: 0},
        scratch_shapes=_swa_scratch(3 * w + PAST_LEN, SWA_GROUP * w),
        compiler_params=_params("parallel", "parallel"),
        name="swa_lat",
    )(sq, sk, sk, sk, sv, sv, sv, k_cache, v_cache, sink_rows, prev)


def _gdn_conv_kernel(x_ref, before_ref, after_ref, w_ref, o_ref):
    i = pl.program_id(0)
    lat_tiles = DEC_SEQ // SEQ
    k_lat = (i - BATCH) % lat_tiles
    first = (i < BATCH) | (k_lat == 0)
    last = (i < BATCH) | (k_lat == lat_tiles - 1)
    x = x_ref[...]
    t = x.shape[0]
    w = w_ref[0]
    xe = jnp.concatenate([jnp.where(first, 0.0, before_ref[...]), x, jnp.where(last, 0.0, after_ref[...])],
                         axis=0)
    te = t + 2 * SUBLANES
    half = GDN_CONV // 2
    acc = x * w[half:half + 1, :]
    for k in range(GDN_CONV):
        d = k - half
        if d == 0:
            continue
        xs = pltpu.roll(xe, (-d) % te, 0)[SUBLANES:SUBLANES + t]
        acc = acc + xs * w[k:k + 1, :]
    y = _silu(acc)
    is_qk = pl.program_id(1) < 2
    cols = []
    for h in range(y.shape[1] // LANES):
        yh = y[:, h * LANES:(h + 1) * LANES]
        nrm = lax.rsqrt(jnp.sum(yh * yh, axis=-1, keepdims=True) + 1e-6)
        cols.append(yh * jnp.where(is_qk, nrm, 1.0))
    o_ref[...] = jnp.concatenate(cols, axis=1)


def _gdn_conv(g3, conv_w, l):
    per = SEQ // SUBLANES
    n_groups = N_TOK // SUBLANES
    tile = pl.BlockSpec((SEQ, GDN_QK_DIM), lambda i, j: (i, j))
    return pl.pallas_call(
        _gdn_conv_kernel,
        grid=(N_TOK // SEQ, 3),
        in_specs=[tile,
                  pl.BlockSpec((SUBLANES, GDN_QK_DIM), lambda i, j: (jnp.maximum(i * per - 1, 0), j)),
                  pl.BlockSpec((SUBLANES, GDN_QK_DIM), lambda i, j: (jnp.minimum((i + 1) * per, n_groups - 1), j)),
                  pl.BlockSpec((1, GDN_CONV, GDN_QK_DIM), lambda i, j: (l, 0, j))],
        out_specs=tile,
        out_shape=jax.ShapeDtypeStruct((N_TOK, GDN_CONV_CH), F32),
        compiler_params=_params("parallel", "parallel"),
        name="gdn_conv",
    )(g3, g3, g3, conv_w)


def _gate_rows(misc):
    shape = (2 * GDN_NH, LANES)
    sel = (lax.broadcasted_iota(jnp.int32, shape, 1)
           == lax.broadcasted_iota(jnp.int32, shape, 0) + _M_G).astype(BF)
    b1, b2, b3 = _split3(misc)
    return _nt(sel, b1) + (_nt(sel, b2) + _nt(sel, b3))


def _gdn_chunk_pairs(pairs, states_f, states_b):
    c = GDN_CHUNK
    shape = (c, 2 * c)
    ri = lax.broadcasted_iota(jnp.int32, shape, 0)
    lane = lax.broadcasted_iota(jnp.int32, shape, 1)
    cj = lane & (c - 1)
    fwd_half = lane < c
    lower, upper = ri >= cj, ri <= cj
    incl = (fwd_half & lower) | (~fwd_half & upper)
    incl_t = (fwd_half & upper) | (~fwd_half & lower)
    strict = incl & (ri != cj)
    incl_f, incl_b = incl & fwd_half, incl & ~fwd_half
    eye = (ri == cj).astype(F32)
    n = len(pairs)
    rng = range(n)
    fw = [p[0] for p in pairs]
    bw = [p[1] for p in pairs]

    def halves(a, b):
        return jnp.where(fwd_half, a, b)

    def split_rows(m):
        return jnp.concatenate([jnp.where(fwd_half, m, 0.0), jnp.where(fwd_half, 0.0, m)], axis=0)

    def pair_dot3(x, p):
        xh = x.astype(BF)
        xl = (x - xh.astype(F32)).astype(BF)
        phf = p.astype(BF).astype(F32)
        bd_hi = split_rows(phf).astype(BF)
        bd_lo = split_rows(p - phf).astype(BF)
        return jnp.dot(jnp.concatenate([xh, xl, xh], axis=1), jnp.concatenate([bd_hi, bd_hi, bd_lo], axis=0),
                       preferred_element_type=F32)

    g_col = [halves(fw[i]["g_col"], bw[i]["g_col"]) for i in rng]
    gc_row = [jnp.sum(jnp.where(incl_t, g_col[i], 0.0), axis=0, keepdims=True) for i in rng]
    gcf = [jnp.sum(jnp.where(incl_f, pairs[i][2], 0.0), axis=1, keepdims=True) for i in rng]
    gcb = [jnp.sum(jnp.where(incl_b, pairs[i][2], 0.0), axis=1, keepdims=True) for i in rng]
    decay = [jnp.where(incl, jnp.exp(halves(gcf[i], gcb[i]) - gc_row[i]), 0.0) for i in rng]
    qf = [p["q"] * (GDN_DK ** -0.5) for p in fw]
    qb = [p["q"] * (GDN_DK ** -0.5) for p in bw]
    kbf = [p["k"] * p["beta"] for p in fw]
    kbb = [p["k"] * p["beta"] for p in bw]
    z = jnp.zeros((c, GDN_DK), F32)
    kq = [_bdot_nt(jnp.concatenate([jnp.concatenate([kbf[i], kbb[i]], axis=1),
                                    jnp.concatenate([qf[i], qb[i]], axis=1)], axis=0),
                   jnp.concatenate([jnp.concatenate([fw[i]["k"], z], axis=1),
                                    jnp.concatenate([z, bw[i]["k"]], axis=1)], axis=0)) for i in rng]
    pw = [jnp.where(strict, -(kq[i][:c] * decay[i]), 0.0) for i in rng]
    inv = [eye + m for m in pw]
    levels = int(np.log2(c)) - 1
    pw = [pair_dot3(m, m) for m in pw]
    for level in range(levels):
        if level + 1 < levels:
            both = [pair_dot3(jnp.concatenate([inv[i], pw[i]], axis=0), pw[i]) for i in rng]
            inv = [inv[i] + both[i][:c] for i in rng]
            pw = [both[i][c:] for i in rng]
        else:
            inv = [inv[i] + pair_dot3(inv[i], pw[i]) for i in rng]
    ef = [jnp.exp(g) for g in gcf]
    eb = [jnp.exp(g) for g in gcb]
    uw = [_bdot(split_rows(inv[i]),
                jnp.concatenate([jnp.concatenate([fw[i]["v"] * fw[i]["beta"], kbf[i] * ef[i]], axis=1),
                                 jnp.concatenate([bw[i]["v"] * bw[i]["beta"], kbb[i] * eb[i]], axis=1)], axis=0))
          for i in rng]
    a = [jnp.where(incl, kq[i][c:] * decay[i], 0.0) for i in rng]
    glf = [g[c - 1:c, :] for g in gcf]
    glb = [g[0:1, :] for g in gcb]
    n_groups = len(pairs) // len(states_f)
    per = len(states_f)
    sf, sb = list(states_f), list(states_b)
    outs = []
    for grp in range(n_groups):
        ids = [grp * per + p for p in range(per)]
        wqf = [_bdot(jnp.concatenate([uw[i][:c, GDN_DV:], qf[i] * ef[i]], axis=0), sf[p])
               for p, i in enumerate(ids)]
        wqb = [_bdot(jnp.concatenate([uw[i][c:, GDN_DV:], qb[i] * eb[i]], axis=0), sb[p])
               for p, i in enumerate(ids)]
        vnf = [uw[i][:c, :GDN_DV] - wqf[p][:c] for p, i in enumerate(ids)]
        vnb = [uw[i][c:, :GDN_DV] - wqb[p][:c] for p, i in enumerate(ids)]
        av = [_bdot(split_rows(a[i]), jnp.concatenate([vnf[p], vnb[p]], axis=0)) for p, i in enumerate(ids)]
        sf = [sf[p] * jnp.exp(glf[i]) + _bdot_tn(fw[i]["k"] * jnp.exp(glf[i] - gcf[i]), vnf[p])
              for p, i in enumerate(ids)]
        sb = [sb[p] * jnp.exp(glb[i]) + _bdot_tn(bw[i]["k"] * jnp.exp(glb[i] - gcb[i]), vnb[p])
              for p, i in enumerate(ids)]
        outs.append([(wqf[p][c:] + av[p][:c], wqb[p][c:] + av[p][c:]) for p in range(per)])
    return outs, sf, sb


def _gdn_kernel(*refs, context, n_par):
    qf_ref, qb_ref, mf_ref, mb_ref = refs[:4]
    st_ref = refs[-1]
    if context:
        of_ref, ob_ref, sfin_ref = refs[-4:-1]
    else:
        s0_ref = refs[4]
        of_ref, ob_ref = refs[-3:-1]
    n = pl.program_id(1)

    @pl.when(n == 0)
    def _():
        if context:
            st_ref[...] = jnp.zeros_like(st_ref)
        else:
            st_ref[...] = s0_ref[:, 0]

    def problem(qkv, misc, h, d):
        i = d * GDN_HEADS + h
        return dict(q=qkv[:, h * GDN_DK:(h + 1) * GDN_DK],
                    k=qkv[:, GDN_QK_DIM + h * GDN_DK:GDN_QK_DIM + (h + 1) * GDN_DK],
                    v=qkv[:, 2 * GDN_QK_DIM + h * GDN_DV:2 * GDN_QK_DIM + (h + 1) * GDN_DV],
                    g_col=misc[:, _M_G + i:_M_G + i + 1], beta=misc[:, _M_B + i:_M_B + i + 1])

    c = GDN_CHUNK
    n_sub = qf_ref.shape[1] // c
    lane = lax.broadcasted_iota(jnp.int32, (1, 2 * c), 1)
    pairs = []
    for j in range(n_sub):
        rows_f = slice(j * c, (j + 1) * c)
        rows_b = slice((n_sub - 1 - j) * c, (n_sub - j) * c)
        for s in range(n_par):
            qkv_f, qkv_b = qf_ref[s, rows_f, :], qb_ref[s, rows_b, :]
            misc_f, misc_b = mf_ref[s, rows_f, :], mb_ref[s, rows_b, :]
            rows = _gate_rows(jnp.concatenate([misc_f, misc_b], axis=0))
            for h in range(GDN_HEADS):
                g_row = jnp.where(lane < c, rows[h:h + 1, :], rows[GDN_HEADS + h:GDN_HEADS + h + 1, :])
                pairs.append((problem(qkv_f, misc_f, h, 0), problem(qkv_b, misc_b, h, 1), g_row))
    per = [(s, h) for s in range(n_par) for h in range(GDN_HEADS)]
    outs, states_f, states_b = _gdn_chunk_pairs(pairs, [st_ref[s, h] for s, h in per],
                                                [st_ref[s, GDN_HEADS + h] for s, h in per])
    for j in range(n_sub):
        for p, (s, h) in enumerate(per):
            o_f, o_b = outs[j][p]
            of_ref[s, j * c:(j + 1) * c, h * GDN_DV:(h + 1) * GDN_DV] = o_f
            ob_ref[s, (n_sub - 1 - j) * c:(n_sub - j) * c, h * GDN_DV:(h + 1) * GDN_DV] = o_b
    for p, (s, h) in enumerate(per):
        st_ref[s, h] = states_f[p]
        st_ref[s, GDN_HEADS + h] = states_b[p]

    if context:
        @pl.when(n == pl.num_programs(1) - 1)
        def _():
            sfin_ref[:, 0] = st_ref[...]


def _gdn(qkv, misc, l, t, n_seq, seq0, s0, prevs, n_par=GDN_PAR, n_sub=1):
    c = n_sub * GDN_CHUNK
    nc = t // c
    context = s0 is None
    sb0 = seq0 // n_par
    n_all = N_TOK // t
    fwd = lambda s, n: (sb0 + s, n, 0)
    bwd = lambda s, n: (sb0 + s, nc - 1 - n, 0)
    st_spec = pl.BlockSpec((n_par, 1, GDN_NH, GDN_DK, GDN_DV), lambda s, n: (s, l, 0, 0, 0))
    qkv3 = qkv.reshape(n_all, t, GDN_CONV_CH)
    misc3 = misc.reshape(n_all, t, LANES)
    in_specs = [pl.BlockSpec((n_par, c, GDN_CONV_CH), fwd), pl.BlockSpec((n_par, c, GDN_CONV_CH), bwd),
                pl.BlockSpec((n_par, c, LANES), fwd), pl.BlockSpec((n_par, c, LANES), bwd)]
    args = [qkv3, qkv3, misc3, misc3]
    if not context:
        in_specs.append(st_spec)
        args.append(s0)
    out_specs = [pl.BlockSpec((n_par, c, GDN_V_DIM), fwd), pl.BlockSpec((n_par, c, GDN_V_DIM), bwd)]
    out_shape = [jax.ShapeDtypeStruct((n_all, t, GDN_V_DIM), F32)] * 2
    if context:
        out_specs.append(st_spec)
        out_shape.append(jax.ShapeDtypeStruct((n_seq, DEPTH, GDN_NH, GDN_DK, GDN_DV), F32))
    aliases = {}
    for j, p in enumerate(prevs):
        if p is not None:
            aliases[len(args)] = j
            args.append(p.reshape(out_shape[j].shape))
            in_specs.append(_any_spec())
    return pl.pallas_call(
        functools.partial(_gdn_kernel, context=context, n_par=n_par),
        grid=(n_seq // n_par, nc),
        in_specs=in_specs, out_specs=out_specs, out_shape=out_shape,
        input_output_aliases=aliases,
        scratch_shapes=[pltpu.VMEM((n_par, GDN_NH, GDN_DK, GDN_DV), F32)],
        compiler_params=_params("parallel", "arbitrary"),
        name="gdn",
    )(*args)


def _out_kernel(x_ref, mod_ref, om_ref, os_ref, gf_ref, gb_ref, gz_ref, gn_ref, w_ref, o_ref):
    gate = mod_ref[0, 0][:, 2 * D_MODEL:]
    s = gf_ref[...] + gb_ref[...]
    gz = gz_ref[...]
    cols = []
    for h in range(GDN_HEADS):
        sh = s[:, h * GDN_DV:(h + 1) * GDN_DV]
        cols.append(_rms(sh, gn_ref[0]) * _silu(gz[:, h * GDN_DV:(h + 1) * GDN_DV]))
    og = jnp.concatenate(cols, axis=1)
    n_m = MLA_HEADS * MLA_V
    n_s = SWA_HEADS * SWA_HD
    y = (jnp.dot(om_ref[...], w_ref[0, :n_m, :], preferred_element_type=F32)
         + jnp.dot(os_ref[...], w_ref[0, n_m:n_m + n_s, :], preferred_element_type=F32)
         + jnp.dot(og.astype(BF), w_ref[0, n_m + n_s:, :], preferred_element_type=F32))
    o_ref[...] = x_ref[...] + gate * y


def _out_proj(x, mods, l, o_mla, o_swa, o_gf, o_gb, gz, gdn_norm, w_out, tm=TM_OUT):
    row = lambda i: (i, 0)
    lay3 = lambda i: (l, 0, 0)
    return pl.pallas_call(
        _out_kernel,
        grid=(N_TOK // tm,),
        in_specs=[pl.BlockSpec((tm, D_MODEL), row),
                  _mod_spec(l, 1, tm),
                  pl.BlockSpec((tm, MLA_HEADS * MLA_V), row),
                  pl.BlockSpec((tm, SWA_HEADS * SWA_HD), row),
                  pl.BlockSpec((tm, GDN_V_DIM), row),
                  pl.BlockSpec((tm, GDN_V_DIM), row),
                  pl.BlockSpec((tm, GDN_V_DIM), row),
                  pl.BlockSpec((1, 1, GDN_DV), lay3),
                  pl.BlockSpec((1,) + w_out.shape[1:], lay3)],
        out_specs=pl.BlockSpec((tm, D_MODEL), row),
        out_shape=jax.ShapeDtypeStruct((N_TOK, D_MODEL), F32),
        compiler_params=_params("parallel"),
        name="out_proj",
    )(x, mods, o_mla, o_swa, o_gf, o_gb, gz, gdn_norm.reshape(DEPTH, 1, GDN_DV), w_out)


def _axial_rope(n_tokens, dim):
    f32 = np.float32
    rows = n_tokens // GRID_W
    row = np.repeat(np.arange(rows, dtype=f32), GRID_W)
    col = np.tile(np.arange(GRID_W, dtype=f32), rows)
    axis_dim = dim // 2
    inv_freq = (f32(1.0) / (f32(ROPE_BASE) ** (np.arange(0, axis_dim, 2, dtype=f32) / f32(axis_dim)))).astype(f32)
    ang_r = row[:, None] * inv_freq[None, :]
    ang_c = col[:, None] * inv_freq[None, :]
    ang = np.concatenate([ang_r, ang_r, ang_c, ang_c], axis=-1).astype(f32)
    return np.cos(ang).astype(f32), np.sin(ang).astype(f32)


def _rope_table(tm):
    cos_m, sin_m = _axial_rope(DEC_SEQ, MLA_ROPE)
    cos_s, sin_s = _axial_rope(DEC_SEQ, SWA_HD)
    t = DEC_SEQ
    one = lambda w: np.ones((t, w), np.float32)
    zero = lambda w: np.zeros((t, w), np.float32)
    lat = np.concatenate([
        cos_s, cos_s, sin_s, sin_s,
        cos_m, one(LANES - MLA_ROPE), sin_m, zero(LANES - MLA_ROPE),
        one(MLA_NOPE), cos_m, one(LANES - MLA_NOPE - MLA_ROPE),
        zero(MLA_NOPE), sin_m, zero(LANES - MLA_NOPE - MLA_ROPE)], axis=1)
    ident_row = np.concatenate([np.ones(LANES), np.zeros(LANES)] * 3).astype(np.float32)
    ident = np.broadcast_to(ident_row[None, :], (tm, _T_END))
    return jnp.asarray(np.concatenate([ident, lat], axis=0))


def _mixer_weights(w_in, mla_w_qb, mla_w_kvb):
    nl = DEPTH
    offs = np.cumsum([0, MLA_Q_LORA, MLA_KV_LORA, MLA_ROPE, SWA_HEADS * SWA_HD, SWA_KV_HEADS * SWA_HD,
                      SWA_KV_HEADS * SWA_HD, GDN_CONV_CH, GDN_V_DIM, 2 * GDN_NH])
    w_t = jnp.swapaxes(w_in, 1, 2)
    cq_ckv, krope, body, gates = (w_t[:, :offs[2]], w_t[:, offs[2]:offs[3]], w_t[:, offs[3]:offs[8]],
                                  w_t[:, offs[8]:])
    misc = jnp.concatenate([krope, gates, jnp.zeros((nl, LANES - _M_END, D_MODEL), F32)], axis=1)
    win = jnp.concatenate([cq_ckv, body, misc], axis=1).astype(BF)

    r = MLA_Q_LORA
    wq = mla_w_qb.reshape(nl, r, MLA_HEADS, MLA_NOPE + MLA_ROPE)
    pad = jnp.zeros((nl, r, MLA_HEADS, LANES - MLA_NOPE - MLA_ROPE), F32)
    wqb = jnp.concatenate([wq, pad], axis=-1).reshape(nl, r, MLA_HEADS * LANES).astype(BF)

    kvb = mla_w_kvb.reshape(nl, MLA_KV_LORA, MLA_HEADS, MLA_NOPE + MLA_V)
    k_nope = jnp.concatenate([kvb[..., :MLA_NOPE],
                              jnp.zeros((nl, MLA_KV_LORA, MLA_HEADS, LANES - MLA_NOPE), F32)],
                             axis=-1).reshape(nl, MLA_KV_LORA, MLA_HEADS * LANES)
    place = np.zeros((LANES, MLA_HEADS, LANES), np.float32)
    for i in range(MLA_ROPE):
        place[i, :, MLA_NOPE + i] = 1.0
    place = jnp.broadcast_to(jnp.asarray(place.reshape(1, LANES, MLA_HEADS * LANES)),
                             (nl, LANES, MLA_HEADS * LANES))
    wk = jnp.concatenate([k_nope, place], axis=1).astype(BF)
    wv = kvb[..., MLA_NOPE:].reshape(nl, MLA_KV_LORA, MLA_HEADS * MLA_V).astype(BF)
    return win, wqb, wk, wv


def _misc_rows(vals):
    rows = jnp.zeros((DEPTH, 1, LANES), F32)
    return rows.at[:, 0, _M_G:_M_B].set(vals.reshape(DEPTH, GDN_NH).astype(F32))


def kernel(x_prompt, x_sample, cache_mla_ckv, cache_mla_krope, cache_swa_k, cache_swa_v, state_gdn, c, c_ctx,
           w_ada, b_ada, norm_ffn1, ffn1_w1, ffn1_w2, norm_mix, w_in, mla_q_norm, mla_w_qb, mla_kv_norm,
           mla_w_kvb, swa_sink, gdn_conv_w, gdn_a_log, gdn_dt_bias, gdn_norm, w_out, norm_ffn2, ffn2_w1,
           ffn2_w2, final_norm):
    cond = jnp.concatenate([c_ctx[None, :], c, jnp.zeros((COND_ROWS - N_GROUPS, D_MODEL), F32)], axis=0)
    mods = _adaln(cond, w_ada, b_ada)[:, :N_GROUPS].reshape(DEPTH, N_GROUPS, 1, N_MOD * D_MODEL)
    tab = _rope_table(TM_PROJ)
    win, wqb, wk, wv = _mixer_weights(w_in, mla_w_qb, mla_w_kvb)
    w11, w12, w21, w22 = ffn1_w1, ffn1_w2, ffn2_w1, ffn2_w2
    wo = w_out.astype(BF)
    alog, dtb = _misc_rows(gdn_a_log), _misc_rows(gdn_dt_bias)
    sink_rows = jnp.broadcast_to(swa_sink[:, :, None], (DEPTH, SWA_HEADS, LANES))
    cache_misc = jnp.pad(cache_mla_krope, ((0, 0), (0, 0), (0, 0), (0, LANES - MLA_ROPE)))
    cache_k = cache_swa_k.reshape(DEC_BATCH, DEPTH, PAST_LEN, LANES)
    cache_v = cache_swa_v.reshape(DEC_BATCH, DEPTH, PAST_LEN, LANES)
    s0 = state_gdn.reshape(DEC_BATCH, DEPTH, GDN_NH, GDN_DK, GDN_DV)

    xs = [x_prompt.reshape(N_CTX, D_MODEL), x_sample.reshape(N_LAT, D_MODEL)]
    caches = new_st = None
    for l in range(DEPTH):
        x = _ffn(xs, mods, l, 0, norm_ffn1, w11, w12)
        (q_mla, misc, k_mla, v_mla, sq, sk, sv, g3, gz, *caches) = _proj(
            x, mods, l, norm_mix, win, mla_q_norm, wqb, mla_kv_norm, wk, wv, tab, alog, dtb, caches)

        k_c, v_c = _kv_cache(cache_mla_ckv, cache_misc, l, wk, wv)
        o_mla = _mla_attend(q_mla, [k_mla], [v_mla], BATCH, SEQ, 0, [(0, SEQ)], None)
        o_mla = _mla_attend(q_mla, [k_mla, k_c], [v_mla, v_c], DEC_BATCH, DEC_SEQ, N_CTX,
                            [(N_CTX // DEC_SEQ, DEC_SEQ), (0, PAST_LEN)], o_mla)

        o_swa = _swa_ctx(sq, sk, sv, sink_rows, l)
        o_swa = _swa_lat(sq, sk, sv, cache_k, cache_v, sink_rows, l, o_swa)

        qkv = _gdn_conv(g3, gdn_conv_w, l)
        o_gf, o_gb, new_st = _gdn(qkv, misc, l, SEQ, BATCH, 0, None, [None, None, new_st], n_par=GDN_PAR_CTX)
        o_gf, o_gb = _gdn(qkv, misc, l, DEC_SEQ, DEC_BATCH, N_CTX // DEC_SEQ, s0, [o_gf, o_gb],
                          n_sub=GDN_SUB_LAT)

        x = _out_proj(x, mods, l, o_mla, o_swa, o_gf.reshape(N_TOK, GDN_V_DIM), o_gb.reshape(N_TOK, GDN_V_DIM),
                      gz, gdn_norm, wo)
        if l + 1 < DEPTH:
            xs = [_ffn([x], mods, l, 2, norm_ffn2, w21, w22)]
        else:
            y_prompt, y_sample = _ffn([x], mods, l, 2, norm_ffn2, w21, w22, final_gain=final_norm)

    new_ckv, new_krope, new_sk, new_sv = caches
    kv_shape = (BATCH, DEPTH, SEQ, SWA_KV_HEADS, SWA_HD)
    return (y_prompt.reshape(BATCH, SEQ, D_MODEL), y_sample.reshape(DEC_BATCH, DEC_SEQ, D_MODEL), new_ckv,
            new_krope, new_sk.reshape(kv_shape), new_sv.reshape(kv_shape),
            new_st.reshape(BATCH, DEPTH, 2, GDN_HEADS, GDN_DK, GDN_DV))
```
